```python
import math
import jax, jax.numpy as jnp
from jax import lax
import numpy as np

D_MODEL = 2048
BATCH = 4
SEQ = 2048
DEPTH = 1
DEC_BATCH = 128
DEC_SEQ = 8
PAST_LEN = 16384
PAGE_SIZE = 128

MIX_WIDTH = 2 * D_MODEL
C_A = MIX_WIDTH // 2
C_B = MIX_WIDTH - C_A
CHUNK = 128
N_HEADS_A = 8
HEAD_DIM_A = C_A // N_HEADS_A
GROUP_B = 16
N_GROUPS_B = C_B // GROUP_B
STATE_P = 64
IN_WIDTH = 3 * C_A + 2 * C_B
EPS = 1e-6
DT_MIN = 1e-3
DT_MAX = 1e-1

kernel_name = "hymba_gmlp_s5_decode_step"


def rmsnorm(x, g):
    xf = x.astype(jnp.float32)
    return xf * lax.rsqrt(jnp.mean(xf * xf, axis=-1, keepdims=True) + EPS) * g.astype(jnp.float32)


def chunk_spatial_mix(v, w_s, b_s):
    bsz, seqlen, nh, dh = v.shape
    pad = (-seqlen) % CHUNK
    vp = jnp.pad(v, ((0, 0), (0, pad), (0, 0), (0, 0)))
    n_chunks = (seqlen + pad) // CHUNK
    vp = vp.reshape(bsz, n_chunks, CHUNK, nh, dh)
    mask = jnp.tril(jnp.ones((CHUNK, CHUNK), jnp.float32))
    w = w_s.astype(jnp.float32) * mask[None]
    out = jnp.einsum('hts,bnshd->bnthd', w, vp) + b_s.astype(jnp.float32).T[None, None, :, :, None]
    return out.reshape(bsz, n_chunks * CHUNK, nh, dh)[:, :seqlen]


def ssm_combine(left, right):
    a1, b1 = left
    a2, b2 = right
    return a1 * a2, a2 * b1 + b2


def s5_branch(xb, h0, a_re, a_im, log_dt, b_re, b_im, c_re, c_im, d_skip):
    bsz, seqlen, _ = xb.shape
    lam = lax.complex(a_re.astype(jnp.float32), a_im.astype(jnp.float32))
    dt = jnp.exp(log_dt.astype(jnp.float32))[:, None]
    lam_bar = jnp.exp(lam * dt)
    b_c = lax.complex(b_re.astype(jnp.float32), b_im.astype(jnp.float32))
    b_bar = ((lam_bar - 1.0) / lam)[:, :, None] * b_c
    u = xb.reshape(bsz, seqlen, N_GROUPS_B, GROUP_B)
    bu = jnp.einsum('gpc,blgc->blgp', b_bar, u.astype(jnp.complex64))
    bu = bu.at[:, 0].add(lam_bar[None] * h0)
    a = jnp.broadcast_to(lam_bar, bu.shape)
    _, h = lax.associative_scan(ssm_combine, (a, bu), axis=1)
    c_c = lax.complex(c_re.astype(jnp.float32), c_im.astype(jnp.float32))
    y = jnp.einsum('gcp,blgp->blgc', c_c, h).real + d_skip.astype(jnp.float32).reshape(N_GROUPS_B, GROUP_B) * u
    return y.reshape(bsz, seqlen, C_B), h[:, -1]


def hybrid_layer(x, h0, g_norm, w_in, g_v, w_s, b_s, a_re, a_im, log_dt,
                 b_re, b_im, c_re, c_im, d_skip, w_glu, b_glu, w_out):
    bsz, seqlen, _ = x.shape
    xn = rmsnorm(x, g_norm)
    z = xn @ w_in.astype(jnp.float32)
    u, v, gate_a, xb, gate_b = jnp.split(z, [C_A, 2 * C_A, 3 * C_A, 3 * C_A + C_B], axis=-1)
    u = jax.nn.gelu(u)
    v = jax.nn.gelu(v).reshape(bsz, seqlen, N_HEADS_A, HEAD_DIM_A)
    v = rmsnorm(v, g_v.reshape(N_HEADS_A, HEAD_DIM_A))
    mixed = chunk_spatial_mix(v, w_s, b_s).reshape(bsz, seqlen, C_A)
    out_a = u * mixed * jax.nn.silu(gate_a)
    y_b, h_last = s5_branch(xb, h0, a_re, a_im, log_dt, b_re, b_im, c_re, c_im, d_skip)
    y_b = jax.nn.gelu(y_b)
    y_b = y_b * jax.nn.sigmoid(y_b @ w_glu.astype(jnp.float32) + b_glu.astype(jnp.float32))
    out_b = y_b * jax.nn.silu(gate_b)
    mix = jnp.concatenate([out_a, out_b], axis=-1)
    x_new = x + mix @ w_out.astype(jnp.float32)
    return x_new, h_last, v.reshape(bsz, seqlen, C_A)


def setup_inputs(seed: int = 0) -> dict:
    key = jax.random.key(seed)
    ks = jax.random.split(key, 24)
    f32 = jnp.float32
    nrm = lambda k, shape, s: jax.random.normal(k, shape, f32) * s
    x_prompt = nrm(ks[0], (BATCH, SEQ, D_MODEL), 1.0)
    x_sample = nrm(ks[1], (DEC_BATCH, DEC_SEQ, D_MODEL), 1.0)
    state_ssm_re = nrm(ks[2], (DEPTH, DEC_BATCH, N_GROUPS_B, STATE_P), 0.3)
    state_ssm_im = nrm(ks[3], (DEPTH, DEC_BATCH, N_GROUPS_B, STATE_P), 0.3)
    g_norm = 1.0 + nrm(ks[4], (DEPTH, D_MODEL), 0.02)
    w_in = nrm(ks[5], (DEPTH, D_MODEL, IN_WIDTH), D_MODEL ** -0.5)
    g_v = 1.0 + nrm(ks[6], (DEPTH, C_A), 0.02)
    w_s = nrm(ks[7], (DEPTH, N_HEADS_A, CHUNK, CHUNK), CHUNK ** -0.5)
    b_s = 1.0 + nrm(ks[8], (DEPTH, N_HEADS_A, CHUNK), 0.02)
    n_idx = jnp.arange(STATE_P, dtype=f32)
    a_re = -0.5 + nrm(ks[9], (DEPTH, N_GROUPS_B, STATE_P), 0.01)
    a_im = math.pi * n_idx[None, None, :] + nrm(ks[10], (DEPTH, N_GROUPS_B, STATE_P), 0.01)
    log_dt = jax.random.uniform(ks[11], (DEPTH, N_GROUPS_B), f32, math.log(DT_MIN), math.log(DT_MAX))
    b_re = nrm(ks[12], (DEPTH, N_GROUPS_B, STATE_P, GROUP_B), (2.0 * GROUP_B) ** -0.5)
    b_im = nrm(ks[13], (DEPTH, N_GROUPS_B, STATE_P, GROUP_B), (2.0 * GROUP_B) ** -0.5)
    c_re = nrm(ks[14], (DEPTH, N_GROUPS_B, GROUP_B, STATE_P), (2.0 * STATE_P) ** -0.5)
    c_im = nrm(ks[15], (DEPTH, N_GROUPS_B, GROUP_B, STATE_P), (2.0 * STATE_P) ** -0.5)
    d_skip = nrm(ks[16], (DEPTH, C_B), 1.0)
    w_glu = nrm(ks[17], (DEPTH, C_B, C_B), C_B ** -0.5)
    b_glu = nrm(ks[18], (DEPTH, C_B), 0.01)
    w_out = nrm(ks[19], (DEPTH, MIX_WIDTH, D_MODEL), MIX_WIDTH ** -0.5)
    g_final = 1.0 + nrm(ks[20], (D_MODEL,), 0.02)
    return {"x_prompt": x_prompt, "x_sample": x_sample,
            "state_ssm_re": state_ssm_re, "state_ssm_im": state_ssm_im,
            "g_norm": g_norm, "w_in": w_in, "g_v": g_v, "w_s": w_s, "b_s": b_s,
            "a_re": a_re, "a_im": a_im, "log_dt": log_dt,
            "b_re": b_re, "b_im": b_im, "c_re": c_re, "c_im": c_im, "d_skip": d_skip,
            "w_glu": w_glu, "b_glu": b_glu, "w_out": w_out, "g_final": g_final}


def reference(x_prompt, x_sample, state_ssm_re, state_ssm_im, g_norm, w_in, g_v, w_s, b_s,
              a_re, a_im, log_dt, b_re, b_im, c_re, c_im, d_skip, w_glu, b_glu, w_out, g_final):
    hp = x_prompt.astype(jnp.float32)
    hs = x_sample.astype(jnp.float32)
    re_p, im_p, re_s, im_s, v_s = [], [], [], [], []
    for l in range(DEPTH):
        params = (g_norm[l], w_in[l], g_v[l], w_s[l], b_s[l], a_re[l], a_im[l], log_dt[l],
                  b_re[l], b_im[l], c_re[l], c_im[l], d_skip[l], w_glu[l], b_glu[l], w_out[l])
        h0_p = jnp.zeros((hp.shape[0], N_GROUPS_B, STATE_P), jnp.complex64)
        h0_s = lax.complex(state_ssm_re[l].astype(jnp.float32), state_ssm_im[l].astype(jnp.float32))
        hp, hl_p, _ = hybrid_layer(hp, h0_p, *params)
        hs, hl_s, vrows_s = hybrid_layer(hs, h0_s, *params)
        re_p.append(hl_p.real)
        im_p.append(hl_p.imag)
        re_s.append(hl_s.real)
        im_s.append(hl_s.imag)
        v_s.append(vrows_s)
    y_prompt = rmsnorm(hp, g_final).astype(x_prompt.dtype)
    y_sample = rmsnorm(hs, g_final).astype(x_sample.dtype)
    new_ssm_re_prompt = jnp.stack(re_p).astype(x_prompt.dtype)
    new_ssm_im_prompt = jnp.stack(im_p).astype(x_prompt.dtype)
    new_ssm_re_sample = jnp.stack(re_s).astype(x_sample.dtype)
    new_ssm_im_sample = jnp.stack(im_s).astype(x_sample.dtype)
    new_chunk_v_sample = jnp.stack(v_s).astype(x_sample.dtype)
    return (y_prompt, y_sample, new_ssm_re_prompt, new_ssm_im_prompt,
            new_ssm_re_sample, new_ssm_im_sample, new_chunk_v_sample)
```

```python
import functools

import jax
import jax.numpy as jnp
from jax import lax
from jax.experimental import pallas as pl
from jax.experimental.pallas import tpu as pltpu

EPS = 1e-6
LANES = 128
CHUNK = 128
GROUP = 16
TC_PROMPT = 16
SCAN_STEPS = 7
VMEM_LIMIT = 56 * 1024 * 1024

F32 = jnp.float32
BF16 = jnp.bfloat16


def _params(*sem):
    return pltpu.CompilerParams(dimension_semantics=sem, vmem_limit_bytes=VMEM_LIMIT)


def _norm_permute_kernel(x_ref, g_ref, o_ref, *, tc, n):
    g = g_ref[...]
    for j in range(tc):
        rows = x_ref[:, j, :]
        ms = jnp.mean(rows * rows, axis=-1, keepdims=True)
        o_ref[j] = (rows * lax.rsqrt(ms + EPS) * g).astype(o_ref.dtype)


def _norm_permute(x3d, g, n):
    nk, tc, d = x3d.shape
    return pl.pallas_call(
        functools.partial(_norm_permute_kernel, tc=tc, n=n),
        grid=(nk // n,),
        in_specs=[pl.BlockSpec((n, tc, d), lambda i: (i, 0, 0)),
                  pl.BlockSpec((1, d), lambda i: (0, 0))],
        out_specs=pl.BlockSpec((tc, n, d), lambda i: (0, i, 0)),
        out_shape=jax.ShapeDtypeStruct((tc, nk, d), BF16),
        compiler_params=_params("parallel"),
    )(x3d, g.reshape(1, d))


def _branch_a_kernel(xn_ref, w_ref, gv_ref, wmix_ref, bmix_ref, oa_ref, *v_refs, tc, n, r, hd):
    d = xn_ref.shape[-1]
    x2 = xn_ref[...].reshape(tc * n, d)
    z = jnp.dot(x2, w_ref[...], preferred_element_type=F32)
    u = jax.nn.gelu(z[:, :hd])
    v = jax.nn.gelu(z[:, hd:2 * hd])
    ga = z[:, 2 * hd:]
    ms = jnp.mean(v * v, axis=-1, keepdims=True)
    vn3 = (v * lax.rsqrt(ms + EPS) * gv_ref[...]).reshape(tc, n, hd)
    pre3 = (u * jax.nn.silu(ga)).reshape(tc, n, hd)
    if v_refs:
        v_refs[0][...] = vn3
    wm = wmix_ref[...]
    bm = bmix_ref[...]
    for c in range(n // r):
        vg = vn3[:, c * r:(c + 1) * r, :].reshape(CHUNK, hd)
        mixed = jnp.dot(wm, vg.astype(BF16), preferred_element_type=F32) + bm
        og = pre3[:, c * r:(c + 1) * r, :].reshape(CHUNK, hd) * mixed
        oa_ref[:, c * r:(c + 1) * r, :] = og.reshape(tc, r, hd).astype(oa_ref.dtype)


def _branch_a(xn, w_a, gv, wmix, bmix, n, want_v):
    tc, nk, d = xn.shape
    nh, _, hd3 = w_a.shape
    hd = hd3 // 3
    r = CHUNK // tc
    out_shape = [jax.ShapeDtypeStruct((tc, nk, nh * hd), BF16)]
    out_specs = [pl.BlockSpec((tc, n, hd), lambda i, h: (0, i, h))]
    if want_v:
        out_shape.append(jax.ShapeDtypeStruct((tc, nk, nh * hd), F32))
        out_specs.append(pl.BlockSpec((tc, n, hd), lambda i, h: (0, i, h)))
    res = pl.pallas_call(
        functools.partial(_branch_a_kernel, tc=tc, n=n, r=r, hd=hd),
        grid=(nk // n, nh),
        in_specs=[pl.BlockSpec((tc, n, d), lambda i, h: (0, i, 0)),
                  pl.BlockSpec((None, d, hd3), lambda i, h: (h, 0, 0)),
                  pl.BlockSpec((None, 1, hd), lambda i, h: (h, 0, 0)),
                  pl.BlockSpec((None, CHUNK, CHUNK), lambda i, h: (h, 0, 0)),
                  pl.BlockSpec((None, CHUNK, hd), lambda i, h: (h, 0, 0))],
        out_specs=out_specs,
        out_shape=out_shape,
        compiler_params=_params("parallel", "arbitrary"),
    )(xn, w_a, gv, wmix, bmix)
    return res if want_v else (res[0], None)


def _proj_b_kernel(w_ref, xn_ref, o_ref):
    o_ref[...] = lax.dot_general(
        w_ref[...], xn_ref[...], (((1,), (1,)), ((), ())),
        preferred_element_type=F32).astype(o_ref.dtype)


def _proj_b(w_bt, xn, rb, nkt):
    tc, nk, d = xn.shape
    rows = w_bt.shape[0]
    return pl.pallas_call(
        _proj_b_kernel,
        grid=(rows // rb, tc, nk // nkt),
        in_specs=[pl.BlockSpec((rb, d), lambda r, j, k: (r, 0)),
                  pl.BlockSpec((None, nkt, d), lambda r, j, k: (j, k, 0))],
        out_specs=pl.BlockSpec((None, rb, nkt), lambda r, j, k: (j, r, k)),
        out_shape=jax.ShapeDtypeStruct((tc, rows, nk), BF16),
        compiler_params=_params("parallel", "parallel", "parallel"),
    )(w_bt, xn)


def _ssm_kernel(x_ref, mp_ref, q_ref, ct_ref, dsk_ref, *rest, tc, gb, nk, has_h0):
    if has_h0:
        h0_ref, y_ref, hf_ref = rest
    else:
        y_ref, hf_ref = rest
    rows = tc * GROUP
    half = hf_ref.shape[1] // 2
    lane = lax.broadcasted_iota(jnp.int32, (half, LANES), 1)

    def body(g, carry):
        r0 = pl.multiple_of(g * GROUP, GROUP)
        xg = x_ref[:, pl.ds(r0, GROUP), :]
        res = jnp.dot(mp_ref[g], xg.reshape(rows, nk), preferred_element_type=F32)
        y_local = res[:rows]
        s_re = res[rows:rows + half]
        s_im = res[rows + half:]
        ct = ct_ref[g]
        if has_h0:
            h0 = h0_ref[g]
            h_re, h_im = h0[:half], h0[half:]
            a_re, a_im = ct[:, 14:15], ct[:, 15:16]
            hf_ref[g] = jnp.concatenate(
                [a_re * h_re - a_im * h_im + s_re, a_re * h_im + a_im * h_re + s_im], axis=0)
            h_in = h0
        else:
            in_re, in_im = [], []
            fin = jnp.zeros((2 * half, LANES), F32)
            for b in range(nk // LANES):
                xr = s_re[:, b * LANES:(b + 1) * LANES]
                xi = s_im[:, b * LANES:(b + 1) * LANES]
                for s in range(SCAN_STEPS):
                    sh = 1 << s
                    cr = jnp.where(lane >= sh, ct[:, s:s + 1], 0.0)
                    ci = jnp.where(lane >= sh, ct[:, 7 + s:8 + s], 0.0)
                    rr = pltpu.roll(xr, sh, 1)
                    ri = pltpu.roll(xi, sh, 1)
                    xr, xi = xr + cr * rr - ci * ri, xi + cr * ri + ci * rr
                last = jnp.concatenate([xr[:, LANES - 1:], xi[:, LANES - 1:]], axis=0)
                lane2 = lax.broadcasted_iota(jnp.int32, (2 * half, LANES), 1)
                fin = jnp.where(lane2 == b, last, fin)
                in_re.append(jnp.where(lane >= 1, pltpu.roll(xr, 1, 1), 0.0))
                in_im.append(jnp.where(lane >= 1, pltpu.roll(xi, 1, 1), 0.0))
            hf_ref[g] = fin
            h_in = jnp.concatenate(
                [jnp.concatenate(in_re, axis=1), jnp.concatenate(in_im, axis=1)], axis=0)
        y = y_local + jnp.dot(q_ref[g], h_in.astype(BF16), preferred_element_type=F32)
        y3 = y.reshape(tc, GROUP, nk) + dsk_ref[pl.ds(r0, GROUP), :][None] * xg.astype(F32)
        y_ref[:, pl.ds(r0, GROUP), :] = jax.nn.gelu(y3).astype(y_ref.dtype)
        return carry

    lax.fori_loop(0, gb, body, 0)


def _ssm(zb, mp, q, ct, dsk, h0, gb):
    tc, _, nk = zb.shape
    ng, mrows, rows = mp.shape
    p2 = mrows - rows
    cb = ng * GROUP
    has_h0 = h0 is not None
    in_specs = [pl.BlockSpec((tc, gb * GROUP, nk), lambda i: (0, i, 0)),
                pl.BlockSpec((gb, mrows, rows), lambda i: (i, 0, 0)),
                pl.BlockSpec((gb, rows, p2), lambda i: (i, 0, 0)),
                pl.BlockSpec((gb, p2 // 2, 16), lambda i: (i, 0, 0)),
                pl.BlockSpec((gb * GROUP, nk), lambda i: (i, 0))]
    args = [zb, mp, q, ct, dsk]
    if has_h0:
        in_specs.append(pl.BlockSpec((gb, p2, nk), lambda i: (i, 0, 0)))
        args.append(h0)
    hf_lanes = nk if has_h0 else LANES
    return pl.pallas_call(
        functools.partial(_ssm_kernel, tc=tc, gb=gb, nk=nk, has_h0=has_h0),
        grid=(ng // gb,),
        in_specs=in_specs,
        out_specs=[pl.BlockSpec((tc, gb * GROUP, nk), lambda i: (0, i, 0)),
                   pl.BlockSpec((gb, p2, hf_lanes), lambda i: (i, 0, 0))],
        out_shape=[jax.ShapeDtypeStruct((tc, cb, nk), BF16),
                   jax.ShapeDtypeStruct((ng, p2, hf_lanes), F32)],
        compiler_params=_params("parallel"),
    )(*args)


def _glu_out_kernel(y_ref, gate_ref, wg_ref, bg_ref, wo_ref, r_ref):
    y = y_ref[...]
    g = jnp.dot(wg_ref[...], y, preferred_element_type=F32) + bg_ref[...]
    ob = y.astype(F32) * jax.nn.sigmoid(g) * jax.nn.silu(gate_ref[...].astype(F32))
    rt = jnp.dot(wo_ref[...], ob.astype(BF16), preferred_element_type=F32)
    r_ref[...] = rt.T


def _glu_out(y_t, zb, wg_t, bg, wo_t, nkt):
    tc, cb, nk = y_t.shape
    d = wo_t.shape[0]
    return pl.pallas_call(
        _glu_out_kernel,
        grid=(tc, nk // nkt),
        in_specs=[pl.BlockSpec((None, cb, nkt), lambda j, k: (j, 0, k)),
                  pl.BlockSpec((None, cb, nkt), lambda j, k: (j, 1, k)),
                  pl.BlockSpec((cb, cb), lambda j, k: (0, 0)),
                  pl.BlockSpec((cb, nkt), lambda j, k: (0, 0)),
                  pl.BlockSpec((d, cb), lambda j, k: (0, 0))],
        out_specs=pl.BlockSpec((None, nkt, d), lambda j, k: (j, k, 0)),
        out_shape=jax.ShapeDtypeStruct((tc, nk, d), F32),
        compiler_params=_params("parallel", "parallel"),
    )(y_t, zb, wg_t, bg, wo_t)


def _out_norm_kernel(x_ref, oa_ref, r_ref, wo_ref, g_ref, y_ref, *, tc, n):
    ca = oa_ref.shape[-1]
    p = jnp.dot(oa_ref[...].reshape(tc * n, ca), wo_ref[...], preferred_element_type=F32)
    g = g_ref[...]
    for j in range(tc):
        acc = x_ref[:, j, :] + p[j * n:(j + 1) * n] + r_ref[j]
        ms = jnp.mean(acc * acc, axis=-1, keepdims=True)
        y_ref[:, j, :] = acc * lax.rsqrt(ms + EPS) * g


def _out_norm(x3d, oa, r, wo_a, gf, n):
    tc, nk, ca = oa.shape
    d = x3d.shape[-1]
    return pl.pallas_call(
        functools.partial(_out_norm_kernel, tc=tc, n=n),
        grid=(nk // n,),
        in_specs=[pl.BlockSpec((n, tc, d), lambda i: (i, 0, 0)),
                  pl.BlockSpec((tc, n, ca), lambda i: (0, i, 0)),
                  pl.BlockSpec((tc, n, d), lambda i: (0, i, 0)),
                  pl.BlockSpec((ca, d), lambda i: (0, 0)),
                  pl.BlockSpec((1, d), lambda i: (0, 0))],
        out_specs=pl.BlockSpec((n, tc, d), lambda i: (i, 0, 0)),
        out_shape=jax.ShapeDtypeStruct((nk, tc, d), F32),
        compiler_params=_params("parallel"),
    )(x3d, oa, r, wo_a, gf.reshape(1, d))


def _ssm_tables(a_re, a_im, log_dt, b_re, b_im, c_re, c_im, tc_sample):
    hi = lax.Precision.HIGHEST
    tc = TC_PROMPT
    dt = jnp.exp(log_dt)[:, None]
    n = jnp.arange(tc + 1, dtype=F32)[:, None, None]
    mag = jnp.exp(n * (a_re * dt)[None])
    ang = n * (a_im * dt)[None]
    pr, pi = mag * jnp.cos(ang), mag * jnp.sin(ang)
    den = a_re * a_re + a_im * a_im
    nr, ni = pr[1] - 1.0, pi[1]
    qr, qi = (nr * a_re + ni * a_im) / den, (ni * a_re - nr * a_im) / den
    bbr = qr[:, :, None] * b_re - qi[:, :, None] * b_im
    bbi = qr[:, :, None] * b_im + qi[:, :, None] * b_re
    er = pr[:tc, :, :, None] * bbr[None] - pi[:tc, :, :, None] * bbi[None]
    ei = pr[:tc, :, :, None] * bbi[None] + pi[:tc, :, :, None] * bbr[None]
    k = (jnp.einsum('gcp,ngpd->ngcd', c_re, er, precision=hi)
         - jnp.einsum('gcp,ngpd->ngcd', c_im, ei, precision=hi))
    ng, npst = a_re.shape
    i_idx = jnp.arange(tc)[:, None]
    j_idx = jnp.arange(tc)[None, :]
    kpad = jnp.concatenate([k, jnp.zeros_like(k[:1])], axis=0)
    m = kpad[jnp.where(i_idx >= j_idx, i_idx - j_idx, tc)]
    m = m.transpose(2, 0, 3, 1, 4).reshape(ng, tc * GROUP, tc * GROUP)
    p_re = er[::-1].transpose(1, 2, 0, 3).reshape(ng, npst, tc * GROUP)
    p_im = ei[::-1].transpose(1, 2, 0, 3).reshape(ng, npst, tc * GROUP)
    mp = jnp.concatenate([m, p_re, p_im], axis=1)
    f_re = c_re[None] * pr[1:, :, None, :] - c_im[None] * pi[1:, :, None, :]
    f_im = c_re[None] * pi[1:, :, None, :] + c_im[None] * pr[1:, :, None, :]
    q = jnp.concatenate([f_re, -f_im], axis=-1).transpose(1, 0, 2, 3).reshape(ng, tc * GROUP, 2 * npst)
    cr, ci = [pr[tc]], [pi[tc]]
    for _ in range(SCAN_STEPS - 1):
        cr, ci = cr + [cr[-1] * cr[-1] - ci[-1] * ci[-1]], ci + [2.0 * cr[-1] * ci[-1]]
    ct = jnp.stack(cr + ci + [pr[tc_sample], pi[tc_sample]], axis=-1)
    return mp, q, ct


def _mix_tables(w_s, b_s, hd, tc):
    nh = w_s.shape[0]
    w = w_s * jnp.tril(jnp.ones((CHUNK, CHUNK), F32))[None]
    r = CHUNK // tc
    if tc == TC_PROMPT:
        wm = w.reshape(nh, r, tc, r, tc).transpose(0, 2, 1, 4, 3).reshape(nh, CHUNK, CHUNK)
        bm = b_s.reshape(nh, r, tc).transpose(0, 2, 1).reshape(nh, CHUNK)
    else:
        eye = jnp.eye(r, dtype=F32)
        wm = jnp.einsum('hts,ab->htasb', w[:, :tc, :tc], eye).reshape(nh, CHUNK, CHUNK)
        bm = jnp.repeat(b_s[:, :tc], r, axis=1)
    return wm.astype(BF16), jnp.broadcast_to(bm[:, :, None], (nh, CHUNK, hd)).astype(F32)


def _layer(x3d, h0, wts, tiles):
    (g_norm, w_a, gv, wmix, bmix, w_bt, mp, q, ct, d_skip, wg_t, b_glu, wo_a, wo_bt, g_final) = wts
    n_norm, n_a, nkt, gb = tiles
    cb = wg_t.shape[0]
    nk = x3d.shape[0]
    xn = _norm_permute(x3d, g_norm, n_norm)
    out_a, v_rows = _branch_a(xn, w_a, gv, wmix, bmix, n_a, h0 is not None)
    zb = _proj_b(w_bt, xn, 1024, nkt)
    dsk = jnp.broadcast_to(d_skip[:, None], (cb, nk))
    y_t, h_fin = _ssm(zb, mp, q, ct, dsk, h0, gb)
    bg = jnp.broadcast_to(b_glu[:, None], (cb, nkt))
    r = _glu_out(y_t, zb, wg_t, bg, wo_bt, nkt)
    y = _out_norm(x3d, out_a, r, wo_a, g_final, n_norm)
    return y, h_fin, v_rows


def kernel(x_prompt, x_sample, state_ssm_re, state_ssm_im, g_norm, w_in, g_v, w_s, b_s,
           a_re, a_im, log_dt, b_re, b_im, c_re, c_im, d_skip, w_glu, b_glu, w_out, g_final):
    depth = g_norm.shape[0]
    assert depth == 1, "single-layer step"
    bsz, seq, d = x_prompt.shape
    dbsz, dseq, _ = x_sample.shape
    nh = w_s.shape[1]
    ng, npst = a_re.shape[1:]
    cb = ng * GROUP
    ca = (w_in.shape[2] - 2 * cb) // 3
    hd = ca // nh
    assert seq // TC_PROMPT == LANES and dbsz == LANES and CHUNK % dseq == 0

    l = 0
    w_in_l = w_in[l]
    wu = w_in_l[:, :ca].reshape(d, nh, hd)
    wv = w_in_l[:, ca:2 * ca].reshape(d, nh, hd)
    wga = w_in_l[:, 2 * ca:3 * ca].reshape(d, nh, hd)
    w_a = jnp.concatenate([wu, wv, wga], axis=-1).transpose(1, 0, 2).astype(BF16)
    w_bt = w_in_l[:, 3 * ca:].T.astype(BF16)
    gv = g_v[l].reshape(nh, 1, hd)
    mp, q, ct = _ssm_tables(a_re[l], a_im[l], log_dt[l], b_re[l], b_im[l], c_re[l], c_im[l], dseq)
    wg_t = w_glu[l].T.astype(BF16)
    wo_a = w_out[l][:ca].astype(BF16)
    wo_bt = w_out[l][ca:].T.astype(BF16)

    def weights(tc, mp_, q_):
        wmix, bmix = _mix_tables(w_s[l], b_s[l], hd, tc)
        return (g_norm[l], w_a, gv, wmix, bmix, w_bt, mp_.astype(BF16), q_.astype(BF16), ct,
                d_skip[l], wg_t, b_glu[l], wo_a, wo_bt, g_final)

    tc_p, nk_p = TC_PROMPT, bsz * seq // TC_PROMPT
    y_p, hf_p, _ = _layer(x_prompt.reshape(nk_p, tc_p, d), None,
                          weights(tc_p, mp, q), (32, 64, 256, 8))
    hf_p = hf_p[:, :, :bsz]
    re_p = hf_p[:, :npst].transpose(2, 0, 1)[None]
    im_p = hf_p[:, npst:].transpose(2, 0, 1)[None]

    tc_s, nk_s = dseq, dbsz
    rows_s = tc_s * GROUP
    mp_s = jnp.concatenate([mp[:, :rows_s, :rows_s], mp[:, TC_PROMPT * GROUP:, -rows_s:]], axis=1)
    q_s = q[:, :rows_s]
    h0 = jnp.concatenate([state_ssm_re[l], state_ssm_im[l]], axis=-1).transpose(1, 2, 0)
    y_s, hf_s, v_s = _layer(x_sample.reshape(nk_s, tc_s, d), h0,
                            weights(tc_s, mp_s, q_s), (64, 128, 128, 8))
    re_s = hf_s[:, :npst].transpose(2, 0, 1)[None]
    im_s = hf_s[:, npst:].transpose(2, 0, 1)[None]
    v_out = v_s.transpose(1, 0, 2)[None]

    dt_p, dt_s = x_prompt.dtype, x_sample.dtype
    return (y_p.reshape(bsz, seq, d).astype(dt_p), y_s.reshape(dbsz, dseq, d).astype(dt_s),
            re_p.astype(dt_p), im_p.astype(dt_p), re_s.astype(dt_s), im_s.astype(dt_s),
            v_out.astype(dt_s))
```

```python
import functools

import jax
import jax.numpy as jnp
from jax import lax
from jax.experimental import pallas as pl
from jax.experimental.pallas import tpu as pltpu

EPS = 1e-6
LANES = 128
CHUNK = 128
GROUP = 16
TC_PROMPT = 16
SCAN_STEPS = 7
ROW_CHUNK = 512
VMEM_LIMIT = 56 * 1024 * 1024

F32 = jnp.float32
BF16 = jnp.bfloat16


def _params(*sem):
    return pltpu.CompilerParams(dimension_semantics=sem, vmem_limit_bytes=VMEM_LIMIT)


def _norm_permute_kernel(x_ref, g_ref, o_ref, xs_ref, *, tc, n):
    x = x_ref[...]
    ms = jnp.mean(x * x, axis=-1, keepdims=True)
    xn = x * lax.rsqrt(ms + EPS) * g_ref[...]
    nc = x.shape[1] // LANES
    for c in range(nc):
        xs_ref[c] = xn[:, c * LANES:(c + 1) * LANES]
    for j in range(tc):
        rows = jnp.concatenate([xs_ref[c, pl.ds(j, n, stride=tc), :] for c in range(nc)], axis=1)
        o_ref[j] = rows.astype(o_ref.dtype)


def _norm_permute(x2d, g, tc, n):
    t, d = x2d.shape
    nk = t // tc
    return pl.pallas_call(
        functools.partial(_norm_permute_kernel, tc=tc, n=n),
        grid=(nk // n,),
        in_specs=[pl.BlockSpec((n * tc, d), lambda i: (i, 0)),
                  pl.BlockSpec((1, d), lambda i: (0, 0))],
        out_specs=pl.BlockSpec((tc, n, d), lambda i: (0, i, 0)),
        out_shape=jax.ShapeDtypeStruct((tc, nk, d), BF16),
        scratch_shapes=[pltpu.VMEM((d // LANES, n * tc, LANES), F32)],
        compiler_params=_params("parallel"),
    )(x2d, g.reshape(1, d))


def _branch_a_kernel(xn_ref, w_ref, gv_ref, wmix_ref, bmix_ref, oa_ref, *v_refs, tc, n, r, hd):
    d = xn_ref.shape[-1]
    x2 = xn_ref[...].reshape(tc * n, d)
    z = jnp.dot(x2, w_ref[...], preferred_element_type=F32)
    u = jax.nn.gelu(z[:, :hd])
    v = jax.nn.gelu(z[:, hd:2 * hd])
    ga = z[:, 2 * hd:]
    ms = jnp.mean(v * v, axis=-1, keepdims=True)
    vn3 = (v * lax.rsqrt(ms + EPS) * gv_ref[...]).reshape(tc, n, hd)
    pre3 = (u * jax.nn.silu(ga)).reshape(tc, n, hd)
    if v_refs:
        v_refs[0][...] = vn3
    wm = wmix_ref[...]
    bm = bmix_ref[...]
    for c in range(n // r):
        vg = vn3[:, c * r:(c + 1) * r, :].reshape(CHUNK, hd)
        mixed = jnp.dot(wm, vg.astype(BF16), preferred_element_type=F32) + bm
        og = pre3[:, c * r:(c + 1) * r, :].reshape(CHUNK, hd) * mixed
        oa_ref[:, c * r:(c + 1) * r, :] = og.reshape(tc, r, hd).astype(oa_ref.dtype)


def _branch_a(xn, w_a, gv, wmix, bmix, n, want_v):
    tc, nk, d = xn.shape
    nh, _, hd3 = w_a.shape
    hd = hd3 // 3
    r = CHUNK // tc
    out_shape = [jax.ShapeDtypeStruct((tc, nk, nh * hd), BF16)]
    out_specs = [pl.BlockSpec((tc, n, hd), lambda i, h: (0, i, h))]
    if want_v:
        out_shape.append(jax.ShapeDtypeStruct((tc, nk, nh * hd), F32))
        out_specs.append(pl.BlockSpec((tc, n, hd), lambda i, h: (0, i, h)))
    res = pl.pallas_call(
        functools.partial(_branch_a_kernel, tc=tc, n=n, r=r, hd=hd),
        grid=(nk // n, nh),
        in_specs=[pl.BlockSpec((tc, n, d), lambda i, h: (0, i, 0)),
                  pl.BlockSpec((None, d, hd3), lambda i, h: (h, 0, 0)),
                  pl.BlockSpec((None, 1, hd), lambda i, h: (h, 0, 0)),
                  pl.BlockSpec((None, CHUNK, CHUNK), lambda i, h: (h, 0, 0)),
                  pl.BlockSpec((None, CHUNK, hd), lambda i, h: (h, 0, 0))],
        out_specs=out_specs,
        out_shape=out_shape,
        compiler_params=_params("parallel", "arbitrary"),
    )(xn, w_a, gv, wmix, bmix)
    return res if want_v else (res[0], None)


def _proj_b_kernel(w_ref, xn_ref, o_ref):
    xn = xn_ref[...]
    for m in range(w_ref.shape[0] // ROW_CHUNK):
        sl = slice(m * ROW_CHUNK, (m + 1) * ROW_CHUNK)
        o_ref[sl, :] = lax.dot_general(
            w_ref[sl, :], xn, (((1,), (1,)), ((), ())),
            preferred_element_type=F32).astype(o_ref.dtype)


def _proj_b(w_bt, xn, nkt):
    tc, nk, d = xn.shape
    rows = w_bt.shape[0]
    return pl.pallas_call(
        _proj_b_kernel,
        grid=(tc, nk // nkt),
        in_specs=[pl.BlockSpec((rows, d), lambda j, k: (0, 0), pipeline_mode=pl.Buffered(1)),
                  pl.BlockSpec((None, nkt, d), lambda j, k: (j, k, 0))],
        out_specs=pl.BlockSpec((None, rows, nkt), lambda j, k: (j, 0, k)),
        out_shape=jax.ShapeDtypeStruct((tc, rows, nk), BF16),
        compiler_params=_params("parallel", "parallel"),
    )(w_bt, xn)


def _ssm_kernel(x_ref, mp_ref, q_ref, ct_ref, dsk_ref, *rest, tc, gb, nk, has_h0):
    if has_h0:
        h0_ref, y_ref, hf_ref = rest
    else:
        y_ref, hf_ref = rest
    rows = tc * GROUP
    half = hf_ref.shape[1] // 2
    lane = lax.broadcasted_iota(jnp.int32, (half, LANES), 1)

    def body(g, carry):
        r0 = pl.multiple_of(g * GROUP, GROUP)
        xg = x_ref[:, pl.ds(r0, GROUP), :]
        res = jnp.dot(mp_ref[g], xg.reshape(rows, nk), preferred_element_type=F32)
        y_local = res[:rows]
        s_re = res[rows:rows + half]
        s_im = res[rows + half:]
        ct = ct_ref[g]
        if has_h0:
            h0 = h0_ref[g]
            h_re, h_im = h0[:half], h0[half:]
            a_re, a_im = ct[:, 14:15], ct[:, 15:16]
            hf_ref[g] = jnp.concatenate(
                [a_re * h_re - a_im * h_im + s_re, a_re * h_im + a_im * h_re + s_im], axis=0)
            h_in = h0
        else:
            in_re, in_im = [], []
            fin = jnp.zeros((2 * half, LANES), F32)
            for b in range(nk // LANES):
                xr = s_re[:, b * LANES:(b + 1) * LANES]
                xi = s_im[:, b * LANES:(b + 1) * LANES]
                for s in range(SCAN_STEPS):
                    sh = 1 << s
                    cr = jnp.where(lane >= sh, ct[:, s:s + 1], 0.0)
                    ci = jnp.where(lane >= sh, ct[:, 7 + s:8 + s], 0.0)
                    rr = pltpu.roll(xr, sh, 1)
                    ri = pltpu.roll(xi, sh, 1)
                    xr, xi = xr + cr * rr - ci * ri, xi + cr * ri + ci * rr
                last = jnp.concatenate([xr[:, LANES - 1:], xi[:, LANES - 1:]], axis=0)
                lane2 = lax.broadcasted_iota(jnp.int32, (2 * half, LANES), 1)
                fin = jnp.where(lane2 == b, last, fin)
                in_re.append(jnp.where(lane >= 1, pltpu.roll(xr, 1, 1), 0.0))
                in_im.append(jnp.where(lane >= 1, pltpu.roll(xi, 1, 1), 0.0))
            hf_ref[g] = fin
            h_in = jnp.concatenate(
                [jnp.concatenate(in_re, axis=1), jnp.concatenate(in_im, axis=1)], axis=0)
        y = y_local + jnp.dot(q_ref[g], h_in.astype(BF16), preferred_element_type=F32)
        y3 = y.reshape(tc, GROUP, nk) + dsk_ref[pl.ds(r0, GROUP), :][None] * xg.astype(F32)
        y_ref[:, pl.ds(r0, GROUP), :] = jax.nn.gelu(y3).astype(y_ref.dtype)
        return carry

    lax.fori_loop(0, gb, body, 0)


def _ssm(zb, mp, q, ct, dsk, h0, gb):
    tc, _, nk = zb.shape
    ng, mrows, rows = mp.shape
    p2 = mrows - rows
    cb = ng * GROUP
    has_h0 = h0 is not None
    in_specs = [pl.BlockSpec((tc, gb * GROUP, nk), lambda i: (0, i, 0)),
                pl.BlockSpec((gb, mrows, rows), lambda i: (i, 0, 0)),
                pl.BlockSpec((gb, rows, p2), lambda i: (i, 0, 0)),
                pl.BlockSpec((gb, p2 // 2, 16), lambda i: (i, 0, 0)),
                pl.BlockSpec((gb * GROUP, nk), lambda i: (i, 0))]
    args = [zb, mp, q, ct, dsk]
    if has_h0:
        in_specs.append(pl.BlockSpec((gb, p2, nk), lambda i: (i, 0, 0)))
        args.append(h0)
    hf_lanes = nk if has_h0 else LANES
    return pl.pallas_call(
        functools.partial(_ssm_kernel, tc=tc, gb=gb, nk=nk, has_h0=has_h0),
        grid=(ng // gb,),
        in_specs=in_specs,
        out_specs=[pl.BlockSpec((tc, gb * GROUP, nk), lambda i: (0, i, 0)),
                   pl.BlockSpec((gb, p2, hf_lanes), lambda i: (i, 0, 0))],
        out_shape=[jax.ShapeDtypeStruct((tc, cb, nk), BF16),
                   jax.ShapeDtypeStruct((ng, p2, hf_lanes), F32)],
        compiler_params=_params("parallel"),
    )(*args)


def _glu_out_kernel(y_ref, gate_ref, wg_ref, bg_ref, wo_ref, r_ref, ob_ref):
    y = y_ref[...]
    for m in range(wg_ref.shape[0] // ROW_CHUNK):
        sl = slice(m * ROW_CHUNK, (m + 1) * ROW_CHUNK)
        g = jnp.dot(wg_ref[sl, :], y, preferred_element_type=F32) + bg_ref[sl, :]
        ob = y_ref[sl, :].astype(F32) * jax.nn.sigmoid(g) * jax.nn.silu(gate_ref[sl, :].astype(F32))
        ob_ref[sl, :] = ob.astype(ob_ref.dtype)
    ob = ob_ref[...]
    for m in range(wo_ref.shape[0] // ROW_CHUNK):
        sl = slice(m * ROW_CHUNK, (m + 1) * ROW_CHUNK)
        r_ref[:, sl] = jnp.dot(wo_ref[sl, :], ob, preferred_element_type=F32).T


def _glu_out(y_t, zb, wg_t, bg, wo_t, nkt):
    tc, cb, nk = y_t.shape
    d = wo_t.shape[0]
    return pl.pallas_call(
        _glu_out_kernel,
        grid=(tc, nk // nkt),
        in_specs=[pl.BlockSpec((None, cb, nkt), lambda j, k: (j, 0, k)),
                  pl.BlockSpec((None, cb, nkt), lambda j, k: (j, 1, k)),
                  pl.BlockSpec((cb, cb), lambda j, k: (0, 0), pipeline_mode=pl.Buffered(1)),
                  pl.BlockSpec((cb, nkt), lambda j, k: (0, 0), pipeline_mode=pl.Buffered(1)),
                  pl.BlockSpec((d, cb), lambda j, k: (0, 0), pipeline_mode=pl.Buffered(1))],
        out_specs=pl.BlockSpec((None, nkt, d), lambda j, k: (j, k, 0)),
        out_shape=jax.ShapeDtypeStruct((tc, nk, d), F32),
        scratch_shapes=[pltpu.VMEM((cb, nkt), BF16)],
        compiler_params=_params("parallel", "parallel"),
    )(y_t, zb, wg_t, bg, wo_t)


def _out_norm_kernel(x_ref, oa_ref, r_ref, wo_ref, g_ref, y_ref, ms_ref, *, tc, n):
    ca = oa_ref.shape[-1]
    d = x_ref.shape[-1]
    nc = d // LANES
    p = jnp.dot(oa_ref[...].reshape(tc * n, ca), wo_ref[...], preferred_element_type=F32)
    for j in range(tc):
        s = p[j * n:(j + 1) * n] + r_ref[j]
        for c in range(nc):
            ms_ref[c, pl.ds(j, n, stride=tc), :] = s[:, c * LANES:(c + 1) * LANES]
    acc = x_ref[...] + jnp.concatenate([ms_ref[c] for c in range(nc)], axis=1)
    ms = jnp.mean(acc * acc, axis=-1, keepdims=True)
    y_ref[...] = acc * lax.rsqrt(ms + EPS) * g_ref[...]


def _out_norm(x2d, oa, r, wo_a, gf, n):
    tc, nk, ca = oa.shape
    t, d = x2d.shape
    return pl.pallas_call(
        functools.partial(_out_norm_kernel, tc=tc, n=n),
        grid=(nk // n,),
        in_specs=[pl.BlockSpec((n * tc, d), lambda i: (i, 0)),
                  pl.BlockSpec((tc, n, ca), lambda i: (0, i, 0)),
                  pl.BlockSpec((tc, n, d), lambda i: (0, i, 0)),
                  pl.BlockSpec((ca, d), lambda i: (0, 0), pipeline_mode=pl.Buffered(1)),
                  pl.BlockSpec((1, d), lambda i: (0, 0))],
        out_specs=pl.BlockSpec((n * tc, d), lambda i: (i, 0)),
        out_shape=jax.ShapeDtypeStruct((t, d), F32),
        scratch_shapes=[pltpu.VMEM((d // LANES, n * tc, LANES), F32)],
        compiler_params=_params("parallel"),
    )(x2d, oa, r, wo_a, gf.reshape(1, d))


def _ssm_tables(a_re, a_im, log_dt, b_re, b_im, c_re, c_im, tc_sample):
    hi = lax.Precision.HIGHEST
    tc = TC_PROMPT
    dt = jnp.exp(log_dt)[:, None]
    n = jnp.arange(tc + 1, dtype=F32)[:, None, None]
    mag = jnp.exp(n * (a_re * dt)[None])
    ang = n * (a_im * dt)[None]
    pr, pi = mag * jnp.cos(ang), mag * jnp.sin(ang)
    den = a_re * a_re + a_im * a_im
    nr, ni = pr[1] - 1.0, pi[1]
    qr, qi = (nr * a_re + ni * a_im) / den, (ni * a_re - nr * a_im) / den
    bbr = qr[:, :, None] * b_re - qi[:, :, None] * b_im
    bbi = qr[:, :, None] * b_im + qi[:, :, None] * b_re
    er = pr[:tc, :, :, None] * bbr[None] - pi[:tc, :, :, None] * bbi[None]
    ei = pr[:tc, :, :, None] * bbi[None] + pi[:tc, :, :, None] * bbr[None]
    k = (jnp.einsum('gcp,ngpd->ngcd', c_re, er, precision=hi)
         - jnp.einsum('gcp,ngpd->ngcd', c_im, ei, precision=hi))
    ng, npst = a_re.shape
    i_idx = jnp.arange(tc)[:, None]
    j_idx = jnp.arange(tc)[None, :]
    kpad = jnp.concatenate([k, jnp.zeros_like(k[:1])], axis=0)
    m = kpad[jnp.where(i_idx >= j_idx, i_idx - j_idx, tc)]
    m = m.transpose(2, 0, 3, 1, 4).reshape(ng, tc * GROUP, tc * GROUP)
    p_re = er[::-1].transpose(1, 2, 0, 3).reshape(ng, npst, tc * GROUP)
    p_im = ei[::-1].transpose(1, 2, 0, 3).reshape(ng, npst, tc * GROUP)
    mp = jnp.concatenate([m, p_re, p_im], axis=1)
    f_re = c_re[None] * pr[1:, :, None, :] - c_im[None] * pi[1:, :, None, :]
    f_im = c_re[None] * pi[1:, :, None, :] + c_im[None] * pr[1:, :, None, :]
    q = jnp.concatenate([f_re, -f_im], axis=-1).transpose(1, 0, 2, 3).reshape(ng, tc * GROUP, 2 * npst)
    cr, ci = [pr[tc]], [pi[tc]]
    for _ in range(SCAN_STEPS - 1):
        cr, ci = cr + [cr[-1] * cr[-1] - ci[-1] * ci[-1]], ci + [2.0 * cr[-1] * ci[-1]]
    ct = jnp.stack(cr + ci + [pr[tc_sample], pi[tc_sample]], axis=-1)
    return mp, q, ct


def _mix_tables(w_s, b_s, hd, tc):
    nh = w_s.shape[0]
    w = w_s * jnp.tril(jnp.ones((CHUNK, CHUNK), F32))[None]
    r = CHUNK // tc
    if tc == TC_PROMPT:
        wm = w.reshape(nh, r, tc, r, tc).transpose(0, 2, 1, 4, 3).reshape(nh, CHUNK, CHUNK)
        bm = b_s.reshape(nh, r, tc).transpose(0, 2, 1).reshape(nh, CHUNK)
    else:
        eye = jnp.eye(r, dtype=F32)
        wm = jnp.einsum('hts,ab->htasb', w[:, :tc, :tc], eye).reshape(nh, CHUNK, CHUNK)
        bm = jnp.repeat(b_s[:, :tc], r, axis=1)
    return wm.astype(BF16), jnp.broadcast_to(bm[:, :, None], (nh, CHUNK, hd)).astype(F32)


def _layer(x2d, tc, h0, wts, tiles):
    (g_norm, w_a, gv, wmix, bmix, w_bt, mp, q, ct, d_skip, wg_t, b_glu, wo_a, wo_bt, g_final) = wts
    n_norm, n_a, nkt, gb = tiles
    cb = wg_t.shape[0]
    nk = x2d.shape[0] // tc
    xn = _norm_permute(x2d, g_norm, tc, n_norm)
    out_a, v_rows = _branch_a(xn, w_a, gv, wmix, bmix, n_a, h0 is not None)
    zb = _proj_b(w_bt, xn, nkt)
    dsk = jnp.broadcast_to(d_skip[:, None], (cb, nk))
    y_t, h_fin = _ssm(zb, mp, q, ct, dsk, h0, gb)
    bg = jnp.broadcast_to(b_glu[:, None], (cb, nkt))
    r = _glu_out(y_t, zb, wg_t, bg, wo_bt, nkt)
    y = _out_norm(x2d, out_a, r, wo_a, g_final, n_norm)
    return y, h_fin, v_rows


def kernel(x_prompt, x_sample, state_ssm_re, state_ssm_im, g_norm, w_in, g_v, w_s, b_s,
           a_re, a_im, log_dt, b_re, b_im, c_re, c_im, d_skip, w_glu, b_glu, w_out, g_final):
    depth = g_norm.shape[0]
    assert depth == 1, "single-layer step"
    bsz, seq, d = x_prompt.shape
    dbsz, dseq, _ = x_sample.shape
    nh = w_s.shape[1]
    ng, npst = a_re.shape[1:]
    cb = ng * GROUP
    ca = (w_in.shape[2] - 2 * cb) // 3
    hd = ca // nh
    assert seq // TC_PROMPT == LANES and dbsz == LANES and CHUNK % dseq == 0

    l = 0
    w_in_l = w_in[l]
    wu = w_in_l[:, :ca].reshape(d, nh, hd)
    wv = w_in_l[:, ca:2 * ca].reshape(d, nh, hd)
    wga = w_in_l[:, 2 * ca:3 * ca].reshape(d, nh, hd)
    w_a = jnp.concatenate([wu, wv, wga], axis=-1).transpose(1, 0, 2).astype(BF16)
    w_bt = w_in_l[:, 3 * ca:].T.astype(BF16)
    gv = g_v[l].reshape(nh, 1, hd)
    mp, q, ct = _ssm_tables(a_re[l], a_im[l], log_dt[l], b_re[l], b_im[l], c_re[l], c_im[l], dseq)
    wg_t = w_glu[l].T.astype(BF16)
    wo_a = w_out[l][:ca].astype(BF16)
    wo_bt = w_out[l][ca:].T.astype(BF16)

    def weights(tc, mp_, q_):
        wmix, bmix = _mix_tables(w_s[l], b_s[l], hd, tc)
        return (g_norm[l], w_a, gv, wmix, bmix, w_bt, mp_.astype(BF16), q_.astype(BF16), ct,
                d_skip[l], wg_t, b_glu[l], wo_a, wo_bt, g_final)

    tc_p, nk_p = TC_PROMPT, bsz * seq // TC_PROMPT
    y_p, hf_p, _ = _layer(x_prompt.reshape(nk_p * tc_p, d), tc_p, None,
                          weights(tc_p, mp, q), (32, 64, 256, 8))
    hf_p = hf_p[:, :, :bsz]
    re_p = hf_p[:, :npst].transpose(2, 0, 1)[None]
    im_p = hf_p[:, npst:].transpose(2, 0, 1)[None]

    tc_s, nk_s = dseq, dbsz
    rows_s = tc_s * GROUP
    mp_s = jnp.concatenate([mp[:, :rows_s, :rows_s], mp[:, TC_PROMPT * GROUP:, -rows_s:]], axis=1)
    q_s = q[:, :rows_s]
    h0 = jnp.concatenate([state_ssm_re[l], state_ssm_im[l]], axis=-1).transpose(1, 2, 0)
    y_s, hf_s, v_s = _layer(x_sample.reshape(nk_s * tc_s, d), tc_s, h0,
                            weights(tc_s, mp_s, q_s), (64, 128, 128, 8))
    re_s = hf_s[:, :npst].transpose(2, 0, 1)[None]
    im_s = hf_s[:, npst:].transpose(2, 0, 1)[None]
    v_out = v_s.transpose(1, 0, 2)[None]

    dt_p, dt_s = x_prompt.dtype, x_sample.dtype
    return (y_p.reshape(bsz, seq, d).astype(dt_p), y_s.reshape(dbsz, dseq, d).astype(dt_s),
            re_p.astype(dt_p), im_p.astype(dt_p), re_s.astype(dt_s), im_s.astype(dt_s),
            v_out.astype(dt_s))
```

```python
import functools

import jax
import jax.numpy as jnp
from jax import lax
from jax.experimental import pallas as pl
from jax.experimental.pallas import tpu as pltpu

EPS = 1e-6
LANES = 128
CHUNK = 128
GROUP = 16
TC_PROMPT = 16
SCAN_STEPS = 7
ROW_CHUNK = 512
POW_COLS = 16
N_POW = 11
VMEM_LIMIT = 56 * 1024 * 1024

F32 = jnp.float32
BF16 = jnp.bfloat16


def _params(*sem):
    return pltpu.CompilerParams(dimension_semantics=sem, vmem_limit_bytes=VMEM_LIMIT)


def _norm_permute_kernel(x_ref, g_ref, o_ref, xs_ref, *, tc, n):
    x = x_ref[...]
    ms = jnp.mean(x * x, axis=-1, keepdims=True)
    xn = x * lax.rsqrt(ms + EPS) * g_ref[...]
    nc = x.shape[1] // LANES
    for c in range(nc):
        xs_ref[c] = xn[:, c * LANES:(c + 1) * LANES]
    for j in range(tc):
        rows = jnp.concatenate([xs_ref[c, pl.ds(j, n, stride=tc), :] for c in range(nc)], axis=1)
        o_ref[j] = rows.astype(o_ref.dtype)


def _norm_permute(x2d, g, tc, n):
    t, d = x2d.shape
    nk = t // tc
    return pl.pallas_call(
        functools.partial(_norm_permute_kernel, tc=tc, n=n),
        grid=(nk // n,),
        in_specs=[pl.BlockSpec((n * tc, d), lambda i: (i, 0)),
                  pl.BlockSpec((1, d), lambda i: (0, 0))],
        out_specs=pl.BlockSpec((tc, n, d), lambda i: (0, i, 0)),
        out_shape=jax.ShapeDtypeStruct((tc, nk, d), BF16),
        scratch_shapes=[pltpu.VMEM((d // LANES, n * tc, LANES), F32)],
        compiler_params=_params("parallel"),
    )(x2d, g.reshape(1, d))


def _branch_a_kernel(xn_ref, wu_ref, wv_ref, wg_ref, gv_ref, wmix_ref, bmix_ref, oa_ref, *rest,
                     tc, n, r, hd):
    *v_refs, w_ref = rest
    d = xn_ref.shape[-1]

    @pl.when(pl.program_id(1) == 0)
    def _():
        w_ref[:, :hd] = wu_ref[...].astype(BF16)
        w_ref[:, hd:2 * hd] = wv_ref[...].astype(BF16)
        w_ref[:, 2 * hd:] = wg_ref[...].astype(BF16)

    x2 = xn_ref[...].reshape(tc * n, d)
    z = jnp.dot(x2, w_ref[...], preferred_element_type=F32)
    u = jax.nn.gelu(z[:, :hd])
    v = jax.nn.gelu(z[:, hd:2 * hd])
    ga = z[:, 2 * hd:]
    ms = jnp.mean(v * v, axis=-1, keepdims=True)
    vn3 = (v * lax.rsqrt(ms + EPS) * gv_ref[...]).reshape(tc, n, hd)
    pre3 = (u * jax.nn.silu(ga)).reshape(tc, n, hd)
    if v_refs:
        v_refs[0][...] = vn3
    wm = wmix_ref[...]
    bm = bmix_ref[...]
    for c in range(n // r):
        vg = vn3[:, c * r:(c + 1) * r, :].reshape(CHUNK, hd)
        mixed = jnp.dot(wm, vg.astype(BF16), preferred_element_type=F32) + bm
        og = pre3[:, c * r:(c + 1) * r, :].reshape(CHUNK, hd) * mixed
        oa_ref[:, c * r:(c + 1) * r, :] = og.reshape(tc, r, hd).astype(oa_ref.dtype)


def _branch_a(xn, w_in, gv, wmix, bmix, n, want_v):
    tc, nk, d = xn.shape
    nh, _, hd = gv.shape
    r = CHUNK // tc
    out_shape = [jax.ShapeDtypeStruct((tc, nk, nh * hd), BF16)]
    out_specs = [pl.BlockSpec((tc, n, hd), lambda h, i: (0, i, h))]
    if want_v:
        out_shape.append(jax.ShapeDtypeStruct((tc, nk, nh * hd), F32))
        out_specs.append(pl.BlockSpec((tc, n, hd), lambda h, i: (0, i, h)))
    res = pl.pallas_call(
        functools.partial(_branch_a_kernel, tc=tc, n=n, r=r, hd=hd),
        grid=(nh, nk // n),
        in_specs=[pl.BlockSpec((tc, n, d), lambda h, i: (0, i, 0)),
                  pl.BlockSpec((d, hd), lambda h, i: (0, h)),
                  pl.BlockSpec((d, hd), lambda h, i: (0, nh + h)),
                  pl.BlockSpec((d, hd), lambda h, i: (0, 2 * nh + h)),
                  pl.BlockSpec((None, 1, hd), lambda h, i: (h, 0, 0)),
                  pl.BlockSpec((None, CHUNK, CHUNK), lambda h, i: (h, 0, 0)),
                  pl.BlockSpec((None, CHUNK, hd), lambda h, i: (h, 0, 0))],
        out_specs=out_specs,
        out_shape=out_shape,
        scratch_shapes=[pltpu.VMEM((d, 3 * hd), BF16)],
        compiler_params=_params("parallel", "arbitrary"),
    )(xn, w_in, w_in, w_in, gv, wmix, bmix)
    return res if want_v else (res[0], None)


def _transpose_cast_kernel(w_ref, o_ref):
    o_ref[...] = w_ref[...].T.astype(o_ref.dtype)


def _transpose_cast(w, row_block, col_block0, n_rows, n_cols):
    cblk = 512
    return pl.pallas_call(
        _transpose_cast_kernel,
        grid=(n_cols // cblk,),
        in_specs=[pl.BlockSpec((n_rows, cblk), lambda i: (row_block, col_block0 + i))],
        out_specs=pl.BlockSpec((cblk, n_rows), lambda i: (i, 0)),
        out_shape=jax.ShapeDtypeStruct((n_cols, n_rows), BF16),
        compiler_params=_params("parallel"),
    )(w)


def _proj_b_kernel(w_ref, xn_ref, o_ref):
    xn = xn_ref[...]
    for m in range(w_ref.shape[0] // ROW_CHUNK):
        sl = slice(m * ROW_CHUNK, (m + 1) * ROW_CHUNK)
        o_ref[sl, :] = lax.dot_general(
            w_ref[sl, :], xn, (((1,), (1,)), ((), ())),
            preferred_element_type=F32).astype(o_ref.dtype)


def _proj_b(w_bt, xn, nkt):
    tc, nk, d = xn.shape
    rows = w_bt.shape[0]
    return pl.pallas_call(
        _proj_b_kernel,
        grid=(tc, nk // nkt),
        in_specs=[pl.BlockSpec((rows, d), lambda j, k: (0, 0), pipeline_mode=pl.Buffered(1)),
                  pl.BlockSpec((None, nkt, d), lambda j, k: (j, k, 0))],
        out_specs=pl.BlockSpec((None, rows, nkt), lambda j, k: (j, 0, k)),
        out_shape=jax.ShapeDtypeStruct((tc, rows, nk), BF16),
        compiler_params=_params("parallel", "parallel"),
    )(w_bt, xn)


def _ssm_kernel(x_ref, mp_ref, q_ref, ct_ref, dsk_ref, *rest, tc, gb, nk, has_h0):
    if has_h0:
        h0_ref, y_ref, hf_ref = rest
    else:
        y_ref, hf_ref = rest
    rows = tc * GROUP
    half = hf_ref.shape[1] // 2
    lane = lax.broadcasted_iota(jnp.int32, (half, LANES), 1)
    b_tc = tc.bit_length() - 1

    def body(g, carry):
        r0 = pl.multiple_of(g * GROUP, GROUP)
        xg = x_ref[:, pl.ds(r0, GROUP), :]
        res = jnp.dot(mp_ref[g], xg.reshape(rows, nk), preferred_element_type=F32)
        y_local = res[:rows]
        s_re = res[rows:rows + half]
        s_im = res[rows + half:]
        ct = ct_ref[g]
        if has_h0:
            h0 = h0_ref[g]
            h_re, h_im = h0[:half], h0[half:]
            a_re, a_im = ct[:, b_tc:b_tc + 1], ct[:, POW_COLS + b_tc:POW_COLS + b_tc + 1]
            hf_ref[g] = jnp.concatenate(
                [a_re * h_re - a_im * h_im + s_re, a_re * h_im + a_im * h_re + s_im], axis=0)
            h_in = h0
        else:
            in_re, in_im = [], []
            fin = jnp.zeros((2 * half, LANES), F32)
            for b in range(nk // LANES):
                xr = s_re[:, b * LANES:(b + 1) * LANES]
                xi = s_im[:, b * LANES:(b + 1) * LANES]
                for s in range(SCAN_STEPS):
                    sh = 1 << s
                    cr = jnp.where(lane >= sh, ct[:, b_tc + s:b_tc + s + 1], 0.0)
                    ci = jnp.where(lane >= sh, ct[:, POW_COLS + b_tc + s:POW_COLS + b_tc + s + 1], 0.0)
                    rr = pltpu.roll(xr, sh, 1)
                    ri = pltpu.roll(xi, sh, 1)
                    xr, xi = xr + cr * rr - ci * ri, xi + cr * ri + ci * rr
                last = jnp.concatenate([xr[:, LANES - 1:], xi[:, LANES - 1:]], axis=0)
                lane2 = lax.broadcasted_iota(jnp.int32, (2 * half, LANES), 1)
                fin = jnp.where(lane2 == b, last, fin)
                in_re.append(jnp.where(lane >= 1, pltpu.roll(xr, 1, 1), 0.0))
                in_im.append(jnp.where(lane >= 1, pltpu.roll(xi, 1, 1), 0.0))
            hf_ref[g] = fin
            h_in = jnp.concatenate(
                [jnp.concatenate(in_re, axis=1), jnp.concatenate(in_im, axis=1)], axis=0)
        y = y_local + jnp.dot(q_ref[g], h_in.astype(BF16), preferred_element_type=F32)
        y3 = y.reshape(tc, GROUP, nk) + dsk_ref[pl.ds(r0, GROUP), :][None] * xg.astype(F32)
        y_ref[:, pl.ds(r0, GROUP), :] = jax.nn.gelu(y3).astype(y_ref.dtype)
        return carry

    lax.fori_loop(0, gb, body, 0)


def _ssm(zb, mp, q, ct, dsk, h0, gb):
    tc, _, nk = zb.shape
    ng, mrows, rows = mp.shape
    p2 = mrows - rows
    cb = ng * GROUP
    has_h0 = h0 is not None
    in_specs = [pl.BlockSpec((tc, gb * GROUP, nk), lambda i: (0, i, 0)),
                pl.BlockSpec((gb, mrows, rows), lambda i: (i, 0, 0)),
                pl.BlockSpec((gb, rows, p2), lambda i: (i, 0, 0)),
                pl.BlockSpec((gb, p2 // 2, 2 * POW_COLS), lambda i: (i, 0, 0)),
                pl.BlockSpec((gb * GROUP, nk), lambda i: (i, 0))]
    args = [zb, mp, q, ct, dsk]
    if has_h0:
        in_specs.append(pl.BlockSpec((gb, p2, nk), lambda i: (i, 0, 0)))
        args.append(h0)
    hf_lanes = nk if has_h0 else LANES
    return pl.pallas_call(
        functools.partial(_ssm_kernel, tc=tc, gb=gb, nk=nk, has_h0=has_h0),
        grid=(ng // gb,),
        in_specs=in_specs,
        out_specs=[pl.BlockSpec((tc, gb * GROUP, nk), lambda i: (0, i, 0)),
                   pl.BlockSpec((gb, p2, hf_lanes), lambda i: (i, 0, 0))],
        out_shape=[jax.ShapeDtypeStruct((tc, cb, nk), BF16),
                   jax.ShapeDtypeStruct((ng, p2, hf_lanes), F32)],
        compiler_params=_params("parallel"),
    )(*args)


def _glu_out_kernel(y_ref, gate_ref, wg_ref, bg_ref, wo_ref, r_ref, ob_ref):
    y = y_ref[...]
    for m in range(wg_ref.shape[0] // ROW_CHUNK):
        sl = slice(m * ROW_CHUNK, (m + 1) * ROW_CHUNK)
        g = jnp.dot(wg_ref[sl, :], y, preferred_element_type=F32) + bg_ref[sl, :]
        ob = y_ref[sl, :].astype(F32) * jax.nn.sigmoid(g) * jax.nn.silu(gate_ref[sl, :].astype(F32))
        ob_ref[sl, :] = ob.astype(ob_ref.dtype)
    ob = ob_ref[...]
    for m in range(wo_ref.shape[0] // ROW_CHUNK):
        sl = slice(m * ROW_CHUNK, (m + 1) * ROW_CHUNK)
        r_ref[:, sl] = jnp.dot(wo_ref[sl, :], ob, preferred_element_type=F32).T


def _glu_out(y_t, zb, wg_t, bg, wo_t, nkt):
    tc, cb, nk = y_t.shape
    d = wo_t.shape[0]
    return pl.pallas_call(
        _glu_out_kernel,
        grid=(tc, nk // nkt),
        in_specs=[pl.BlockSpec((None, cb, nkt), lambda j, k: (j, 0, k)),
                  pl.BlockSpec((None, cb, nkt), lambda j, k: (j, 1, k)),
                  pl.BlockSpec((cb, cb), lambda j, k: (0, 0), pipeline_mode=pl.Buffered(1)),
                  pl.BlockSpec((cb, nkt), lambda j, k: (0, 0), pipeline_mode=pl.Buffered(1)),
                  pl.BlockSpec((d, cb), lambda j, k: (0, 0), pipeline_mode=pl.Buffered(1))],
        out_specs=pl.BlockSpec((None, nkt, d), lambda j, k: (j, k, 0)),
        out_shape=jax.ShapeDtypeStruct((tc, nk, d), F32),
        scratch_shapes=[pltpu.VMEM((cb, nkt), BF16)],
        compiler_params=_params("parallel", "parallel"),
    )(y_t, zb, wg_t, bg, wo_t)


def _out_norm_kernel(x_ref, oa_ref, r_ref, wo_ref, g_ref, y_ref, ms_ref, *, tc, n):
    ca = oa_ref.shape[-1]
    d = x_ref.shape[-1]
    nc = d // LANES
    p = jnp.dot(oa_ref[...].reshape(tc * n, ca), wo_ref[...], preferred_element_type=F32)
    for j in range(tc):
        s = p[j * n:(j + 1) * n] + r_ref[j]
        for c in range(nc):
            ms_ref[c, pl.ds(j, n, stride=tc), :] = s[:, c * LANES:(c + 1) * LANES]
    acc = x_ref[...] + jnp.concatenate([ms_ref[c] for c in range(nc)], axis=1)
    ms = jnp.mean(acc * acc, axis=-1, keepdims=True)
    y_ref[...] = acc * lax.rsqrt(ms + EPS) * g_ref[...]


def _out_norm(x2d, oa, r, wo_a, gf, n):
    tc, nk, ca = oa.shape
    t, d = x2d.shape
    return pl.pallas_call(
        functools.partial(_out_norm_kernel, tc=tc, n=n),
        grid=(nk // n,),
        in_specs=[pl.BlockSpec((n * tc, d), lambda i: (i, 0)),
                  pl.BlockSpec((tc, n, ca), lambda i: (0, i, 0)),
                  pl.BlockSpec((tc, n, d), lambda i: (0, i, 0)),
                  pl.BlockSpec((ca, d), lambda i: (0, 0), pipeline_mode=pl.Buffered(1)),
                  pl.BlockSpec((1, d), lambda i: (0, 0))],
        out_specs=pl.BlockSpec((n * tc, d), lambda i: (i, 0)),
        out_shape=jax.ShapeDtypeStruct((t, d), F32),
        scratch_shapes=[pltpu.VMEM((d // LANES, n * tc, LANES), F32)],
        compiler_params=_params("parallel"),
    )(x2d, oa, r, wo_a, gf.reshape(1, d))


def _discretise(a_re, a_im, log_dt, b_re, b_im):
    dt = jnp.exp(log_dt)[:, None]
    mag = jnp.exp(a_re * dt)
    ang = a_im * dt
    lr, li = mag * jnp.cos(ang), mag * jnp.sin(ang)
    den = a_re * a_re + a_im * a_im
    nr, ni = lr - 1.0, li
    qr, qi = (nr * a_re + ni * a_im) / den, (ni * a_re - nr * a_im) / den
    bbr = qr[:, :, None] * b_re - qi[:, :, None] * b_im
    bbi = qr[:, :, None] * b_im + qi[:, :, None] * b_re
    prs, pis = [lr], [li]
    for _ in range(N_POW - 1):
        prs, pis = prs + [prs[-1] * prs[-1] - pis[-1] * pis[-1]], pis + [2.0 * prs[-1] * pis[-1]]
    pad = [jnp.zeros_like(lr)] * (POW_COLS - N_POW)
    return jnp.stack(prs + pad + pis + pad, axis=-1), bbr, bbi


def _ssm_tables_kernel(lt_ref, lrow_ref, bbr_ref, bbi_ref, c2r_ref, c2i_ref,
                       mp_ref, q_ref, mps_ref, qs_ref, *, gb, tc, tcs):
    rows = tc * GROUP
    half = lt_ref.shape[1]
    rows_s = tcs * GROUP
    hi = lax.Precision.HIGHEST
    expo = (tc - 1) - (lax.broadcasted_iota(jnp.int32, (half, rows), 1) // GROUP)
    lane_k = lax.broadcasted_iota(jnp.int32, (GROUP, rows), 1)
    lane_q = lax.broadcasted_iota(jnp.int32, (1, 2 * half), 1)

    def body(g, carry):
        lt = lt_ref[g]
        pr = jnp.ones((half, rows), F32)
        pi = jnp.zeros((half, rows), F32)
        for b in range(tc.bit_length() - 1):
            bit = ((expo >> b) & 1) == 1
            fr = jnp.where(bit, lt[:, b:b + 1], 1.0)
            fi = jnp.where(bit, lt[:, POW_COLS + b:POW_COLS + b + 1], 0.0)
            pr, pi = pr * fr - pi * fi, pr * fi + pi * fr
        bbr, bbi = bbr_ref[g], bbi_ref[g]
        er = pr * bbr - pi * bbi
        ei = pr * bbi + pi * bbr
        mp_ref[g, rows:rows + half, :] = er.astype(mp_ref.dtype)
        mp_ref[g, rows + half:, :] = ei.astype(mp_ref.dtype)
        mps_ref[g, rows_s:rows_s + half, :] = er[:, rows - rows_s:].astype(mps_ref.dtype)
        mps_ref[g, rows_s + half:, :] = ei[:, rows - rows_s:].astype(mps_ref.dtype)
        c2r, c2i = c2r_ref[g], c2i_ref[g]
        krow = (jnp.dot(c2r[:, :half], er, precision=hi, preferred_element_type=F32)
                - jnp.dot(c2i[:, :half], ei, precision=hi, preferred_element_type=F32))
        for i in range(tc):
            sh = (tc - 1 - i) * GROUP
            blk = krow if sh == 0 else pltpu.roll(krow, rows - sh, 1)
            blk = jnp.where(lane_k < (i + 1) * GROUP, blk, 0.0)
            mp_ref[g, i * GROUP:(i + 1) * GROUP, :] = blk.astype(mp_ref.dtype)
            if i < tcs:
                mps_ref[g, i * GROUP:(i + 1) * GROUP, :] = blk[:, :rows_s].astype(mps_ref.dtype)
        lrow = lrow_ref[g]
        l_r, l_i = lrow[0:1], lrow[1:2]
        cur_r, cur_i = l_r, l_i
        for i in range(tc):
            u = jnp.where(lane_q < half, cur_r, -cur_i)
            v = jnp.where(lane_q < half, -cur_i, -cur_r)
            blk = (c2r * u + c2i * v).astype(q_ref.dtype)
            q_ref[g, i * GROUP:(i + 1) * GROUP, :] = blk
            if i < tcs:
                qs_ref[g, i * GROUP:(i + 1) * GROUP, :] = blk
            cur_r, cur_i = cur_r * l_r - cur_i * l_i, cur_r * l_i + cur_i * l_r
        return carry

    lax.fori_loop(0, gb, body, 0)


def _ssm_tables(lt, bbr, bbi, c_re, c_im, tcs, gb):
    ng, npst, _ = lt.shape
    tc = TC_PROMPT
    rows, rows_s = tc * GROUP, tcs * GROUP
    lrow = jnp.stack([jnp.tile(lt[:, :, 0], (1, 2)), jnp.tile(lt[:, :, POW_COLS], (1, 2))], axis=1)
    args = (lt, lrow, jnp.tile(bbr, (1, 1, tc)), jnp.tile(bbi, (1, 1, tc)),
            jnp.tile(c_re, (1, 1, 2)), jnp.tile(c_im, (1, 1, 2)))
    shapes = [(ng, rows + 2 * npst, rows), (ng, rows, 2 * npst),
              (ng, rows_s + 2 * npst, rows_s), (ng, rows_s, 2 * npst)]
    return pl.pallas_call(
        functools.partial(_ssm_tables_kernel, gb=gb, tc=tc, tcs=tcs),
        grid=(ng // gb,),
        in_specs=[pl.BlockSpec((gb,) + a.shape[1:], lambda i: (i, 0, 0)) for a in args],
        out_specs=[pl.BlockSpec((gb,) + s[1:], lambda i: (i, 0, 0)) for s in shapes],
        out_shape=[jax.ShapeDtypeStruct(s, BF16) for s in shapes],
        compiler_params=_params("parallel"),
    )(*args)


def _mix_tables(w_s, b_s, hd, tc):
    nh = w_s.shape[0]
    w = w_s * jnp.tril(jnp.ones((CHUNK, CHUNK), F32))[None]
    r = CHUNK // tc
    if tc == TC_PROMPT:
        wm = w.reshape(nh, r, tc, r, tc).transpose(0, 2, 1, 4, 3).reshape(nh, CHUNK, CHUNK)
        bm = b_s.reshape(nh, r, tc).transpose(0, 2, 1).reshape(nh, CHUNK)
    else:
        eye = jnp.eye(r, dtype=F32)
        wm = jnp.einsum('hts,ab->htasb', w[:, :tc, :tc], eye).reshape(nh, CHUNK, CHUNK)
        bm = jnp.repeat(b_s[:, :tc], r, axis=1)
    return wm.astype(BF16), jnp.broadcast_to(bm[:, :, None], (nh, CHUNK, hd)).astype(F32)


def _layer(x2d, tc, h0, wts, tiles):
    (g_norm, w_a, gv, wmix, bmix, w_bt, mp, q, ct, d_skip, wg_t, b_glu, wo_a, wo_bt, g_final) = wts
    n_norm, n_a, nkt, gb = tiles
    cb = wg_t.shape[0]
    nk = x2d.shape[0] // tc
    xn = _norm_permute(x2d, g_norm, tc, n_norm)
    out_a, v_rows = _branch_a(xn, w_a, gv, wmix, bmix, n_a, h0 is not None)
    zb = _proj_b(w_bt, xn, nkt)
    dsk = jnp.broadcast_to(d_skip[:, None], (cb, nk))
    y_t, h_fin = _ssm(zb, mp, q, ct, dsk, h0, gb)
    bg = jnp.broadcast_to(b_glu[:, None], (cb, nkt))
    r = _glu_out(y_t, zb, wg_t, bg, wo_bt, nkt)
    y = _out_norm(x2d, out_a, r, wo_a, g_final, n_norm)
    return y, h_fin, v_rows


def kernel(x_prompt, x_sample, state_ssm_re, state_ssm_im, g_norm, w_in, g_v, w_s, b_s,
           a_re, a_im, log_dt, b_re, b_im, c_re, c_im, d_skip, w_glu, b_glu, w_out, g_final):
    depth = g_norm.shape[0]
    assert depth == 1, "single-layer step"
    bsz, seq, d = x_prompt.shape
    dbsz, dseq, _ = x_sample.shape
    nh = w_s.shape[1]
    ng, npst = a_re.shape[1:]
    cb = ng * GROUP
    ca = (w_in.shape[2] - 2 * cb) // 3
    hd = ca // nh
    assert seq // TC_PROMPT == LANES and dbsz == LANES and CHUNK % dseq == 0
    assert dseq & (dseq - 1) == 0 and dseq <= TC_PROMPT, "sample chunk must be a power of two"
    assert 3 * ca % 512 == 0 and ca == cb == d

    l = 0
    w_in_l = w_in[l]
    w_bt = _transpose_cast(w_in_l, 0, 3 * ca // 512, d, 2 * cb)
    gv = g_v[l].reshape(nh, 1, hd)
    lt, bbr, bbi = _discretise(a_re[l], a_im[l], log_dt[l], b_re[l], b_im[l])
    mp, q, mp_s, q_s = _ssm_tables(lt, bbr, bbi, c_re[l], c_im[l], dseq, 8)
    wg_t = _transpose_cast(w_glu[l], 0, 0, cb, cb)
    wo_a = w_out[l][:ca].astype(BF16)
    wo_bt = _transpose_cast(w_out[l], 1, 0, cb, d)

    def weights(tc, mp_, q_):
        wmix, bmix = _mix_tables(w_s[l], b_s[l], hd, tc)
        return (g_norm[l], w_in_l, gv, wmix, bmix, w_bt, mp_, q_, lt,
                d_skip[l], wg_t, b_glu[l], wo_a, wo_bt, g_final)

    tc_p, nk_p = TC_PROMPT, bsz * seq // TC_PROMPT
    y_p, hf_p, _ = _layer(x_prompt.reshape(nk_p * tc_p, d), tc_p, None,
                          weights(tc_p, mp, q), (32, 64, 256, 8))
    hf_p = hf_p[:, :, :bsz]
    re_p = hf_p[:, :npst].transpose(2, 0, 1)[None]
    im_p = hf_p[:, npst:].transpose(2, 0, 1)[None]

    tc_s, nk_s = dseq, dbsz
    h0 = jnp.concatenate([state_ssm_re[l], state_ssm_im[l]], axis=-1).transpose(1, 2, 0)
    y_s, hf_s, v_s = _layer(x_sample.reshape(nk_s * tc_s, d), tc_s, h0,
                            weights(tc_s, mp_s, q_s), (64, 128, 128, 8))
    re_s = hf_s[:, :npst].transpose(2, 0, 1)[None]
    im_s = hf_s[:, npst:].transpose(2, 0, 1)[None]
    v_out = v_s.transpose(1, 0, 2)[None]

    dt_p, dt_s = x_prompt.dtype, x_sample.dtype
    return (y_p.reshape(bsz, seq, d).astype(dt_p), y_s.reshape(dbsz, dseq, d).astype(dt_s),
            re_p.astype(dt_p), im_p.astype(dt_p), re_s.astype(dt_s), im_s.astype(dt_s),
            v_out.astype(dt_s))
```

```python
import functools

import jax
import jax.numpy as jnp
from jax import lax
from jax.experimental import pallas as pl
from jax.experimental.pallas import tpu as pltpu

EPS = 1e-6
LANES = 128
CHUNK = 128
GROUP = 16
TC_PROMPT = 16
SCAN_STEPS = 7
ROW_CHUNK = 512
POW_COLS = 16
N_POW = 11
VMEM_LIMIT = 56 * 1024 * 1024

F32 = jnp.float32
BF16 = jnp.bfloat16


def _params(*sem):
    return pltpu.CompilerParams(dimension_semantics=sem, vmem_limit_bytes=VMEM_LIMIT)


def _norm_permute_kernel(x_ref, g_ref, o_ref, xs_ref, *, tc, n):
    x = x_ref[...]
    ms = jnp.mean(x * x, axis=-1, keepdims=True)
    xn = x * lax.rsqrt(ms + EPS) * g_ref[...]
    nc = x.shape[1] // LANES
    for c in range(nc):
        xs_ref[c] = xn[:, c * LANES:(c + 1) * LANES]
    for j in range(tc):
        rows = jnp.concatenate([xs_ref[c, pl.ds(j, n, stride=tc), :] for c in range(nc)], axis=1)
        o_ref[j] = rows.astype(o_ref.dtype)


def _norm_permute(x2d, g, tc, n):
    t, d = x2d.shape
    nk = t // tc
    return pl.pallas_call(
        functools.partial(_norm_permute_kernel, tc=tc, n=n),
        grid=(nk // n,),
        in_specs=[pl.BlockSpec((n * tc, d), lambda i: (i, 0)),
                  pl.BlockSpec((1, d), lambda i: (0, 0))],
        out_specs=pl.BlockSpec((tc, n, d), lambda i: (0, i, 0)),
        out_shape=jax.ShapeDtypeStruct((tc, nk, d), BF16),
        scratch_shapes=[pltpu.VMEM((d // LANES, n * tc, LANES), F32)],
        compiler_params=_params("parallel"),
    )(x2d, g.reshape(1, d))


def _branch_a_kernel(xn_ref, wu_ref, wv_ref, wg_ref, gv_ref, wmix_ref, bmix_ref, oa_ref, *rest,
                     tc, n, r, hd):
    *v_refs, w_ref = rest
    d = xn_ref.shape[-1]

    @pl.when(pl.program_id(1) == 0)
    def _():
        w_ref[:, :hd] = wu_ref[...].astype(BF16)
        w_ref[:, hd:2 * hd] = wv_ref[...].astype(BF16)
        w_ref[:, 2 * hd:] = wg_ref[...].astype(BF16)

    x2 = xn_ref[...].reshape(tc * n, d)
    z = jnp.dot(x2, w_ref[...], preferred_element_type=F32)
    u = jax.nn.gelu(z[:, :hd])
    v = jax.nn.gelu(z[:, hd:2 * hd])
    ga = z[:, 2 * hd:]
    ms = jnp.mean(v * v, axis=-1, keepdims=True)
    vn3 = (v * lax.rsqrt(ms + EPS) * gv_ref[...]).reshape(tc, n, hd)
    pre3 = (u * jax.nn.silu(ga)).reshape(tc, n, hd)
    if v_refs:
        v_refs[0][...] = vn3
    wm = wmix_ref[...]
    bm = bmix_ref[...]
    for c in range(n // r):
        vg = vn3[:, c * r:(c + 1) * r, :].reshape(CHUNK, hd)
        mixed = jnp.dot(wm, vg.astype(BF16), preferred_element_type=F32) + bm
        og = pre3[:, c * r:(c + 1) * r, :].reshape(CHUNK, hd) * mixed
        oa_ref[:, c * r:(c + 1) * r, :] = og.reshape(tc, r, hd).astype(oa_ref.dtype)


def _branch_a(xn, w_in, gv, wmix, bmix, n, want_v):
    tc, nk, d = xn.shape
    nh, _, hd = gv.shape
    r = CHUNK // tc
    out_shape = [jax.ShapeDtypeStruct((tc, nk, nh * hd), BF16)]
    out_specs = [pl.BlockSpec((tc, n, hd), lambda h, i: (0, i, h))]
    if want_v:
        out_shape.append(jax.ShapeDtypeStruct((tc, nk, nh * hd), F32))
        out_specs.append(pl.BlockSpec((tc, n, hd), lambda h, i: (0, i, h)))
    res = pl.pallas_call(
        functools.partial(_branch_a_kernel, tc=tc, n=n, r=r, hd=hd),
        grid=(nh, nk // n),
        in_specs=[pl.BlockSpec((tc, n, d), lambda h, i: (0, i, 0)),
                  pl.BlockSpec((d, hd), lambda h, i: (0, h)),
                  pl.BlockSpec((d, hd), lambda h, i: (0, nh + h)),
                  pl.BlockSpec((d, hd), lambda h, i: (0, 2 * nh + h)),
                  pl.BlockSpec((None, 1, hd), lambda h, i: (h, 0, 0)),
                  pl.BlockSpec((None, CHUNK, CHUNK), lambda h, i: (h, 0, 0)),
                  pl.BlockSpec((None, CHUNK, hd), lambda h, i: (h, 0, 0))],
        out_specs=out_specs,
        out_shape=out_shape,
        scratch_shapes=[pltpu.VMEM((d, 3 * hd), BF16)],
        compiler_params=_params("parallel", "arbitrary"),
    )(xn, w_in, w_in, w_in, gv, wmix, bmix)
    return res if want_v else (res[0], None)


def _transpose_cast_kernel(w_ref, o_ref):
    o_ref[...] = w_ref[...].T.astype(o_ref.dtype)


def _transpose_cast(w, row_block, col_block0, n_rows, n_cols):
    cblk = 512
    return pl.pallas_call(
        _transpose_cast_kernel,
        grid=(n_cols // cblk,),
        in_specs=[pl.BlockSpec((n_rows, cblk), lambda i: (row_block, col_block0 + i))],
        out_specs=pl.BlockSpec((cblk, n_rows), lambda i: (i, 0)),
        out_shape=jax.ShapeDtypeStruct((n_cols, n_rows), BF16),
        compiler_params=_params("parallel"),
    )(w)


def _proj_b_kernel(w_ref, xn_ref, o_ref):
    xn = xn_ref[...]
    for m in range(w_ref.shape[0] // ROW_CHUNK):
        sl = slice(m * ROW_CHUNK, (m + 1) * ROW_CHUNK)
        o_ref[sl, :] = lax.dot_general(
            w_ref[sl, :], xn, (((1,), (1,)), ((), ())),
            preferred_element_type=F32).astype(o_ref.dtype)


def _proj_b(w_bt, xn, nkt):
    tc, nk, d = xn.shape
    rows = w_bt.shape[0]
    return pl.pallas_call(
        _proj_b_kernel,
        grid=(tc, nk // nkt),
        in_specs=[pl.BlockSpec((rows, d), lambda j, k: (0, 0), pipeline_mode=pl.Buffered(1)),
                  pl.BlockSpec((None, nkt, d), lambda j, k: (j, k, 0))],
        out_specs=pl.BlockSpec((None, rows, nkt), lambda j, k: (j, 0, k)),
        out_shape=jax.ShapeDtypeStruct((tc, rows, nk), BF16),
        compiler_params=_params("parallel", "parallel"),
    )(w_bt, xn)


def _ssm_kernel(x_ref, mp_ref, q_ref, ct_ref, dsk_ref, *rest, tc, gb, nk, fold, has_h0):
    if has_h0:
        h0_ref, y_ref, hf_ref = rest
    else:
        y_ref, hf_ref = rest
    rows = tc * GROUP
    half = hf_ref.shape[1] // 2
    lane = lax.broadcasted_iota(jnp.int32, (half, LANES), 1)
    b_tc = tc.bit_length() - 1

    def body(g, carry):
        r0 = pl.multiple_of(g * GROUP, GROUP)
        if fold == 1:
            xg = x_ref[:, pl.ds(r0, GROUP), :]
        else:
            xg = jnp.stack([x_ref[j // fold, pl.ds(r0, GROUP), (j % fold) * nk:(j % fold + 1) * nk]
                            for j in range(tc)])
        res = jnp.dot(mp_ref[g], xg.reshape(rows, nk), preferred_element_type=F32)
        y_local = res[:rows]
        s_re = res[rows:rows + half]
        s_im = res[rows + half:]
        ct = ct_ref[g]
        if has_h0:
            h0 = h0_ref[g]
            h_re, h_im = h0[:half], h0[half:]
            a_re, a_im = ct[:, b_tc:b_tc + 1], ct[:, POW_COLS + b_tc:POW_COLS + b_tc + 1]
            hf_ref[g] = jnp.concatenate(
                [a_re * h_re - a_im * h_im + s_re, a_re * h_im + a_im * h_re + s_im], axis=0)
            h_in = h0
        else:
            in_re, in_im = [], []
            fin = jnp.zeros((2 * half, LANES), F32)
            for b in range(nk // LANES):
                xr = s_re[:, b * LANES:(b + 1) * LANES]
                xi = s_im[:, b * LANES:(b + 1) * LANES]
                for s in range(SCAN_STEPS):
                    sh = 1 << s
                    cr = jnp.where(lane >= sh, ct[:, b_tc + s:b_tc + s + 1], 0.0)
                    ci = jnp.where(lane >= sh, ct[:, POW_COLS + b_tc + s:POW_COLS + b_tc + s + 1], 0.0)
                    rr = pltpu.roll(xr, sh, 1)
                    ri = pltpu.roll(xi, sh, 1)
                    xr, xi = xr + cr * rr - ci * ri, xi + cr * ri + ci * rr
                last = jnp.concatenate([xr[:, LANES - 1:], xi[:, LANES - 1:]], axis=0)
                lane2 = lax.broadcasted_iota(jnp.int32, (2 * half, LANES), 1)
                fin = jnp.where(lane2 == b, last, fin)
                in_re.append(jnp.where(lane >= 1, pltpu.roll(xr, 1, 1), 0.0))
                in_im.append(jnp.where(lane >= 1, pltpu.roll(xi, 1, 1), 0.0))
            hf_ref[g] = fin
            h_in = jnp.concatenate(
                [jnp.concatenate(in_re, axis=1), jnp.concatenate(in_im, axis=1)], axis=0)
        y = y_local + jnp.dot(q_ref[g], h_in.astype(BF16), preferred_element_type=F32)
        y3 = y.reshape(tc, GROUP, nk) + dsk_ref[pl.ds(r0, GROUP), :][None] * xg.astype(F32)
        ya = jax.nn.gelu(y3).astype(y_ref.dtype)
        if fold == 1:
            y_ref[:, pl.ds(r0, GROUP), :] = ya
        else:
            for j in range(tc):
                y_ref[j // fold, pl.ds(r0, GROUP), (j % fold) * nk:(j % fold + 1) * nk] = ya[j]
        return carry

    lax.fori_loop(0, gb, body, 0)


def _ssm(zb, mp, q, ct, dsk, h0, gb, fold):
    tcf, _, nkf = zb.shape
    tc, nk = tcf * fold, nkf // fold
    ng, mrows, rows = mp.shape
    p2 = mrows - rows
    cb = ng * GROUP
    has_h0 = h0 is not None
    in_specs = [pl.BlockSpec((tcf, gb * GROUP, nkf), lambda i: (0, i, 0)),
                pl.BlockSpec((gb, mrows, rows), lambda i: (i, 0, 0)),
                pl.BlockSpec((gb, rows, p2), lambda i: (i, 0, 0)),
                pl.BlockSpec((gb, p2 // 2, 2 * POW_COLS), lambda i: (i, 0, 0)),
                pl.BlockSpec((gb * GROUP, nk), lambda i: (i, 0))]
    args = [zb, mp, q, ct, dsk]
    if has_h0:
        in_specs.append(pl.BlockSpec((gb, p2, nk), lambda i: (i, 0, 0)))
        args.append(h0)
    hf_lanes = nk if has_h0 else LANES
    return pl.pallas_call(
        functools.partial(_ssm_kernel, tc=tc, gb=gb, nk=nk, fold=fold, has_h0=has_h0),
        grid=(ng // gb,),
        in_specs=in_specs,
        out_specs=[pl.BlockSpec((tcf, gb * GROUP, nkf), lambda i: (0, i, 0)),
                   pl.BlockSpec((gb, p2, hf_lanes), lambda i: (i, 0, 0))],
        out_shape=[jax.ShapeDtypeStruct((tcf, cb, nkf), BF16),
                   jax.ShapeDtypeStruct((ng, p2, hf_lanes), F32)],
        compiler_params=_params("parallel"),
    )(*args)


def _glu_out_kernel(y_ref, gate_ref, wg_ref, bg_ref, wo_ref, r_ref, ob_ref):
    y = y_ref[...]
    for m in range(wg_ref.shape[0] // ROW_CHUNK):
        sl = slice(m * ROW_CHUNK, (m + 1) * ROW_CHUNK)
        g = jnp.dot(wg_ref[sl, :], y, preferred_element_type=F32) + bg_ref[sl, :]
        ob = y_ref[sl, :].astype(F32) * jax.nn.sigmoid(g) * jax.nn.silu(gate_ref[sl, :].astype(F32))
        ob_ref[sl, :] = ob.astype(ob_ref.dtype)
    ob = ob_ref[...]
    for m in range(wo_ref.shape[0] // ROW_CHUNK):
        sl = slice(m * ROW_CHUNK, (m + 1) * ROW_CHUNK)
        r_ref[:, sl] = jnp.dot(wo_ref[sl, :], ob, preferred_element_type=F32).T


def _glu_out(y_t, zb, wg_t, bg, wo_t, nkt):
    tc, cb, nk = y_t.shape
    d = wo_t.shape[0]
    return pl.pallas_call(
        _glu_out_kernel,
        grid=(tc, nk // nkt),
        in_specs=[pl.BlockSpec((None, cb, nkt), lambda j, k: (j, 0, k)),
                  pl.BlockSpec((None, cb, nkt), lambda j, k: (j, 1, k)),
                  pl.BlockSpec((cb, cb), lambda j, k: (0, 0), pipeline_mode=pl.Buffered(1)),
                  pl.BlockSpec((cb, nkt), lambda j, k: (0, 0), pipeline_mode=pl.Buffered(1)),
                  pl.BlockSpec((d, cb), lambda j, k: (0, 0), pipeline_mode=pl.Buffered(1))],
        out_specs=pl.BlockSpec((None, nkt, d), lambda j, k: (j, k, 0)),
        out_shape=jax.ShapeDtypeStruct((tc, nk, d), F32),
        scratch_shapes=[pltpu.VMEM((cb, nkt), BF16)],
        compiler_params=_params("parallel", "parallel"),
    )(y_t, zb, wg_t, bg, wo_t)


def _out_norm_kernel(x_ref, oa_ref, r_ref, wo_ref, g_ref, y_ref, ms_ref, *, tc, n):
    ca = oa_ref.shape[-1]
    d = x_ref.shape[-1]
    nc = d // LANES
    p = jnp.dot(oa_ref[...].reshape(tc * n, ca), wo_ref[...], preferred_element_type=F32)
    for j in range(tc):
        s = p[j * n:(j + 1) * n] + r_ref[j]
        for c in range(nc):
            ms_ref[c, pl.ds(j, n, stride=tc), :] = s[:, c * LANES:(c + 1) * LANES]
    acc = x_ref[...] + jnp.concatenate([ms_ref[c] for c in range(nc)], axis=1)
    ms = jnp.mean(acc * acc, axis=-1, keepdims=True)
    y_ref[...] = acc * lax.rsqrt(ms + EPS) * g_ref[...]


def _out_norm(x2d, oa, r, wo_a, gf, n):
    tc, nk, ca = oa.shape
    t, d = x2d.shape
    return pl.pallas_call(
        functools.partial(_out_norm_kernel, tc=tc, n=n),
        grid=(nk // n,),
        in_specs=[pl.BlockSpec((n * tc, d), lambda i: (i, 0)),
                  pl.BlockSpec((tc, n, ca), lambda i: (0, i, 0)),
                  pl.BlockSpec((tc, n, d), lambda i: (0, i, 0)),
                  pl.BlockSpec((ca, d), lambda i: (0, 0), pipeline_mode=pl.Buffered(1)),
                  pl.BlockSpec((1, d), lambda i: (0, 0))],
        out_specs=pl.BlockSpec((n * tc, d), lambda i: (i, 0)),
        out_shape=jax.ShapeDtypeStruct((t, d), F32),
        scratch_shapes=[pltpu.VMEM((d // LANES, n * tc, LANES), F32)],
        compiler_params=_params("parallel"),
    )(x2d, oa, r, wo_a, gf.reshape(1, d))


def _discretise(a_re, a_im, log_dt):
    dt = jnp.exp(log_dt)[:, None]
    mag = jnp.exp(a_re * dt)
    ang = a_im * dt
    lr, li = mag * jnp.cos(ang), mag * jnp.sin(ang)
    den = a_re * a_re + a_im * a_im
    nr, ni = lr - 1.0, li
    qr, qi = (nr * a_re + ni * a_im) / den, (ni * a_re - nr * a_im) / den
    cat = lambda a, b: jnp.concatenate([a, b], axis=-1)
    rowtab = jnp.stack([cat(lr, lr), cat(li, li), cat(qr, qi), cat(-qi, qr)], axis=1)
    prs, pis = [lr], [li]
    for _ in range(N_POW - 1):
        prs, pis = prs + [prs[-1] * prs[-1] - pis[-1] * pis[-1]], pis + [2.0 * prs[-1] * pis[-1]]
    pad = [jnp.zeros_like(lr)] * (POW_COLS - N_POW)
    return rowtab, jnp.stack(prs + pad + pis + pad, axis=1).transpose(0, 2, 1)


def _ssm_tables_kernel(rt_ref, btr_ref, bti_ref, c2r_ref, c2i_ref, c2n_ref,
                       mp_ref, q_ref, mps_ref, qs_ref, *, gb, tc, tcs, unroll):
    rows = tc * GROUP
    rows_s = tcs * GROUP
    half = rt_ref.shape[2] // 2
    lane_k = lax.broadcasted_iota(jnp.int32, (GROUP, rows), 1)
    lane_q = lax.broadcasted_iota(jnp.int32, (1, 2 * half), 1)

    def one_group(g):
        rt = rt_ref[g]
        l_r, l_i, qu, qv = rt[0:1], rt[1:2], rt[2:3], rt[3:4]
        btr, bti = btr_ref[g], bti_ref[g]
        bb = btr * qu + bti * qv
        bs = btr * qv - bti * qu
        c2r, c2i = c2r_ref[g], c2i_ref[g]
        cur_r = jnp.ones((1, 2 * half), F32)
        cur_i = jnp.zeros((1, 2 * half), F32)
        et_blocks = []
        for n in range(tc):
            et_blocks.append(cur_r * bb + cur_i * bs)
            cur_r, cur_i = cur_r * l_r - cur_i * l_i, cur_r * l_i + cur_i * l_r
            u = jnp.where(lane_q < half, cur_r, -cur_i)
            v = jnp.where(lane_q < half, -cur_i, -cur_r)
            blk = (c2r * u + c2i * v).astype(q_ref.dtype)
            q_ref[g, n * GROUP:(n + 1) * GROUP, :] = blk
            if n < tcs:
                qs_ref[g, n * GROUP:(n + 1) * GROUP, :] = blk
        p = jnp.concatenate(et_blocks[::-1], axis=0).T
        mp_ref[g, rows:, :] = p.astype(mp_ref.dtype)
        mps_ref[g, rows_s:, :] = p[:, rows - rows_s:].astype(mps_ref.dtype)
        krow = jnp.dot(c2n_ref[g], p, precision=lax.Precision.HIGHEST, preferred_element_type=F32)
        for i in range(tc):
            sh = (tc - 1 - i) * GROUP
            blk = krow if sh == 0 else pltpu.roll(krow, rows - sh, 1)
            blk = jnp.where(lane_k < (i + 1) * GROUP, blk, 0.0)
            mp_ref[g, i * GROUP:(i + 1) * GROUP, :] = blk.astype(mp_ref.dtype)
            if i < tcs:
                mps_ref[g, i * GROUP:(i + 1) * GROUP, :] = blk[:, :rows_s].astype(mps_ref.dtype)

    def body(gi, carry):
        for k in range(unroll):
            one_group(gi * unroll + k)
        return carry

    lax.fori_loop(0, gb // unroll, body, 0)


def _ssm_tables(rowtab, b_re, b_im, c_re, c_im, tcs, gb):
    ng, _, p2 = rowtab.shape
    tc = TC_PROMPT
    rows, rows_s = tc * GROUP, tcs * GROUP
    dup = lambda a: jnp.concatenate([a, a], axis=-1)
    args = (rowtab, dup(b_re.transpose(0, 2, 1)), dup(b_im.transpose(0, 2, 1)),
            dup(c_re), dup(c_im), jnp.concatenate([c_re, -c_im], axis=-1))
    shapes = [(ng, rows + p2, rows), (ng, rows, p2), (ng, rows_s + p2, rows_s), (ng, rows_s, p2)]
    return pl.pallas_call(
        functools.partial(_ssm_tables_kernel, gb=gb, tc=tc, tcs=tcs, unroll=2),
        grid=(ng // gb,),
        in_specs=[pl.BlockSpec((gb,) + a.shape[1:], lambda i: (i, 0, 0)) for a in args],
        out_specs=[pl.BlockSpec((gb,) + s[1:], lambda i: (i, 0, 0)) for s in shapes],
        out_shape=[jax.ShapeDtypeStruct(s, BF16) for s in shapes],
        compiler_params=_params("parallel"),
    )(*args)


def _mix_tables(w_s, b_s, hd, tc):
    nh = w_s.shape[0]
    w = w_s * jnp.tril(jnp.ones((CHUNK, CHUNK), F32))[None]
    r = CHUNK // tc
    if tc == TC_PROMPT:
        wm = w.reshape(nh, r, tc, r, tc).transpose(0, 2, 1, 4, 3).reshape(nh, CHUNK, CHUNK)
        bm = b_s.reshape(nh, r, tc).transpose(0, 2, 1).reshape(nh, CHUNK)
    else:
        eye = jnp.eye(r, dtype=F32)
        wm = jnp.einsum('hts,ab->htasb', w[:, :tc, :tc], eye).reshape(nh, CHUNK, CHUNK)
        bm = jnp.repeat(b_s[:, :tc], r, axis=1)
    return wm.astype(BF16), jnp.broadcast_to(bm[:, :, None], (nh, CHUNK, hd)).astype(F32)


def _layer(x2d, tc, h0, wts, tiles):
    (g_norm, w_a, gv, wmix, bmix, w_bt, mp, q, ct, d_skip, wg_t, b_glu, wo_a, wo_bt, g_final) = wts
    n_norm, n_a, nkt, gb, fold = tiles
    cb = wg_t.shape[0]
    d = x2d.shape[1]
    nk = x2d.shape[0] // tc
    xn = _norm_permute(x2d, g_norm, tc, n_norm)
    out_a, v_rows = _branch_a(xn, w_a, gv, wmix, bmix, n_a, h0 is not None)
    zb = _proj_b(w_bt, xn.reshape(tc // fold, fold * nk, d), nkt)
    dsk = jnp.broadcast_to(d_skip[:, None], (cb, nk))
    y_t, h_fin = _ssm(zb, mp, q, ct, dsk, h0, gb, fold)
    bg = jnp.broadcast_to(b_glu[:, None], (cb, nkt))
    r = _glu_out(y_t, zb, wg_t, bg, wo_bt, nkt).reshape(tc, nk, d)
    y = _out_norm(x2d, out_a, r, wo_a, g_final, n_norm)
    return y, h_fin, v_rows


def kernel(x_prompt, x_sample, state_ssm_re, state_ssm_im, g_norm, w_in, g_v, w_s, b_s,
           a_re, a_im, log_dt, b_re, b_im, c_re, c_im, d_skip, w_glu, b_glu, w_out, g_final):
    depth = g_norm.shape[0]
    assert depth == 1, "single-layer step"
    bsz, seq, d = x_prompt.shape
    dbsz, dseq, _ = x_sample.shape
    nh = w_s.shape[1]
    ng, npst = a_re.shape[1:]
    cb = ng * GROUP
    ca = (w_in.shape[2] - 2 * cb) // 3
    hd = ca // nh
    assert seq // TC_PROMPT == LANES and dbsz == LANES and CHUNK % dseq == 0
    assert dseq & (dseq - 1) == 0 and dseq <= TC_PROMPT, "sample chunk must be a power of two"
    assert 3 * ca % 512 == 0 and ca == cb == d

    l = 0
    w_in_l = w_in[l]
    w_bt = _transpose_cast(w_in_l, 0, 3 * ca // 512, d, 2 * cb)
    gv = g_v[l].reshape(nh, 1, hd)
    rowtab, lt = _discretise(a_re[l], a_im[l], log_dt[l])
    mp, q, mp_s, q_s = _ssm_tables(rowtab, b_re[l], b_im[l], c_re[l], c_im[l], dseq, 8)
    wg_t = _transpose_cast(w_glu[l], 0, 0, cb, cb)
    wo_a = w_out[l][:ca].astype(BF16)
    wo_bt = _transpose_cast(w_out[l], 1, 0, cb, d)

    def weights(tc, mp_, q_):
        wmix, bmix = _mix_tables(w_s[l], b_s[l], hd, tc)
        return (g_norm[l], w_in_l, gv, wmix, bmix, w_bt, mp_, q_, lt,
                d_skip[l], wg_t, b_glu[l], wo_a, wo_bt, g_final)

    tc_p, nk_p = TC_PROMPT, bsz * seq // TC_PROMPT
    y_p, hf_p, _ = _layer(x_prompt.reshape(nk_p * tc_p, d), tc_p, None,
                          weights(tc_p, mp, q), (32, 64, 256, 8, 1))
    hf_p = hf_p[:, :, :bsz]
    re_p = hf_p[:, :npst].transpose(2, 0, 1)[None]
    im_p = hf_p[:, npst:].transpose(2, 0, 1)[None]

    tc_s, nk_s = dseq, dbsz
    h0 = jnp.concatenate([state_ssm_re[l], state_ssm_im[l]], axis=-1).transpose(1, 2, 0)
    y_s, hf_s, v_s = _layer(x_sample.reshape(nk_s * tc_s, d), tc_s, h0,
                            weights(tc_s, mp_s, q_s), (64, 128, 256, 8, 2))
    re_s = hf_s[:, :npst].transpose(2, 0, 1)[None]
    im_s = hf_s[:, npst:].transpose(2, 0, 1)[None]
    v_out = v_s.transpose(1, 0, 2)[None]

    dt_p, dt_s = x_prompt.dtype, x_sample.dtype
    return (y_p.reshape(bsz, seq, d).astype(dt_p), y_s.reshape(dbsz, dseq, d).astype(dt_s),
            re_p.astype(dt_p), im_p.astype(dt_p), re_s.astype(dt_s), im_s.astype(dt_s),
            v_out.astype(dt_s))
```

```python
import functools

import jax
import jax.numpy as jnp
from jax import lax
from jax.experimental import pallas as pl
from jax.experimental.pallas import tpu as pltpu

EPS = 1e-6
LANES = 128
CHUNK = 128
GROUP = 16
TC_PROMPT = 16
SCAN_STEPS = 7
ROW_CHUNK = 512
POW_COLS = 16
N_POW = 11
SCAN_ROWS = 8
GROUPS_PER_TRIP = 2
VMEM_LIMIT = 56 * 1024 * 1024

F32 = jnp.float32
BF16 = jnp.bfloat16


def _params(*sem):
    return pltpu.CompilerParams(dimension_semantics=sem, vmem_limit_bytes=VMEM_LIMIT)


def _norm_permute_kernel(x_ref, g_ref, o_ref, xs_ref, *, tc, n):
    x = x_ref[...]
    ms = jnp.mean(x * x, axis=-1, keepdims=True)
    xn = x * lax.rsqrt(ms + EPS) * g_ref[...]
    nc = x.shape[1] // LANES
    for c in range(nc):
        xs_ref[c] = xn[:, c * LANES:(c + 1) * LANES]
    for j in range(tc):
        rows = jnp.concatenate([xs_ref[c, pl.ds(j, n, stride=tc), :] for c in range(nc)], axis=1)
        o_ref[j] = rows.astype(o_ref.dtype)


def _norm_permute(x2d, g, tc, n):
    t, d = x2d.shape
    nk = t // tc
    return pl.pallas_call(
        functools.partial(_norm_permute_kernel, tc=tc, n=n),
        grid=(nk // n,),
        in_specs=[pl.BlockSpec((n * tc, d), lambda i: (i, 0)),
                  pl.BlockSpec((1, d), lambda i: (0, 0))],
        out_specs=pl.BlockSpec((tc, n, d), lambda i: (0, i, 0)),
        out_shape=jax.ShapeDtypeStruct((tc, nk, d), BF16),
        scratch_shapes=[pltpu.VMEM((d // LANES, n * tc, LANES), F32)],
        compiler_params=_params("parallel"),
    )(x2d, g.reshape(1, d))


def _branch_a_kernel(xn_ref, wu_ref, wv_ref, wg_ref, gv_ref, wmix_ref, bmix_ref, oa_ref, *rest,
                     tc, n, r, hd):
    *v_refs, w_ref = rest
    d = xn_ref.shape[-1]

    @pl.when(pl.program_id(1) == 0)
    def _():
        w_ref[:, :hd] = wu_ref[...].astype(BF16)
        w_ref[:, hd:2 * hd] = wv_ref[...].astype(BF16)
        w_ref[:, 2 * hd:] = wg_ref[...].astype(BF16)

    x2 = xn_ref[...].reshape(tc * n, d)
    z = jnp.dot(x2, w_ref[...], preferred_element_type=F32)
    u = jax.nn.gelu(z[:, :hd])
    v = jax.nn.gelu(z[:, hd:2 * hd])
    ga = z[:, 2 * hd:]
    ms = jnp.mean(v * v, axis=-1, keepdims=True)
    vn3 = (v * lax.rsqrt(ms + EPS) * gv_ref[...]).reshape(tc, n, hd)
    pre3 = (u * jax.nn.silu(ga)).reshape(tc, n, hd)
    if v_refs:
        v_refs[0][...] = vn3
    wm = wmix_ref[...]
    bm = bmix_ref[...]
    for c in range(n // r):
        vg = vn3[:, c * r:(c + 1) * r, :].reshape(CHUNK, hd)
        mixed = jnp.dot(wm, vg.astype(BF16), preferred_element_type=F32) + bm
        og = pre3[:, c * r:(c + 1) * r, :].reshape(CHUNK, hd) * mixed
        oa_ref[:, c * r:(c + 1) * r, :] = og.reshape(tc, r, hd).astype(oa_ref.dtype)


def _branch_a(xn, w_in, gv, wmix, bmix, n, want_v):
    tc, nk, d = xn.shape
    nh, _, hd = gv.shape
    r = CHUNK // tc
    out_shape = [jax.ShapeDtypeStruct((tc, nk, nh * hd), BF16)]
    out_specs = [pl.BlockSpec((tc, n, hd), lambda h, i: (0, i, h))]
    if want_v:
        out_shape.append(jax.ShapeDtypeStruct((tc, nk, nh * hd), F32))
        out_specs.append(pl.BlockSpec((tc, n, hd), lambda h, i: (0, i, h)))
    res = pl.pallas_call(
        functools.partial(_branch_a_kernel, tc=tc, n=n, r=r, hd=hd),
        grid=(nh, nk // n),
        in_specs=[pl.BlockSpec((tc, n, d), lambda h, i: (0, i, 0)),
                  pl.BlockSpec((d, hd), lambda h, i: (0, h)),
                  pl.BlockSpec((d, hd), lambda h, i: (0, nh + h)),
                  pl.BlockSpec((d, hd), lambda h, i: (0, 2 * nh + h)),
                  pl.BlockSpec((None, 1, hd), lambda h, i: (h, 0, 0)),
                  pl.BlockSpec((None, CHUNK, CHUNK), lambda h, i: (h, 0, 0)),
                  pl.BlockSpec((None, CHUNK, hd), lambda h, i: (h, 0, 0))],
        out_specs=out_specs,
        out_shape=out_shape,
        scratch_shapes=[pltpu.VMEM((d, 3 * hd), BF16)],
        compiler_params=_params("parallel", "arbitrary"),
    )(xn, w_in, w_in, w_in, gv, wmix, bmix)
    return res if want_v else (res[0], None)


def _transpose_cast_kernel(w_ref, o_ref):
    o_ref[...] = w_ref[...].T.astype(o_ref.dtype)


def _transpose_cast(w, row_block, col_block0, n_rows, n_cols):
    cblk = 512
    return pl.pallas_call(
        _transpose_cast_kernel,
        grid=(n_cols // cblk,),
        in_specs=[pl.BlockSpec((n_rows, cblk), lambda i: (row_block, col_block0 + i))],
        out_specs=pl.BlockSpec((cblk, n_rows), lambda i: (i, 0)),
        out_shape=jax.ShapeDtypeStruct((n_cols, n_rows), BF16),
        compiler_params=_params("parallel"),
    )(w)


def _proj_b_kernel(w_ref, xn_ref, o_ref):
    xn = xn_ref[...]
    for m in range(w_ref.shape[0] // ROW_CHUNK):
        sl = slice(m * ROW_CHUNK, (m + 1) * ROW_CHUNK)
        o_ref[sl, :] = lax.dot_general(
            w_ref[sl, :], xn, (((1,), (1,)), ((), ())),
            preferred_element_type=F32).astype(o_ref.dtype)


def _proj_b(w_bt, xn, nkt):
    tc, nk, d = xn.shape
    rows = w_bt.shape[0]
    return pl.pallas_call(
        _proj_b_kernel,
        grid=(tc, nk // nkt),
        in_specs=[pl.BlockSpec((rows, d), lambda j, k: (0, 0), pipeline_mode=pl.Buffered(1)),
                  pl.BlockSpec((None, nkt, d), lambda j, k: (j, k, 0))],
        out_specs=pl.BlockSpec((None, rows, nkt), lambda j, k: (j, 0, k)),
        out_shape=jax.ShapeDtypeStruct((tc, rows, nk), BF16),
        compiler_params=_params("parallel", "parallel"),
    )(w_bt, xn)


def _shift_rows(x, d):
    if d % 8 == 0:
        return jnp.concatenate([jnp.zeros((d, x.shape[1]), x.dtype), x[:-d]], axis=0)
    row = lax.broadcasted_iota(jnp.int32, x.shape, 0)
    return jnp.where(row >= d, pltpu.roll(x, d, 0), 0.0)


def _ssm_kernel(x_ref, mp_ref, q_ref, ct_ref, dsk_ref, *rest, tc, gb, nk, fold, has_h0):
    if has_h0:
        h0_ref, y_ref, hf_ref = rest
    else:
        y_ref, hf_ref = rest
    rows = tc * GROUP
    half = q_ref.shape[2] // 2
    b_tc = tc.bit_length() - 1

    def one_group(g):
        r0 = pl.multiple_of(g * GROUP, GROUP)
        if fold == 1:
            xg = x_ref[:, pl.ds(r0, GROUP), :]
        else:
            xg = jnp.stack([x_ref[j // fold, pl.ds(r0, GROUP), (j % fold) * nk:(j % fold + 1) * nk]
                            for j in range(tc)])
        res = jnp.dot(mp_ref[g], xg.reshape(rows, nk), preferred_element_type=F32)
        y_local = res[:rows]
        s_re = res[rows:rows + half]
        s_im = res[rows + half:]
        ct = ct_ref[g]
        if has_h0:
            h0 = h0_ref[g]
            h_re, h_im = h0[:half], h0[half:]
            a_re, a_im = ct[:, b_tc:b_tc + 1], ct[:, POW_COLS + b_tc:POW_COLS + b_tc + 1]
            hf_ref[g] = jnp.concatenate(
                [a_re * h_re - a_im * h_im + s_re, a_re * h_im + a_im * h_re + s_im], axis=0)
            h_in = h0
        else:
            tiles = []
            for pair in range(nk // (2 * LANES)):
                lo, mid, hi = 2 * pair * LANES, (2 * pair + 1) * LANES, (2 * pair + 2) * LANES
                xr = jnp.concatenate([s_re[:, lo:mid], s_re[:, mid:hi]], axis=0).T
                xi = jnp.concatenate([s_im[:, lo:mid], s_im[:, mid:hi]], axis=0).T
                for s in range(SCAN_STEPS):
                    cr, ci = ct[s:s + 1], ct[SCAN_ROWS + s:SCAN_ROWS + s + 1]
                    rr, ri = _shift_rows(xr, 1 << s), _shift_rows(xi, 1 << s)
                    xr, xi = xr + cr * rr - ci * ri, xi + cr * ri + ci * rr
                hf_ref[g, 2 * pair:2 * pair + 1, :] = xr[LANES - 1:]
                hf_ref[g, 2 * pair + 1:2 * pair + 2, :] = xi[LANES - 1:]
                hr_t = _shift_rows(xr, 1).T
                hi_t = _shift_rows(xi, 1).T
                tiles.append(jnp.concatenate([hr_t[:half], hi_t[:half]], axis=0))
                tiles.append(jnp.concatenate([hr_t[half:], hi_t[half:]], axis=0))
            h_in = jnp.concatenate(tiles, axis=1)
        y = y_local + jnp.dot(q_ref[g], h_in.astype(BF16), preferred_element_type=F32)
        y3 = y.reshape(tc, GROUP, nk) + dsk_ref[pl.ds(r0, GROUP), :][None] * xg.astype(F32)
        ya = jax.nn.gelu(y3).astype(y_ref.dtype)
        if fold == 1:
            y_ref[:, pl.ds(r0, GROUP), :] = ya
        else:
            for j in range(tc):
                y_ref[j // fold, pl.ds(r0, GROUP), (j % fold) * nk:(j % fold + 1) * nk] = ya[j]

    def body(gi, carry):
        for k in range(GROUPS_PER_TRIP):
            one_group(gi * GROUPS_PER_TRIP + k)
        return carry

    lax.fori_loop(0, gb // GROUPS_PER_TRIP, body, 0)


def _ssm(zb, mp, q, ct, dsk, h0, gb, fold):
    tcf, _, nkf = zb.shape
    tc, nk = tcf * fold, nkf // fold
    ng, mrows, rows = mp.shape
    p2 = mrows - rows
    cb = ng * GROUP
    has_h0 = h0 is not None
    in_specs = [pl.BlockSpec((tcf, gb * GROUP, nkf), lambda i: (0, i, 0)),
                pl.BlockSpec((gb, mrows, rows), lambda i: (i, 0, 0)),
                pl.BlockSpec((gb, rows, p2), lambda i: (i, 0, 0)),
                pl.BlockSpec((gb,) + ct.shape[1:], lambda i: (i, 0, 0)),
                pl.BlockSpec((gb * GROUP, nk), lambda i: (i, 0))]
    args = [zb, mp, q, ct, dsk]
    if has_h0:
        in_specs.append(pl.BlockSpec((gb, p2, nk), lambda i: (i, 0, 0)))
        args.append(h0)
    hf_rows, hf_lanes = (p2, nk) if has_h0 else (nk // LANES, LANES)
    return pl.pallas_call(
        functools.partial(_ssm_kernel, tc=tc, gb=gb, nk=nk, fold=fold, has_h0=has_h0),
        grid=(ng // gb,),
        in_specs=in_specs,
        out_specs=[pl.BlockSpec((tcf, gb * GROUP, nkf), lambda i: (0, i, 0)),
                   pl.BlockSpec((gb, hf_rows, hf_lanes), lambda i: (i, 0, 0))],
        out_shape=[jax.ShapeDtypeStruct((tcf, cb, nkf), BF16),
                   jax.ShapeDtypeStruct((ng, hf_rows, hf_lanes), F32)],
        compiler_params=_params("parallel"),
    )(*args)


def _glu_out_kernel(y_ref, gate_ref, wg_ref, bg_ref, wo_ref, r_ref, ob_ref):
    y = y_ref[...]
    for m in range(wg_ref.shape[0] // ROW_CHUNK):
        sl = slice(m * ROW_CHUNK, (m + 1) * ROW_CHUNK)
        g = jnp.dot(wg_ref[sl, :], y, preferred_element_type=F32) + bg_ref[sl, :]
        ob = y_ref[sl, :].astype(F32) * jax.nn.sigmoid(g) * jax.nn.silu(gate_ref[sl, :].astype(F32))
        ob_ref[sl, :] = ob.astype(ob_ref.dtype)
    ob = ob_ref[...]
    for m in range(wo_ref.shape[0] // ROW_CHUNK):
        sl = slice(m * ROW_CHUNK, (m + 1) * ROW_CHUNK)
        r_ref[:, sl] = jnp.dot(wo_ref[sl, :], ob, preferred_element_type=F32).T


def _glu_out(y_t, zb, wg_t, bg, wo_t, nkt):
    tc, cb, nk = y_t.shape
    d = wo_t.shape[0]
    return pl.pallas_call(
        _glu_out_kernel,
        grid=(tc, nk // nkt),
        in_specs=[pl.BlockSpec((None, cb, nkt), lambda j, k: (j, 0, k)),
                  pl.BlockSpec((None, cb, nkt), lambda j, k: (j, 1, k)),
                  pl.BlockSpec((cb, cb), lambda j, k: (0, 0), pipeline_mode=pl.Buffered(1)),
                  pl.BlockSpec((cb, nkt), lambda j, k: (0, 0), pipeline_mode=pl.Buffered(1)),
                  pl.BlockSpec((d, cb), lambda j, k: (0, 0), pipeline_mode=pl.Buffered(1))],
        out_specs=pl.BlockSpec((None, nkt, d), lambda j, k: (j, k, 0)),
        out_shape=jax.ShapeDtypeStruct((tc, nk, d), F32),
        scratch_shapes=[pltpu.VMEM((cb, nkt), BF16)],
        compiler_params=_params("parallel", "parallel"),
    )(y_t, zb, wg_t, bg, wo_t)


def _out_norm_kernel(x_ref, oa_ref, r_ref, wo_ref, g_ref, y_ref, ms_ref, *, tc, n):
    ca = oa_ref.shape[-1]
    d = x_ref.shape[-1]
    nc = d // LANES
    p = jnp.dot(oa_ref[...].reshape(tc * n, ca), wo_ref[...], preferred_element_type=F32)
    for j in range(tc):
        s = p[j * n:(j + 1) * n] + r_ref[j]
        for c in range(nc):
            ms_ref[c, pl.ds(j, n, stride=tc), :] = s[:, c * LANES:(c + 1) * LANES]
    acc = x_ref[...] + jnp.concatenate([ms_ref[c] for c in range(nc)], axis=1)
    ms = jnp.mean(acc * acc, axis=-1, keepdims=True)
    y_ref[...] = acc * lax.rsqrt(ms + EPS) * g_ref[...]


def _out_norm(x2d, oa, r, wo_a, gf, n):
    tc, nk, ca = oa.shape
    t, d = x2d.shape
    return pl.pallas_call(
        functools.partial(_out_norm_kernel, tc=tc, n=n),
        grid=(nk // n,),
        in_specs=[pl.BlockSpec((n * tc, d), lambda i: (i, 0)),
                  pl.BlockSpec((tc, n, ca), lambda i: (0, i, 0)),
                  pl.BlockSpec((tc, n, d), lambda i: (0, i, 0)),
                  pl.BlockSpec((ca, d), lambda i: (0, 0), pipeline_mode=pl.Buffered(1)),
                  pl.BlockSpec((1, d), lambda i: (0, 0))],
        out_specs=pl.BlockSpec((n * tc, d), lambda i: (i, 0)),
        out_shape=jax.ShapeDtypeStruct((t, d), F32),
        scratch_shapes=[pltpu.VMEM((d // LANES, n * tc, LANES), F32)],
        compiler_params=_params("parallel"),
    )(x2d, oa, r, wo_a, gf.reshape(1, d))


def _discretise(a_re, a_im, log_dt):
    dt = jnp.exp(log_dt)[:, None]
    mag = jnp.exp(a_re * dt)
    ang = a_im * dt
    lr, li = mag * jnp.cos(ang), mag * jnp.sin(ang)
    den = a_re * a_re + a_im * a_im
    nr, ni = lr - 1.0, li
    qr, qi = (nr * a_re + ni * a_im) / den, (ni * a_re - nr * a_im) / den
    cat = lambda a, b: jnp.concatenate([a, b], axis=-1)
    rowtab = jnp.stack([cat(lr, lr), cat(li, li), cat(qr, qi), cat(-qi, qr)], axis=1)
    prs, pis = [lr], [li]
    for _ in range(N_POW - 1):
        prs, pis = prs + [prs[-1] * prs[-1] - pis[-1] * pis[-1]], pis + [2.0 * prs[-1] * pis[-1]]
    pad = [jnp.zeros_like(lr)] * (POW_COLS - N_POW)
    lt = jnp.stack(prs + pad + pis + pad, axis=1).transpose(0, 2, 1)
    b0 = TC_PROMPT.bit_length() - 1
    zrow = [jnp.zeros_like(cat(lr, lr))] * (SCAN_ROWS - SCAN_STEPS)
    scan_rows = jnp.stack([cat(c, c) for c in prs[b0:b0 + SCAN_STEPS]] + zrow
                          + [cat(c, c) for c in pis[b0:b0 + SCAN_STEPS]] + zrow, axis=1)
    return rowtab, lt, scan_rows


def _ssm_tables_kernel(rt_ref, btr_ref, bti_ref, c2r_ref, c2i_ref, c2n_ref,
                       mp_ref, q_ref, mps_ref, qs_ref, *, gb, tc, tcs, unroll):
    rows = tc * GROUP
    rows_s = tcs * GROUP
    half = rt_ref.shape[2] // 2
    lane_k = lax.broadcasted_iota(jnp.int32, (GROUP, rows), 1)
    lane_q = lax.broadcasted_iota(jnp.int32, (1, 2 * half), 1)

    def one_group(g):
        rt = rt_ref[g]
        l_r, l_i, qu, qv = rt[0:1], rt[1:2], rt[2:3], rt[3:4]
        btr, bti = btr_ref[g], bti_ref[g]
        bb = btr * qu + bti * qv
        bs = btr * qv - bti * qu
        c2r, c2i = c2r_ref[g], c2i_ref[g]
        cur_r = jnp.ones((1, 2 * half), F32)
        cur_i = jnp.zeros((1, 2 * half), F32)
        et_blocks = []
        for n in range(tc):
            et_blocks.append(cur_r * bb + cur_i * bs)
            cur_r, cur_i = cur_r * l_r - cur_i * l_i, cur_r * l_i + cur_i * l_r
            u = jnp.where(lane_q < half, cur_r, -cur_i)
            v = jnp.where(lane_q < half, -cur_i, -cur_r)
            blk = (c2r * u + c2i * v).astype(q_ref.dtype)
            q_ref[g, n * GROUP:(n + 1) * GROUP, :] = blk
            if n < tcs:
                qs_ref[g, n * GROUP:(n + 1) * GROUP, :] = blk
        p = jnp.concatenate(et_blocks[::-1], axis=0).T
        mp_ref[g, rows:, :] = p.astype(mp_ref.dtype)
        mps_ref[g, rows_s:, :] = p[:, rows - rows_s:].astype(mps_ref.dtype)
        krow = jnp.dot(c2n_ref[g], p, precision=lax.Precision.HIGHEST, preferred_element_type=F32)
        for i in range(tc):
            sh = (tc - 1 - i) * GROUP
            blk = krow if sh == 0 else pltpu.roll(krow, rows - sh, 1)
            blk = jnp.where(lane_k < (i + 1) * GROUP, blk, 0.0)
            mp_ref[g, i * GROUP:(i + 1) * GROUP, :] = blk.astype(mp_ref.dtype)
            if i < tcs:
                mps_ref[g, i * GROUP:(i + 1) * GROUP, :] = blk[:, :rows_s].astype(mps_ref.dtype)

    def body(gi, carry):
        for k in range(unroll):
            one_group(gi * unroll + k)
        return carry

    lax.fori_loop(0, gb // unroll, body, 0)


def _ssm_tables(rowtab, b_re, b_im, c_re, c_im, tcs, gb):
    ng, _, p2 = rowtab.shape
    tc = TC_PROMPT
    rows, rows_s = tc * GROUP, tcs * GROUP
    dup = lambda a: jnp.concatenate([a, a], axis=-1)
    args = (rowtab, dup(b_re.transpose(0, 2, 1)), dup(b_im.transpose(0, 2, 1)),
            dup(c_re), dup(c_im), jnp.concatenate([c_re, -c_im], axis=-1))
    shapes = [(ng, rows + p2, rows), (ng, rows, p2), (ng, rows_s + p2, rows_s), (ng, rows_s, p2)]
    return pl.pallas_call(
        functools.partial(_ssm_tables_kernel, gb=gb, tc=tc, tcs=tcs, unroll=2),
        grid=(ng // gb,),
        in_specs=[pl.BlockSpec((gb,) + a.shape[1:], lambda i: (i, 0, 0)) for a in args],
        out_specs=[pl.BlockSpec((gb,) + s[1:], lambda i: (i, 0, 0)) for s in shapes],
        out_shape=[jax.ShapeDtypeStruct(s, BF16) for s in shapes],
        compiler_params=_params("parallel"),
    )(*args)


def _mix_tables(w_s, b_s, hd, tc):
    nh = w_s.shape[0]
    w = w_s * jnp.tril(jnp.ones((CHUNK, CHUNK), F32))[None]
    r = CHUNK // tc
    if tc == TC_PROMPT:
        wm = w.reshape(nh, r, tc, r, tc).transpose(0, 2, 1, 4, 3).reshape(nh, CHUNK, CHUNK)
        bm = b_s.reshape(nh, r, tc).transpose(0, 2, 1).reshape(nh, CHUNK)
    else:
        eye = jnp.eye(r, dtype=F32)
        wm = jnp.einsum('hts,ab->htasb', w[:, :tc, :tc], eye).reshape(nh, CHUNK, CHUNK)
        bm = jnp.repeat(b_s[:, :tc], r, axis=1)
    return wm.astype(BF16), jnp.broadcast_to(bm[:, :, None], (nh, CHUNK, hd)).astype(F32)


def _layer(x2d, tc, h0, wts, tiles):
    (g_norm, w_a, gv, wmix, bmix, w_bt, mp, q, ct, d_skip, wg_t, b_glu, wo_a, wo_bt, g_final) = wts
    n_norm, n_a, nkt, gb, fold = tiles
    cb = wg_t.shape[0]
    d = x2d.shape[1]
    nk = x2d.shape[0] // tc
    xn = _norm_permute(x2d, g_norm, tc, n_norm)
    out_a, v_rows = _branch_a(xn, w_a, gv, wmix, bmix, n_a, h0 is not None)
    zb = _proj_b(w_bt, xn.reshape(tc // fold, fold * nk, d), nkt)
    dsk = jnp.broadcast_to(d_skip[:, None], (cb, nk))
    y_t, h_fin = _ssm(zb, mp, q, ct, dsk, h0, gb, fold)
    bg = jnp.broadcast_to(b_glu[:, None], (cb, nkt))
    r = _glu_out(y_t, zb, wg_t, bg, wo_bt, nkt).reshape(tc, nk, d)
    y = _out_norm(x2d, out_a, r, wo_a, g_final, n_norm)
    return y, h_fin, v_rows


def kernel(x_prompt, x_sample, state_ssm_re, state_ssm_im, g_norm, w_in, g_v, w_s, b_s,
           a_re, a_im, log_dt, b_re, b_im, c_re, c_im, d_skip, w_glu, b_glu, w_out, g_final):
    depth = g_norm.shape[0]
    assert depth == 1, "single-layer step"
    bsz, seq, d = x_prompt.shape
    dbsz, dseq, _ = x_sample.shape
    nh = w_s.shape[1]
    ng, npst = a_re.shape[1:]
    cb = ng * GROUP
    ca = (w_in.shape[2] - 2 * cb) // 3
    hd = ca // nh
    assert seq // TC_PROMPT == LANES and dbsz == LANES and CHUNK % dseq == 0
    assert dseq & (dseq - 1) == 0 and dseq <= TC_PROMPT, "sample chunk must be a power of two"
    assert 3 * ca % 512 == 0 and ca == cb == d

    l = 0
    w_in_l = w_in[l]
    w_bt = _transpose_cast(w_in_l, 0, 3 * ca // 512, d, 2 * cb)
    gv = g_v[l].reshape(nh, 1, hd)
    rowtab, lt, scan_rows = _discretise(a_re[l], a_im[l], log_dt[l])
    mp, q, mp_s, q_s = _ssm_tables(rowtab, b_re[l], b_im[l], c_re[l], c_im[l], dseq, 8)
    wg_t = _transpose_cast(w_glu[l], 0, 0, cb, cb)
    wo_a = w_out[l][:ca].astype(BF16)
    wo_bt = _transpose_cast(w_out[l], 1, 0, cb, d)

    def weights(tc, mp_, q_, ct):
        wmix, bmix = _mix_tables(w_s[l], b_s[l], hd, tc)
        return (g_norm[l], w_in_l, gv, wmix, bmix, w_bt, mp_, q_, ct,
                d_skip[l], wg_t, b_glu[l], wo_a, wo_bt, g_final)

    assert bsz % 2 == 0
    tc_p, nk_p = TC_PROMPT, bsz * seq // TC_PROMPT
    y_p, hf_p, _ = _layer(x_prompt.reshape(nk_p * tc_p, d), tc_p, None,
                          weights(tc_p, mp, q, scan_rows), (32, 64, 256, 8, 1))
    hf_p = hf_p.reshape(ng, bsz // 2, 2, 2, npst)
    re_p = hf_p[:, :, 0].reshape(ng, bsz, npst).transpose(1, 0, 2)[None]
    im_p = hf_p[:, :, 1].reshape(ng, bsz, npst).transpose(1, 0, 2)[None]

    tc_s, nk_s = dseq, dbsz
    h0 = jnp.concatenate([state_ssm_re[l], state_ssm_im[l]], axis=-1).transpose(1, 2, 0)
    y_s, hf_s, v_s = _layer(x_sample.reshape(nk_s * tc_s, d), tc_s, h0,
                            weights(tc_s, mp_s, q_s, lt), (64, 128, 256, 8, 2))
    re_s = hf_s[:, :npst].transpose(2, 0, 1)[None]
    im_s = hf_s[:, npst:].transpose(2, 0, 1)[None]
    v_out = v_s.transpose(1, 0, 2)[None]

    dt_p, dt_s = x_prompt.dtype, x_sample.dtype
    return (y_p.reshape(bsz, seq, d).astype(dt_p), y_s.reshape(dbsz, dseq, d).astype(dt_s),
            re_p.astype(dt_p), im_p.astype(dt_p), re_s.astype(dt_s), im_s.astype(dt_s),
            v_out.astype(dt_s))
```

```python
import functools

import jax
import jax.numpy as jnp
from jax import lax
from jax.experimental import pallas as pl
from jax.experimental.pallas import tpu as pltpu

EPS = 1e-6
LANES = 128
CHUNK = 128
GROUP = 16
TC_PROMPT = 16
SCAN_STEPS = 7
ROW_CHUNK = 512
POW_COLS = 16
N_POW = 11
SCAN_ROWS = 8
GROUPS_PER_TRIP = 2
VMEM_LIMIT = 56 * 1024 * 1024

F32 = jnp.float32
BF16 = jnp.bfloat16


def _params(*sem):
    return pltpu.CompilerParams(dimension_semantics=sem, vmem_limit_bytes=VMEM_LIMIT)


def _norm_permute_kernel(x_ref, g_ref, o_ref, xs_ref, *, tc, n):
    x = x_ref[...]
    ms = jnp.mean(x * x, axis=-1, keepdims=True)
    xn = x * lax.rsqrt(ms + EPS) * g_ref[...]
    nc = x.shape[1] // LANES
    for c in range(nc):
        xs_ref[c] = xn[:, c * LANES:(c + 1) * LANES]
    for j in range(tc):
        rows = jnp.concatenate([xs_ref[c, pl.ds(j, n, stride=tc), :] for c in range(nc)], axis=1)
        o_ref[j] = rows.astype(o_ref.dtype)


def _norm_permute(x2d, g, tc, n):
    t, d = x2d.shape
    nk = t // tc
    return pl.pallas_call(
        functools.partial(_norm_permute_kernel, tc=tc, n=n),
        grid=(nk // n,),
        in_specs=[pl.BlockSpec((n * tc, d), lambda i: (i, 0)),
                  pl.BlockSpec((1, d), lambda i: (0, 0))],
        out_specs=pl.BlockSpec((tc, n, d), lambda i: (0, i, 0)),
        out_shape=jax.ShapeDtypeStruct((tc, nk, d), BF16),
        scratch_shapes=[pltpu.VMEM((d // LANES, n * tc, LANES), F32)],
        compiler_params=_params("parallel"),
    )(x2d, g.reshape(1, d))


def _branch_a_kernel(xn_ref, wu_ref, wv_ref, wg_ref, gv_ref, ws_ref, bmix_ref, oa_ref, *rest,
                     tc, n, r, hd):
    *v_refs, w_ref, wm_ref = rest
    d = xn_ref.shape[-1]

    @pl.when(pl.program_id(1) == 0)
    def _():
        w_ref[:, :hd] = wu_ref[...].astype(BF16)
        w_ref[:, hd:2 * hd] = wv_ref[...].astype(BF16)
        w_ref[:, 2 * hd:] = wg_ref[...].astype(BF16)
        row = lax.broadcasted_iota(jnp.int32, (CHUNK, CHUNK), 0)
        col = lax.broadcasted_iota(jnp.int32, (CHUNK, CHUNK), 1)
        w_tril = jnp.where(row >= col, ws_ref[...], 0.0).astype(BF16)
        if tc == TC_PROMPT:
            time_of_row = tc * (row % r) + row // r
            same_seq = None
        else:
            time_of_row = row // r
            same_seq = (row % r) == (col % r)
        sel = (col == time_of_row).astype(BF16)
        sw = jnp.dot(sel, w_tril, preferred_element_type=F32).astype(BF16)
        wm = lax.dot_general(sw, sel, (((1,), (1,)), ((), ())), preferred_element_type=F32)
        if same_seq is not None:
            wm = jnp.where(same_seq, wm, 0.0)
        wm_ref[...] = wm.astype(BF16)

    z = jnp.dot(xn_ref[...].reshape(tc * n, d), w_ref[...], preferred_element_type=F32)
    u = jax.nn.gelu(z[:, :hd])
    v = jax.nn.gelu(z[:, hd:2 * hd])
    ga = z[:, 2 * hd:]
    ms = jnp.mean(v * v, axis=-1, keepdims=True)
    vn3 = (v * lax.rsqrt(ms + EPS) * gv_ref[...]).reshape(tc, n, hd)
    pre3 = (u * jax.nn.silu(ga)).reshape(tc, n, hd)
    if v_refs:
        v_refs[0][...] = vn3
    wm = wm_ref[...]
    bm = bmix_ref[...]
    for c in range(n // r):
        vg = vn3[:, c * r:(c + 1) * r, :].reshape(CHUNK, hd)
        mixed = jnp.dot(wm, vg.astype(BF16), preferred_element_type=F32) + bm
        og = pre3[:, c * r:(c + 1) * r, :].reshape(CHUNK, hd) * mixed
        oa_ref[:, c * r:(c + 1) * r, :] = og.reshape(tc, r, hd).astype(oa_ref.dtype)


def _branch_a(xn, w_in, gv, w_s, bmix, n, want_v):
    tc, nk, d = xn.shape
    nh, _, hd = gv.shape
    r = CHUNK // tc
    out_map = lambda h, i: (0, i, h)
    out_shape = [jax.ShapeDtypeStruct((tc, nk, nh * hd), BF16)]
    out_specs = [pl.BlockSpec((tc, n, hd), out_map)]
    if want_v:
        out_shape.append(jax.ShapeDtypeStruct((tc, nk, nh * hd), F32))
        out_specs.append(pl.BlockSpec((tc, n, hd), out_map))
    res = pl.pallas_call(
        functools.partial(_branch_a_kernel, tc=tc, n=n, r=r, hd=hd),
        grid=(nh, nk // n),
        in_specs=[pl.BlockSpec((tc, n, d), lambda h, i: (0, i, 0)),
                  pl.BlockSpec((d, hd), lambda h, i: (0, h)),
                  pl.BlockSpec((d, hd), lambda h, i: (0, nh + h)),
                  pl.BlockSpec((d, hd), lambda h, i: (0, 2 * nh + h)),
                  pl.BlockSpec((None, 1, hd), lambda h, i: (h, 0, 0)),
                  pl.BlockSpec((None, CHUNK, CHUNK), lambda h, i: (h, 0, 0)),
                  pl.BlockSpec((None, CHUNK, hd), lambda h, i: (h, 0, 0))],
        out_specs=out_specs,
        out_shape=out_shape,
        scratch_shapes=[pltpu.VMEM((d, 3 * hd), BF16), pltpu.VMEM((CHUNK, CHUNK), BF16)],
        compiler_params=_params("parallel", "arbitrary"),
    )(xn, w_in, w_in, w_in, gv, w_s, bmix)
    return res if want_v else (res[0], None)


def _transpose_cast_kernel(w_ref, o_ref):
    o_ref[...] = w_ref[...].T.astype(o_ref.dtype)


def _transpose_cast(w, row_block, col_block0, n_rows, n_cols):
    cblk = 512
    return pl.pallas_call(
        _transpose_cast_kernel,
        grid=(n_cols // cblk,),
        in_specs=[pl.BlockSpec((n_rows, cblk), lambda i: (row_block, col_block0 + i))],
        out_specs=pl.BlockSpec((cblk, n_rows), lambda i: (i, 0)),
        out_shape=jax.ShapeDtypeStruct((n_cols, n_rows), BF16),
        compiler_params=_params("parallel"),
    )(w)


def _proj_b_kernel(w_ref, xn_ref, o_ref):
    xn = xn_ref[...]
    for m in range(w_ref.shape[0] // ROW_CHUNK):
        sl = slice(m * ROW_CHUNK, (m + 1) * ROW_CHUNK)
        o_ref[sl, :] = lax.dot_general(
            w_ref[sl, :], xn, (((1,), (1,)), ((), ())),
            preferred_element_type=F32).astype(o_ref.dtype)


def _proj_b(w_bt, xn, nkt):
    tc, nk, d = xn.shape
    rows = w_bt.shape[0]
    return pl.pallas_call(
        _proj_b_kernel,
        grid=(tc, nk // nkt),
        in_specs=[pl.BlockSpec((rows, d), lambda j, k: (0, 0), pipeline_mode=pl.Buffered(1)),
                  pl.BlockSpec((None, nkt, d), lambda j, k: (j, k, 0))],
        out_specs=pl.BlockSpec((None, rows, nkt), lambda j, k: (j, 0, k)),
        out_shape=jax.ShapeDtypeStruct((tc, rows, nk), BF16),
        compiler_params=_params("parallel", "parallel"),
    )(w_bt, xn)


def _shift_rows(x, d):
    if d % 8 == 0:
        return jnp.concatenate([jnp.zeros((d, x.shape[1]), x.dtype), x[:-d]], axis=0)
    row = lax.broadcasted_iota(jnp.int32, x.shape, 0)
    return jnp.where(row >= d, pltpu.roll(x, d, 0), 0.0)


def _ssm_kernel(x_ref, mp_ref, q_ref, ct_ref, dsk_ref, *rest, tc, gb, nk, fold, has_h0):
    if has_h0:
        h0_ref, y_ref, hf_ref = rest
    else:
        y_ref, hf_ref = rest
    rows = tc * GROUP
    half = q_ref.shape[2] // 2
    b_tc = tc.bit_length() - 1

    def one_group(g):
        r0 = pl.multiple_of(g * GROUP, GROUP)
        if fold == 1:
            xg = x_ref[:, pl.ds(r0, GROUP), :]
        else:
            xg = jnp.stack([x_ref[j // fold, pl.ds(r0, GROUP), (j % fold) * nk:(j % fold + 1) * nk]
                            for j in range(tc)])
        res = jnp.dot(mp_ref[g], xg.reshape(rows, nk), preferred_element_type=F32)
        y_local = res[:rows]
        s_re = res[rows:rows + half]
        s_im = res[rows + half:]
        ct = ct_ref[g]
        if has_h0:
            h0 = h0_ref[g]
            h_re, h_im = h0[:half], h0[half:]
            a_re, a_im = ct[:, b_tc:b_tc + 1], ct[:, POW_COLS + b_tc:POW_COLS + b_tc + 1]
            hf_ref[g] = jnp.concatenate(
                [a_re * h_re - a_im * h_im + s_re, a_re * h_im + a_im * h_re + s_im], axis=0)
            h_in = h0
        else:
            tiles = []
            for pair in range(nk // (2 * LANES)):
                lo, mid, hi = 2 * pair * LANES, (2 * pair + 1) * LANES, (2 * pair + 2) * LANES
                xr = jnp.concatenate([s_re[:, lo:mid], s_re[:, mid:hi]], axis=0).T
                xi = jnp.concatenate([s_im[:, lo:mid], s_im[:, mid:hi]], axis=0).T
                for s in range(SCAN_STEPS):
                    cr, ci = ct[s:s + 1], ct[SCAN_ROWS + s:SCAN_ROWS + s + 1]
                    rr, ri = _shift_rows(xr, 1 << s), _shift_rows(xi, 1 << s)
                    xr, xi = xr + cr * rr - ci * ri, xi + cr * ri + ci * rr
                hf_ref[g, 2 * pair:2 * pair + 1, :] = xr[LANES - 1:]
                hf_ref[g, 2 * pair + 1:2 * pair + 2, :] = xi[LANES - 1:]
                hr_t = _shift_rows(xr, 1).T
                hi_t = _shift_rows(xi, 1).T
                tiles.append(jnp.concatenate([hr_t[:half], hi_t[:half]], axis=0))
                tiles.append(jnp.concatenate([hr_t[half:], hi_t[half:]], axis=0))
            h_in = jnp.concatenate(tiles, axis=1)
        y = y_local + jnp.dot(q_ref[g], h_in.astype(BF16), preferred_element_type=F32)
        y3 = y.reshape(tc, GROUP, nk) + dsk_ref[pl.ds(r0, GROUP), :][None] * xg.astype(F32)
        ya = jax.nn.gelu(y3).astype(y_ref.dtype)
        if fold == 1:
            y_ref[:, pl.ds(r0, GROUP), :] = ya
        else:
            for j in range(tc):
                y_ref[j // fold, pl.ds(r0, GROUP), (j % fold) * nk:(j % fold + 1) * nk] = ya[j]

    def body(gi, carry):
        for k in range(GROUPS_PER_TRIP):
            one_group(gi * GROUPS_PER_TRIP + k)
        return carry

    lax.fori_loop(0, gb // GROUPS_PER_TRIP, body, 0)


def _ssm(zb, mp, q, ct, dsk, h0, gb, fold):
    tcf, _, nkf = zb.shape
    tc, nk = tcf * fold, nkf // fold
    ng, mrows, rows = mp.shape
    p2 = mrows - rows
    cb = ng * GROUP
    has_h0 = h0 is not None
    in_specs = [pl.BlockSpec((tcf, gb * GROUP, nkf), lambda i: (0, i, 0)),
                pl.BlockSpec((gb, mrows, rows), lambda i: (i, 0, 0)),
                pl.BlockSpec((gb, rows, p2), lambda i: (i, 0, 0)),
                pl.BlockSpec((gb,) + ct.shape[1:], lambda i: (i, 0, 0)),
                pl.BlockSpec((gb * GROUP, nk), lambda i: (i, 0))]
    args = [zb, mp, q, ct, dsk]
    if has_h0:
        in_specs.append(pl.BlockSpec((gb, p2, nk), lambda i: (i, 0, 0)))
        args.append(h0)
    hf_rows, hf_lanes = (p2, nk) if has_h0 else (nk // LANES, LANES)
    return pl.pallas_call(
        functools.partial(_ssm_kernel, tc=tc, gb=gb, nk=nk, fold=fold, has_h0=has_h0),
        grid=(ng // gb,),
        in_specs=in_specs,
        out_specs=[pl.BlockSpec((tcf, gb * GROUP, nkf), lambda i: (0, i, 0)),
                   pl.BlockSpec((gb, hf_rows, hf_lanes), lambda i: (i, 0, 0))],
        out_shape=[jax.ShapeDtypeStruct((tcf, cb, nkf), BF16),
                   jax.ShapeDtypeStruct((ng, hf_rows, hf_lanes), F32)],
        compiler_params=_params("parallel"),
    )(*args)


def _glu_out_kernel(y_ref, gate_ref, wg_ref, bg_ref, wo_ref, r_ref, ob_ref):
    y = y_ref[...]
    for m in range(wg_ref.shape[0] // ROW_CHUNK):
        sl = slice(m * ROW_CHUNK, (m + 1) * ROW_CHUNK)
        g = jnp.dot(wg_ref[sl, :], y, preferred_element_type=F32) + bg_ref[sl, :]
        ob = y_ref[sl, :].astype(F32) * jax.nn.sigmoid(g) * jax.nn.silu(gate_ref[sl, :].astype(F32))
        ob_ref[sl, :] = ob.astype(ob_ref.dtype)
    ob = ob_ref[...]
    for m in range(wo_ref.shape[0] // ROW_CHUNK):
        sl = slice(m * ROW_CHUNK, (m + 1) * ROW_CHUNK)
        r_ref[:, sl] = jnp.dot(wo_ref[sl, :], ob, preferred_element_type=F32).T


def _glu_out(y_t, zb, wg_t, bg, wo_t, nkt):
    tc, cb, nk = y_t.shape
    d = wo_t.shape[0]
    return pl.pallas_call(
        _glu_out_kernel,
        grid=(tc, nk // nkt),
        in_specs=[pl.BlockSpec((None, cb, nkt), lambda j, k: (j, 0, k)),
                  pl.BlockSpec((None, cb, nkt), lambda j, k: (j, 1, k)),
                  pl.BlockSpec((cb, cb), lambda j, k: (0, 0), pipeline_mode=pl.Buffered(1)),
                  pl.BlockSpec((cb, nkt), lambda j, k: (0, 0), pipeline_mode=pl.Buffered(1)),
                  pl.BlockSpec((d, cb), lambda j, k: (0, 0), pipeline_mode=pl.Buffered(1))],
        out_specs=pl.BlockSpec((None, nkt, d), lambda j, k: (j, k, 0)),
        out_shape=jax.ShapeDtypeStruct((tc, nk, d), F32),
        scratch_shapes=[pltpu.VMEM((cb, nkt), BF16)],
        compiler_params=_params("parallel", "parallel"),
    )(y_t, zb, wg_t, bg, wo_t)


def _out_norm_kernel(x_ref, oa_ref, r_ref, wo_ref, g_ref, y_ref, ms_ref, *, tc, n):
    ca = oa_ref.shape[-1]
    d = x_ref.shape[-1]
    nc = d // LANES
    p = jnp.dot(oa_ref[...].reshape(tc * n, ca), wo_ref[...], preferred_element_type=F32)
    for j in range(tc):
        s = p[j * n:(j + 1) * n] + r_ref[j]
        for c in range(nc):
            ms_ref[c, pl.ds(j, n, stride=tc), :] = s[:, c * LANES:(c + 1) * LANES]
    acc = x_ref[...] + jnp.concatenate([ms_ref[c] for c in range(nc)], axis=1)
    ms = jnp.mean(acc * acc, axis=-1, keepdims=True)
    y_ref[...] = acc * lax.rsqrt(ms + EPS) * g_ref[...]


def _out_norm(x2d, oa, r, wo_a, gf, n):
    tc, nk, ca = oa.shape
    t, d = x2d.shape
    return pl.pallas_call(
        functools.partial(_out_norm_kernel, tc=tc, n=n),
        grid=(nk // n,),
        in_specs=[pl.BlockSpec((n * tc, d), lambda i: (i, 0)),
                  pl.BlockSpec((tc, n, ca), lambda i: (0, i, 0)),
                  pl.BlockSpec((tc, n, d), lambda i: (0, i, 0)),
                  pl.BlockSpec((ca, d), lambda i: (0, 0), pipeline_mode=pl.Buffered(1)),
                  pl.BlockSpec((1, d), lambda i: (0, 0))],
        out_specs=pl.BlockSpec((n * tc, d), lambda i: (i, 0)),
        out_shape=jax.ShapeDtypeStruct((t, d), F32),
        scratch_shapes=[pltpu.VMEM((d // LANES, n * tc, LANES), F32)],
        compiler_params=_params("parallel"),
    )(x2d, oa, r, wo_a, gf.reshape(1, d))


def _discretise(a_re, a_im, log_dt):
    dt = jnp.exp(log_dt)[:, None]
    x_re, x_im = a_re * dt, a_im * dt
    n = (2.0 ** jnp.arange(N_POW, dtype=F32))[None, :, None]
    mag = jnp.exp(n * x_re[:, None, :])
    ang = n * x_im[:, None, :]
    pw_r, pw_i = mag * jnp.cos(ang), mag * jnp.sin(ang)
    lr, li = pw_r[:, 0], pw_i[:, 0]
    den = a_re * a_re + a_im * a_im
    nr, ni = lr - 1.0, li
    qr, qi = (nr * a_re + ni * a_im) / den, (ni * a_re - nr * a_im) / den
    cat = lambda a, b: jnp.concatenate([a, b], axis=-1)
    rowtab = jnp.stack([cat(lr, lr), cat(li, li), cat(qr, qi), cat(-qi, qr)], axis=1)
    zpad = jnp.zeros_like(pw_r[:, :POW_COLS - N_POW])
    lt = jnp.concatenate([pw_r, zpad, pw_i, zpad], axis=1).transpose(0, 2, 1)
    b0 = TC_PROMPT.bit_length() - 1
    sc_r, sc_i = pw_r[:, b0:b0 + SCAN_STEPS], pw_i[:, b0:b0 + SCAN_STEPS]
    zrow = jnp.zeros_like(cat(sc_r, sc_r)[:, :SCAN_ROWS - SCAN_STEPS])
    scan_rows = jnp.concatenate([cat(sc_r, sc_r), zrow, cat(sc_i, sc_i), zrow], axis=1)
    return rowtab, lt, scan_rows


def _ssm_tables_kernel(rt_ref, btr_ref, bti_ref, c2r_ref, c2i_ref, c2n_ref,
                       mp_ref, q_ref, mps_ref, qs_ref, *, gb, tc, tcs, unroll):
    rows = tc * GROUP
    rows_s = tcs * GROUP
    half = rt_ref.shape[2] // 2
    lane_k = lax.broadcasted_iota(jnp.int32, (GROUP, rows), 1)
    lane_q = lax.broadcasted_iota(jnp.int32, (1, 2 * half), 1)

    def one_group(g):
        rt = rt_ref[g]
        l_r, l_i, qu, qv = rt[0:1], rt[1:2], rt[2:3], rt[3:4]
        btr, bti = btr_ref[g], bti_ref[g]
        bb = btr * qu + bti * qv
        bs = btr * qv - bti * qu
        c2r, c2i = c2r_ref[g], c2i_ref[g]
        cur_r = jnp.ones((1, 2 * half), F32)
        cur_i = jnp.zeros((1, 2 * half), F32)
        et_blocks = []
        for n in range(tc):
            et_blocks.append(cur_r * bb + cur_i * bs)
            cur_r, cur_i = cur_r * l_r - cur_i * l_i, cur_r * l_i + cur_i * l_r
            u = jnp.where(lane_q < half, cur_r, -cur_i)
            v = jnp.where(lane_q < half, -cur_i, -cur_r)
            blk = (c2r * u + c2i * v).astype(q_ref.dtype)
            q_ref[g, n * GROUP:(n + 1) * GROUP, :] = blk
            if n < tcs:
                qs_ref[g, n * GROUP:(n + 1) * GROUP, :] = blk
        p = jnp.concatenate(et_blocks[::-1], axis=0).T
        mp_ref[g, rows:, :] = p.astype(mp_ref.dtype)
        mps_ref[g, rows_s:, :] = p[:, rows - rows_s:].astype(mps_ref.dtype)
        krow = jnp.dot(c2n_ref[g], p, precision=lax.Precision.HIGHEST, preferred_element_type=F32)
        for i in range(tc):
            sh = (tc - 1 - i) * GROUP
            blk = krow if sh == 0 else pltpu.roll(krow, rows - sh, 1)
            blk = jnp.where(lane_k < (i + 1) * GROUP, blk, 0.0)
            mp_ref[g, i * GROUP:(i + 1) * GROUP, :] = blk.astype(mp_ref.dtype)
            if i < tcs:
                mps_ref[g, i * GROUP:(i + 1) * GROUP, :] = blk[:, :rows_s].astype(mps_ref.dtype)

    def body(gi, carry):
        for k in range(unroll):
            one_group(gi * unroll + k)
        return carry

    lax.fori_loop(0, gb // unroll, body, 0)


def _ssm_tables(rowtab, b_re, b_im, c_re, c_im, tcs, gb):
    ng, _, p2 = rowtab.shape
    tc = TC_PROMPT
    rows, rows_s = tc * GROUP, tcs * GROUP
    dup = lambda a: jnp.concatenate([a, a], axis=-1)
    args = (rowtab, dup(b_re.transpose(0, 2, 1)), dup(b_im.transpose(0, 2, 1)),
            dup(c_re), dup(c_im), jnp.concatenate([c_re, -c_im], axis=-1))
    shapes = [(ng, rows + p2, rows), (ng, rows, p2), (ng, rows_s + p2, rows_s), (ng, rows_s, p2)]
    return pl.pallas_call(
        functools.partial(_ssm_tables_kernel, gb=gb, tc=tc, tcs=tcs, unroll=2),
        grid=(ng // gb,),
        in_specs=[pl.BlockSpec((gb,) + a.shape[1:], lambda i: (i, 0, 0)) for a in args],
        out_specs=[pl.BlockSpec((gb,) + s[1:], lambda i: (i, 0, 0)) for s in shapes],
        out_shape=[jax.ShapeDtypeStruct(s, BF16) for s in shapes],
        compiler_params=_params("parallel"),
    )(*args)


def _mix_bias(b_s, hd, tc):
    nh = b_s.shape[0]
    r = CHUNK // tc
    if tc == TC_PROMPT:
        bm = b_s.reshape(nh, r, tc).transpose(0, 2, 1).reshape(nh, CHUNK)
    else:
        bm = jnp.repeat(b_s[:, :tc], r, axis=1)
    return jnp.broadcast_to(bm[:, :, None], (nh, CHUNK, hd))


def _layer(x2d, tc, h0, wts, tiles):
    (g_norm, w_a, gv, wmix, bmix, w_bt, mp, q, ct, d_skip, wg_t, b_glu, wo_a, wo_bt, g_final) = wts
    n_norm, n_a, nkt, gb, fold = tiles
    cb = wg_t.shape[0]
    d = x2d.shape[1]
    nk = x2d.shape[0] // tc
    xn = _norm_permute(x2d, g_norm, tc, n_norm)
    out_a, v_rows = _branch_a(xn, w_a, gv, wmix, bmix, n_a, h0 is not None)
    zb = _proj_b(w_bt, xn.reshape(tc // fold, fold * nk, d), nkt)
    dsk = jnp.broadcast_to(d_skip[:, None], (cb, nk))
    y_t, h_fin = _ssm(zb, mp, q, ct, dsk, h0, gb, fold)
    bg = jnp.broadcast_to(b_glu[:, None], (cb, nkt))
    r = _glu_out(y_t, zb, wg_t, bg, wo_bt, nkt).reshape(tc, nk, d)
    y = _out_norm(x2d, out_a, r, wo_a, g_final, n_norm)
    return y, h_fin, v_rows


def kernel(x_prompt, x_sample, state_ssm_re, state_ssm_im, g_norm, w_in, g_v, w_s, b_s,
           a_re, a_im, log_dt, b_re, b_im, c_re, c_im, d_skip, w_glu, b_glu, w_out, g_final):
    depth = g_norm.shape[0]
    assert depth == 1, "single-layer step"
    bsz, seq, d = x_prompt.shape
    dbsz, dseq, _ = x_sample.shape
    nh = w_s.shape[1]
    ng, npst = a_re.shape[1:]
    cb = ng * GROUP
    ca = (w_in.shape[2] - 2 * cb) // 3
    hd = ca // nh
    assert seq // TC_PROMPT == LANES and dbsz == LANES and CHUNK % dseq == 0
    assert dseq & (dseq - 1) == 0 and dseq <= TC_PROMPT, "sample chunk must be a power of two"
    assert 3 * ca % 512 == 0 and ca == cb == d

    l = 0
    w_in_l = w_in[l]
    w_bt = _transpose_cast(w_in_l, 0, 3 * ca // 512, d, 2 * cb)
    gv = g_v[l].reshape(nh, 1, hd)
    rowtab, lt, scan_rows = _discretise(a_re[l], a_im[l], log_dt[l])
    mp, q, mp_s, q_s = _ssm_tables(rowtab, b_re[l], b_im[l], c_re[l], c_im[l], dseq, 8)
    wg_t = _transpose_cast(w_glu[l], 0, 0, cb, cb)
    wo_a = w_out[l][:ca].astype(BF16)
    wo_bt = _transpose_cast(w_out[l], 1, 0, cb, d)

    def weights(tc, mp_, q_, ct):
        return (g_norm[l], w_in_l, gv, w_s[l], _mix_bias(b_s[l], hd, tc), w_bt, mp_, q_, ct,
                d_skip[l], wg_t, b_glu[l], wo_a, wo_bt, g_final)

    assert bsz % 2 == 0
    tc_p, nk_p = TC_PROMPT, bsz * seq // TC_PROMPT
    y_p, hf_p, _ = _layer(x_prompt.reshape(nk_p * tc_p, d), tc_p, None,
                          weights(tc_p, mp, q, scan_rows), (32, 64, 256, 8, 1))
    hf_p = hf_p.reshape(ng, bsz // 2, 2, 2, npst)
    re_p = hf_p[:, :, 0].reshape(ng, bsz, npst).transpose(1, 0, 2)[None]
    im_p = hf_p[:, :, 1].reshape(ng, bsz, npst).transpose(1, 0, 2)[None]

    tc_s, nk_s = dseq, dbsz
    h0 = jnp.concatenate([state_ssm_re[l], state_ssm_im[l]], axis=-1).transpose(1, 2, 0)
    y_s, hf_s, v_s = _layer(x_sample.reshape(nk_s * tc_s, d), tc_s, h0,
                            weights(tc_s, mp_s, q_s, lt), (64, 128, 256, 8, 2))
    re_s = hf_s[:, :npst].transpose(2, 0, 1)[None]
    im_s = hf_s[:, npst:].transpose(2, 0, 1)[None]
    v_out = v_s.transpose(1, 0, 2)[None]

    dt_p, dt_s = x_prompt.dtype, x_sample.dtype
    return (y_p.reshape(bsz, seq, d).astype(dt_p), y_s.reshape(dbsz, dseq, d).astype(dt_s),
            re_p.astype(dt_p), im_p.astype(dt_p), re_s.astype(dt_s), im_s.astype(dt_s),
            v_out.astype(dt_s))
```

```python
import functools

import jax
import jax.numpy as jnp
from jax import lax
from jax.experimental import pallas as pl
from jax.experimental.pallas import tpu as pltpu

EPS = 1e-6
LANES = 128
CHUNK = 128
GROUP = 16
TC_PROMPT = 16
SCAN_STEPS = 7
ROW_CHUNK = 512
POW_COLS = 16
N_POW = 11
SCAN_ROWS = 8
GROUPS_PER_TRIP = 2
VMEM_LIMIT = 56 * 1024 * 1024

F32 = jnp.float32
BF16 = jnp.bfloat16


def _params(*sem):
    return pltpu.CompilerParams(dimension_semantics=sem, vmem_limit_bytes=VMEM_LIMIT)


def _norm_permute_kernel(x_ref, g_ref, o_ref, *, tc, n):
    x = x_ref[...]
    ms = jnp.mean(x * x, axis=-1, keepdims=True)
    xn = (x * lax.rsqrt(ms + EPS) * g_ref[...]).astype(BF16)
    rows = n * tc
    dst = lax.broadcasted_iota(jnp.int32, (rows, rows), 0)
    src = lax.broadcasted_iota(jnp.int32, (rows, rows), 1)
    sel = (src == tc * (dst % n) + dst // n).astype(BF16)
    o_ref[...] = jnp.dot(sel, xn, preferred_element_type=F32).astype(o_ref.dtype).reshape(o_ref.shape)


def _norm_permute(x2d, g, tc, n):
    t, d = x2d.shape
    nk = t // tc
    return pl.pallas_call(
        functools.partial(_norm_permute_kernel, tc=tc, n=n),
        grid=(nk // n,),
        in_specs=[pl.BlockSpec((n * tc, d), lambda i: (i, 0)),
                  pl.BlockSpec((1, d), lambda i: (0, 0))],
        out_specs=pl.BlockSpec((tc, n, d), lambda i: (0, i, 0)),
        out_shape=jax.ShapeDtypeStruct((tc, nk, d), BF16),
        compiler_params=_params("parallel"),
    )(x2d, g.reshape(1, d))


def _branch_a_kernel(xn_ref, wu_ref, wv_ref, wg_ref, gv_ref, ws_ref, bmix_ref, oa_ref, *rest,
                     tc, n, r, hd):
    *v_refs, w_ref, wm_ref = rest
    d = xn_ref.shape[-1]

    @pl.when(pl.program_id(1) == 0)
    def _():
        w_ref[:, :hd] = wu_ref[...].astype(BF16)
        w_ref[:, hd:2 * hd] = wv_ref[...].astype(BF16)
        w_ref[:, 2 * hd:] = wg_ref[...].astype(BF16)
        row = lax.broadcasted_iota(jnp.int32, (CHUNK, CHUNK), 0)
        col = lax.broadcasted_iota(jnp.int32, (CHUNK, CHUNK), 1)
        w_tril = jnp.where(row >= col, ws_ref[...], 0.0).astype(BF16)
        if tc == TC_PROMPT:
            time_of_row = tc * (row % r) + row // r
            same_seq = None
        else:
            time_of_row = row // r
            same_seq = (row % r) == (col % r)
        sel = (col == time_of_row).astype(BF16)
        sw = jnp.dot(sel, w_tril, preferred_element_type=F32).astype(BF16)
        wm = lax.dot_general(sw, sel, (((1,), (1,)), ((), ())), preferred_element_type=F32)
        if same_seq is not None:
            wm = jnp.where(same_seq, wm, 0.0)
        wm_ref[...] = wm.astype(BF16)

    z = jnp.dot(xn_ref[...].reshape(tc * n, d), w_ref[...], preferred_element_type=F32)
    u = jax.nn.gelu(z[:, :hd])
    v = jax.nn.gelu(z[:, hd:2 * hd])
    ga = z[:, 2 * hd:]
    ms = jnp.mean(v * v, axis=-1, keepdims=True)
    vn3 = (v * lax.rsqrt(ms + EPS) * gv_ref[...]).reshape(tc, n, hd)
    pre3 = (u * jax.nn.silu(ga)).reshape(tc, n, hd)
    if v_refs:
        v_refs[0][...] = vn3
    wm = wm_ref[...]
    bm = bmix_ref[...]
    for c in range(n // r):
        vg = vn3[:, c * r:(c + 1) * r, :].reshape(CHUNK, hd)
        mixed = jnp.dot(wm, vg.astype(BF16), preferred_element_type=F32) + bm
        og = pre3[:, c * r:(c + 1) * r, :].reshape(CHUNK, hd) * mixed
        oa_ref[:, c * r:(c + 1) * r, :] = og.reshape(tc, r, hd).astype(oa_ref.dtype)


def _branch_a(xn, w_in, gv, w_s, bmix, n, want_v):
    tc, nk, d = xn.shape
    nh, _, hd = gv.shape
    r = CHUNK // tc
    out_map = lambda h, i: (0, i, h)
    out_shape = [jax.ShapeDtypeStruct((tc, nk, nh * hd), BF16)]
    out_specs = [pl.BlockSpec((tc, n, hd), out_map)]
    if want_v:
        out_shape.append(jax.ShapeDtypeStruct((tc, nk, nh * hd), F32))
        out_specs.append(pl.BlockSpec((tc, n, hd), out_map))
    res = pl.pallas_call(
        functools.partial(_branch_a_kernel, tc=tc, n=n, r=r, hd=hd),
        grid=(nh, nk // n),
        in_specs=[pl.BlockSpec((tc, n, d), lambda h, i: (0, i, 0)),
                  pl.BlockSpec((d, hd), lambda h, i: (0, h)),
                  pl.BlockSpec((d, hd), lambda h, i: (0, nh + h)),
                  pl.BlockSpec((d, hd), lambda h, i: (0, 2 * nh + h)),
                  pl.BlockSpec((None, 1, hd), lambda h, i: (h, 0, 0)),
                  pl.BlockSpec((None, CHUNK, CHUNK), lambda h, i: (h, 0, 0)),
                  pl.BlockSpec((None, CHUNK, hd), lambda h, i: (h, 0, 0))],
        out_specs=out_specs,
        out_shape=out_shape,
        scratch_shapes=[pltpu.VMEM((d, 3 * hd), BF16), pltpu.VMEM((CHUNK, CHUNK), BF16)],
        compiler_params=_params("parallel", "arbitrary"),
    )(xn, w_in, w_in, w_in, gv, w_s, bmix)
    return res if want_v else (res[0], None)


def _transpose_cast_kernel(w_ref, o_ref):
    o_ref[...] = w_ref[...].T.astype(o_ref.dtype)


def _transpose_cast(w, row_block, col_block0, n_rows, n_cols):
    cblk = 512
    return pl.pallas_call(
        _transpose_cast_kernel,
        grid=(n_cols // cblk,),
        in_specs=[pl.BlockSpec((n_rows, cblk), lambda i: (row_block, col_block0 + i))],
        out_specs=pl.BlockSpec((cblk, n_rows), lambda i: (i, 0)),
        out_shape=jax.ShapeDtypeStruct((n_cols, n_rows), BF16),
        compiler_params=_params("parallel"),
    )(w)


def _proj_b_kernel(w_ref, xn_ref, o_ref):
    xn = xn_ref[...]
    for m in range(w_ref.shape[0] // ROW_CHUNK):
        sl = slice(m * ROW_CHUNK, (m + 1) * ROW_CHUNK)
        o_ref[sl, :] = lax.dot_general(
            w_ref[sl, :], xn, (((1,), (1,)), ((), ())),
            preferred_element_type=F32).astype(o_ref.dtype)


def _proj_b(w_bt, xn, nkt):
    tc, nk, d = xn.shape
    rows = w_bt.shape[0]
    return pl.pallas_call(
        _proj_b_kernel,
        grid=(tc, nk // nkt),
        in_specs=[pl.BlockSpec((rows, d), lambda j, k: (0, 0), pipeline_mode=pl.Buffered(1)),
                  pl.BlockSpec((None, nkt, d), lambda j, k: (j, k, 0))],
        out_specs=pl.BlockSpec((None, rows, nkt), lambda j, k: (j, 0, k)),
        out_shape=jax.ShapeDtypeStruct((tc, rows, nk), BF16),
        compiler_params=_params("parallel", "parallel"),
    )(w_bt, xn)


def _shift_rows(x, d):
    if d % 8 == 0:
        return jnp.concatenate([jnp.zeros((d, x.shape[1]), x.dtype), x[:-d]], axis=0)
    row = lax.broadcasted_iota(jnp.int32, x.shape, 0)
    return jnp.where(row >= d, pltpu.roll(x, d, 0), 0.0)


def _ssm_kernel(x_ref, mp_ref, q_ref, ct_ref, dsk_ref, *rest, tc, gb, nk, fold, has_h0):
    if has_h0:
        h0r_ref, h0i_ref, y_ref, hfr_ref, hfi_ref = rest
    else:
        y_ref, hf_ref = rest
    rows = tc * GROUP
    half = q_ref.shape[2] // 2
    b_tc = tc.bit_length() - 1

    def one_group(g):
        r0 = pl.multiple_of(g * GROUP, GROUP)
        if fold == 1:
            xg = x_ref[:, pl.ds(r0, GROUP), :]
        else:
            xg = jnp.stack([x_ref[j // fold, pl.ds(r0, GROUP), (j % fold) * nk:(j % fold + 1) * nk]
                            for j in range(tc)])
        res = jnp.dot(mp_ref[g], xg.reshape(rows, nk), preferred_element_type=F32)
        y_local = res[:rows]
        s_re = res[rows:rows + half]
        s_im = res[rows + half:]
        ct = ct_ref[g]
        if has_h0:
            h_re, h_im = h0r_ref[g], h0i_ref[g]
            a_re, a_im = ct[:, b_tc:b_tc + 1], ct[:, POW_COLS + b_tc:POW_COLS + b_tc + 1]
            hfr_ref[g] = a_re * h_re - a_im * h_im + s_re
            hfi_ref[g] = a_re * h_im + a_im * h_re + s_im
            h_in = jnp.concatenate([h_re, h_im], axis=0)
        else:
            tiles = []
            for pair in range(nk // (2 * LANES)):
                lo, mid, hi = 2 * pair * LANES, (2 * pair + 1) * LANES, (2 * pair + 2) * LANES
                xr = jnp.concatenate([s_re[:, lo:mid], s_re[:, mid:hi]], axis=0).T
                xi = jnp.concatenate([s_im[:, lo:mid], s_im[:, mid:hi]], axis=0).T
                for s in range(SCAN_STEPS):
                    cr, ci = ct[s:s + 1], ct[SCAN_ROWS + s:SCAN_ROWS + s + 1]
                    rr, ri = _shift_rows(xr, 1 << s), _shift_rows(xi, 1 << s)
                    xr, xi = xr + cr * rr - ci * ri, xi + cr * ri + ci * rr
                hf_ref[g, 2 * pair:2 * pair + 1, :] = xr[LANES - 1:]
                hf_ref[g, 2 * pair + 1:2 * pair + 2, :] = xi[LANES - 1:]
                hr_t = _shift_rows(xr, 1).T
                hi_t = _shift_rows(xi, 1).T
                tiles.append(jnp.concatenate([hr_t[:half], hi_t[:half]], axis=0))
                tiles.append(jnp.concatenate([hr_t[half:], hi_t[half:]], axis=0))
            h_in = jnp.concatenate(tiles, axis=1)
        y = y_local + jnp.dot(q_ref[g], h_in.astype(BF16), preferred_element_type=F32)
        y3 = y.reshape(tc, GROUP, nk) + dsk_ref[pl.ds(r0, GROUP), :][None] * xg.astype(F32)
        ya = jax.nn.gelu(y3).astype(y_ref.dtype)
        if fold == 1:
            y_ref[:, pl.ds(r0, GROUP), :] = ya
        else:
            for j in range(tc):
                y_ref[j // fold, pl.ds(r0, GROUP), (j % fold) * nk:(j % fold + 1) * nk] = ya[j]

    def body(gi, carry):
        for k in range(GROUPS_PER_TRIP):
            one_group(gi * GROUPS_PER_TRIP + k)
        return carry

    lax.fori_loop(0, gb // GROUPS_PER_TRIP, body, 0)


def _ssm(zb, mp, q, ct, dsk, h0, gb, fold):
    tcf, _, nkf = zb.shape
    tc, nk = tcf * fold, nkf // fold
    ng, mrows, rows = mp.shape
    p2 = mrows - rows
    cb = ng * GROUP
    has_h0 = h0 is not None
    in_specs = [pl.BlockSpec((tcf, gb * GROUP, nkf), lambda i: (0, i, 0)),
                pl.BlockSpec((gb, mrows, rows), lambda i: (i, 0, 0)),
                pl.BlockSpec((gb, rows, p2), lambda i: (i, 0, 0)),
                pl.BlockSpec((gb,) + ct.shape[1:], lambda i: (i, 0, 0)),
                pl.BlockSpec((gb * GROUP, nk), lambda i: (i, 0))]
    args = [zb, mp, q, ct, dsk]
    if has_h0:
        state_spec = pl.BlockSpec((gb, p2 // 2, nk), lambda i: (i, 0, 0))
        in_specs += [state_spec, state_spec]
        args += list(h0)
        hf_specs = [state_spec, state_spec]
        hf_shapes = [jax.ShapeDtypeStruct((ng, p2 // 2, nk), F32)] * 2
    else:
        hf_specs = [pl.BlockSpec((gb, nk // LANES, LANES), lambda i: (i, 0, 0))]
        hf_shapes = [jax.ShapeDtypeStruct((ng, nk // LANES, LANES), F32)]
    res = pl.pallas_call(
        functools.partial(_ssm_kernel, tc=tc, gb=gb, nk=nk, fold=fold, has_h0=has_h0),
        grid=(ng // gb,),
        in_specs=in_specs,
        out_specs=[pl.BlockSpec((tcf, gb * GROUP, nkf), lambda i: (0, i, 0))] + hf_specs,
        out_shape=[jax.ShapeDtypeStruct((tcf, cb, nkf), BF16)] + hf_shapes,
        compiler_params=_params("parallel"),
    )(*args)
    return res[0], tuple(res[1:])


def _glu_out_kernel(y_ref, gate_ref, wg_ref, bg_ref, wo_ref, r_ref, ob_ref):
    y = y_ref[...]
    for m in range(wg_ref.shape[0] // ROW_CHUNK):
        sl = slice(m * ROW_CHUNK, (m + 1) * ROW_CHUNK)
        g = jnp.dot(wg_ref[sl, :], y, preferred_element_type=F32) + bg_ref[sl, :]
        ob = y_ref[sl, :].astype(F32) * jax.nn.sigmoid(g) * jax.nn.silu(gate_ref[sl, :].astype(F32))
        ob_ref[sl, :] = ob.astype(ob_ref.dtype)
    ob = ob_ref[...]
    for m in range(wo_ref.shape[0] // ROW_CHUNK):
        sl = slice(m * ROW_CHUNK, (m + 1) * ROW_CHUNK)
        r_ref[:, sl] = jnp.dot(wo_ref[sl, :], ob, preferred_element_type=F32).T


def _glu_out(y_t, zb, wg_t, bg, wo_t, nkt):
    tc, cb, nk = y_t.shape
    d = wo_t.shape[0]
    return pl.pallas_call(
        _glu_out_kernel,
        grid=(tc, nk // nkt),
        in_specs=[pl.BlockSpec((None, cb, nkt), lambda j, k: (j, 0, k)),
                  pl.BlockSpec((None, cb, nkt), lambda j, k: (j, 1, k)),
                  pl.BlockSpec((cb, cb), lambda j, k: (0, 0), pipeline_mode=pl.Buffered(1)),
                  pl.BlockSpec((cb, nkt), lambda j, k: (0, 0), pipeline_mode=pl.Buffered(1)),
                  pl.BlockSpec((d, cb), lambda j, k: (0, 0), pipeline_mode=pl.Buffered(1))],
        out_specs=pl.BlockSpec((None, nkt, d), lambda j, k: (j, k, 0)),
        out_shape=jax.ShapeDtypeStruct((tc, nk, d), F32),
        scratch_shapes=[pltpu.VMEM((cb, nkt), BF16)],
        compiler_params=_params("parallel", "parallel"),
    )(y_t, zb, wg_t, bg, wo_t)


def _out_norm_kernel(x_ref, oa_ref, r_ref, wo_ref, g_ref, y_ref, ms_ref, *, tc, n):
    ca = oa_ref.shape[-1]
    d = x_ref.shape[-1]
    nc = d // LANES
    p = jnp.dot(oa_ref[...].reshape(tc * n, ca), wo_ref[...], preferred_element_type=F32)
    for j in range(tc):
        s = p[j * n:(j + 1) * n] + r_ref[j]
        for c in range(nc):
            ms_ref[c, pl.ds(j, n, stride=tc), :] = s[:, c * LANES:(c + 1) * LANES]
    acc = x_ref[...] + jnp.concatenate([ms_ref[c] for c in range(nc)], axis=1)
    ms = jnp.mean(acc * acc, axis=-1, keepdims=True)
    y_ref[...] = acc * lax.rsqrt(ms + EPS) * g_ref[...]


def _out_norm(x2d, oa, r, wo_a, gf, n):
    tc, nk, ca = oa.shape
    t, d = x2d.shape
    return pl.pallas_call(
        functools.partial(_out_norm_kernel, tc=tc, n=n),
        grid=(nk // n,),
        in_specs=[pl.BlockSpec((n * tc, d), lambda i: (i, 0)),
                  pl.BlockSpec((tc, n, ca), lambda i: (0, i, 0)),
                  pl.BlockSpec((tc, n, d), lambda i: (0, i, 0)),
                  pl.BlockSpec((ca, d), lambda i: (0, 0), pipeline_mode=pl.Buffered(1)),
                  pl.BlockSpec((1, d), lambda i: (0, 0))],
        out_specs=pl.BlockSpec((n * tc, d), lambda i: (i, 0)),
        out_shape=jax.ShapeDtypeStruct((t, d), F32),
        scratch_shapes=[pltpu.VMEM((d // LANES, n * tc, LANES), F32)],
        compiler_params=_params("parallel"),
    )(x2d, oa, r, wo_a, gf.reshape(1, d))


def _discretise(a_re, a_im, log_dt):
    dt = jnp.exp(log_dt)[:, None]
    x_re, x_im = a_re * dt, a_im * dt
    n = (2.0 ** jnp.arange(N_POW, dtype=F32))[None, :, None]
    mag = jnp.exp(n * x_re[:, None, :])
    ang = n * x_im[:, None, :]
    pw_r, pw_i = mag * jnp.cos(ang), mag * jnp.sin(ang)
    lr, li = pw_r[:, 0], pw_i[:, 0]
    den = a_re * a_re + a_im * a_im
    nr, ni = lr - 1.0, li
    qr, qi = (nr * a_re + ni * a_im) / den, (ni * a_re - nr * a_im) / den
    cat = lambda a, b: jnp.concatenate([a, b], axis=-1)
    rowtab = jnp.stack([cat(lr, lr), cat(li, li), cat(qr, qi), cat(-qi, qr)], axis=1)
    zpad = jnp.zeros_like(pw_r[:, :POW_COLS - N_POW])
    lt = jnp.concatenate([pw_r, zpad, pw_i, zpad], axis=1).transpose(0, 2, 1)
    b0 = TC_PROMPT.bit_length() - 1
    sc_r, sc_i = pw_r[:, b0:b0 + SCAN_STEPS], pw_i[:, b0:b0 + SCAN_STEPS]
    zrow = jnp.zeros_like(cat(sc_r, sc_r)[:, :SCAN_ROWS - SCAN_STEPS])
    scan_rows = jnp.concatenate([cat(sc_r, sc_r), zrow, cat(sc_i, sc_i), zrow], axis=1)
    return rowtab, lt, scan_rows


def _ssm_tables_kernel(rt_ref, btr_ref, bti_ref, c2r_ref, c2i_ref, c2n_ref,
                       mp_ref, q_ref, mps_ref, qs_ref, *, gb, tc, tcs, unroll):
    rows = tc * GROUP
    rows_s = tcs * GROUP
    half = rt_ref.shape[2] // 2
    lane_k = lax.broadcasted_iota(jnp.int32, (GROUP, rows), 1)
    lane_q = lax.broadcasted_iota(jnp.int32, (1, 2 * half), 1)

    def one_group(g):
        rt = rt_ref[g]
        l_r, l_i, qu, qv = rt[0:1], rt[1:2], rt[2:3], rt[3:4]
        btr, bti = btr_ref[g], bti_ref[g]
        bb = btr * qu + bti * qv
        bs = btr * qv - bti * qu
        c2r, c2i = c2r_ref[g], c2i_ref[g]
        cur_r = jnp.ones((1, 2 * half), F32)
        cur_i = jnp.zeros((1, 2 * half), F32)
        et_blocks = []
        for n in range(tc):
            et_blocks.append(cur_r * bb + cur_i * bs)
            cur_r, cur_i = cur_r * l_r - cur_i * l_i, cur_r * l_i + cur_i * l_r
            u = jnp.where(lane_q < half, cur_r, -cur_i)
            v = jnp.where(lane_q < half, -cur_i, -cur_r)
            blk = (c2r * u + c2i * v).astype(q_ref.dtype)
            q_ref[g, n * GROUP:(n + 1) * GROUP, :] = blk
            if n < tcs:
                qs_ref[g, n * GROUP:(n + 1) * GROUP, :] = blk
        p = jnp.concatenate(et_blocks[::-1], axis=0).T
        mp_ref[g, rows:, :] = p.astype(mp_ref.dtype)
        mps_ref[g, rows_s:, :] = p[:, rows - rows_s:].astype(mps_ref.dtype)
        krow = jnp.dot(c2n_ref[g], p, precision=lax.Precision.HIGHEST, preferred_element_type=F32)
        for i in range(tc):
            sh = (tc - 1 - i) * GROUP
            blk = krow if sh == 0 else pltpu.roll(krow, rows - sh, 1)
            blk = jnp.where(lane_k < (i + 1) * GROUP, blk, 0.0)
            mp_ref[g, i * GROUP:(i + 1) * GROUP, :] = blk.astype(mp_ref.dtype)
            if i < tcs:
                mps_ref[g, i * GROUP:(i + 1) * GROUP, :] = blk[:, :rows_s].astype(mps_ref.dtype)

    def body(gi, carry):
        for k in range(unroll):
            one_group(gi * unroll + k)
        return carry

    lax.fori_loop(0, gb // unroll, body, 0)


def _ssm_tables(rowtab, b_re, b_im, c_re, c_im, tcs, gb):
    ng, _, p2 = rowtab.shape
    tc = TC_PROMPT
    rows, rows_s = tc * GROUP, tcs * GROUP
    dup = lambda a: jnp.concatenate([a, a], axis=-1)
    args = (rowtab, dup(b_re.transpose(0, 2, 1)), dup(b_im.transpose(0, 2, 1)),
            dup(c_re), dup(c_im), jnp.concatenate([c_re, -c_im], axis=-1))
    shapes = [(ng, rows + p2, rows), (ng, rows, p2), (ng, rows_s + p2, rows_s), (ng, rows_s, p2)]
    return pl.pallas_call(
        functools.partial(_ssm_tables_kernel, gb=gb, tc=tc, tcs=tcs, unroll=4),
        grid=(ng // gb,),
        in_specs=[pl.BlockSpec((gb,) + a.shape[1:], lambda i: (i, 0, 0)) for a in args],
        out_specs=[pl.BlockSpec((gb,) + s[1:], lambda i: (i, 0, 0)) for s in shapes],
        out_shape=[jax.ShapeDtypeStruct(s, BF16) for s in shapes],
        compiler_params=_params("parallel"),
    )(*args)


def _mix_bias(b_s, hd, tc):
    nh = b_s.shape[0]
    r = CHUNK // tc
    if tc == TC_PROMPT:
        bm = b_s.reshape(nh, r, tc).transpose(0, 2, 1).reshape(nh, CHUNK)
    else:
        bm = jnp.repeat(b_s[:, :tc], r, axis=1)
    return jnp.broadcast_to(bm[:, :, None], (nh, CHUNK, hd))


def _layer(x2d, tc, h0, wts, tiles):
    (g_norm, w_a, gv, wmix, bmix, w_bt, mp, q, ct, d_skip, wg_t, b_glu, wo_a, wo_bt, g_final) = wts
    n_norm, n_a, nkt, gb, fold = tiles
    cb = wg_t.shape[0]
    d = x2d.shape[1]
    nk = x2d.shape[0] // tc
    xn = _norm_permute(x2d, g_norm, tc, n_norm)
    out_a, v_rows = _branch_a(xn, w_a, gv, wmix, bmix, n_a, h0 is not None)
    zb = _proj_b(w_bt, xn.reshape(tc // fold, fold * nk, d), nkt)
    dsk = jnp.broadcast_to(d_skip[:, None], (cb, nk))
    y_t, h_fin = _ssm(zb, mp, q, ct, dsk, h0, gb, fold)
    bg = jnp.broadcast_to(b_glu[:, None], (cb, nkt))
    r = _glu_out(y_t, zb, wg_t, bg, wo_bt, nkt).reshape(tc, nk, d)
    y = _out_norm(x2d, out_a, r, wo_a, g_final, n_norm)
    return y, h_fin, v_rows


def kernel(x_prompt, x_sample, state_ssm_re, state_ssm_im, g_norm, w_in, g_v, w_s, b_s,
           a_re, a_im, log_dt, b_re, b_im, c_re, c_im, d_skip, w_glu, b_glu, w_out, g_final):
    depth = g_norm.shape[0]
    assert depth == 1, "single-layer step"
    bsz, seq, d = x_prompt.shape
    dbsz, dseq, _ = x_sample.shape
    nh = w_s.shape[1]
    ng, npst = a_re.shape[1:]
    cb = ng * GROUP
    ca = (w_in.shape[2] - 2 * cb) // 3
    hd = ca // nh
    assert seq // TC_PROMPT == LANES and dbsz == LANES and CHUNK % dseq == 0
    assert dseq & (dseq - 1) == 0 and dseq <= TC_PROMPT, "sample chunk must be a power of two"
    assert 3 * ca % 512 == 0 and ca == cb == d

    l = 0
    w_in_l = w_in[l]
    w_bt = _transpose_cast(w_in_l, 0, 3 * ca // 512, d, 2 * cb)
    gv = g_v[l].reshape(nh, 1, hd)
    rowtab, lt, scan_rows = _discretise(a_re[l], a_im[l], log_dt[l])
    mp, q, mp_s, q_s = _ssm_tables(rowtab, b_re[l], b_im[l], c_re[l], c_im[l], dseq, 8)
    wg_t = _transpose_cast(w_glu[l], 0, 0, cb, cb)
    wo_a = w_out[l][:ca].astype(BF16)
    wo_bt = _transpose_cast(w_out[l], 1, 0, cb, d)

    def weights(tc, mp_, q_, ct):
        return (g_norm[l], w_in_l, gv, w_s[l], _mix_bias(b_s[l], hd, tc), w_bt, mp_, q_, ct,
                d_skip[l], wg_t, b_glu[l], wo_a, wo_bt, g_final)

    assert bsz % 2 == 0
    tc_p, nk_p = TC_PROMPT, bsz * seq // TC_PROMPT
    y_p, hf_p, _ = _layer(x_prompt.reshape(nk_p * tc_p, d), tc_p, None,
                          weights(tc_p, mp, q, scan_rows), (32, 64, 256, 8, 1))
    hf_p = hf_p[0].reshape(ng, bsz // 2, 2, 2, npst)
    re_p = hf_p[:, :, 0].reshape(ng, bsz, npst).transpose(1, 0, 2)[None]
    im_p = hf_p[:, :, 1].reshape(ng, bsz, npst).transpose(1, 0, 2)[None]

    tc_s, nk_s = dseq, dbsz
    h0 = (state_ssm_re[l].transpose(1, 2, 0), state_ssm_im[l].transpose(1, 2, 0))
    y_s, hf_s, v_s = _layer(x_sample.reshape(nk_s * tc_s, d), tc_s, h0,
                            weights(tc_s, mp_s, q_s, lt), (64, 128, 256, 8, 2))
    re_s = hf_s[0].transpose(2, 0, 1)[None]
    im_s = hf_s[1].transpose(2, 0, 1)[None]
    v_out = v_s.transpose(1, 0, 2)[None]

    dt_p, dt_s = x_prompt.dtype, x_sample.dtype
    return (y_p.reshape(bsz, seq, d).astype(dt_p), y_s.reshape(dbsz, dseq, d).astype(dt_s),
            re_p.astype(dt_p), im_p.astype(dt_p), re_s.astype(dt_s), im_s.astype(dt_s),
            v_out.astype(dt_s))
```

```python
import functools

import jax
import jax.numpy as jnp
from jax import lax
from jax.experimental import pallas as pl
from jax.experimental.pallas import tpu as pltpu

EPS = 1e-6
LANES = 128
CHUNK = 128
GROUP = 16
TC_PROMPT = 16
SCAN_STEPS = 7
ROW_CHUNK = 512
LANE_TILE = 256
POW_COLS = 16
N_POW = 11
SCAN_ROWS = 8
GROUPS_PER_TRIP = 2
VMEM_LIMIT = 56 * 1024 * 1024

F32 = jnp.float32
BF16 = jnp.bfloat16


def _params(*sem):
    return pltpu.CompilerParams(dimension_semantics=sem, vmem_limit_bytes=VMEM_LIMIT)


def _norm_permute_kernel(x_ref, g_ref, o_ref, *, tc, n):
    x = x_ref[...]
    ms = jnp.mean(x * x, axis=-1, keepdims=True)
    xn = (x * lax.rsqrt(ms + EPS) * g_ref[...]).astype(BF16)
    rows = n * tc
    dst = lax.broadcasted_iota(jnp.int32, (rows, rows), 0)
    src = lax.broadcasted_iota(jnp.int32, (rows, rows), 1)
    sel = (src == tc * (dst % n) + dst // n).astype(BF16)
    o_ref[...] = jnp.dot(sel, xn, preferred_element_type=F32).astype(o_ref.dtype).reshape(o_ref.shape)


def _norm_permute(x2d, g, tc, n):
    t, d = x2d.shape
    nk = t // tc
    return pl.pallas_call(
        functools.partial(_norm_permute_kernel, tc=tc, n=n),
        grid=(nk // n,),
        in_specs=[pl.BlockSpec((n * tc, d), lambda i: (i, 0)),
                  pl.BlockSpec((1, d), lambda i: (0, 0))],
        out_specs=pl.BlockSpec((tc, n, d), lambda i: (0, i, 0)),
        out_shape=jax.ShapeDtypeStruct((tc, nk, d), BF16),
        compiler_params=_params("parallel"),
    )(x2d, g.reshape(1, d))


def _branch_a_kernel(xn_ref, wu_ref, wv_ref, wg_ref, gv_ref, ws_ref, bmix_ref, oa_ref, *rest,
                     tc, n, r, hd):
    *v_refs, w_ref, wm_ref = rest
    d = xn_ref.shape[-1]

    @pl.when(pl.program_id(1) == 0)
    def _():
        w_ref[:, :hd] = wu_ref[...].astype(BF16)
        w_ref[:, hd:2 * hd] = wv_ref[...].astype(BF16)
        w_ref[:, 2 * hd:] = wg_ref[...].astype(BF16)
        row = lax.broadcasted_iota(jnp.int32, (CHUNK, CHUNK), 0)
        col = lax.broadcasted_iota(jnp.int32, (CHUNK, CHUNK), 1)
        w_tril = jnp.where(row >= col, ws_ref[...], 0.0).astype(BF16)
        if tc == TC_PROMPT:
            time_of_row = tc * (row % r) + row // r
            same_seq = None
        else:
            time_of_row = row // r
            same_seq = (row % r) == (col % r)
        sel = (col == time_of_row).astype(BF16)
        sw = jnp.dot(sel, w_tril, preferred_element_type=F32).astype(BF16)
        wm = lax.dot_general(sw, sel, (((1,), (1,)), ((), ())), preferred_element_type=F32)
        if same_seq is not None:
            wm = jnp.where(same_seq, wm, 0.0)
        wm_ref[...] = wm.astype(BF16)

    z = jnp.dot(xn_ref[...].reshape(tc * n, d), w_ref[...], preferred_element_type=F32)
    u = jax.nn.gelu(z[:, :hd])
    v = jax.nn.gelu(z[:, hd:2 * hd])
    ga = z[:, 2 * hd:]
    ms = jnp.mean(v * v, axis=-1, keepdims=True)
    vn3 = (v * lax.rsqrt(ms + EPS) * gv_ref[...]).reshape(tc, n, hd)
    pre3 = (u * jax.nn.silu(ga)).reshape(tc, n, hd)
    if v_refs:
        v_refs[0][...] = vn3
    wm = wm_ref[...]
    bm = bmix_ref[...]
    for c in range(n // r):
        vg = vn3[:, c * r:(c + 1) * r, :].reshape(CHUNK, hd)
        mixed = jnp.dot(wm, vg.astype(BF16), preferred_element_type=F32) + bm
        og = pre3[:, c * r:(c + 1) * r, :].reshape(CHUNK, hd) * mixed
        oa_ref[:, c * r:(c + 1) * r, :] = og.reshape(tc, r, hd).astype(oa_ref.dtype)


def _branch_a(xn, w_in, gv, w_s, bmix, n, want_v):
    tc, nk, d = xn.shape
    nh, _, hd = gv.shape
    r = CHUNK // tc
    out_map = lambda h, i: (0, i, h)
    out_shape = [jax.ShapeDtypeStruct((tc, nk, nh * hd), BF16)]
    out_specs = [pl.BlockSpec((tc, n, hd), out_map)]
    if want_v:
        out_shape.append(jax.ShapeDtypeStruct((tc, nk, nh * hd), F32))
        out_specs.append(pl.BlockSpec((tc, n, hd), out_map))
    res = pl.pallas_call(
        functools.partial(_branch_a_kernel, tc=tc, n=n, r=r, hd=hd),
        grid=(nh, nk // n),
        in_specs=[pl.BlockSpec((tc, n, d), lambda h, i: (0, i, 0)),
                  pl.BlockSpec((d, hd), lambda h, i: (0, h)),
                  pl.BlockSpec((d, hd), lambda h, i: (0, nh + h)),
                  pl.BlockSpec((d, hd), lambda h, i: (0, 2 * nh + h)),
                  pl.BlockSpec((None, 1, hd), lambda h, i: (h, 0, 0)),
                  pl.BlockSpec((None, CHUNK, CHUNK), lambda h, i: (h, 0, 0)),
                  pl.BlockSpec((None, CHUNK, hd), lambda h, i: (h, 0, 0))],
        out_specs=out_specs,
        out_shape=out_shape,
        scratch_shapes=[pltpu.VMEM((d, 3 * hd), BF16), pltpu.VMEM((CHUNK, CHUNK), BF16)],
        compiler_params=_params("parallel", "arbitrary"),
    )(xn, w_in, w_in, w_in, gv, w_s, bmix)
    return res if want_v else (res[0], None)


def _transpose_cast_kernel(w_ref, o_ref):
    o_ref[...] = w_ref[...].T.astype(o_ref.dtype)


def _transpose_cast(w, row_block, col_block0, n_rows, n_cols):
    cblk = 512
    return pl.pallas_call(
        _transpose_cast_kernel,
        grid=(n_cols // cblk,),
        in_specs=[pl.BlockSpec((n_rows, cblk), lambda i: (row_block, col_block0 + i))],
        out_specs=pl.BlockSpec((cblk, n_rows), lambda i: (i, 0)),
        out_shape=jax.ShapeDtypeStruct((n_cols, n_rows), BF16),
        compiler_params=_params("parallel"),
    )(w)


def _proj_b_kernel(w_ref, xn_ref, o_ref):
    for t in range(xn_ref.shape[0] // LANE_TILE):
        ls = slice(t * LANE_TILE, (t + 1) * LANE_TILE)
        xn = xn_ref[ls, :]
        for m in range(w_ref.shape[0] // ROW_CHUNK):
            sl = slice(m * ROW_CHUNK, (m + 1) * ROW_CHUNK)
            o_ref[sl, ls] = lax.dot_general(
                w_ref[sl, :], xn, (((1,), (1,)), ((), ())),
                preferred_element_type=F32).astype(o_ref.dtype)


def _proj_b(w_bt, xn, nkt):
    tc, nk, d = xn.shape
    rows = w_bt.shape[0]
    return pl.pallas_call(
        _proj_b_kernel,
        grid=(tc, nk // nkt),
        in_specs=[pl.BlockSpec((rows, d), lambda j, k: (0, 0), pipeline_mode=pl.Buffered(1)),
                  pl.BlockSpec((None, nkt, d), lambda j, k: (j, k, 0))],
        out_specs=pl.BlockSpec((None, rows, nkt), lambda j, k: (j, 0, k)),
        out_shape=jax.ShapeDtypeStruct((tc, rows, nk), BF16),
        compiler_params=_params("parallel", "parallel"),
    )(w_bt, xn)


def _shift_rows(x, d):
    if d % 8 == 0:
        return jnp.concatenate([jnp.zeros((d, x.shape[1]), x.dtype), x[:-d]], axis=0)
    row = lax.broadcasted_iota(jnp.int32, x.shape, 0)
    return jnp.where(row >= d, pltpu.roll(x, d, 0), 0.0)


def _ssm_kernel(x_ref, mp_ref, q_ref, ct_ref, dsk_ref, *rest, tc, gb, nk, fold, has_h0):
    if has_h0:
        h0r_ref, h0i_ref, y_ref, hfr_ref, hfi_ref = rest
    else:
        y_ref, hf_ref = rest
    rows = tc * GROUP
    half = q_ref.shape[2] // 2
    b_tc = tc.bit_length() - 1

    def one_group(g):
        r0 = pl.multiple_of(g * GROUP, GROUP)
        if fold == 1:
            xg = x_ref[:, pl.ds(r0, GROUP), :]
        else:
            xg = jnp.stack([x_ref[j // fold, pl.ds(r0, GROUP), (j % fold) * nk:(j % fold + 1) * nk]
                            for j in range(tc)])
        res = jnp.dot(mp_ref[g], xg.reshape(rows, nk), preferred_element_type=F32)
        y_local = res[:rows]
        s_re = res[rows:rows + half]
        s_im = res[rows + half:]
        ct = ct_ref[g]
        if has_h0:
            h_re, h_im = h0r_ref[g], h0i_ref[g]
            a_re, a_im = ct[:, b_tc:b_tc + 1], ct[:, POW_COLS + b_tc:POW_COLS + b_tc + 1]
            hfr_ref[g] = a_re * h_re - a_im * h_im + s_re
            hfi_ref[g] = a_re * h_im + a_im * h_re + s_im
            h_in = jnp.concatenate([h_re, h_im], axis=0)
        else:
            tiles = []
            for pair in range(nk // (2 * LANES)):
                lo, mid, hi = 2 * pair * LANES, (2 * pair + 1) * LANES, (2 * pair + 2) * LANES
                xr = jnp.concatenate([s_re[:, lo:mid], s_re[:, mid:hi]], axis=0).T
                xi = jnp.concatenate([s_im[:, lo:mid], s_im[:, mid:hi]], axis=0).T
                for s in range(SCAN_STEPS):
                    cr, ci = ct[s:s + 1], ct[SCAN_ROWS + s:SCAN_ROWS + s + 1]
                    rr, ri = _shift_rows(xr, 1 << s), _shift_rows(xi, 1 << s)
                    xr, xi = xr + cr * rr - ci * ri, xi + cr * ri + ci * rr
                hf_ref[g, 2 * pair:2 * pair + 1, :] = xr[LANES - 1:]
                hf_ref[g, 2 * pair + 1:2 * pair + 2, :] = xi[LANES - 1:]
                hr_t = _shift_rows(xr, 1).T
                hi_t = _shift_rows(xi, 1).T
                tiles.append(jnp.concatenate([hr_t[:half], hi_t[:half]], axis=0))
                tiles.append(jnp.concatenate([hr_t[half:], hi_t[half:]], axis=0))
            h_in = jnp.concatenate(tiles, axis=1)
        y = y_local + jnp.dot(q_ref[g], h_in.astype(BF16), preferred_element_type=F32)
        y3 = y.reshape(tc, GROUP, nk) + dsk_ref[pl.ds(r0, GROUP), :][None] * xg.astype(F32)
        ya = jax.nn.gelu(y3).astype(y_ref.dtype)
        if fold == 1:
            y_ref[:, pl.ds(r0, GROUP), :] = ya
        else:
            for j in range(tc):
                y_ref[j // fold, pl.ds(r0, GROUP), (j % fold) * nk:(j % fold + 1) * nk] = ya[j]

    def body(gi, carry):
        for k in range(GROUPS_PER_TRIP):
            one_group(gi * GROUPS_PER_TRIP + k)
        return carry

    lax.fori_loop(0, gb // GROUPS_PER_TRIP, body, 0)


def _ssm(zb, mp, q, ct, dsk, h0, gb, fold):
    tcf, _, nkf = zb.shape
    tc, nk = tcf * fold, nkf // fold
    ng, mrows, rows = mp.shape
    p2 = mrows - rows
    cb = ng * GROUP
    has_h0 = h0 is not None
    in_specs = [pl.BlockSpec((tcf, gb * GROUP, nkf), lambda i: (0, i, 0)),
                pl.BlockSpec((gb, mrows, rows), lambda i: (i, 0, 0)),
                pl.BlockSpec((gb, rows, p2), lambda i: (i, 0, 0)),
                pl.BlockSpec((gb,) + ct.shape[1:], lambda i: (i, 0, 0)),
                pl.BlockSpec((gb * GROUP, nk), lambda i: (i, 0))]
    args = [zb, mp, q, ct, dsk]
    if has_h0:
        state_spec = pl.BlockSpec((gb, p2 // 2, nk), lambda i: (i, 0, 0))
        in_specs += [state_spec, state_spec]
        args += list(h0)
        hf_specs = [state_spec, state_spec]
        hf_shapes = [jax.ShapeDtypeStruct((ng, p2 // 2, nk), F32)] * 2
    else:
        hf_specs = [pl.BlockSpec((gb, nk // LANES, LANES), lambda i: (i, 0, 0))]
        hf_shapes = [jax.ShapeDtypeStruct((ng, nk // LANES, LANES), F32)]
    res = pl.pallas_call(
        functools.partial(_ssm_kernel, tc=tc, gb=gb, nk=nk, fold=fold, has_h0=has_h0),
        grid=(ng // gb,),
        in_specs=in_specs,
        out_specs=[pl.BlockSpec((tcf, gb * GROUP, nkf), lambda i: (0, i, 0))] + hf_specs,
        out_shape=[jax.ShapeDtypeStruct((tcf, cb, nkf), BF16)] + hf_shapes,
        compiler_params=_params("parallel"),
    )(*args)
    return res[0], tuple(res[1:])


def _glu_out_kernel(y_ref, gate_ref, wg_ref, bg_ref, wo_ref, r_ref, *ob_refs):
    for t, ob_ref in enumerate(ob_refs):
        ls = slice(t * LANE_TILE, (t + 1) * LANE_TILE)
        y = y_ref[:, ls]
        for m in range(wg_ref.shape[0] // ROW_CHUNK):
            sl = slice(m * ROW_CHUNK, (m + 1) * ROW_CHUNK)
            g = jnp.dot(wg_ref[sl, :], y, preferred_element_type=F32) + bg_ref[sl, :]
            ob = (y_ref[sl, ls].astype(F32) * jax.nn.sigmoid(g)
                  * jax.nn.silu(gate_ref[sl, ls].astype(F32)))
            ob_ref[sl, :] = ob.astype(ob_ref.dtype)
    for t, ob_ref in enumerate(ob_refs):
        ob = ob_ref[...]
        for m in range(wo_ref.shape[0] // ROW_CHUNK):
            sl = slice(m * ROW_CHUNK, (m + 1) * ROW_CHUNK)
            r_ref[t * LANE_TILE:(t + 1) * LANE_TILE, sl] = jnp.dot(
                wo_ref[sl, :], ob, preferred_element_type=F32).T


def _glu_out(y_t, zb, wg_t, bg, wo_t, nkt):
    tc, cb, nk = y_t.shape
    d = wo_t.shape[0]
    return pl.pallas_call(
        _glu_out_kernel,
        grid=(tc, nk // nkt),
        in_specs=[pl.BlockSpec((None, cb, nkt), lambda j, k: (j, 0, k)),
                  pl.BlockSpec((None, cb, nkt), lambda j, k: (j, 1, k)),
                  pl.BlockSpec((cb, cb), lambda j, k: (0, 0), pipeline_mode=pl.Buffered(1)),
                  pl.BlockSpec((cb, LANE_TILE), lambda j, k: (0, 0), pipeline_mode=pl.Buffered(1)),
                  pl.BlockSpec((d, cb), lambda j, k: (0, 0), pipeline_mode=pl.Buffered(1))],
        out_specs=pl.BlockSpec((None, nkt, d), lambda j, k: (j, k, 0)),
        out_shape=jax.ShapeDtypeStruct((tc, nk, d), F32),
        scratch_shapes=[pltpu.VMEM((cb, LANE_TILE), BF16)] * (nkt // LANE_TILE),
        compiler_params=_params("parallel", "parallel"),
    )(y_t, zb, wg_t, bg, wo_t)


def _out_norm_kernel(x_ref, oa_ref, r_ref, wo_ref, g_ref, y_ref, ms_ref, *, tc, n):
    ca = oa_ref.shape[-1]
    d = x_ref.shape[-1]
    nc = d // LANES
    p = jnp.dot(oa_ref[...].reshape(tc * n, ca), wo_ref[...], preferred_element_type=F32)
    for j in range(tc):
        s = p[j * n:(j + 1) * n] + r_ref[j]
        for c in range(nc):
            ms_ref[c, pl.ds(j, n, stride=tc), :] = s[:, c * LANES:(c + 1) * LANES]
    acc = x_ref[...] + jnp.concatenate([ms_ref[c] for c in range(nc)], axis=1)
    ms = jnp.mean(acc * acc, axis=-1, keepdims=True)
    y_ref[...] = acc * lax.rsqrt(ms + EPS) * g_ref[...]


def _out_norm(x2d, oa, r, wo_a, gf, n):
    tc, nk, ca = oa.shape
    t, d = x2d.shape
    return pl.pallas_call(
        functools.partial(_out_norm_kernel, tc=tc, n=n),
        grid=(nk // n,),
        in_specs=[pl.BlockSpec((n * tc, d), lambda i: (i, 0)),
                  pl.BlockSpec((tc, n, ca), lambda i: (0, i, 0)),
                  pl.BlockSpec((tc, n, d), lambda i: (0, i, 0)),
                  pl.BlockSpec((ca, d), lambda i: (0, 0), pipeline_mode=pl.Buffered(1)),
                  pl.BlockSpec((1, d), lambda i: (0, 0))],
        out_specs=pl.BlockSpec((n * tc, d), lambda i: (i, 0)),
        out_shape=jax.ShapeDtypeStruct((t, d), F32),
        scratch_shapes=[pltpu.VMEM((d // LANES, n * tc, LANES), F32)],
        compiler_params=_params("parallel"),
    )(x2d, oa, r, wo_a, gf.reshape(1, d))


def _discretise(a_re, a_im, log_dt):
    dt = jnp.exp(log_dt)[:, None]
    x_re, x_im = a_re * dt, a_im * dt
    n = (2.0 ** jnp.arange(N_POW, dtype=F32))[None, :, None]
    mag = jnp.exp(n * x_re[:, None, :])
    ang = n * x_im[:, None, :]
    pw_r, pw_i = mag * jnp.cos(ang), mag * jnp.sin(ang)
    lr, li = pw_r[:, 0], pw_i[:, 0]
    den = a_re * a_re + a_im * a_im
    nr, ni = lr - 1.0, li
    qr, qi = (nr * a_re + ni * a_im) / den, (ni * a_re - nr * a_im) / den
    cat = lambda a, b: jnp.concatenate([a, b], axis=-1)
    rowtab = jnp.stack([cat(lr, lr), cat(li, li), cat(qr, qi), cat(-qi, qr)], axis=1)
    zpad = jnp.zeros_like(pw_r[:, :POW_COLS - N_POW])
    lt = jnp.concatenate([pw_r, zpad, pw_i, zpad], axis=1).transpose(0, 2, 1)
    b0 = TC_PROMPT.bit_length() - 1
    sc_r, sc_i = pw_r[:, b0:b0 + SCAN_STEPS], pw_i[:, b0:b0 + SCAN_STEPS]
    zrow = jnp.zeros_like(cat(sc_r, sc_r)[:, :SCAN_ROWS - SCAN_STEPS])
    scan_rows = jnp.concatenate([cat(sc_r, sc_r), zrow, cat(sc_i, sc_i), zrow], axis=1)
    return rowtab, lt, scan_rows


def _ssm_tables_kernel(rt_ref, btr_ref, bti_ref, c2r_ref, c2i_ref, c2n_ref,
                       mp_ref, q_ref, mps_ref, qs_ref, *, gb, tc, tcs, unroll):
    rows = tc * GROUP
    rows_s = tcs * GROUP
    half = rt_ref.shape[2] // 2
    lane_k = lax.broadcasted_iota(jnp.int32, (GROUP, rows), 1)
    lane_q = lax.broadcasted_iota(jnp.int32, (1, 2 * half), 1)

    def one_group(g):
        rt = rt_ref[g]
        l_r, l_i, qu, qv = rt[0:1], rt[1:2], rt[2:3], rt[3:4]
        btr, bti = btr_ref[g], bti_ref[g]
        bb = btr * qu + bti * qv
        bs = btr * qv - bti * qu
        c2r, c2i = c2r_ref[g], c2i_ref[g]
        cur_r = jnp.ones((1, 2 * half), F32)
        cur_i = jnp.zeros((1, 2 * half), F32)
        et_blocks = []
        for n in range(tc):
            et_blocks.append(cur_r * bb + cur_i * bs)
            cur_r, cur_i = cur_r * l_r - cur_i * l_i, cur_r * l_i + cur_i * l_r
            u = jnp.where(lane_q < half, cur_r, -cur_i)
            v = jnp.where(lane_q < half, -cur_i, -cur_r)
            blk = (c2r * u + c2i * v).astype(q_ref.dtype)
            q_ref[g, n * GROUP:(n + 1) * GROUP, :] = blk
            if n < tcs:
                qs_ref[g, n * GROUP:(n + 1) * GROUP, :] = blk
        p = jnp.concatenate(et_blocks[::-1], axis=0).T
        mp_ref[g, rows:, :] = p.astype(mp_ref.dtype)
        mps_ref[g, rows_s:, :] = p[:, rows - rows_s:].astype(mps_ref.dtype)
        krow = jnp.dot(c2n_ref[g], p, precision=lax.Precision.HIGHEST, preferred_element_type=F32)
        for i in range(tc):
            sh = (tc - 1 - i) * GROUP
            blk = krow if sh == 0 else pltpu.roll(krow, rows - sh, 1)
            blk = jnp.where(lane_k < (i + 1) * GROUP, blk, 0.0)
            mp_ref[g, i * GROUP:(i + 1) * GROUP, :] = blk.astype(mp_ref.dtype)
            if i < tcs:
                mps_ref[g, i * GROUP:(i + 1) * GROUP, :] = blk[:, :rows_s].astype(mps_ref.dtype)

    def body(gi, carry):
        for k in range(unroll):
            one_group(gi * unroll + k)
        return carry

    lax.fori_loop(0, gb // unroll, body, 0)


def _ssm_tables(rowtab, b_re, b_im, c_re, c_im, tcs, gb):
    ng, _, p2 = rowtab.shape
    tc = TC_PROMPT
    rows, rows_s = tc * GROUP, tcs * GROUP
    dup = lambda a: jnp.concatenate([a, a], axis=-1)
    args = (rowtab, dup(b_re.transpose(0, 2, 1)), dup(b_im.transpose(0, 2, 1)),
            dup(c_re), dup(c_im), jnp.concatenate([c_re, -c_im], axis=-1))
    shapes = [(ng, rows + p2, rows), (ng, rows, p2), (ng, rows_s + p2, rows_s), (ng, rows_s, p2)]
    return pl.pallas_call(
        functools.partial(_ssm_tables_kernel, gb=gb, tc=tc, tcs=tcs, unroll=4),
        grid=(ng // gb,),
        in_specs=[pl.BlockSpec((gb,) + a.shape[1:], lambda i: (i, 0, 0)) for a in args],
        out_specs=[pl.BlockSpec((gb,) + s[1:], lambda i: (i, 0, 0)) for s in shapes],
        out_shape=[jax.ShapeDtypeStruct(s, BF16) for s in shapes],
        compiler_params=_params("parallel"),
    )(*args)


def _mix_bias(b_s, hd, tc):
    nh = b_s.shape[0]
    r = CHUNK // tc
    if tc == TC_PROMPT:
        bm = b_s.reshape(nh, r, tc).transpose(0, 2, 1).reshape(nh, CHUNK)
    else:
        bm = jnp.repeat(b_s[:, :tc], r, axis=1)
    return jnp.broadcast_to(bm[:, :, None], (nh, CHUNK, hd))


def _layer(x2d, tc, h0, wts, tiles):
    (g_norm, w_a, gv, wmix, bmix, w_bt, mp, q, ct, d_skip, wg_t, b_glu, wo_a, wo_bt, g_final) = wts
    n_norm, n_a, nkt, gb, fold = tiles
    cb = wg_t.shape[0]
    d = x2d.shape[1]
    nk = x2d.shape[0] // tc
    xn = _norm_permute(x2d, g_norm, tc, n_norm)
    out_a, v_rows = _branch_a(xn, w_a, gv, wmix, bmix, n_a, h0 is not None)
    zb = _proj_b(w_bt, xn.reshape(tc // fold, fold * nk, d), nkt)
    dsk = jnp.broadcast_to(d_skip[:, None], (cb, nk))
    y_t, h_fin = _ssm(zb, mp, q, ct, dsk, h0, gb, fold)
    bg = jnp.broadcast_to(b_glu[:, None], (cb, LANE_TILE))
    r = _glu_out(y_t, zb, wg_t, bg, wo_bt, nkt).reshape(tc, nk, d)
    y = _out_norm(x2d, out_a, r, wo_a, g_final, n_norm)
    return y, h_fin, v_rows


def kernel(x_prompt, x_sample, state_ssm_re, state_ssm_im, g_norm, w_in, g_v, w_s, b_s,
           a_re, a_im, log_dt, b_re, b_im, c_re, c_im, d_skip, w_glu, b_glu, w_out, g_final):
    depth = g_norm.shape[0]
    assert depth == 1, "single-layer step"
    bsz, seq, d = x_prompt.shape
    dbsz, dseq, _ = x_sample.shape
    nh = w_s.shape[1]
    ng, npst = a_re.shape[1:]
    cb = ng * GROUP
    ca = (w_in.shape[2] - 2 * cb) // 3
    hd = ca // nh
    assert seq // TC_PROMPT == LANES and dbsz == LANES and CHUNK % dseq == 0
    assert dseq & (dseq - 1) == 0 and dseq <= TC_PROMPT, "sample chunk must be a power of two"
    assert 3 * ca % 512 == 0 and ca == cb == d

    l = 0
    w_in_l = w_in[l]
    w_bt = _transpose_cast(w_in_l, 0, 3 * ca // 512, d, 2 * cb)
    gv = g_v[l].reshape(nh, 1, hd)
    rowtab, lt, scan_rows = _discretise(a_re[l], a_im[l], log_dt[l])
    mp, q, mp_s, q_s = _ssm_tables(rowtab, b_re[l], b_im[l], c_re[l], c_im[l], dseq, 8)
    wg_t = _transpose_cast(w_glu[l], 0, 0, cb, cb)
    wo_a = w_out[l][:ca].astype(BF16)
    wo_bt = _transpose_cast(w_out[l], 1, 0, cb, d)

    def weights(tc, mp_, q_, ct):
        return (g_norm[l], w_in_l, gv, w_s[l], _mix_bias(b_s[l], hd, tc), w_bt, mp_, q_, ct,
                d_skip[l], wg_t, b_glu[l], wo_a, wo_bt, g_final)

    assert bsz % 2 == 0
    tc_p, nk_p = TC_PROMPT, bsz * seq // TC_PROMPT
    y_p, hf_p, _ = _layer(x_prompt.reshape(nk_p * tc_p, d), tc_p, None,
                          weights(tc_p, mp, q, scan_rows), (32, 64, 512, 8, 1))
    hf_p = hf_p[0].reshape(ng, bsz // 2, 2, 2, npst)
    re_p = hf_p[:, :, 0].reshape(ng, bsz, npst).transpose(1, 0, 2)[None]
    im_p = hf_p[:, :, 1].reshape(ng, bsz, npst).transpose(1, 0, 2)[None]

    tc_s, nk_s = dseq, dbsz
    h0 = (state_ssm_re[l].transpose(1, 2, 0), state_ssm_im[l].transpose(1, 2, 0))
    y_s, hf_s, v_s = _layer(x_sample.reshape(nk_s * tc_s, d), tc_s, h0,
                            weights(tc_s, mp_s, q_s, lt), (64, 128, 256, 8, 2))
    re_s = hf_s[0].transpose(2, 0, 1)[None]
    im_s = hf_s[1].transpose(2, 0, 1)[None]
    v_out = v_s.transpose(1, 0, 2)[None]

    dt_p, dt_s = x_prompt.dtype, x_sample.dtype
    return (y_p.reshape(bsz, seq, d).astype(dt_p), y_s.reshape(dbsz, dseq, d).astype(dt_s),
            re_p.astype(dt_p), im_p.astype(dt_p), re_s.astype(dt_s), im_s.astype(dt_s),
            v_out.astype(dt_s))
```

```python
import functools

import jax
import jax.numpy as jnp
from jax import lax
from jax.experimental import pallas as pl
from jax.experimental.pallas import tpu as pltpu

EPS = 1e-6
LANES = 128
CHUNK = 128
GROUP = 16
TC_PROMPT = 16
SCAN_STEPS = 7
ROW_CHUNK = 512
LANE_TILE = 256
PROJ_ROWS = 1024
POW_COLS = 16
N_POW = 11
SCAN_ROWS = 8
GROUPS_PER_TRIP = 2
VMEM_LIMIT = 56 * 1024 * 1024

F32 = jnp.float32
BF16 = jnp.bfloat16


def _params(*sem):
    return pltpu.CompilerParams(dimension_semantics=sem, vmem_limit_bytes=VMEM_LIMIT)


def _norm_permute_kernel(x_ref, g_ref, o_ref, *, tc, n):
    x = x_ref[...]
    ms = jnp.mean(x * x, axis=-1, keepdims=True)
    xn = (x * lax.rsqrt(ms + EPS) * g_ref[...]).astype(BF16)
    rows = n * tc
    dst = lax.broadcasted_iota(jnp.int32, (rows, rows), 0)
    src = lax.broadcasted_iota(jnp.int32, (rows, rows), 1)
    sel = (src == tc * (dst % n) + dst // n).astype(BF16)
    o_ref[...] = jnp.dot(sel, xn, preferred_element_type=F32).astype(o_ref.dtype).reshape(o_ref.shape)


def _norm_permute(x2d, g, tc, n):
    t, d = x2d.shape
    nk = t // tc
    return pl.pallas_call(
        functools.partial(_norm_permute_kernel, tc=tc, n=n),
        grid=(nk // n,),
        in_specs=[pl.BlockSpec((n * tc, d), lambda i: (i, 0)),
                  pl.BlockSpec((1, d), lambda i: (0, 0))],
        out_specs=pl.BlockSpec((tc, n, d), lambda i: (0, i, 0)),
        out_shape=jax.ShapeDtypeStruct((tc, nk, d), BF16),
        compiler_params=_params("parallel"),
    )(x2d, g.reshape(1, d))


def _branch_a_kernel(xn_ref, wu_ref, wv_ref, wg_ref, gv_ref, ws_ref, bmix_ref, oa_ref, *rest,
                     tc, n, r, hd, n_sub):
    *v_refs, w_ref, wm_ref = rest
    d = xn_ref.shape[-1]

    @pl.when(pl.program_id(1) == 0)
    def _():
        w_ref[:, :hd] = wu_ref[...].astype(BF16)
        w_ref[:, hd:2 * hd] = wv_ref[...].astype(BF16)
        w_ref[:, 2 * hd:] = wg_ref[...].astype(BF16)
        row = lax.broadcasted_iota(jnp.int32, (CHUNK, CHUNK), 0)
        col = lax.broadcasted_iota(jnp.int32, (CHUNK, CHUNK), 1)
        w_tril = jnp.where(row >= col, ws_ref[...], 0.0).astype(BF16)
        if tc == TC_PROMPT:
            time_of_row = tc * (row % r) + row // r
            same_seq = None
        else:
            time_of_row = row // r
            same_seq = (row % r) == (col % r)
        sel = (col == time_of_row).astype(BF16)
        sw = jnp.dot(sel, w_tril, preferred_element_type=F32).astype(BF16)
        wm = lax.dot_general(sw, sel, (((1,), (1,)), ((), ())), preferred_element_type=F32)
        if same_seq is not None:
            wm = jnp.where(same_seq, wm, 0.0)
        wm_ref[...] = wm.astype(BF16)

    ns = n // n_sub
    zs = [jnp.dot(xn_ref[:, k * ns:(k + 1) * ns, :].reshape(tc * ns, d), w_ref[...],
                  preferred_element_type=F32) for k in range(n_sub)]
    wm = wm_ref[...]
    bm = bmix_ref[...]
    for k, z in enumerate(zs):
        k0 = k * ns
        u = jax.nn.gelu(z[:, :hd])
        v = jax.nn.gelu(z[:, hd:2 * hd])
        ga = z[:, 2 * hd:]
        ms = jnp.mean(v * v, axis=-1, keepdims=True)
        vn3 = (v * lax.rsqrt(ms + EPS) * gv_ref[...]).reshape(tc, ns, hd)
        pre3 = (u * jax.nn.silu(ga)).reshape(tc, ns, hd)
        if v_refs:
            v_refs[0][:, k0:k0 + ns, :] = vn3
        for c in range(ns // r):
            vg = vn3[:, c * r:(c + 1) * r, :].reshape(CHUNK, hd)
            mixed = jnp.dot(wm, vg.astype(BF16), preferred_element_type=F32) + bm
            og = pre3[:, c * r:(c + 1) * r, :].reshape(CHUNK, hd) * mixed
            oa_ref[:, k0 + c * r:k0 + (c + 1) * r, :] = og.reshape(tc, r, hd).astype(oa_ref.dtype)


def _branch_a(xn, w_in, gv, w_s, bmix, n, n_sub, want_v):
    tc, nk, d = xn.shape
    nh, _, hd = gv.shape
    r = CHUNK // tc
    out_map = lambda h, i: (0, i, h)
    out_shape = [jax.ShapeDtypeStruct((tc, nk, nh * hd), BF16)]
    out_specs = [pl.BlockSpec((tc, n, hd), out_map)]
    if want_v:
        out_shape.append(jax.ShapeDtypeStruct((tc, nk, nh * hd), F32))
        out_specs.append(pl.BlockSpec((tc, n, hd), out_map))
    res = pl.pallas_call(
        functools.partial(_branch_a_kernel, tc=tc, n=n, r=r, hd=hd, n_sub=n_sub),
        grid=(nh, nk // n),
        in_specs=[pl.BlockSpec((tc, n, d), lambda h, i: (0, i, 0)),
                  pl.BlockSpec((d, hd), lambda h, i: (0, h)),
                  pl.BlockSpec((d, hd), lambda h, i: (0, nh + h)),
                  pl.BlockSpec((d, hd), lambda h, i: (0, 2 * nh + h)),
                  pl.BlockSpec((None, 1, hd), lambda h, i: (h, 0, 0)),
                  pl.BlockSpec((None, CHUNK, CHUNK), lambda h, i: (h, 0, 0)),
                  pl.BlockSpec((None, CHUNK, hd), lambda h, i: (h, 0, 0))],
        out_specs=out_specs,
        out_shape=out_shape,
        scratch_shapes=[pltpu.VMEM((d, 3 * hd), BF16), pltpu.VMEM((CHUNK, CHUNK), BF16)],
        compiler_params=_params("parallel", "arbitrary"),
    )(xn, w_in, w_in, w_in, gv, w_s, bmix)
    return res if want_v else (res[0], None)


def _transpose_cast_kernel(w_ref, o_ref):
    o_ref[...] = w_ref[...].T.astype(o_ref.dtype)


def _transpose_cast(w, row_block, col_block0, n_rows, n_cols):
    cblk = 512
    return pl.pallas_call(
        _transpose_cast_kernel,
        grid=(n_cols // cblk,),
        in_specs=[pl.BlockSpec((n_rows, cblk), lambda i: (row_block, col_block0 + i))],
        out_specs=pl.BlockSpec((cblk, n_rows), lambda i: (i, 0)),
        out_shape=jax.ShapeDtypeStruct((n_cols, n_rows), BF16),
        compiler_params=_params("parallel"),
    )(w)


def _proj_b_kernel(w_ref, xn_ref, o_ref):
    for t in range(xn_ref.shape[0] // LANE_TILE):
        ls = slice(t * LANE_TILE, (t + 1) * LANE_TILE)
        xn = xn_ref[ls, :]
        for m in range(w_ref.shape[0] // ROW_CHUNK):
            sl = slice(m * ROW_CHUNK, (m + 1) * ROW_CHUNK)
            o_ref[sl, ls] = lax.dot_general(
                w_ref[sl, :], xn, (((1,), (1,)), ((), ())),
                preferred_element_type=F32).astype(o_ref.dtype)


def _proj_b(w_bt, xn, nkt):
    tc, nk, d = xn.shape
    rows = w_bt.shape[0]
    return pl.pallas_call(
        _proj_b_kernel,
        grid=(tc, nk // nkt),
        in_specs=[pl.BlockSpec((rows, d), lambda j, k: (0, 0), pipeline_mode=pl.Buffered(1)),
                  pl.BlockSpec((None, nkt, d), lambda j, k: (j, k, 0))],
        out_specs=pl.BlockSpec((None, rows, nkt), lambda j, k: (j, 0, k)),
        out_shape=jax.ShapeDtypeStruct((tc, rows, nk), BF16),
        compiler_params=_params("parallel", "parallel"),
    )(w_bt, xn)


def _shift_rows(x, d):
    if d % 8 == 0:
        return jnp.concatenate([jnp.zeros((d, x.shape[1]), x.dtype), x[:-d]], axis=0)
    row = lax.broadcasted_iota(jnp.int32, x.shape, 0)
    return jnp.where(row >= d, pltpu.roll(x, d, 0), 0.0)


def _ssm_kernel(x_ref, mp_ref, q_ref, ct_ref, dsk_ref, *rest, tc, gb, nk, fold, has_h0):
    if has_h0:
        h0r_ref, h0i_ref, y_ref, hfr_ref, hfi_ref = rest
    else:
        y_ref, hf_ref = rest
    rows = tc * GROUP
    half = q_ref.shape[2] // 2
    b_tc = tc.bit_length() - 1

    def one_group(g):
        r0 = pl.multiple_of(g * GROUP, GROUP)
        if fold == 1:
            xg = x_ref[:, pl.ds(r0, GROUP), :]
        else:
            xg = jnp.stack([x_ref[j // fold, pl.ds(r0, GROUP), (j % fold) * nk:(j % fold + 1) * nk]
                            for j in range(tc)])
        res = jnp.dot(mp_ref[g], xg.reshape(rows, nk), preferred_element_type=F32)
        y_local = res[:rows]
        s_re = res[rows:rows + half]
        s_im = res[rows + half:]
        ct = ct_ref[g]
        if has_h0:
            h_re, h_im = h0r_ref[g], h0i_ref[g]
            a_re, a_im = ct[:, b_tc:b_tc + 1], ct[:, POW_COLS + b_tc:POW_COLS + b_tc + 1]
            hfr_ref[g] = a_re * h_re - a_im * h_im + s_re
            hfi_ref[g] = a_re * h_im + a_im * h_re + s_im
            h_in = jnp.concatenate([h_re, h_im], axis=0)
        else:
            tiles = []
            for pair in range(nk // (2 * LANES)):
                lo, mid, hi = 2 * pair * LANES, (2 * pair + 1) * LANES, (2 * pair + 2) * LANES
                xr = jnp.concatenate([s_re[:, lo:mid], s_re[:, mid:hi]], axis=0).T
                xi = jnp.concatenate([s_im[:, lo:mid], s_im[:, mid:hi]], axis=0).T
                for s in range(SCAN_STEPS):
                    cr, ci = ct[s:s + 1], ct[SCAN_ROWS + s:SCAN_ROWS + s + 1]
                    rr, ri = _shift_rows(xr, 1 << s), _shift_rows(xi, 1 << s)
                    xr, xi = xr + cr * rr - ci * ri, xi + cr * ri + ci * rr
                hf_ref[g, 2 * pair:2 * pair + 1, :] = xr[LANES - 1:]
                hf_ref[g, 2 * pair + 1:2 * pair + 2, :] = xi[LANES - 1:]
                hr_t = _shift_rows(xr, 1).T
                hi_t = _shift_rows(xi, 1).T
                tiles.append(jnp.concatenate([hr_t[:half], hi_t[:half]], axis=0))
                tiles.append(jnp.concatenate([hr_t[half:], hi_t[half:]], axis=0))
            h_in = jnp.concatenate(tiles, axis=1)
        y = y_local + jnp.dot(q_ref[g], h_in.astype(BF16), preferred_element_type=F32)
        y3 = y.reshape(tc, GROUP, nk) + dsk_ref[pl.ds(r0, GROUP), :][None] * xg.astype(F32)
        ya = jax.nn.gelu(y3).astype(y_ref.dtype)
        if fold == 1:
            y_ref[:, pl.ds(r0, GROUP), :] = ya
        else:
            for j in range(tc):
                y_ref[j // fold, pl.ds(r0, GROUP), (j % fold) * nk:(j % fold + 1) * nk] = ya[j]

    def body(gi, carry):
        for k in range(GROUPS_PER_TRIP):
            one_group(gi * GROUPS_PER_TRIP + k)
        return carry

    lax.fori_loop(0, gb // GROUPS_PER_TRIP, body, 0)


def _ssm(zb, mp, q, ct, dsk, h0, gb, fold):
    tcf, _, nkf = zb.shape
    tc, nk = tcf * fold, nkf // fold
    ng, mrows, rows = mp.shape
    p2 = mrows - rows
    cb = ng * GROUP
    has_h0 = h0 is not None
    in_specs = [pl.BlockSpec((tcf, gb * GROUP, nkf), lambda i: (0, i, 0)),
                pl.BlockSpec((gb, mrows, rows), lambda i: (i, 0, 0)),
                pl.BlockSpec((gb, rows, p2), lambda i: (i, 0, 0)),
                pl.BlockSpec((gb,) + ct.shape[1:], lambda i: (i, 0, 0)),
                pl.BlockSpec((gb * GROUP, nk), lambda i: (i, 0))]
    args = [zb, mp, q, ct, dsk]
    if has_h0:
        state_spec = pl.BlockSpec((gb, p2 // 2, nk), lambda i: (i, 0, 0))
        in_specs += [state_spec, state_spec]
        args += list(h0)
        hf_specs = [state_spec, state_spec]
        hf_shapes = [jax.ShapeDtypeStruct((ng, p2 // 2, nk), F32)] * 2
    else:
        hf_specs = [pl.BlockSpec((gb, nk // LANES, LANES), lambda i: (i, 0, 0))]
        hf_shapes = [jax.ShapeDtypeStruct((ng, nk // LANES, LANES), F32)]
    res = pl.pallas_call(
        functools.partial(_ssm_kernel, tc=tc, gb=gb, nk=nk, fold=fold, has_h0=has_h0),
        grid=(ng // gb,),
        in_specs=in_specs,
        out_specs=[pl.BlockSpec((tcf, gb * GROUP, nkf), lambda i: (0, i, 0))] + hf_specs,
        out_shape=[jax.ShapeDtypeStruct((tcf, cb, nkf), BF16)] + hf_shapes,
        compiler_params=_params("parallel"),
    )(*args)
    return res[0], tuple(res[1:])


def _glu_out_kernel(y_ref, gate_ref, wg_ref, bg_ref, wo_ref, r_ref, *ob_refs):
    for t, ob_ref in enumerate(ob_refs):
        ls = slice(t * LANE_TILE, (t + 1) * LANE_TILE)
        y = y_ref[:, ls]
        for m in range(wg_ref.shape[0] // ROW_CHUNK):
            sl = slice(m * ROW_CHUNK, (m + 1) * ROW_CHUNK)
            g = jnp.dot(wg_ref[sl, :], y, preferred_element_type=F32) + bg_ref[sl, :]
            ob = (y_ref[sl, ls].astype(F32) * jax.nn.sigmoid(g)
                  * jax.nn.silu(gate_ref[sl, ls].astype(F32)))
            ob_ref[sl, :] = ob.astype(ob_ref.dtype)
    for t, ob_ref in enumerate(ob_refs):
        ob = ob_ref[...]
        for m in range(wo_ref.shape[0] // ROW_CHUNK):
            sl = slice(m * ROW_CHUNK, (m + 1) * ROW_CHUNK)
            r_ref[t * LANE_TILE:(t + 1) * LANE_TILE, sl] = jnp.dot(
                wo_ref[sl, :], ob, preferred_element_type=F32).T


def _glu_out(y_t, zb, wg_t, bg, wo_t, nkt):
    tc, cb, nk = y_t.shape
    d = wo_t.shape[0]
    return pl.pallas_call(
        _glu_out_kernel,
        grid=(tc, nk // nkt),
        in_specs=[pl.BlockSpec((None, cb, nkt), lambda j, k: (j, 0, k)),
                  pl.BlockSpec((None, cb, nkt), lambda j, k: (j, 1, k)),
                  pl.BlockSpec((cb, cb), lambda j, k: (0, 0), pipeline_mode=pl.Buffered(1)),
                  pl.BlockSpec((cb, LANE_TILE), lambda j, k: (0, 0), pipeline_mode=pl.Buffered(1)),
                  pl.BlockSpec((d, cb), lambda j, k: (0, 0), pipeline_mode=pl.Buffered(1))],
        out_specs=pl.BlockSpec((None, nkt, d), lambda j, k: (j, k, 0)),
        out_shape=jax.ShapeDtypeStruct((tc, nk, d), F32),
        scratch_shapes=[pltpu.VMEM((cb, LANE_TILE), BF16)] * (nkt // LANE_TILE),
        compiler_params=_params("parallel", "parallel"),
    )(y_t, zb, wg_t, bg, wo_t)


def _out_norm_kernel(x_ref, oa_ref, r_ref, wo_ref, g_ref, y_ref, ms_ref, *, tc, n):
    ca = oa_ref.shape[-1]
    d = x_ref.shape[-1]
    nc = d // LANES
    p = jnp.dot(oa_ref[...].reshape(tc * n, ca), wo_ref[...], preferred_element_type=F32)
    for j in range(tc):
        s = p[j * n:(j + 1) * n] + r_ref[j]
        for c in range(nc):
            ms_ref[c, pl.ds(j, n, stride=tc), :] = s[:, c * LANES:(c + 1) * LANES]
    acc = x_ref[...] + jnp.concatenate([ms_ref[c] for c in range(nc)], axis=1)
    ms = jnp.mean(acc * acc, axis=-1, keepdims=True)
    y_ref[...] = acc * lax.rsqrt(ms + EPS) * g_ref[...]


def _out_norm(x2d, oa, r, wo_a, gf, n):
    tc, nk, ca = oa.shape
    t, d = x2d.shape
    return pl.pallas_call(
        functools.partial(_out_norm_kernel, tc=tc, n=n),
        grid=(nk // n,),
        in_specs=[pl.BlockSpec((n * tc, d), lambda i: (i, 0)),
                  pl.BlockSpec((tc, n, ca), lambda i: (0, i, 0)),
                  pl.BlockSpec((tc, n, d), lambda i: (0, i, 0)),
                  pl.BlockSpec((ca, d), lambda i: (0, 0), pipeline_mode=pl.Buffered(1)),
                  pl.BlockSpec((1, d), lambda i: (0, 0))],
        out_specs=pl.BlockSpec((n * tc, d), lambda i: (i, 0)),
        out_shape=jax.ShapeDtypeStruct((t, d), F32),
        scratch_shapes=[pltpu.VMEM((d // LANES, n * tc, LANES), F32)],
        compiler_params=_params("parallel"),
    )(x2d, oa, r, wo_a, gf.reshape(1, d))


def _discretise(a_re, a_im, log_dt):
    dt = jnp.exp(log_dt)[:, None]
    x_re, x_im = a_re * dt, a_im * dt
    n = (2.0 ** jnp.arange(N_POW, dtype=F32))[None, :, None]
    mag = jnp.exp(n * x_re[:, None, :])
    ang = n * x_im[:, None, :]
    pw_r, pw_i = mag * jnp.cos(ang), mag * jnp.sin(ang)
    lr, li = pw_r[:, 0], pw_i[:, 0]
    den = a_re * a_re + a_im * a_im
    nr, ni = lr - 1.0, li
    qr, qi = (nr * a_re + ni * a_im) / den, (ni * a_re - nr * a_im) / den
    cat = lambda a, b: jnp.concatenate([a, b], axis=-1)
    rowtab = jnp.stack([cat(lr, lr), cat(li, li), cat(qr, qi), cat(-qi, qr)], axis=1)
    zpad = jnp.zeros_like(pw_r[:, :POW_COLS - N_POW])
    lt = jnp.concatenate([pw_r, zpad, pw_i, zpad], axis=1).transpose(0, 2, 1)
    b0 = TC_PROMPT.bit_length() - 1
    sc_r, sc_i = pw_r[:, b0:b0 + SCAN_STEPS], pw_i[:, b0:b0 + SCAN_STEPS]
    zrow = jnp.zeros_like(cat(sc_r, sc_r)[:, :SCAN_ROWS - SCAN_STEPS])
    scan_rows = jnp.concatenate([cat(sc_r, sc_r), zrow, cat(sc_i, sc_i), zrow], axis=1)
    return rowtab, lt, scan_rows


def _ssm_tables_kernel(rt_ref, btr_ref, bti_ref, c2r_ref, c2i_ref, c2n_ref,
                       mp_ref, q_ref, mps_ref, qs_ref, *, gb, tc, tcs, unroll):
    rows = tc * GROUP
    rows_s = tcs * GROUP
    half = rt_ref.shape[2] // 2
    lane_k = lax.broadcasted_iota(jnp.int32, (GROUP, rows), 1)
    lane_q = lax.broadcasted_iota(jnp.int32, (1, 2 * half), 1)

    def one_group(g):
        rt = rt_ref[g]
        l_r, l_i, qu, qv = rt[0:1], rt[1:2], rt[2:3], rt[3:4]
        btr, bti = btr_ref[g], bti_ref[g]
        bb = btr * qu + bti * qv
        bs = btr * qv - bti * qu
        c2r, c2i = c2r_ref[g], c2i_ref[g]
        cur_r = jnp.ones((1, 2 * half), F32)
        cur_i = jnp.zeros((1, 2 * half), F32)
        et_blocks = []
        for n in range(tc):
            et_blocks.append(cur_r * bb + cur_i * bs)
            cur_r, cur_i = cur_r * l_r - cur_i * l_i, cur_r * l_i + cur_i * l_r
            u = jnp.where(lane_q < half, cur_r, -cur_i)
            v = jnp.where(lane_q < half, -cur_i, -cur_r)
            blk = (c2r * u + c2i * v).astype(q_ref.dtype)
            q_ref[g, n * GROUP:(n + 1) * GROUP, :] = blk
            if n < tcs:
                qs_ref[g, n * GROUP:(n + 1) * GROUP, :] = blk
        p = jnp.concatenate(et_blocks[::-1], axis=0).T
        mp_ref[g, rows:, :] = p.astype(mp_ref.dtype)
        mps_ref[g, rows_s:, :] = p[:, rows - rows_s:].astype(mps_ref.dtype)
        krow = jnp.dot(c2n_ref[g], p, precision=lax.Precision.HIGHEST, preferred_element_type=F32)
        for i in range(tc):
            sh = (tc - 1 - i) * GROUP
            blk = krow if sh == 0 else pltpu.roll(krow, rows - sh, 1)
            blk = jnp.where(lane_k < (i + 1) * GROUP, blk, 0.0)
            mp_ref[g, i * GROUP:(i + 1) * GROUP, :] = blk.astype(mp_ref.dtype)
            if i < tcs:
                mps_ref[g, i * GROUP:(i + 1) * GROUP, :] = blk[:, :rows_s].astype(mps_ref.dtype)

    def body(gi, carry):
        for k in range(unroll):
            one_group(gi * unroll + k)
        return carry

    lax.fori_loop(0, gb // unroll, body, 0)


def _ssm_tables(rowtab, b_re, b_im, c_re, c_im, tcs, gb):
    ng, _, p2 = rowtab.shape
    tc = TC_PROMPT
    rows, rows_s = tc * GROUP, tcs * GROUP
    dup = lambda a: jnp.concatenate([a, a], axis=-1)
    args = (rowtab, dup(b_re.transpose(0, 2, 1)), dup(b_im.transpose(0, 2, 1)),
            dup(c_re), dup(c_im), jnp.concatenate([c_re, -c_im], axis=-1))
    shapes = [(ng, rows + p2, rows), (ng, rows, p2), (ng, rows_s + p2, rows_s), (ng, rows_s, p2)]
    return pl.pallas_call(
        functools.partial(_ssm_tables_kernel, gb=gb, tc=tc, tcs=tcs, unroll=4),
        grid=(ng // gb,),
        in_specs=[pl.BlockSpec((gb,) + a.shape[1:], lambda i: (i, 0, 0)) for a in args],
        out_specs=[pl.BlockSpec((gb,) + s[1:], lambda i: (i, 0, 0)) for s in shapes],
        out_shape=[jax.ShapeDtypeStruct(s, BF16) for s in shapes],
        compiler_params=_params("parallel"),
    )(*args)


def _mix_bias(b_s, hd, tc):
    nh = b_s.shape[0]
    r = CHUNK // tc
    if tc == TC_PROMPT:
        bm = b_s.reshape(nh, r, tc).transpose(0, 2, 1).reshape(nh, CHUNK)
    else:
        bm = jnp.repeat(b_s[:, :tc], r, axis=1)
    return jnp.broadcast_to(bm[:, :, None], (nh, CHUNK, hd))


def _layer(x2d, tc, h0, wts, tiles):
    (g_norm, w_a, gv, wmix, bmix, w_bt, mp, q, ct, d_skip, wg_t, b_glu, wo_a, wo_bt, g_final) = wts
    n_norm, n_a, nkt, gb, fold = tiles
    cb = wg_t.shape[0]
    d = x2d.shape[1]
    nk = x2d.shape[0] // tc
    xn = _norm_permute(x2d, g_norm, tc, n_norm)
    out_a, v_rows = _branch_a(xn, w_a, gv, wmix, bmix, n_a, tc * n_a // PROJ_ROWS, h0 is not None)
    zb = _proj_b(w_bt, xn.reshape(tc // fold, fold * nk, d), nkt)
    dsk = jnp.broadcast_to(d_skip[:, None], (cb, nk))
    y_t, h_fin = _ssm(zb, mp, q, ct, dsk, h0, gb, fold)
    bg = jnp.broadcast_to(b_glu[:, None], (cb, LANE_TILE))
    r = _glu_out(y_t, zb, wg_t, bg, wo_bt, nkt).reshape(tc, nk, d)
    y = _out_norm(x2d, out_a, r, wo_a, g_final, n_norm)
    return y, h_fin, v_rows


def kernel(x_prompt, x_sample, state_ssm_re, state_ssm_im, g_norm, w_in, g_v, w_s, b_s,
           a_re, a_im, log_dt, b_re, b_im, c_re, c_im, d_skip, w_glu, b_glu, w_out, g_final):
    depth = g_norm.shape[0]
    assert depth == 1, "single-layer step"
    bsz, seq, d = x_prompt.shape
    dbsz, dseq, _ = x_sample.shape
    nh = w_s.shape[1]
    ng, npst = a_re.shape[1:]
    cb = ng * GROUP
    ca = (w_in.shape[2] - 2 * cb) // 3
    hd = ca // nh
    assert seq // TC_PROMPT == LANES and dbsz == LANES and CHUNK % dseq == 0
    assert dseq & (dseq - 1) == 0 and dseq <= TC_PROMPT, "sample chunk must be a power of two"
    assert 3 * ca % 512 == 0 and ca == cb == d

    l = 0
    w_in_l = w_in[l]
    w_bt = _transpose_cast(w_in_l, 0, 3 * ca // 512, d, 2 * cb)
    gv = g_v[l].reshape(nh, 1, hd)
    rowtab, lt, scan_rows = _discretise(a_re[l], a_im[l], log_dt[l])
    mp, q, mp_s, q_s = _ssm_tables(rowtab, b_re[l], b_im[l], c_re[l], c_im[l], dseq, 8)
    wg_t = _transpose_cast(w_glu[l], 0, 0, cb, cb)
    wo_a = w_out[l][:ca].astype(BF16)
    wo_bt = _transpose_cast(w_out[l], 1, 0, cb, d)

    def weights(tc, mp_, q_, ct):
        return (g_norm[l], w_in_l, gv, w_s[l], _mix_bias(b_s[l], hd, tc), w_bt, mp_, q_, ct,
                d_skip[l], wg_t, b_glu[l], wo_a, wo_bt, g_final)

    assert bsz % 2 == 0
    tc_p, nk_p = TC_PROMPT, bsz * seq // TC_PROMPT
    y_p, hf_p, _ = _layer(x_prompt.reshape(nk_p * tc_p, d), tc_p, None,
                          weights(tc_p, mp, q, scan_rows), (32, 128, 512, 8, 1))
    hf_p = hf_p[0].reshape(ng, bsz // 2, 2, 2, npst)
    re_p = hf_p[:, :, 0].reshape(ng, bsz, npst).transpose(1, 0, 2)[None]
    im_p = hf_p[:, :, 1].reshape(ng, bsz, npst).transpose(1, 0, 2)[None]

    tc_s, nk_s = dseq, dbsz
    h0 = (state_ssm_re[l].transpose(1, 2, 0), state_ssm_im[l].transpose(1, 2, 0))
    y_s, hf_s, v_s = _layer(x_sample.reshape(nk_s * tc_s, d), tc_s, h0,
                            weights(tc_s, mp_s, q_s, lt), (64, 128, 256, 8, 2))
    re_s = hf_s[0].transpose(2, 0, 1)[None]
    im_s = hf_s[1].transpose(2, 0, 1)[None]
    v_out = v_s.transpose(1, 0, 2)[None]

    dt_p, dt_s = x_prompt.dtype, x_sample.dtype
    return (y_p.reshape(bsz, seq, d).astype(dt_p), y_s.reshape(dbsz, dseq, d).astype(dt_s),
            re_p.astype(dt_p), im_p.astype(dt_p), re_s.astype(dt_s), im_s.astype(dt_s),
            v_out.astype(dt_s))
```

```python
import functools

import jax
import jax.numpy as jnp
from jax import lax
from jax.experimental import pallas as pl
from jax.experimental.pallas import tpu as pltpu

EPS = 1e-6
LANES = 128
CHUNK = 128
GROUP = 16
TC_PROMPT = 16
ROW_CHUNK = 512
LANE_TILE = 256
PROJ_ROWS = 1024
TRANSPOSE_COLS = 1024
POW_COLS = 16
SCAN_STEPS = 7
N_POW = 11
SCAN_ROWS = 8
GROUPS_PER_TRIP = 4
VMEM_LIMIT = 56 * 1024 * 1024

F32 = jnp.float32
BF16 = jnp.bfloat16


def _params(*sem):
    return pltpu.CompilerParams(dimension_semantics=sem, vmem_limit_bytes=VMEM_LIMIT)


def _norm_permute_kernel(x_ref, g_ref, o_ref, *, tc, n):
    x = x_ref[...]
    ms = jnp.mean(x * x, axis=-1, keepdims=True)
    xn = (x * lax.rsqrt(ms + EPS) * g_ref[...]).astype(BF16)
    rows = n * tc
    dst = lax.broadcasted_iota(jnp.int32, (rows, rows), 0)
    src = lax.broadcasted_iota(jnp.int32, (rows, rows), 1)
    sel = (src == tc * (dst % n) + dst // n).astype(BF16)
    o_ref[...] = jnp.dot(sel, xn, preferred_element_type=F32).astype(o_ref.dtype).reshape(o_ref.shape)


def _norm_permute(x2d, g, tc, n):
    t, d = x2d.shape
    nk = t // tc
    return pl.pallas_call(
        functools.partial(_norm_permute_kernel, tc=tc, n=n),
        grid=(nk // n,),
        in_specs=[pl.BlockSpec((n * tc, d), lambda i: (i, 0)),
                  pl.BlockSpec((1, d), lambda i: (0, 0))],
        out_specs=pl.BlockSpec((tc, n, d), lambda i: (0, i, 0)),
        out_shape=jax.ShapeDtypeStruct((tc, nk, d), BF16),
        compiler_params=_params("parallel"),
    )(x2d, g.reshape(1, d))


def _branch_a_kernel(xn_ref, wu_ref, wv_ref, wg_ref, gv_ref, ws_ref, bmix_ref, oa_ref, *rest,
                     tc, n, r, hd, n_sub):
    *v_refs, w_ref, wm_ref = rest
    d = xn_ref.shape[-1]

    @pl.when(pl.program_id(1) == 0)
    def _():
        w_ref[:, :hd] = wu_ref[...].astype(BF16)
        w_ref[:, hd:2 * hd] = wv_ref[...].astype(BF16)
        w_ref[:, 2 * hd:] = wg_ref[...].astype(BF16)
        row = lax.broadcasted_iota(jnp.int32, (CHUNK, CHUNK), 0)
        col = lax.broadcasted_iota(jnp.int32, (CHUNK, CHUNK), 1)
        w_tril = jnp.where(row >= col, ws_ref[...], 0.0).astype(BF16)
        if tc == TC_PROMPT:
            time_of_row = tc * (row % r) + row // r
            same_seq = None
        else:
            time_of_row = row // r
            same_seq = (row % r) == (col % r)
        sel = (col == time_of_row).astype(BF16)
        sw = jnp.dot(sel, w_tril, preferred_element_type=F32).astype(BF16)
        wm = lax.dot_general(sw, sel, (((1,), (1,)), ((), ())), preferred_element_type=F32)
        if same_seq is not None:
            wm = jnp.where(same_seq, wm, 0.0)
        wm_ref[...] = wm.astype(BF16)

    ns = n // n_sub
    zs = [jnp.dot(xn_ref[:, k * ns:(k + 1) * ns, :].reshape(tc * ns, d), w_ref[...],
                  preferred_element_type=F32) for k in range(n_sub)]
    wm = wm_ref[...]
    bm = bmix_ref[...]
    for k, z in enumerate(zs):
        k0 = k * ns
        u = jax.nn.gelu(z[:, :hd])
        v = jax.nn.gelu(z[:, hd:2 * hd])
        ga = z[:, 2 * hd:]
        ms = jnp.mean(v * v, axis=-1, keepdims=True)
        vn3 = (v * lax.rsqrt(ms + EPS) * gv_ref[...]).reshape(tc, ns, hd)
        pre3 = (u * jax.nn.silu(ga)).reshape(tc, ns, hd)
        if v_refs:
            v_refs[0][:, k0:k0 + ns, :] = vn3
        for c in range(ns // r):
            vg = vn3[:, c * r:(c + 1) * r, :].reshape(CHUNK, hd)
            mixed = jnp.dot(wm, vg.astype(BF16), preferred_element_type=F32) + bm
            og = pre3[:, c * r:(c + 1) * r, :].reshape(CHUNK, hd) * mixed
            oa_ref[:, k0 + c * r:k0 + (c + 1) * r, :] = og.reshape(tc, r, hd).astype(oa_ref.dtype)


def _branch_a(xn, w_in, gv, w_s, bmix, n, n_sub, want_v):
    tc, nk, d = xn.shape
    nh, _, hd = gv.shape
    r = CHUNK // tc
    out_map = lambda h, i: (0, i, h)
    out_shape = [jax.ShapeDtypeStruct((tc, nk, nh * hd), BF16)]
    out_specs = [pl.BlockSpec((tc, n, hd), out_map)]
    if want_v:
        out_shape.append(jax.ShapeDtypeStruct((tc, nk, nh * hd), F32))
        out_specs.append(pl.BlockSpec((tc, n, hd), out_map))
    res = pl.pallas_call(
        functools.partial(_branch_a_kernel, tc=tc, n=n, r=r, hd=hd, n_sub=n_sub),
        grid=(nh, nk // n),
        in_specs=[pl.BlockSpec((tc, n, d), lambda h, i: (0, i, 0)),
                  pl.BlockSpec((d, hd), lambda h, i: (0, h)),
                  pl.BlockSpec((d, hd), lambda h, i: (0, nh + h)),
                  pl.BlockSpec((d, hd), lambda h, i: (0, 2 * nh + h)),
                  pl.BlockSpec((None, 1, hd), lambda h, i: (h, 0, 0)),
                  pl.BlockSpec((None, CHUNK, CHUNK), lambda h, i: (h, 0, 0)),
                  pl.BlockSpec((None, CHUNK, hd), lambda h, i: (h, 0, 0))],
        out_specs=out_specs,
        out_shape=out_shape,
        scratch_shapes=[pltpu.VMEM((d, 3 * hd), BF16), pltpu.VMEM((CHUNK, CHUNK), BF16)],
        compiler_params=_params("parallel", "arbitrary"),
    )(xn, w_in, w_in, w_in, gv, w_s, bmix)
    return res if want_v else (res[0], None)


def _transpose_cast_kernel(w_ref, o_ref):
    o_ref[...] = w_ref[...].T.astype(o_ref.dtype)


def _transpose_cast(w, row_block, col_block0, n_rows, n_cols):
    cblk = TRANSPOSE_COLS
    return pl.pallas_call(
        _transpose_cast_kernel,
        grid=(n_cols // cblk,),
        in_specs=[pl.BlockSpec((n_rows, cblk), lambda i: (row_block, col_block0 + i))],
        out_specs=pl.BlockSpec((cblk, n_rows), lambda i: (i, 0)),
        out_shape=jax.ShapeDtypeStruct((n_cols, n_rows), BF16),
        compiler_params=_params("parallel"),
    )(w)


def _proj_b_kernel(w_ref, xn_ref, o_ref):
    for t in range(xn_ref.shape[0] // LANE_TILE):
        ls = slice(t * LANE_TILE, (t + 1) * LANE_TILE)
        xn = xn_ref[ls, :]
        for m in range(w_ref.shape[0] // ROW_CHUNK):
            sl = slice(m * ROW_CHUNK, (m + 1) * ROW_CHUNK)
            o_ref[sl, ls] = lax.dot_general(
                w_ref[sl, :], xn, (((1,), (1,)), ((), ())),
                preferred_element_type=F32).astype(o_ref.dtype)


def _proj_b(w_bt, xn, nkt):
    tc, nk, d = xn.shape
    rows = w_bt.shape[0]
    return pl.pallas_call(
        _proj_b_kernel,
        grid=(tc, nk // nkt),
        in_specs=[pl.BlockSpec((rows, d), lambda j, k: (0, 0), pipeline_mode=pl.Buffered(1)),
                  pl.BlockSpec((None, nkt, d), lambda j, k: (j, k, 0))],
        out_specs=pl.BlockSpec((None, rows, nkt), lambda j, k: (j, 0, k)),
        out_shape=jax.ShapeDtypeStruct((tc, rows, nk), BF16),
        compiler_params=_params("parallel", "parallel"),
    )(w_bt, xn)


def _shift_rows(x, d):
    if d % 8 == 0:
        return jnp.concatenate([jnp.zeros((d, x.shape[1]), x.dtype), x[:-d]], axis=0)
    row = lax.broadcasted_iota(jnp.int32, x.shape, 0)
    return jnp.where(row >= d, pltpu.roll(x, d, 0), 0.0)


def _ssm_kernel(x_ref, mp_ref, q_ref, ct_ref, dsk_ref, *rest, tc, gb, nk, fold, has_h0):
    if has_h0:
        h0r_ref, h0i_ref, y_ref, hfr_ref, hfi_ref = rest
    else:
        y_ref, hf_ref = rest
    rows = tc * GROUP
    half = q_ref.shape[2] // 2
    b_tc = tc.bit_length() - 1

    def one_group(g):
        r0 = pl.multiple_of(g * GROUP, GROUP)
        if fold == 1:
            xg = x_ref[:, pl.ds(r0, GROUP), :]
        else:
            xg = jnp.stack([x_ref[j // fold, pl.ds(r0, GROUP), (j % fold) * nk:(j % fold + 1) * nk]
                            for j in range(tc)])
        res = jnp.dot(mp_ref[g], xg.reshape(rows, nk), preferred_element_type=F32)
        y_local = res[:rows]
        s_re = res[rows:rows + half]
        s_im = res[rows + half:]
        ct = ct_ref[g]
        if has_h0:
            h_re, h_im = h0r_ref[g], h0i_ref[g]
            a_re, a_im = ct[:, b_tc:b_tc + 1], ct[:, POW_COLS + b_tc:POW_COLS + b_tc + 1]
            hfr_ref[g] = a_re * h_re - a_im * h_im + s_re
            hfi_ref[g] = a_re * h_im + a_im * h_re + s_im
            h_in = jnp.concatenate([h_re, h_im], axis=0)
        else:
            tiles = []
            for pair in range(nk // (2 * LANES)):
                lo, mid, hi = 2 * pair * LANES, (2 * pair + 1) * LANES, (2 * pair + 2) * LANES
                xr = jnp.concatenate([s_re[:, lo:mid], s_re[:, mid:hi]], axis=0).T
                xi = jnp.concatenate([s_im[:, lo:mid], s_im[:, mid:hi]], axis=0).T
                for s in range(SCAN_STEPS):
                    cr, ci = ct[s:s + 1], ct[SCAN_ROWS + s:SCAN_ROWS + s + 1]
                    rr, ri = _shift_rows(xr, 1 << s), _shift_rows(xi, 1 << s)
                    xr, xi = xr + cr * rr - ci * ri, xi + cr * ri + ci * rr
                hf_ref[g, 2 * pair:2 * pair + 1, :] = xr[LANES - 1:]
                hf_ref[g, 2 * pair + 1:2 * pair + 2, :] = xi[LANES - 1:]
                hr_t = _shift_rows(xr, 1).T
                hi_t = _shift_rows(xi, 1).T
                tiles.append(jnp.concatenate([hr_t[:half], hi_t[:half]], axis=0))
                tiles.append(jnp.concatenate([hr_t[half:], hi_t[half:]], axis=0))
            h_in = jnp.concatenate(tiles, axis=1)
        y = y_local + jnp.dot(q_ref[g], h_in.astype(BF16), preferred_element_type=F32)
        dsk = jnp.concatenate([dsk_ref[pl.ds(r0, GROUP), :]] * (nk // LANES), axis=1)
        y3 = y.reshape(tc, GROUP, nk) + dsk[None] * xg.astype(F32)
        ya = jax.nn.gelu(y3).astype(y_ref.dtype)
        if fold == 1:
            y_ref[:, pl.ds(r0, GROUP), :] = ya
        else:
            for j in range(tc):
                y_ref[j // fold, pl.ds(r0, GROUP), (j % fold) * nk:(j % fold + 1) * nk] = ya[j]

    def body(gi, carry):
        for k in range(GROUPS_PER_TRIP):
            one_group(gi * GROUPS_PER_TRIP + k)
        return carry

    lax.fori_loop(0, gb // GROUPS_PER_TRIP, body, 0)


def _ssm(zb, mp, q, ct, dsk, h0, gb, fold):
    tcf, _, nkf = zb.shape
    tc, nk = tcf * fold, nkf // fold
    ng, mrows, rows = mp.shape
    p2 = mrows - rows
    cb = ng * GROUP
    has_h0 = h0 is not None
    in_specs = [pl.BlockSpec((tcf, gb * GROUP, nkf), lambda i: (0, i, 0)),
                pl.BlockSpec((gb, mrows, rows), lambda i: (i, 0, 0)),
                pl.BlockSpec((gb, rows, p2), lambda i: (i, 0, 0)),
                pl.BlockSpec((gb,) + ct.shape[1:], lambda i: (i, 0, 0)),
                pl.BlockSpec((gb * GROUP, LANES), lambda i: (i, 0))]
    args = [zb, mp, q, ct, dsk]
    if has_h0:
        state_spec = pl.BlockSpec((gb, p2 // 2, nk), lambda i: (i, 0, 0))
        in_specs += [state_spec, state_spec]
        args += list(h0)
        hf_specs = [state_spec, state_spec]
        hf_shapes = [jax.ShapeDtypeStruct((ng, p2 // 2, nk), F32)] * 2
    else:
        hf_specs = [pl.BlockSpec((gb, nk // LANES, LANES), lambda i: (i, 0, 0))]
        hf_shapes = [jax.ShapeDtypeStruct((ng, nk // LANES, LANES), F32)]
    res = pl.pallas_call(
        functools.partial(_ssm_kernel, tc=tc, gb=gb, nk=nk, fold=fold, has_h0=has_h0),
        grid=(ng // gb,),
        in_specs=in_specs,
        out_specs=[pl.BlockSpec((tcf, gb * GROUP, nkf), lambda i: (0, i, 0))] + hf_specs,
        out_shape=[jax.ShapeDtypeStruct((tcf, cb, nkf), BF16)] + hf_shapes,
        compiler_params=_params("parallel"),
    )(*args)
    return res[0], tuple(res[1:])


def _glu_out_kernel(y_ref, gate_ref, wg_ref, bg_ref, wo_ref, r_ref, *ob_refs):
    for t, ob_ref in enumerate(ob_refs):
        ls = slice(t * LANE_TILE, (t + 1) * LANE_TILE)
        y = y_ref[:, ls]
        for m in range(wg_ref.shape[0] // ROW_CHUNK):
            sl = slice(m * ROW_CHUNK, (m + 1) * ROW_CHUNK)
            g = jnp.dot(wg_ref[sl, :], y, preferred_element_type=F32) + bg_ref[sl, :]
            ob = (y_ref[sl, ls].astype(F32) * jax.nn.sigmoid(g)
                  * jax.nn.silu(gate_ref[sl, ls].astype(F32)))
            ob_ref[sl, :] = ob.astype(ob_ref.dtype)
    for t, ob_ref in enumerate(ob_refs):
        ob = ob_ref[...]
        for m in range(wo_ref.shape[0] // ROW_CHUNK):
            sl = slice(m * ROW_CHUNK, (m + 1) * ROW_CHUNK)
            r_ref[t * LANE_TILE:(t + 1) * LANE_TILE, sl] = jnp.dot(
                wo_ref[sl, :], ob, preferred_element_type=F32).T


def _glu_out(y_t, zb, wg_t, bg, wo_t, nkt):
    tc, cb, nk = y_t.shape
    d = wo_t.shape[0]
    return pl.pallas_call(
        _glu_out_kernel,
        grid=(tc, nk // nkt),
        in_specs=[pl.BlockSpec((None, cb, nkt), lambda j, k: (j, 0, k)),
                  pl.BlockSpec((None, cb, nkt), lambda j, k: (j, 1, k)),
                  pl.BlockSpec((cb, cb), lambda j, k: (0, 0), pipeline_mode=pl.Buffered(1)),
                  pl.BlockSpec((cb, LANE_TILE), lambda j, k: (0, 0), pipeline_mode=pl.Buffered(1)),
                  pl.BlockSpec((d, cb), lambda j, k: (0, 0), pipeline_mode=pl.Buffered(1))],
        out_specs=pl.BlockSpec((None, nkt, d), lambda j, k: (j, k, 0)),
        out_shape=jax.ShapeDtypeStruct((tc, nk, d), F32),
        scratch_shapes=[pltpu.VMEM((cb, LANE_TILE), BF16)] * (nkt // LANE_TILE),
        compiler_params=_params("parallel", "parallel"),
    )(y_t, zb, wg_t, bg, wo_t)


def _out_norm_kernel(x_ref, oa_ref, r_ref, wo_ref, g_ref, y_ref, ms_ref, *, tc, n):
    ca = oa_ref.shape[-1]
    d = x_ref.shape[-1]
    nc = d // LANES
    p = jnp.dot(oa_ref[...].reshape(tc * n, ca), wo_ref[...], preferred_element_type=F32)
    for j in range(tc):
        s = p[j * n:(j + 1) * n] + r_ref[j]
        for c in range(nc):
            ms_ref[c, pl.ds(j, n, stride=tc), :] = s[:, c * LANES:(c + 1) * LANES]
    acc = x_ref[...] + jnp.concatenate([ms_ref[c] for c in range(nc)], axis=1)
    ms = jnp.mean(acc * acc, axis=-1, keepdims=True)
    y_ref[...] = acc * lax.rsqrt(ms + EPS) * g_ref[...]


def _out_norm(x2d, oa, r, wo_a, gf, n):
    tc, nk, ca = oa.shape
    t, d = x2d.shape
    return pl.pallas_call(
        functools.partial(_out_norm_kernel, tc=tc, n=n),
        grid=(nk // n,),
        in_specs=[pl.BlockSpec((n * tc, d), lambda i: (i, 0)),
                  pl.BlockSpec((tc, n, ca), lambda i: (0, i, 0)),
                  pl.BlockSpec((tc, n, d), lambda i: (0, i, 0)),
                  pl.BlockSpec((ca, d), lambda i: (0, 0), pipeline_mode=pl.Buffered(1)),
                  pl.BlockSpec((1, d), lambda i: (0, 0))],
        out_specs=pl.BlockSpec((n * tc, d), lambda i: (i, 0)),
        out_shape=jax.ShapeDtypeStruct((t, d), F32),
        scratch_shapes=[pltpu.VMEM((d // LANES, n * tc, LANES), F32)],
        compiler_params=_params("parallel"),
    )(x2d, oa, r, wo_a, gf.reshape(1, d))


def _discretise(a_re, a_im, log_dt):
    dt = jnp.exp(log_dt)[:, None]
    x_re, x_im = a_re * dt, a_im * dt
    n = (2.0 ** jnp.arange(N_POW, dtype=F32))[None, :, None]
    mag = jnp.exp(n * x_re[:, None, :])
    ang = n * x_im[:, None, :]
    pw_r, pw_i = mag * jnp.cos(ang), mag * jnp.sin(ang)
    lr, li = pw_r[:, 0], pw_i[:, 0]
    den = a_re * a_re + a_im * a_im
    nr, ni = lr - 1.0, li
    qr, qi = (nr * a_re + ni * a_im) / den, (ni * a_re - nr * a_im) / den
    cat = lambda a, b: jnp.concatenate([a, b], axis=-1)
    rowtab = jnp.stack([cat(lr, lr), cat(li, li), cat(qr, qi), cat(-qi, qr)], axis=1)
    zpad = jnp.zeros_like(pw_r[:, :POW_COLS - N_POW])
    lt = jnp.concatenate([pw_r, zpad, pw_i, zpad], axis=1).transpose(0, 2, 1)
    b0 = TC_PROMPT.bit_length() - 1
    sc_r, sc_i = pw_r[:, b0:b0 + SCAN_STEPS], pw_i[:, b0:b0 + SCAN_STEPS]
    zrow = jnp.zeros_like(cat(sc_r, sc_r)[:, :SCAN_ROWS - SCAN_STEPS])
    scan_rows = jnp.concatenate([cat(sc_r, sc_r), zrow, cat(sc_i, sc_i), zrow], axis=1)
    return rowtab, lt, scan_rows


def _ssm_tables_kernel(rt_ref, btr_ref, bti_ref, c2r_ref, c2i_ref, c2n_ref,
                       mp_ref, q_ref, mps_ref, qs_ref, *, gb, tc, tcs, unroll):
    rows = tc * GROUP
    rows_s = tcs * GROUP
    half = rt_ref.shape[2] // 2
    lane_k = lax.broadcasted_iota(jnp.int32, (GROUP, rows), 1)
    lane_q = lax.broadcasted_iota(jnp.int32, (1, 2 * half), 1)

    def one_group(g):
        rt = rt_ref[g]
        l_r, l_i, qu, qv = rt[0:1], rt[1:2], rt[2:3], rt[3:4]
        btr, bti = btr_ref[g], bti_ref[g]
        bb = btr * qu + bti * qv
        bs = btr * qv - bti * qu
        c2r, c2i = c2r_ref[g], c2i_ref[g]
        cur_r = jnp.ones((1, 2 * half), F32)
        cur_i = jnp.zeros((1, 2 * half), F32)
        et_blocks = []
        for n in range(tc):
            et_blocks.append(cur_r * bb + cur_i * bs)
            cur_r, cur_i = cur_r * l_r - cur_i * l_i, cur_r * l_i + cur_i * l_r
            u = jnp.where(lane_q < half, cur_r, -cur_i)
            v = jnp.where(lane_q < half, -cur_i, -cur_r)
            blk = (c2r * u + c2i * v).astype(q_ref.dtype)
            q_ref[g, n * GROUP:(n + 1) * GROUP, :] = blk
            if n < tcs:
                qs_ref[g, n * GROUP:(n + 1) * GROUP, :] = blk
        p = jnp.concatenate(et_blocks[::-1], axis=0).T
        mp_ref[g, rows:, :] = p.astype(mp_ref.dtype)
        mps_ref[g, rows_s:, :] = p[:, rows - rows_s:].astype(mps_ref.dtype)
        krow = jnp.dot(c2n_ref[g], p, precision=lax.Precision.HIGHEST, preferred_element_type=F32)
        for i in range(tc):
            sh = (tc - 1 - i) * GROUP
            blk = krow if sh == 0 else pltpu.roll(krow, rows - sh, 1)
            blk = jnp.where(lane_k < (i + 1) * GROUP, blk, 0.0)
            mp_ref[g, i * GROUP:(i + 1) * GROUP, :] = blk.astype(mp_ref.dtype)
            if i < tcs:
                mps_ref[g, i * GROUP:(i + 1) * GROUP, :] = blk[:, :rows_s].astype(mps_ref.dtype)

    def body(gi, carry):
        for k in range(unroll):
            one_group(gi * unroll + k)
        return carry

    lax.fori_loop(0, gb // unroll, body, 0)


def _ssm_tables(rowtab, b_re, b_im, c_re, c_im, tcs, gb):
    ng, _, p2 = rowtab.shape
    tc = TC_PROMPT
    rows, rows_s = tc * GROUP, tcs * GROUP
    dup = lambda a: jnp.concatenate([a, a], axis=-1)
    args = (rowtab, dup(b_re.transpose(0, 2, 1)), dup(b_im.transpose(0, 2, 1)),
            dup(c_re), dup(c_im), jnp.concatenate([c_re, -c_im], axis=-1))
    shapes = [(ng, rows + p2, rows), (ng, rows, p2), (ng, rows_s + p2, rows_s), (ng, rows_s, p2)]
    return pl.pallas_call(
        functools.partial(_ssm_tables_kernel, gb=gb, tc=tc, tcs=tcs, unroll=4),
        grid=(ng // gb,),
        in_specs=[pl.BlockSpec((gb,) + a.shape[1:], lambda i: (i, 0, 0)) for a in args],
        out_specs=[pl.BlockSpec((gb,) + s[1:], lambda i: (i, 0, 0)) for s in shapes],
        out_shape=[jax.ShapeDtypeStruct(s, BF16) for s in shapes],
        compiler_params=_params("parallel"),
    )(*args)


def _mix_bias(b_s, hd, tc):
    nh = b_s.shape[0]
    r = CHUNK // tc
    if tc == TC_PROMPT:
        bm = b_s.reshape(nh, r, tc).transpose(0, 2, 1).reshape(nh, CHUNK)
    else:
        bm = jnp.repeat(b_s[:, :tc], r, axis=1)
    return jnp.broadcast_to(bm[:, :, None], (nh, CHUNK, hd))


def _layer(x2d, tc, h0, wts, tiles):
    (g_norm, w_a, gv, wmix, bmix, w_bt, mp, q, ct, d_skip, wg_t, b_glu, wo_a, wo_bt, g_final) = wts
    n_norm, n_a, nkt, gb, fold = tiles
    cb = wg_t.shape[0]
    d = x2d.shape[1]
    nk = x2d.shape[0] // tc
    xn = _norm_permute(x2d, g_norm, tc, n_norm)
    out_a, v_rows = _branch_a(xn, w_a, gv, wmix, bmix, n_a, tc * n_a // PROJ_ROWS, h0 is not None)
    zb = _proj_b(w_bt, xn.reshape(tc // fold, fold * nk, d), nkt)
    dsk = jnp.broadcast_to(d_skip[:, None], (cb, LANES))
    y_t, h_fin = _ssm(zb, mp, q, ct, dsk, h0, gb, fold)
    bg = jnp.broadcast_to(b_glu[:, None], (cb, LANE_TILE))
    r = _glu_out(y_t, zb, wg_t, bg, wo_bt, nkt).reshape(tc, nk, d)
    y = _out_norm(x2d, out_a, r, wo_a, g_final, n_norm)
    return y, h_fin, v_rows


def kernel(x_prompt, x_sample, state_ssm_re, state_ssm_im, g_norm, w_in, g_v, w_s, b_s,
           a_re, a_im, log_dt, b_re, b_im, c_re, c_im, d_skip, w_glu, b_glu, w_out, g_final):
    depth = g_norm.shape[0]
    assert depth == 1, "single-layer step"
    bsz, seq, d = x_prompt.shape
    dbsz, dseq, _ = x_sample.shape
    nh = w_s.shape[1]
    ng, npst = a_re.shape[1:]
    cb = ng * GROUP
    ca = (w_in.shape[2] - 2 * cb) // 3
    hd = ca // nh
    assert seq // TC_PROMPT == LANES and dbsz == LANES and CHUNK % dseq == 0
    assert dseq & (dseq - 1) == 0 and dseq <= TC_PROMPT, "sample chunk must be a power of two"
    assert 3 * ca % TRANSPOSE_COLS == 0 and ca == cb == d

    l = 0
    w_in_l = w_in[l]
    w_bt = _transpose_cast(w_in_l, 0, 3 * ca // TRANSPOSE_COLS, d, 2 * cb)
    gv = g_v[l].reshape(nh, 1, hd)
    rowtab, lt, scan_rows = _discretise(a_re[l], a_im[l], log_dt[l])
    mp, q, mp_s, q_s = _ssm_tables(rowtab, b_re[l], b_im[l], c_re[l], c_im[l], dseq, 8)
    wg_t = _transpose_cast(w_glu[l], 0, 0, cb, cb)
    wo_a = w_out[l][:ca].astype(BF16)
    wo_bt = _transpose_cast(w_out[l], 1, 0, cb, d)

    def weights(tc, mp_, q_, ct):
        return (g_norm[l], w_in_l, gv, w_s[l], _mix_bias(b_s[l], hd, tc), w_bt, mp_, q_, ct,
                d_skip[l], wg_t, b_glu[l], wo_a, wo_bt, g_final)

    assert bsz % 2 == 0
    tc_p, nk_p = TC_PROMPT, bsz * seq // TC_PROMPT
    y_p, hf_p, _ = _layer(x_prompt.reshape(nk_p * tc_p, d), tc_p, None,
                          weights(tc_p, mp, q, scan_rows), (32, 128, 512, 8, 1))
    hf_p = hf_p[0].reshape(ng, bsz // 2, 2, 2, npst)
    re_p = hf_p[:, :, 0].reshape(ng, bsz, npst).transpose(1, 0, 2)[None]
    im_p = hf_p[:, :, 1].reshape(ng, bsz, npst).transpose(1, 0, 2)[None]

    tc_s, nk_s = dseq, dbsz
    h0 = (state_ssm_re[l].transpose(1, 2, 0), state_ssm_im[l].transpose(1, 2, 0))
    y_s, hf_s, v_s = _layer(x_sample.reshape(nk_s * tc_s, d), tc_s, h0,
                            weights(tc_s, mp_s, q_s, lt), (64, 128, 256, 8, 2))
    re_s = hf_s[0].transpose(2, 0, 1)[None]
    im_s = hf_s[1].transpose(2, 0, 1)[None]
    v_out = v_s.transpose(1, 0, 2)[None]

    dt_p, dt_s = x_prompt.dtype, x_sample.dtype
    return (y_p.reshape(bsz, seq, d).astype(dt_p), y_s.reshape(dbsz, dseq, d).astype(dt_s),
            re_p.astype(dt_p), im_p.astype(dt_p), re_s.astype(dt_s), im_s.astype(dt_s),
            v_out.astype(dt_s))
```

```python
import functools

import jax
import jax.numpy as jnp
from jax import lax
from jax.experimental import pallas as pl
from jax.experimental.pallas import tpu as pltpu

EPS = 1e-6
LANES = 128
CHUNK = 128
GROUP = 16
TC_PROMPT = 16
ROW_CHUNK = 512
LANE_TILE = 256
PROJ_ROWS = 1024
NORM_ROWS = 512
SSM_GROUPS_PER_STEP = 8
POW_COLS = 16
SCAN_STEPS = 7
N_POW = 11
SCAN_ROWS = 8
GROUPS_PER_TRIP = 4
VMEM_LIMIT = 56 * 1024 * 1024

F32 = jnp.float32
BF16 = jnp.bfloat16


def _params(*sem):
    return pltpu.CompilerParams(dimension_semantics=sem, vmem_limit_bytes=VMEM_LIMIT)


def _norm_permute_kernel(x_ref, g_ref, o_ref, *, tc, n):
    x = x_ref[...]
    ms = jnp.mean(x * x, axis=-1, keepdims=True)
    xn = (x * lax.rsqrt(ms + EPS) * g_ref[...]).astype(BF16)
    rows = n * tc
    dst = lax.broadcasted_iota(jnp.int32, (rows, rows), 0)
    src = lax.broadcasted_iota(jnp.int32, (rows, rows), 1)
    sel = (src == tc * (dst % n) + dst // n).astype(BF16)
    o_ref[...] = jnp.dot(sel, xn, preferred_element_type=F32).astype(o_ref.dtype).reshape(o_ref.shape)


def _norm_permute(x2d, g, tc, n):
    t, d = x2d.shape
    nk = t // tc
    return pl.pallas_call(
        functools.partial(_norm_permute_kernel, tc=tc, n=n),
        grid=(nk // n,),
        in_specs=[pl.BlockSpec((n * tc, d), lambda i: (i, 0)),
                  pl.BlockSpec((1, d), lambda i: (0, 0))],
        out_specs=pl.BlockSpec((tc, n, d), lambda i: (0, i, 0)),
        out_shape=jax.ShapeDtypeStruct((tc, nk, d), BF16),
        compiler_params=_params("parallel"),
    )(x2d, g.reshape(1, d))


def _branch_a_kernel(*refs, tc, n, r, hd, n_sub, has_v, has_side):
    refs = list(refs)
    xn_ref, wu_ref, wv_ref, wg_ref, gv_ref, ws_ref, bmix_ref = refs[:7]
    del refs[:7]
    side_in = [refs.pop(0)] if has_side else []
    oa_ref = refs.pop(0)
    v_refs = [refs.pop(0)] if has_v else []
    side_out = [refs.pop(0)] if has_side else []
    w_ref, wm_ref = refs
    _transpose_side_blocks(side_in, side_out)
    d = xn_ref.shape[-1]

    @pl.when(pl.program_id(1) == 0)
    def _():
        w_ref[:, :hd] = wu_ref[...].astype(BF16)
        w_ref[:, hd:2 * hd] = wv_ref[...].astype(BF16)
        w_ref[:, 2 * hd:] = wg_ref[...].astype(BF16)
        row = lax.broadcasted_iota(jnp.int32, (CHUNK, CHUNK), 0)
        col = lax.broadcasted_iota(jnp.int32, (CHUNK, CHUNK), 1)
        w_tril = jnp.where(row >= col, ws_ref[...], 0.0).astype(BF16)
        if tc == TC_PROMPT:
            time_of_row = tc * (row % r) + row // r
            same_seq = None
        else:
            time_of_row = row // r
            same_seq = (row % r) == (col % r)
        sel = (col == time_of_row).astype(BF16)
        sw = jnp.dot(sel, w_tril, preferred_element_type=F32).astype(BF16)
        wm = lax.dot_general(sw, sel, (((1,), (1,)), ((), ())), preferred_element_type=F32)
        if same_seq is not None:
            wm = jnp.where(same_seq, wm, 0.0)
        wm_ref[...] = wm.astype(BF16)

    ns = n // n_sub
    zs = [jnp.dot(xn_ref[:, k * ns:(k + 1) * ns, :].reshape(tc * ns, d), w_ref[...],
                  preferred_element_type=F32) for k in range(n_sub)]
    wm = wm_ref[...]
    bm = bmix_ref[...]
    for k, z in enumerate(zs):
        k0 = k * ns
        u = jax.nn.gelu(z[:, :hd])
        v = jax.nn.gelu(z[:, hd:2 * hd])
        ga = z[:, 2 * hd:]
        ms = jnp.mean(v * v, axis=-1, keepdims=True)
        vn3 = (v * lax.rsqrt(ms + EPS) * gv_ref[...]).reshape(tc, ns, hd)
        pre3 = (u * jax.nn.silu(ga)).reshape(tc, ns, hd)
        if v_refs:
            v_refs[0][:, k0:k0 + ns, :] = vn3
        for c in range(ns // r):
            vg = vn3[:, c * r:(c + 1) * r, :].reshape(CHUNK, hd)
            mixed = jnp.dot(wm, vg.astype(BF16), preferred_element_type=F32) + bm
            og = pre3[:, c * r:(c + 1) * r, :].reshape(CHUNK, hd) * mixed
            oa_ref[:, k0 + c * r:k0 + (c + 1) * r, :] = og.reshape(tc, r, hd).astype(oa_ref.dtype)


def _branch_a(xn, w_in, gv, w_s, bmix, n, n_sub, want_v, transpose_b):
    tc, nk, d = xn.shape
    nh, _, hd = gv.shape
    r = CHUNK // tc
    nt = nk // n
    out_map = lambda h, i: (0, i, h)
    in_specs = [pl.BlockSpec((tc, n, d), lambda h, i: (0, i, 0)),
                pl.BlockSpec((d, hd), lambda h, i: (0, h)),
                pl.BlockSpec((d, hd), lambda h, i: (0, nh + h)),
                pl.BlockSpec((d, hd), lambda h, i: (0, 2 * nh + h)),
                pl.BlockSpec((None, 1, hd), lambda h, i: (h, 0, 0)),
                pl.BlockSpec((None, CHUNK, CHUNK), lambda h, i: (h, 0, 0)),
                pl.BlockSpec((None, CHUNK, hd), lambda h, i: (h, 0, 0))]
    args = [xn, w_in, w_in, w_in, gv, w_s, bmix]
    out_shape = [jax.ShapeDtypeStruct((tc, nk, nh * hd), BF16)]
    out_specs = [pl.BlockSpec((tc, n, hd), out_map)]
    if want_v:
        out_shape.append(jax.ShapeDtypeStruct((tc, nk, nh * hd), F32))
        out_specs.append(pl.BlockSpec((tc, n, hd), out_map))
    if transpose_b:
        col0 = 3 * nh * hd // LANES
        n_cols = w_in.shape[1] - 3 * nh * hd
        assert nh * nt * LANES == n_cols, "one 128-column block per grid step"
        in_specs.append(pl.BlockSpec((d, LANES), lambda h, i: (0, col0 + h * nt + i)))
        args.append(w_in)
        out_shape.append(jax.ShapeDtypeStruct((n_cols, d), BF16))
        out_specs.append(pl.BlockSpec((LANES, d), lambda h, i: (h * nt + i, 0)))
    res = list(pl.pallas_call(
        functools.partial(_branch_a_kernel, tc=tc, n=n, r=r, hd=hd, n_sub=n_sub,
                          has_v=want_v, has_side=transpose_b),
        grid=(nh, nt),
        in_specs=in_specs,
        out_specs=out_specs,
        out_shape=out_shape,
        scratch_shapes=[pltpu.VMEM((d, 3 * hd), BF16), pltpu.VMEM((CHUNK, CHUNK), BF16)],
        compiler_params=_params("parallel", "arbitrary"),
    )(*args))
    out_a = res.pop(0)
    v_rows = res.pop(0) if want_v else None
    w_bt = res.pop(0) if transpose_b else None
    return out_a, v_rows, w_bt


def _transpose_side_blocks(in_refs, out_refs):
    for w_ref, o_ref in zip(in_refs, out_refs):
        o_ref[...] = w_ref[...].T.astype(o_ref.dtype)


def _proj_b_kernel(w_ref, xn_ref, *refs):
    n_side = (len(refs) - 1) // 2
    o_ref = refs[n_side]
    _transpose_side_blocks(refs[:n_side], refs[n_side + 1:])
    for t in range(xn_ref.shape[0] // LANE_TILE):
        ls = slice(t * LANE_TILE, (t + 1) * LANE_TILE)
        xn = xn_ref[ls, :]
        for m in range(w_ref.shape[0] // ROW_CHUNK):
            sl = slice(m * ROW_CHUNK, (m + 1) * ROW_CHUNK)
            o_ref[sl, ls] = lax.dot_general(
                w_ref[sl, :], xn, (((1,), (1,)), ((), ())),
                preferred_element_type=F32).astype(o_ref.dtype)


def _proj_b(w_bt, xn, nkt, side=()):
    tc, nk, d = xn.shape
    rows = w_bt.shape[0]
    nkb = nk // nkt
    in_specs = [pl.BlockSpec((rows, d), lambda j, k: (0, 0), pipeline_mode=pl.Buffered(1)),
                pl.BlockSpec((None, nkt, d), lambda j, k: (j, k, 0))]
    out_specs = [pl.BlockSpec((None, rows, nkt), lambda j, k: (j, 0, k))]
    out_shape = [jax.ShapeDtypeStruct((tc, rows, nk), BF16)]
    for w, n_rows, row_block in side:
        n_cols = w.shape[1]
        assert tc * nkb * LANES == n_cols, "one 128-column block per grid step"
        in_specs.append(pl.BlockSpec((n_rows, LANES), lambda j, k, rb=row_block: (rb, j * nkb + k)))
        out_specs.append(pl.BlockSpec((LANES, n_rows), lambda j, k: (j * nkb + k, 0)))
        out_shape.append(jax.ShapeDtypeStruct((n_cols, n_rows), BF16))
    res = pl.pallas_call(
        _proj_b_kernel,
        grid=(tc, nkb),
        in_specs=in_specs,
        out_specs=out_specs,
        out_shape=out_shape,
        compiler_params=_params("parallel", "parallel"),
    )(w_bt, xn, *[w for w, _, _ in side])
    return res[0], tuple(res[1:])


def _shift_rows(x, d):
    if d % 8 == 0:
        return jnp.concatenate([jnp.zeros((d, x.shape[1]), x.dtype), x[:-d]], axis=0)
    row = lax.broadcasted_iota(jnp.int32, x.shape, 0)
    return jnp.where(row >= d, pltpu.roll(x, d, 0), 0.0)


def _ssm_kernel(x_ref, mp_ref, q_ref, ct_ref, dsk_ref, *rest, tc, gb, nk, fold, has_h0):
    if has_h0:
        h0r_ref, h0i_ref, y_ref, hfr_ref, hfi_ref = rest
    else:
        y_ref, hf_ref = rest
    rows = tc * GROUP
    half = q_ref.shape[2] // 2
    b_tc = tc.bit_length() - 1

    def one_group(g):
        r0 = pl.multiple_of(g * GROUP, GROUP)
        if fold == 1:
            xg = x_ref[:, pl.ds(r0, GROUP), :]
        else:
            xg = jnp.stack([x_ref[j // fold, pl.ds(r0, GROUP), (j % fold) * nk:(j % fold + 1) * nk]
                            for j in range(tc)])
        res = jnp.dot(mp_ref[g], xg.reshape(rows, nk), preferred_element_type=F32)
        y_local = res[:rows]
        s_re = res[rows:rows + half]
        s_im = res[rows + half:]
        ct = ct_ref[g]
        if has_h0:
            h_re, h_im = h0r_ref[g], h0i_ref[g]
            a_re, a_im = ct[:, b_tc:b_tc + 1], ct[:, POW_COLS + b_tc:POW_COLS + b_tc + 1]
            hfr_ref[g] = a_re * h_re - a_im * h_im + s_re
            hfi_ref[g] = a_re * h_im + a_im * h_re + s_im
            h_in = jnp.concatenate([h_re, h_im], axis=0)
        else:
            tiles = []
            for pair in range(nk // (2 * LANES)):
                lo, mid, hi = 2 * pair * LANES, (2 * pair + 1) * LANES, (2 * pair + 2) * LANES
                xr = jnp.concatenate([s_re[:, lo:mid], s_re[:, mid:hi]], axis=0).T
                xi = jnp.concatenate([s_im[:, lo:mid], s_im[:, mid:hi]], axis=0).T
                for s in range(SCAN_STEPS):
                    cr, ci = ct[s:s + 1], ct[SCAN_ROWS + s:SCAN_ROWS + s + 1]
                    rr, ri = _shift_rows(xr, 1 << s), _shift_rows(xi, 1 << s)
                    xr, xi = xr + cr * rr - ci * ri, xi + cr * ri + ci * rr
                hf_ref[g, 2 * pair:2 * pair + 1, :] = xr[LANES - 1:]
                hf_ref[g, 2 * pair + 1:2 * pair + 2, :] = xi[LANES - 1:]
                hr_t = _shift_rows(xr, 1).T
                hi_t = _shift_rows(xi, 1).T
                tiles.append(jnp.concatenate([hr_t[:half], hi_t[:half]], axis=0))
                tiles.append(jnp.concatenate([hr_t[half:], hi_t[half:]], axis=0))
            h_in = jnp.concatenate(tiles, axis=1)
        y = y_local + jnp.dot(q_ref[g], h_in.astype(BF16), preferred_element_type=F32)
        dsk = jnp.concatenate([dsk_ref[pl.ds(r0, GROUP), :]] * (nk // LANES), axis=1)
        y3 = y.reshape(tc, GROUP, nk) + dsk[None] * xg.astype(F32)
        ya = jax.nn.gelu(y3).astype(y_ref.dtype)
        if fold == 1:
            y_ref[:, pl.ds(r0, GROUP), :] = ya
        else:
            for j in range(tc):
                y_ref[j // fold, pl.ds(r0, GROUP), (j % fold) * nk:(j % fold + 1) * nk] = ya[j]

    def body(gi, carry):
        for k in range(GROUPS_PER_TRIP):
            one_group(gi * GROUPS_PER_TRIP + k)
        return carry

    lax.fori_loop(0, gb // GROUPS_PER_TRIP, body, 0)


def _ssm(zb, mp, q, ct, dsk, h0, gb, fold):
    tcf, _, nkf = zb.shape
    tc, nk = tcf * fold, nkf // fold
    ng, mrows, rows = mp.shape
    p2 = mrows - rows
    cb = ng * GROUP
    has_h0 = h0 is not None
    in_specs = [pl.BlockSpec((tcf, gb * GROUP, nkf), lambda i: (0, i, 0)),
                pl.BlockSpec((gb, mrows, rows), lambda i: (i, 0, 0)),
                pl.BlockSpec((gb, rows, p2), lambda i: (i, 0, 0)),
                pl.BlockSpec((gb,) + ct.shape[1:], lambda i: (i, 0, 0)),
                pl.BlockSpec((gb * GROUP, LANES), lambda i: (i, 0))]
    args = [zb, mp, q, ct, dsk]
    if has_h0:
        state_spec = pl.BlockSpec((gb, p2 // 2, nk), lambda i: (i, 0, 0))
        in_specs += [state_spec, state_spec]
        args += list(h0)
        hf_specs = [state_spec, state_spec]
        hf_shapes = [jax.ShapeDtypeStruct((ng, p2 // 2, nk), F32)] * 2
    else:
        hf_specs = [pl.BlockSpec((gb, nk // LANES, LANES), lambda i: (i, 0, 0))]
        hf_shapes = [jax.ShapeDtypeStruct((ng, nk // LANES, LANES), F32)]
    res = pl.pallas_call(
        functools.partial(_ssm_kernel, tc=tc, gb=gb, nk=nk, fold=fold, has_h0=has_h0),
        grid=(ng // gb,),
        in_specs=in_specs,
        out_specs=[pl.BlockSpec((tcf, gb * GROUP, nkf), lambda i: (0, i, 0))] + hf_specs,
        out_shape=[jax.ShapeDtypeStruct((tcf, cb, nkf), BF16)] + hf_shapes,
        compiler_params=_params("parallel"),
    )(*args)
    return res[0], tuple(res[1:])


def _glu_out_kernel(y_ref, gate_ref, wg_ref, bg_ref, wo_ref, r_ref, *ob_refs):
    for t, ob_ref in enumerate(ob_refs):
        ls = slice(t * LANE_TILE, (t + 1) * LANE_TILE)
        y = y_ref[:, ls]
        for m in range(wg_ref.shape[0] // ROW_CHUNK):
            sl = slice(m * ROW_CHUNK, (m + 1) * ROW_CHUNK)
            g = jnp.dot(wg_ref[sl, :], y, preferred_element_type=F32) + bg_ref[sl, :]
            ob = (y_ref[sl, ls].astype(F32) * jax.nn.sigmoid(g)
                  * jax.nn.silu(gate_ref[sl, ls].astype(F32)))
            ob_ref[sl, :] = ob.astype(ob_ref.dtype)
    for t, ob_ref in enumerate(ob_refs):
        ob = ob_ref[...]
        for m in range(wo_ref.shape[0] // ROW_CHUNK):
            sl = slice(m * ROW_CHUNK, (m + 1) * ROW_CHUNK)
            r_ref[t * LANE_TILE:(t + 1) * LANE_TILE, sl] = jnp.dot(
                wo_ref[sl, :], ob, preferred_element_type=F32).T


def _glu_out(y_t, zb, wg_t, bg, wo_t, nkt):
    tc, cb, nk = y_t.shape
    d = wo_t.shape[0]
    return pl.pallas_call(
        _glu_out_kernel,
        grid=(tc, nk // nkt),
        in_specs=[pl.BlockSpec((None, cb, nkt), lambda j, k: (j, 0, k)),
                  pl.BlockSpec((None, cb, nkt), lambda j, k: (j, 1, k)),
                  pl.BlockSpec((cb, cb), lambda j, k: (0, 0), pipeline_mode=pl.Buffered(1)),
                  pl.BlockSpec((cb, LANE_TILE), lambda j, k: (0, 0), pipeline_mode=pl.Buffered(1)),
                  pl.BlockSpec((d, cb), lambda j, k: (0, 0), pipeline_mode=pl.Buffered(1))],
        out_specs=pl.BlockSpec((None, nkt, d), lambda j, k: (j, k, 0)),
        out_shape=jax.ShapeDtypeStruct((tc, nk, d), F32),
        scratch_shapes=[pltpu.VMEM((cb, LANE_TILE), BF16)] * (nkt // LANE_TILE),
        compiler_params=_params("parallel", "parallel"),
    )(y_t, zb, wg_t, bg, wo_t)


def _out_norm_kernel(x_ref, oa_ref, r_ref, wo_ref, g_ref, y_ref, ms_ref, *, tc, n):
    ca = oa_ref.shape[-1]
    d = x_ref.shape[-1]
    nc = d // LANES
    p = jnp.dot(oa_ref[...].reshape(tc * n, ca), wo_ref[...], preferred_element_type=F32)
    for j in range(tc):
        s = p[j * n:(j + 1) * n] + r_ref[j]
        for c in range(nc):
            ms_ref[c, pl.ds(j, n, stride=tc), :] = s[:, c * LANES:(c + 1) * LANES]
    acc = x_ref[...] + jnp.concatenate([ms_ref[c] for c in range(nc)], axis=1)
    ms = jnp.mean(acc * acc, axis=-1, keepdims=True)
    y_ref[...] = acc * lax.rsqrt(ms + EPS) * g_ref[...]


def _out_norm(x2d, oa, r, wo_a, gf, n):
    tc, nk, ca = oa.shape
    t, d = x2d.shape
    return pl.pallas_call(
        functools.partial(_out_norm_kernel, tc=tc, n=n),
        grid=(nk // n,),
        in_specs=[pl.BlockSpec((n * tc, d), lambda i: (i, 0)),
                  pl.BlockSpec((tc, n, ca), lambda i: (0, i, 0)),
                  pl.BlockSpec((tc, n, d), lambda i: (0, i, 0)),
                  pl.BlockSpec((ca, d), lambda i: (0, 0), pipeline_mode=pl.Buffered(1)),
                  pl.BlockSpec((1, d), lambda i: (0, 0))],
        out_specs=pl.BlockSpec((n * tc, d), lambda i: (i, 0)),
        out_shape=jax.ShapeDtypeStruct((t, d), F32),
        scratch_shapes=[pltpu.VMEM((d // LANES, n * tc, LANES), F32)],
        compiler_params=_params("parallel"),
    )(x2d, oa, r, wo_a, gf.reshape(1, d))


def _discretise(a_re, a_im, log_dt):
    dt = jnp.exp(log_dt)[:, None]
    x_re, x_im = a_re * dt, a_im * dt
    n = (2.0 ** jnp.arange(N_POW, dtype=F32))[None, :, None]
    mag = jnp.exp(n * x_re[:, None, :])
    ang = n * x_im[:, None, :]
    pw_r, pw_i = mag * jnp.cos(ang), mag * jnp.sin(ang)
    lr, li = pw_r[:, 0], pw_i[:, 0]
    den = a_re * a_re + a_im * a_im
    nr, ni = lr - 1.0, li
    qr, qi = (nr * a_re + ni * a_im) / den, (ni * a_re - nr * a_im) / den
    cat = lambda a, b: jnp.concatenate([a, b], axis=-1)
    rowtab = jnp.stack([cat(lr, lr), cat(li, li), cat(qr, qi), cat(-qi, qr)], axis=1)
    zpad = jnp.zeros_like(pw_r[:, :POW_COLS - N_POW])
    lt = jnp.concatenate([pw_r, zpad, pw_i, zpad], axis=1).transpose(0, 2, 1)
    b0 = TC_PROMPT.bit_length() - 1
    sc_r, sc_i = pw_r[:, b0:b0 + SCAN_STEPS], pw_i[:, b0:b0 + SCAN_STEPS]
    zrow = jnp.zeros_like(cat(sc_r, sc_r)[:, :SCAN_ROWS - SCAN_STEPS])
    scan_rows = jnp.concatenate([cat(sc_r, sc_r), zrow, cat(sc_i, sc_i), zrow], axis=1)
    return rowtab, lt, scan_rows


def _ssm_tables_kernel(rt_ref, btr_ref, bti_ref, c2r_ref, c2i_ref, c2n_ref,
                       mp_ref, q_ref, mps_ref, qs_ref, *, gb, tc, tcs, unroll):
    rows = tc * GROUP
    rows_s = tcs * GROUP
    half = rt_ref.shape[2] // 2
    lane_k = lax.broadcasted_iota(jnp.int32, (GROUP, rows), 1)
    lane_q = lax.broadcasted_iota(jnp.int32, (1, 2 * half), 1)

    def one_group(g):
        rt = rt_ref[g]
        l_r, l_i, qu, qv = rt[0:1], rt[1:2], rt[2:3], rt[3:4]
        btr, bti = btr_ref[g], bti_ref[g]
        bb = btr * qu + bti * qv
        bs = btr * qv - bti * qu
        c2r, c2i = c2r_ref[g], c2i_ref[g]
        cur_r = jnp.ones((1, 2 * half), F32)
        cur_i = jnp.zeros((1, 2 * half), F32)
        et_blocks = []
        for n in range(tc):
            et_blocks.append(cur_r * bb + cur_i * bs)
            cur_r, cur_i = cur_r * l_r - cur_i * l_i, cur_r * l_i + cur_i * l_r
            u = jnp.where(lane_q < half, cur_r, -cur_i)
            v = jnp.where(lane_q < half, -cur_i, -cur_r)
            blk = (c2r * u + c2i * v).astype(q_ref.dtype)
            q_ref[g, n * GROUP:(n + 1) * GROUP, :] = blk
            if n < tcs:
                qs_ref[g, n * GROUP:(n + 1) * GROUP, :] = blk
        p = jnp.concatenate(et_blocks[::-1], axis=0).T
        mp_ref[g, rows:, :] = p.astype(mp_ref.dtype)
        mps_ref[g, rows_s:, :] = p[:, rows - rows_s:].astype(mps_ref.dtype)
        krow = jnp.dot(c2n_ref[g], p, precision=lax.Precision.HIGHEST, preferred_element_type=F32)
        for i in range(tc):
            sh = (tc - 1 - i) * GROUP
            blk = krow if sh == 0 else pltpu.roll(krow, rows - sh, 1)
            blk = jnp.where(lane_k < (i + 1) * GROUP, blk, 0.0)
            mp_ref[g, i * GROUP:(i + 1) * GROUP, :] = blk.astype(mp_ref.dtype)
            if i < tcs:
                mps_ref[g, i * GROUP:(i + 1) * GROUP, :] = blk[:, :rows_s].astype(mps_ref.dtype)

    def body(gi, carry):
        for k in range(unroll):
            one_group(gi * unroll + k)
        return carry

    lax.fori_loop(0, gb // unroll, body, 0)


def _ssm_tables(rowtab, b_re, b_im, c_re, c_im, tcs, gb):
    ng, _, p2 = rowtab.shape
    tc = TC_PROMPT
    rows, rows_s = tc * GROUP, tcs * GROUP
    dup = lambda a: jnp.concatenate([a, a], axis=-1)
    args = (rowtab, dup(b_re.transpose(0, 2, 1)), dup(b_im.transpose(0, 2, 1)),
            dup(c_re), dup(c_im), jnp.concatenate([c_re, -c_im], axis=-1))
    shapes = [(ng, rows + p2, rows), (ng, rows, p2), (ng, rows_s + p2, rows_s), (ng, rows_s, p2)]
    return pl.pallas_call(
        functools.partial(_ssm_tables_kernel, gb=gb, tc=tc, tcs=tcs, unroll=4),
        grid=(ng // gb,),
        in_specs=[pl.BlockSpec((gb,) + a.shape[1:], lambda i: (i, 0, 0)) for a in args],
        out_specs=[pl.BlockSpec((gb,) + s[1:], lambda i: (i, 0, 0)) for s in shapes],
        out_shape=[jax.ShapeDtypeStruct(s, BF16) for s in shapes],
        compiler_params=_params("parallel"),
    )(*args)


def _mix_bias(b_s, hd, tc):
    nh = b_s.shape[0]
    r = CHUNK // tc
    if tc == TC_PROMPT:
        bm = b_s.reshape(nh, r, tc).transpose(0, 2, 1).reshape(nh, CHUNK)
    else:
        bm = jnp.repeat(b_s[:, :tc], r, axis=1)
    return jnp.broadcast_to(bm[:, :, None], (nh, CHUNK, hd))


def _tiles(tc, nk):
    fold = max(1, LANE_TILE // nk)
    return (NORM_ROWS // tc, min(2 * PROJ_ROWS, tc * nk) // tc, min(2 * LANE_TILE, fold * nk), fold)


def _layer(x2d, tc, h0, wts, w_t):
    (g_norm, w_in, gv, w_s, bmix, mp, q, ct, d_skip, w_glu, b_glu, w_out, wo_a, g_final) = wts
    cb = w_glu.shape[0]
    d = x2d.shape[1]
    nk = x2d.shape[0] // tc
    n_norm, n_a, nkt, fold = _tiles(tc, nk)
    gb = SSM_GROUPS_PER_STEP
    make_w_t = w_t is None
    xn = _norm_permute(x2d, g_norm, tc, n_norm)
    out_a, v_rows, w_bt = _branch_a(xn, w_in, gv, w_s, bmix, n_a, tc * n_a // PROJ_ROWS,
                                    h0 is not None, make_w_t)
    if not make_w_t:
        w_bt, wg_t, wo_bt = w_t
    side = ((w_glu, cb, 0), (w_out, cb, 1)) if make_w_t else ()
    zb, made = _proj_b(w_bt, xn.reshape(tc // fold, fold * nk, d), nkt, side)
    if make_w_t:
        wg_t, wo_bt = made
    dsk = jnp.broadcast_to(d_skip[:, None], (cb, LANES))
    y_t, h_fin = _ssm(zb, mp, q, ct, dsk, h0, gb, fold)
    bg = jnp.broadcast_to(b_glu[:, None], (cb, LANE_TILE))
    r = _glu_out(y_t, zb, wg_t, bg, wo_bt, nkt).reshape(tc, nk, d)
    y = _out_norm(x2d, out_a, r, wo_a, g_final, n_norm)
    return y, h_fin, v_rows, (w_bt, wg_t, wo_bt)


def kernel(x_prompt, x_sample, state_ssm_re, state_ssm_im, g_norm, w_in, g_v, w_s, b_s,
           a_re, a_im, log_dt, b_re, b_im, c_re, c_im, d_skip, w_glu, b_glu, w_out, g_final):
    depth = g_norm.shape[0]
    assert depth == 1, "single-layer step"
    bsz, seq, d = x_prompt.shape
    dbsz, dseq, _ = x_sample.shape
    nh = w_s.shape[1]
    ng, npst = a_re.shape[1:]
    cb = ng * GROUP
    ca = (w_in.shape[2] - 2 * cb) // 3
    hd = ca // nh
    assert seq // TC_PROMPT == LANES and dbsz == LANES and CHUNK % dseq == 0
    assert dseq & (dseq - 1) == 0 and dseq <= TC_PROMPT, "sample chunk must be a power of two"
    assert ca == cb == d

    l = 0
    gv = g_v[l].reshape(nh, 1, hd)
    rowtab, lt, scan_rows = _discretise(a_re[l], a_im[l], log_dt[l])
    mp, q, mp_s, q_s = _ssm_tables(rowtab, b_re[l], b_im[l], c_re[l], c_im[l], dseq, SSM_GROUPS_PER_STEP)
    wo_a = w_out[l][:ca].astype(BF16)

    def weights(tc, mp_, q_, ct):
        return (g_norm[l], w_in[l], gv, w_s[l], _mix_bias(b_s[l], hd, tc), mp_, q_, ct,
                d_skip[l], w_glu[l], b_glu[l], w_out[l], wo_a, g_final)

    assert bsz % 2 == 0
    tc_p, nk_p = TC_PROMPT, bsz * seq // TC_PROMPT
    y_p, hf_p, _, w_t = _layer(x_prompt.reshape(nk_p * tc_p, d), tc_p, None,
                               weights(tc_p, mp, q, scan_rows), None)
    hf_p = hf_p[0].reshape(ng, bsz // 2, 2, 2, npst)
    re_p = hf_p[:, :, 0].reshape(ng, bsz, npst).transpose(1, 0, 2)[None]
    im_p = hf_p[:, :, 1].reshape(ng, bsz, npst).transpose(1, 0, 2)[None]

    tc_s, nk_s = dseq, dbsz
    h0 = (state_ssm_re[l].transpose(1, 2, 0), state_ssm_im[l].transpose(1, 2, 0))
    y_s, hf_s, v_s, _ = _layer(x_sample.reshape(nk_s * tc_s, d), tc_s, h0,
                               weights(tc_s, mp_s, q_s, lt), w_t)
    re_s = hf_s[0].transpose(2, 0, 1)[None]
    im_s = hf_s[1].transpose(2, 0, 1)[None]
    v_out = v_s.transpose(1, 0, 2)[None]

    dt_p, dt_s = x_prompt.dtype, x_sample.dtype
    return (y_p.reshape(bsz, seq, d).astype(dt_p), y_s.reshape(dbsz, dseq, d).astype(dt_s),
            re_p.astype(dt_p), im_p.astype(dt_p), re_s.astype(dt_s), im_s.astype(dt_s),
            v_out.astype(dt_s))
```

```python
import functools

import jax
import jax.numpy as jnp
from jax import lax
from jax.experimental import pallas as pl
from jax.experimental.pallas import tpu as pltpu

EPS = 1e-6
LANES = 128
CHUNK = 128
GROUP = 16
TC_PROMPT = 16
ROW_CHUNK = 512
LANE_TILE = 256
PROJ_ROWS = 1024
NORM_ROWS = 512
SSM_GROUPS_PER_STEP = 8
POW_COLS = 16
SCAN_STEPS = 7
N_POW = 11
SCAN_ROWS = 8
GROUPS_PER_TRIP = 4
VMEM_LIMIT = 56 * 1024 * 1024

F32 = jnp.float32
BF16 = jnp.bfloat16


def _params(*sem):
    return pltpu.CompilerParams(dimension_semantics=sem, vmem_limit_bytes=VMEM_LIMIT)


def _norm_permute_kernel(x_ref, g_ref, *refs, tc, n, n_side_in, side_body):
    o_ref = refs[n_side_in]
    if side_body is not None:
        side_body(*refs[:n_side_in], *refs[n_side_in + 1:])
    x = x_ref[...]
    ms = jnp.mean(x * x, axis=-1, keepdims=True)
    xn = (x * lax.rsqrt(ms + EPS) * g_ref[...]).astype(BF16)
    rows = n * tc
    dst = lax.broadcasted_iota(jnp.int32, (rows, rows), 0)
    src = lax.broadcasted_iota(jnp.int32, (rows, rows), 1)
    sel = (src == tc * (dst % n) + dst // n).astype(BF16)
    o_ref[...] = jnp.dot(sel, xn, preferred_element_type=F32).astype(o_ref.dtype).reshape(o_ref.shape)


def _norm_permute(x2d, g, tc, n, side_job=None):
    t, d = x2d.shape
    nk = t // tc
    s_args, s_in, s_out, s_shape, s_body, s_steps = side_job or ((), [], [], [], None, nk // n)
    assert s_steps == nk // n, "the side job needs one block per grid step"
    res = pl.pallas_call(
        functools.partial(_norm_permute_kernel, tc=tc, n=n, n_side_in=len(s_args), side_body=s_body),
        grid=(nk // n,),
        in_specs=[pl.BlockSpec((n * tc, d), lambda i: (i, 0)),
                  pl.BlockSpec((1, d), lambda i: (0, 0))] + list(s_in),
        out_specs=[pl.BlockSpec((tc, n, d), lambda i: (0, i, 0))] + list(s_out),
        out_shape=[jax.ShapeDtypeStruct((tc, nk, d), BF16)] + list(s_shape),
        compiler_params=_params("parallel"),
    )(x2d, g.reshape(1, d), *s_args)
    return res[0], tuple(res[1:])


def _branch_a_kernel(*refs, tc, n, r, hd, n_sub, has_v, has_side):
    refs = list(refs)
    xn_ref, wu_ref, wv_ref, wg_ref, gv_ref, ws_ref, bmix_ref = refs[:7]
    del refs[:7]
    side_in = [refs.pop(0)] if has_side else []
    oa_ref = refs.pop(0)
    v_refs = [refs.pop(0)] if has_v else []
    side_out = [refs.pop(0)] if has_side else []
    w_ref, wm_ref = refs
    _transpose_side_blocks(side_in, side_out)
    d = xn_ref.shape[-1]

    @pl.when(pl.program_id(1) == 0)
    def _():
        w_ref[:, :hd] = wu_ref[...].astype(BF16)
        w_ref[:, hd:2 * hd] = wv_ref[...].astype(BF16)
        w_ref[:, 2 * hd:] = wg_ref[...].astype(BF16)
        row = lax.broadcasted_iota(jnp.int32, (CHUNK, CHUNK), 0)
        col = lax.broadcasted_iota(jnp.int32, (CHUNK, CHUNK), 1)
        w_tril = jnp.where(row >= col, ws_ref[...], 0.0).astype(BF16)
        if tc == TC_PROMPT:
            time_of_row = tc * (row % r) + row // r
            same_seq = None
        else:
            time_of_row = row // r
            same_seq = (row % r) == (col % r)
        sel = (col == time_of_row).astype(BF16)
        sw = jnp.dot(sel, w_tril, preferred_element_type=F32).astype(BF16)
        wm = lax.dot_general(sw, sel, (((1,), (1,)), ((), ())), preferred_element_type=F32)
        if same_seq is not None:
            wm = jnp.where(same_seq, wm, 0.0)
        wm_ref[...] = wm.astype(BF16)

    ns = n // n_sub
    zs = [jnp.dot(xn_ref[:, k * ns:(k + 1) * ns, :].reshape(tc * ns, d), w_ref[...],
                  preferred_element_type=F32) for k in range(n_sub)]
    wm = wm_ref[...]
    bm = bmix_ref[...]
    for k, z in enumerate(zs):
        k0 = k * ns
        u = jax.nn.gelu(z[:, :hd])
        v = jax.nn.gelu(z[:, hd:2 * hd])
        ga = z[:, 2 * hd:]
        ms = jnp.mean(v * v, axis=-1, keepdims=True)
        vn3 = (v * lax.rsqrt(ms + EPS) * gv_ref[...]).reshape(tc, ns, hd)
        pre3 = (u * jax.nn.silu(ga)).reshape(tc, ns, hd)
        if v_refs:
            v_refs[0][:, k0:k0 + ns, :] = vn3
        for c in range(ns // r):
            vg = vn3[:, c * r:(c + 1) * r, :].reshape(CHUNK, hd)
            mixed = jnp.dot(wm, vg.astype(BF16), preferred_element_type=F32) + bm
            og = pre3[:, c * r:(c + 1) * r, :].reshape(CHUNK, hd) * mixed
            oa_ref[:, k0 + c * r:k0 + (c + 1) * r, :] = og.reshape(tc, r, hd).astype(oa_ref.dtype)


def _branch_a(xn, w_in, gv, w_s, bmix, n, n_sub, want_v, transpose_b):
    tc, nk, d = xn.shape
    nh, _, hd = gv.shape
    r = CHUNK // tc
    nt = nk // n
    out_map = lambda h, i: (0, i, h)
    in_specs = [pl.BlockSpec((tc, n, d), lambda h, i: (0, i, 0)),
                pl.BlockSpec((d, hd), lambda h, i: (0, h)),
                pl.BlockSpec((d, hd), lambda h, i: (0, nh + h)),
                pl.BlockSpec((d, hd), lambda h, i: (0, 2 * nh + h)),
                pl.BlockSpec((None, 1, hd), lambda h, i: (h, 0, 0)),
                pl.BlockSpec((None, CHUNK, CHUNK), lambda h, i: (h, 0, 0)),
                pl.BlockSpec((None, CHUNK, hd), lambda h, i: (h, 0, 0))]
    args = [xn, w_in, w_in, w_in, gv, w_s, bmix]
    out_shape = [jax.ShapeDtypeStruct((tc, nk, nh * hd), BF16)]
    out_specs = [pl.BlockSpec((tc, n, hd), out_map)]
    if want_v:
        out_shape.append(jax.ShapeDtypeStruct((tc, nk, nh * hd), F32))
        out_specs.append(pl.BlockSpec((tc, n, hd), out_map))
    if transpose_b:
        col0 = 3 * nh * hd // LANES
        n_cols = w_in.shape[1] - 3 * nh * hd
        assert nh * nt * LANES == n_cols, "one 128-column block per grid step"
        in_specs.append(pl.BlockSpec((d, LANES), lambda h, i: (0, col0 + h * nt + i)))
        args.append(w_in)
        out_shape.append(jax.ShapeDtypeStruct((n_cols, d), BF16))
        out_specs.append(pl.BlockSpec((LANES, d), lambda h, i: (h * nt + i, 0)))
    res = list(pl.pallas_call(
        functools.partial(_branch_a_kernel, tc=tc, n=n, r=r, hd=hd, n_sub=n_sub,
                          has_v=want_v, has_side=transpose_b),
        grid=(nh, nt),
        in_specs=in_specs,
        out_specs=out_specs,
        out_shape=out_shape,
        scratch_shapes=[pltpu.VMEM((d, 3 * hd), BF16), pltpu.VMEM((CHUNK, CHUNK), BF16)],
        compiler_params=_params("parallel", "arbitrary"),
    )(*args))
    out_a = res.pop(0)
    v_rows = res.pop(0) if want_v else None
    w_bt = res.pop(0) if transpose_b else None
    return out_a, v_rows, w_bt


def _transpose_side_blocks(in_refs, out_refs):
    for w_ref, o_ref in zip(in_refs, out_refs):
        o_ref[...] = w_ref[...].T.astype(o_ref.dtype)


def _proj_b_kernel(w_ref, xn_ref, *refs):
    n_side = (len(refs) - 1) // 2
    o_ref = refs[n_side]
    _transpose_side_blocks(refs[:n_side], refs[n_side + 1:])
    for t in range(xn_ref.shape[0] // LANE_TILE):
        ls = slice(t * LANE_TILE, (t + 1) * LANE_TILE)
        xn = xn_ref[ls, :]
        for m in range(w_ref.shape[0] // ROW_CHUNK):
            sl = slice(m * ROW_CHUNK, (m + 1) * ROW_CHUNK)
            o_ref[sl, ls] = lax.dot_general(
                w_ref[sl, :], xn, (((1,), (1,)), ((), ())),
                preferred_element_type=F32).astype(o_ref.dtype)


def _proj_b(w_bt, xn, nkt, side=()):
    tc, nk, d = xn.shape
    rows = w_bt.shape[0]
    nkb = nk // nkt
    in_specs = [pl.BlockSpec((rows, d), lambda j, k: (0, 0), pipeline_mode=pl.Buffered(1)),
                pl.BlockSpec((None, nkt, d), lambda j, k: (j, k, 0))]
    out_specs = [pl.BlockSpec((None, rows, nkt), lambda j, k: (j, 0, k))]
    out_shape = [jax.ShapeDtypeStruct((tc, rows, nk), BF16)]
    for w, n_rows, row_block in side:
        n_cols = w.shape[1]
        assert tc * nkb * LANES == n_cols, "one 128-column block per grid step"
        in_specs.append(pl.BlockSpec((n_rows, LANES), lambda j, k, rb=row_block: (rb, j * nkb + k)))
        out_specs.append(pl.BlockSpec((LANES, n_rows), lambda j, k: (j * nkb + k, 0)))
        out_shape.append(jax.ShapeDtypeStruct((n_cols, n_rows), BF16))
    res = pl.pallas_call(
        _proj_b_kernel,
        grid=(tc, nkb),
        in_specs=in_specs,
        out_specs=out_specs,
        out_shape=out_shape,
        compiler_params=_params("parallel", "parallel"),
    )(w_bt, xn, *[w for w, _, _ in side])
    return res[0], tuple(res[1:])


def _shift_rows(x, d):
    if d % 8 == 0:
        return jnp.concatenate([jnp.zeros((d, x.shape[1]), x.dtype), x[:-d]], axis=0)
    row = lax.broadcasted_iota(jnp.int32, x.shape, 0)
    return jnp.where(row >= d, pltpu.roll(x, d, 0), 0.0)


def _ssm_kernel(x_ref, mp_ref, q_ref, ct_ref, dsk_ref, *rest, tc, gb, nk, fold, has_h0):
    if has_h0:
        h0r_ref, h0i_ref, y_ref, hfr_ref, hfi_ref = rest
    else:
        y_ref, hf_ref = rest
    rows = tc * GROUP
    half = q_ref.shape[2] // 2
    b_tc = tc.bit_length() - 1

    def one_group(g):
        r0 = pl.multiple_of(g * GROUP, GROUP)
        if fold == 1:
            xg = x_ref[:, pl.ds(r0, GROUP), :]
        else:
            xg = jnp.stack([x_ref[j // fold, pl.ds(r0, GROUP), (j % fold) * nk:(j % fold + 1) * nk]
                            for j in range(tc)])
        res = jnp.dot(mp_ref[g], xg.reshape(rows, nk), preferred_element_type=F32)
        y_local = res[:rows]
        s_re = res[rows:rows + half]
        s_im = res[rows + half:]
        ct = ct_ref[g]
        if has_h0:
            h_re, h_im = h0r_ref[g], h0i_ref[g]
            a_re, a_im = ct[:, b_tc:b_tc + 1], ct[:, POW_COLS + b_tc:POW_COLS + b_tc + 1]
            hfr_ref[g] = a_re * h_re - a_im * h_im + s_re
            hfi_ref[g] = a_re * h_im + a_im * h_re + s_im
            h_in = jnp.concatenate([h_re, h_im], axis=0)
        else:
            tiles = []
            for pair in range(nk // (2 * LANES)):
                lo, mid, hi = 2 * pair * LANES, (2 * pair + 1) * LANES, (2 * pair + 2) * LANES
                xr = jnp.concatenate([s_re[:, lo:mid], s_re[:, mid:hi]], axis=0).T
                xi = jnp.concatenate([s_im[:, lo:mid], s_im[:, mid:hi]], axis=0).T
                for s in range(SCAN_STEPS):
                    cr, ci = ct[s:s + 1], ct[SCAN_ROWS + s:SCAN_ROWS + s + 1]
                    rr, ri = _shift_rows(xr, 1 << s), _shift_rows(xi, 1 << s)
                    xr, xi = xr + cr * rr - ci * ri, xi + cr * ri + ci * rr
                hf_ref[g, 2 * pair:2 * pair + 1, :] = xr[LANES - 1:]
                hf_ref[g, 2 * pair + 1:2 * pair + 2, :] = xi[LANES - 1:]
                hr_t = _shift_rows(xr, 1).T
                hi_t = _shift_rows(xi, 1).T
                tiles.append(jnp.concatenate([hr_t[:half], hi_t[:half]], axis=0))
                tiles.append(jnp.concatenate([hr_t[half:], hi_t[half:]], axis=0))
            h_in = jnp.concatenate(tiles, axis=1)
        y = y_local + jnp.dot(q_ref[g], h_in.astype(BF16), preferred_element_type=F32)
        dsk = jnp.concatenate([dsk_ref[pl.ds(r0, GROUP), :]] * (nk // LANES), axis=1)
        y3 = y.reshape(tc, GROUP, nk) + dsk[None] * xg.astype(F32)
        ya = jax.nn.gelu(y3).astype(y_ref.dtype)
        if fold == 1:
            y_ref[:, pl.ds(r0, GROUP), :] = ya
        else:
            for j in range(tc):
                y_ref[j // fold, pl.ds(r0, GROUP), (j % fold) * nk:(j % fold + 1) * nk] = ya[j]

    def body(gi, carry):
        for k in range(GROUPS_PER_TRIP):
            one_group(gi * GROUPS_PER_TRIP + k)
        return carry

    lax.fori_loop(0, gb // GROUPS_PER_TRIP, body, 0)


def _ssm(zb, mp, q, ct, dsk, h0, gb, fold):
    tcf, _, nkf = zb.shape
    tc, nk = tcf * fold, nkf // fold
    ng, mrows, rows = mp.shape
    p2 = mrows - rows
    cb = ng * GROUP
    has_h0 = h0 is not None
    in_specs = [pl.BlockSpec((tcf, gb * GROUP, nkf), lambda i: (0, i, 0)),
                pl.BlockSpec((gb, mrows, rows), lambda i: (i, 0, 0)),
                pl.BlockSpec((gb, rows, p2), lambda i: (i, 0, 0)),
                pl.BlockSpec((gb,) + ct.shape[1:], lambda i: (i, 0, 0)),
                pl.BlockSpec((gb * GROUP, LANES), lambda i: (i, 0))]
    args = [zb, mp, q, ct, dsk]
    if has_h0:
        state_spec = pl.BlockSpec((gb, p2 // 2, nk), lambda i: (i, 0, 0))
        in_specs += [state_spec, state_spec]
        args += list(h0)
        hf_specs = [state_spec, state_spec]
        hf_shapes = [jax.ShapeDtypeStruct((ng, p2 // 2, nk), F32)] * 2
    else:
        hf_specs = [pl.BlockSpec((gb, nk // LANES, LANES), lambda i: (i, 0, 0))]
        hf_shapes = [jax.ShapeDtypeStruct((ng, nk // LANES, LANES), F32)]
    res = pl.pallas_call(
        functools.partial(_ssm_kernel, tc=tc, gb=gb, nk=nk, fold=fold, has_h0=has_h0),
        grid=(ng // gb,),
        in_specs=in_specs,
        out_specs=[pl.BlockSpec((tcf, gb * GROUP, nkf), lambda i: (0, i, 0))] + hf_specs,
        out_shape=[jax.ShapeDtypeStruct((tcf, cb, nkf), BF16)] + hf_shapes,
        compiler_params=_params("parallel"),
    )(*args)
    return res[0], tuple(res[1:])


def _glu_out_kernel(y_ref, gate_ref, wg_ref, bg_ref, wo_ref, *refs, has_cast):
    refs = list(refs)
    cast_in = refs.pop(0) if has_cast else None
    r_ref = refs.pop(0)
    if has_cast:
        cast_out = refs.pop(0)
        cast_out[...] = cast_in[...].astype(cast_out.dtype)
    ob_refs = refs
    for t, ob_ref in enumerate(ob_refs):
        ls = slice(t * LANE_TILE, (t + 1) * LANE_TILE)
        y = y_ref[:, ls]
        for m in range(wg_ref.shape[0] // ROW_CHUNK):
            sl = slice(m * ROW_CHUNK, (m + 1) * ROW_CHUNK)
            g = jnp.dot(wg_ref[sl, :], y, preferred_element_type=F32) + bg_ref[sl, :]
            ob = (y_ref[sl, ls].astype(F32) * jax.nn.sigmoid(g)
                  * jax.nn.silu(gate_ref[sl, ls].astype(F32)))
            ob_ref[sl, :] = ob.astype(ob_ref.dtype)
    for t, ob_ref in enumerate(ob_refs):
        ob = ob_ref[...]
        for m in range(wo_ref.shape[0] // ROW_CHUNK):
            sl = slice(m * ROW_CHUNK, (m + 1) * ROW_CHUNK)
            r_ref[t * LANE_TILE:(t + 1) * LANE_TILE, sl] = jnp.dot(
                wo_ref[sl, :], ob, preferred_element_type=F32).T


def _glu_out(y_t, zb, wg_t, bg, wo_t, nkt, cast_rows=None):
    tc, cb, nk = y_t.shape
    d = wo_t.shape[0]
    nkb = nk // nkt
    in_specs = [pl.BlockSpec((None, cb, nkt), lambda j, k: (j, 0, k)),
                pl.BlockSpec((None, cb, nkt), lambda j, k: (j, 1, k)),
                pl.BlockSpec((cb, cb), lambda j, k: (0, 0), pipeline_mode=pl.Buffered(1)),
                pl.BlockSpec((cb, LANE_TILE), lambda j, k: (0, 0), pipeline_mode=pl.Buffered(1)),
                pl.BlockSpec((d, cb), lambda j, k: (0, 0), pipeline_mode=pl.Buffered(1))]
    args = [y_t, zb, wg_t, bg, wo_t]
    out_specs = [pl.BlockSpec((None, nkt, d), lambda j, k: (j, k, 0))]
    out_shape = [jax.ShapeDtypeStruct((tc, nk, d), F32)]
    if cast_rows is not None:
        w, n_rows = cast_rows
        assert tc * nkb * LANES == n_rows, "one 128-row block per grid step"
        blk = pl.BlockSpec((LANES, w.shape[1]), lambda j, k: (j * nkb + k, 0))
        in_specs.append(blk)
        args.append(w)
        out_specs.append(blk)
        out_shape.append(jax.ShapeDtypeStruct((n_rows, w.shape[1]), BF16))
    res = pl.pallas_call(
        functools.partial(_glu_out_kernel, has_cast=cast_rows is not None),
        grid=(tc, nkb),
        in_specs=in_specs,
        out_specs=out_specs,
        out_shape=out_shape,
        scratch_shapes=[pltpu.VMEM((cb, LANE_TILE), BF16)] * (nkt // LANE_TILE),
        compiler_params=_params("parallel", "parallel"),
    )(*args)
    return res[0], (res[1] if cast_rows is not None else None)


def _out_norm_kernel(x_ref, oa_ref, r_ref, wo_ref, g_ref, y_ref, ms_ref, *, tc, n):
    ca = oa_ref.shape[-1]
    d = x_ref.shape[-1]
    nc = d // LANES
    p = jnp.dot(oa_ref[...].reshape(tc * n, ca), wo_ref[...], preferred_element_type=F32)
    for j in range(tc):
        s = p[j * n:(j + 1) * n] + r_ref[j]
        for c in range(nc):
            ms_ref[c, pl.ds(j, n, stride=tc), :] = s[:, c * LANES:(c + 1) * LANES]
    acc = x_ref[...] + jnp.concatenate([ms_ref[c] for c in range(nc)], axis=1)
    ms = jnp.mean(acc * acc, axis=-1, keepdims=True)
    y_ref[...] = acc * lax.rsqrt(ms + EPS) * g_ref[...]


def _out_norm(x2d, oa, r, wo_a, gf, n):
    tc, nk, ca = oa.shape
    t, d = x2d.shape
    return pl.pallas_call(
        functools.partial(_out_norm_kernel, tc=tc, n=n),
        grid=(nk // n,),
        in_specs=[pl.BlockSpec((n * tc, d), lambda i: (i, 0)),
                  pl.BlockSpec((tc, n, ca), lambda i: (0, i, 0)),
                  pl.BlockSpec((tc, n, d), lambda i: (0, i, 0)),
                  pl.BlockSpec((ca, d), lambda i: (0, 0), pipeline_mode=pl.Buffered(1)),
                  pl.BlockSpec((1, d), lambda i: (0, 0))],
        out_specs=pl.BlockSpec((n * tc, d), lambda i: (i, 0)),
        out_shape=jax.ShapeDtypeStruct((t, d), F32),
        scratch_shapes=[pltpu.VMEM((d // LANES, n * tc, LANES), F32)],
        compiler_params=_params("parallel"),
    )(x2d, oa, r, wo_a, gf.reshape(1, d))


def _discretise(a_re, a_im, log_dt):
    dt = jnp.exp(log_dt)[:, None]
    x_re, x_im = a_re * dt, a_im * dt
    n = (2.0 ** jnp.arange(N_POW, dtype=F32))[None, :, None]
    mag = jnp.exp(n * x_re[:, None, :])
    ang = n * x_im[:, None, :]
    pw_r, pw_i = mag * jnp.cos(ang), mag * jnp.sin(ang)
    lr, li = pw_r[:, 0], pw_i[:, 0]
    den = a_re * a_re + a_im * a_im
    nr, ni = lr - 1.0, li
    qr, qi = (nr * a_re + ni * a_im) / den, (ni * a_re - nr * a_im) / den
    cat = lambda a, b: jnp.concatenate([a, b], axis=-1)
    rowtab = jnp.stack([cat(lr, lr), cat(li, li), cat(qr, qi), cat(-qi, qr)], axis=1)
    zpad = jnp.zeros_like(pw_r[:, :POW_COLS - N_POW])
    lt = jnp.concatenate([pw_r, zpad, pw_i, zpad], axis=1).transpose(0, 2, 1)
    b0 = TC_PROMPT.bit_length() - 1
    sc_r, sc_i = pw_r[:, b0:b0 + SCAN_STEPS], pw_i[:, b0:b0 + SCAN_STEPS]
    zrow = jnp.zeros_like(cat(sc_r, sc_r)[:, :SCAN_ROWS - SCAN_STEPS])
    scan_rows = jnp.concatenate([cat(sc_r, sc_r), zrow, cat(sc_i, sc_i), zrow], axis=1)
    return rowtab, lt, scan_rows


def _ssm_tables_kernel(rt_ref, btr_ref, bti_ref, c2r_ref, c2i_ref, c2n_ref,
                       mp_ref, q_ref, mps_ref, qs_ref, *, gb, tc, tcs, unroll):
    rows = tc * GROUP
    rows_s = tcs * GROUP
    half = rt_ref.shape[2] // 2
    lane_k = lax.broadcasted_iota(jnp.int32, (GROUP, rows), 1)
    lane_q = lax.broadcasted_iota(jnp.int32, (1, 2 * half), 1)

    def one_group(g):
        rt = rt_ref[g]
        l_r, l_i, qu, qv = rt[0:1], rt[1:2], rt[2:3], rt[3:4]
        btr, bti = btr_ref[g], bti_ref[g]
        bb = btr * qu + bti * qv
        bs = btr * qv - bti * qu
        c2r, c2i = c2r_ref[g], c2i_ref[g]
        cur_r = jnp.ones((1, 2 * half), F32)
        cur_i = jnp.zeros((1, 2 * half), F32)
        et_blocks = []
        for n in range(tc):
            et_blocks.append(cur_r * bb + cur_i * bs)
            cur_r, cur_i = cur_r * l_r - cur_i * l_i, cur_r * l_i + cur_i * l_r
            u = jnp.where(lane_q < half, cur_r, -cur_i)
            v = jnp.where(lane_q < half, -cur_i, -cur_r)
            blk = (c2r * u + c2i * v).astype(q_ref.dtype)
            q_ref[g, n * GROUP:(n + 1) * GROUP, :] = blk
            if n < tcs:
                qs_ref[g, n * GROUP:(n + 1) * GROUP, :] = blk
        p = jnp.concatenate(et_blocks[::-1], axis=0).T
        mp_ref[g, rows:, :] = p.astype(mp_ref.dtype)
        mps_ref[g, rows_s:, :] = p[:, rows - rows_s:].astype(mps_ref.dtype)
        krow = jnp.dot(c2n_ref[g], p, precision=lax.Precision.HIGHEST, preferred_element_type=F32)
        for i in range(tc):
            sh = (tc - 1 - i) * GROUP
            blk = krow if sh == 0 else pltpu.roll(krow, rows - sh, 1)
            blk = jnp.where(lane_k < (i + 1) * GROUP, blk, 0.0)
            mp_ref[g, i * GROUP:(i + 1) * GROUP, :] = blk.astype(mp_ref.dtype)
            if i < tcs:
                mps_ref[g, i * GROUP:(i + 1) * GROUP, :] = blk[:, :rows_s].astype(mps_ref.dtype)

    def body(gi, carry):
        for k in range(unroll):
            one_group(gi * unroll + k)
        return carry

    lax.fori_loop(0, gb // unroll, body, 0)


def _ssm_tables_job(rowtab, b_re, b_im, c_re, c_im, tcs, gb):
    ng, _, p2 = rowtab.shape
    tc = TC_PROMPT
    rows, rows_s = tc * GROUP, tcs * GROUP
    dup = lambda a: jnp.concatenate([a, a], axis=-1)
    args = (rowtab, dup(b_re.transpose(0, 2, 1)), dup(b_im.transpose(0, 2, 1)),
            dup(c_re), dup(c_im), jnp.concatenate([c_re, -c_im], axis=-1))
    shapes = [(ng, rows + p2, rows), (ng, rows, p2), (ng, rows_s + p2, rows_s), (ng, rows_s, p2)]
    in_specs = [pl.BlockSpec((gb,) + a.shape[1:], lambda i: (i, 0, 0)) for a in args]
    out_specs = [pl.BlockSpec((gb,) + s[1:], lambda i: (i, 0, 0)) for s in shapes]
    out_shape = [jax.ShapeDtypeStruct(s, BF16) for s in shapes]
    body = functools.partial(_ssm_tables_kernel, gb=gb, tc=tc, tcs=tcs, unroll=4)
    return args, in_specs, out_specs, out_shape, body, ng // gb


def _mix_bias(b_s, hd, tc):
    nh = b_s.shape[0]
    r = CHUNK // tc
    if tc == TC_PROMPT:
        bm = b_s.reshape(nh, r, tc).transpose(0, 2, 1).reshape(nh, CHUNK)
    else:
        bm = jnp.repeat(b_s[:, :tc], r, axis=1)
    return jnp.broadcast_to(bm[:, :, None], (nh, CHUNK, hd))


def _tiles(tc, nk):
    fold = max(1, LANE_TILE // nk)
    return (NORM_ROWS // tc, min(2 * PROJ_ROWS, tc * nk) // tc, min(2 * LANE_TILE, fold * nk), fold)


def _layer(x2d, tc, h0, wts, ct, tables_job, shared):
    (g_norm, w_in, gv, w_s, bmix, d_skip, w_glu, b_glu, w_out, g_final) = wts
    cb = w_glu.shape[0]
    d = x2d.shape[1]
    nk = x2d.shape[0] // tc
    n_norm, n_a, nkt, fold = _tiles(tc, nk)
    gb = SSM_GROUPS_PER_STEP
    make_w_t = shared is None
    xn, tabs = _norm_permute(x2d, g_norm, tc, n_norm, tables_job if make_w_t else None)
    if make_w_t:
        mp, q, mp_s, q_s = tabs
    else:
        (w_bt, wg_t, wo_bt, wo_a), (mp, q) = shared
        mp_s, q_s = mp, q
    out_a, v_rows, made_bt = _branch_a(xn, w_in, gv, w_s, bmix, n_a, tc * n_a // PROJ_ROWS,
                                       h0 is not None, make_w_t)
    if make_w_t:
        w_bt = made_bt
    side = ((w_glu, cb, 0), (w_out, cb, 1)) if make_w_t else ()
    zb, made = _proj_b(w_bt, xn.reshape(tc // fold, fold * nk, d), nkt, side)
    if make_w_t:
        wg_t, wo_bt = made
    dsk = jnp.broadcast_to(d_skip[:, None], (cb, LANES))
    y_t, h_fin = _ssm(zb, mp, q, ct, dsk, h0, gb, fold)
    bg = jnp.broadcast_to(b_glu[:, None], (cb, LANE_TILE))
    ca = w_out.shape[0] - cb
    r, made_a = _glu_out(y_t, zb, wg_t, bg, wo_bt, nkt, (w_out, ca) if make_w_t else None)
    if make_w_t:
        wo_a = made_a
    y = _out_norm(x2d, out_a, r.reshape(tc, nk, d), wo_a, g_final, n_norm)
    return y, h_fin, v_rows, ((w_bt, wg_t, wo_bt, wo_a), (mp_s, q_s))


def kernel(x_prompt, x_sample, state_ssm_re, state_ssm_im, g_norm, w_in, g_v, w_s, b_s,
           a_re, a_im, log_dt, b_re, b_im, c_re, c_im, d_skip, w_glu, b_glu, w_out, g_final):
    depth = g_norm.shape[0]
    assert depth == 1, "single-layer step"
    bsz, seq, d = x_prompt.shape
    dbsz, dseq, _ = x_sample.shape
    nh = w_s.shape[1]
    ng, npst = a_re.shape[1:]
    cb = ng * GROUP
    ca = (w_in.shape[2] - 2 * cb) // 3
    hd = ca // nh
    assert seq // TC_PROMPT == LANES and dbsz == LANES and CHUNK % dseq == 0
    assert dseq & (dseq - 1) == 0 and dseq <= TC_PROMPT, "sample chunk must be a power of two"
    assert ca == cb == d

    l = 0
    gv = g_v[l].reshape(nh, 1, hd)
    rowtab, lt, scan_rows = _discretise(a_re[l], a_im[l], log_dt[l])
    tables_job = _ssm_tables_job(rowtab, b_re[l], b_im[l], c_re[l], c_im[l], dseq, SSM_GROUPS_PER_STEP)

    def weights(tc):
        return (g_norm[l], w_in[l], gv, w_s[l], _mix_bias(b_s[l], hd, tc),
                d_skip[l], w_glu[l], b_glu[l], w_out[l], g_final)

    assert bsz % 2 == 0
    tc_p, nk_p = TC_PROMPT, bsz * seq // TC_PROMPT
    y_p, hf_p, _, shared = _layer(x_prompt.reshape(nk_p * tc_p, d), tc_p, None,
                                  weights(tc_p), scan_rows, tables_job, None)
    hf_p = hf_p[0].reshape(ng, bsz // 2, 2, 2, npst)
    re_p = hf_p[:, :, 0].reshape(ng, bsz, npst).transpose(1, 0, 2)[None]
    im_p = hf_p[:, :, 1].reshape(ng, bsz, npst).transpose(1, 0, 2)[None]

    tc_s, nk_s = dseq, dbsz
    h0 = (state_ssm_re[l].transpose(1, 2, 0), state_ssm_im[l].transpose(1, 2, 0))
    y_s, hf_s, v_s, _ = _layer(x_sample.reshape(nk_s * tc_s, d), tc_s, h0,
                               weights(tc_s), lt, None, shared)
    re_s = hf_s[0].transpose(2, 0, 1)[None]
    im_s = hf_s[1].transpose(2, 0, 1)[None]
    v_out = v_s.transpose(1, 0, 2)[None]

    dt_p, dt_s = x_prompt.dtype, x_sample.dtype
    return (y_p.reshape(bsz, seq, d).astype(dt_p), y_s.reshape(dbsz, dseq, d).astype(dt_s),
            re_p.astype(dt_p), im_p.astype(dt_p), re_s.astype(dt_s), im_s.astype(dt_s),
            v_out.astype(dt_s))
```

```python
import functools

import jax
import jax.numpy as jnp
from jax import lax
from jax.experimental import pallas as pl
from jax.experimental.pallas import tpu as pltpu

EPS = 1e-6
LANES = 128
CHUNK = 128
GROUP = 16
TC_PROMPT = 16
ROW_CHUNK = 512
LANE_TILE = 256
PROJ_ROWS = 1024
NORM_ROWS = 512
SSM_GROUPS_PER_STEP = 8
POW_COLS = 16
SCAN_STEPS = 7
N_POW = 11
SCAN_ROWS = 8
GROUPS_PER_TRIP = 8
VMEM_LIMIT = 56 * 1024 * 1024

F32 = jnp.float32
BF16 = jnp.bfloat16


def _params(*sem):
    return pltpu.CompilerParams(dimension_semantics=sem, vmem_limit_bytes=VMEM_LIMIT)


def _norm_permute_kernel(x_ref, g_ref, *refs, tc, n, n_side_in, side_body):
    o_ref = refs[n_side_in]
    x = x_ref[...]
    ms = jnp.mean(x * x, axis=-1, keepdims=True)
    xn = (x * lax.rsqrt(ms + EPS) * g_ref[...]).astype(BF16)
    rows = n * tc
    dst = lax.broadcasted_iota(jnp.int32, (rows, rows), 0)
    src = lax.broadcasted_iota(jnp.int32, (rows, rows), 1)
    sel = (src == tc * (dst % n) + dst // n).astype(BF16)
    o_ref[...] = jnp.dot(sel, xn, preferred_element_type=F32).astype(o_ref.dtype).reshape(o_ref.shape)
    if side_body is not None:
        side_body(*refs[:n_side_in], *refs[n_side_in + 1:])


def _norm_permute(x2d, g, tc, n, side_job=None):
    t, d = x2d.shape
    nk = t // tc
    s_args, s_in, s_out, s_shape, s_body, s_steps = side_job or ((), [], [], [], None, nk // n)
    assert s_steps == nk // n, "the side job needs one block per grid step"
    res = pl.pallas_call(
        functools.partial(_norm_permute_kernel, tc=tc, n=n, n_side_in=len(s_args), side_body=s_body),
        grid=(nk // n,),
        in_specs=[pl.BlockSpec((n * tc, d), lambda i: (i, 0)),
                  pl.BlockSpec((1, d), lambda i: (0, 0))] + list(s_in),
        out_specs=[pl.BlockSpec((tc, n, d), lambda i: (0, i, 0))] + list(s_out),
        out_shape=[jax.ShapeDtypeStruct((tc, nk, d), BF16)] + list(s_shape),
        compiler_params=_params("parallel"),
    )(x2d, g.reshape(1, d), *s_args)
    return res[0], tuple(res[1:])


def _branch_a_kernel(*refs, tc, n, r, hd, n_sub, has_v, has_side):
    refs = list(refs)
    xn_ref, wu_ref, wv_ref, wg_ref, gv_ref, ws_ref, bmix_ref = refs[:7]
    del refs[:7]
    side_in = [refs.pop(0)] if has_side else []
    oa_ref = refs.pop(0)
    v_refs = [refs.pop(0)] if has_v else []
    side_out = [refs.pop(0)] if has_side else []
    w_ref, wm_ref = refs
    _transpose_side_blocks(side_in, side_out)
    d = xn_ref.shape[-1]

    @pl.when(pl.program_id(1) == 0)
    def _():
        w_ref[:, :hd] = wu_ref[...].astype(BF16)
        w_ref[:, hd:2 * hd] = wv_ref[...].astype(BF16)
        w_ref[:, 2 * hd:] = wg_ref[...].astype(BF16)
        row = lax.broadcasted_iota(jnp.int32, (CHUNK, CHUNK), 0)
        col = lax.broadcasted_iota(jnp.int32, (CHUNK, CHUNK), 1)
        w_tril = jnp.where(row >= col, ws_ref[...], 0.0).astype(BF16)
        if tc == TC_PROMPT:
            time_of_row = tc * (row % r) + row // r
            same_seq = None
        else:
            time_of_row = row // r
            same_seq = (row % r) == (col % r)
        sel = (col == time_of_row).astype(BF16)
        sw = jnp.dot(sel, w_tril, preferred_element_type=F32).astype(BF16)
        wm = lax.dot_general(sw, sel, (((1,), (1,)), ((), ())), preferred_element_type=F32)
        if same_seq is not None:
            wm = jnp.where(same_seq, wm, 0.0)
        wm_ref[...] = wm.astype(BF16)

    ns = n // n_sub
    zs = [jnp.dot(xn_ref[:, k * ns:(k + 1) * ns, :].reshape(tc * ns, d), w_ref[...],
                  preferred_element_type=F32) for k in range(n_sub)]
    wm = wm_ref[...]
    bm = bmix_ref[...]
    for k, z in enumerate(zs):
        k0 = k * ns
        u = jax.nn.gelu(z[:, :hd])
        v = jax.nn.gelu(z[:, hd:2 * hd])
        ga = z[:, 2 * hd:]
        ms = jnp.mean(v * v, axis=-1, keepdims=True)
        vn3 = (v * lax.rsqrt(ms + EPS) * gv_ref[...]).reshape(tc, ns, hd)
        pre3 = (u * jax.nn.silu(ga)).reshape(tc, ns, hd)
        if v_refs:
            v_refs[0][:, k0:k0 + ns, :] = vn3
        for c in range(ns // r):
            vg = vn3[:, c * r:(c + 1) * r, :].reshape(CHUNK, hd)
            mixed = jnp.dot(wm, vg.astype(BF16), preferred_element_type=F32) + bm
            og = pre3[:, c * r:(c + 1) * r, :].reshape(CHUNK, hd) * mixed
            oa_ref[:, k0 + c * r:k0 + (c + 1) * r, :] = og.reshape(tc, r, hd).astype(oa_ref.dtype)


def _branch_a(xn, w_in, gv, w_s, bmix, n, n_sub, want_v, transpose_b):
    tc, nk, d = xn.shape
    nh, _, hd = gv.shape
    r = CHUNK // tc
    nt = nk // n
    out_map = lambda h, i: (0, i, h)
    in_specs = [pl.BlockSpec((tc, n, d), lambda h, i: (0, i, 0)),
                pl.BlockSpec((d, hd), lambda h, i: (0, h)),
                pl.BlockSpec((d, hd), lambda h, i: (0, nh + h)),
                pl.BlockSpec((d, hd), lambda h, i: (0, 2 * nh + h)),
                pl.BlockSpec((None, 1, hd), lambda h, i: (h, 0, 0)),
                pl.BlockSpec((None, CHUNK, CHUNK), lambda h, i: (h, 0, 0)),
                pl.BlockSpec((None, CHUNK, hd), lambda h, i: (h, 0, 0))]
    args = [xn, w_in, w_in, w_in, gv, w_s, bmix]
    out_shape = [jax.ShapeDtypeStruct((tc, nk, nh * hd), BF16)]
    out_specs = [pl.BlockSpec((tc, n, hd), out_map)]
    if want_v:
        out_shape.append(jax.ShapeDtypeStruct((tc, nk, nh * hd), F32))
        out_specs.append(pl.BlockSpec((tc, n, hd), out_map))
    if transpose_b:
        col0 = 3 * nh * hd // LANES
        n_cols = w_in.shape[1] - 3 * nh * hd
        assert nh * nt * LANES == n_cols, "one 128-column block per grid step"
        in_specs.append(pl.BlockSpec((d, LANES), lambda h, i: (0, col0 + h * nt + i)))
        args.append(w_in)
        out_shape.append(jax.ShapeDtypeStruct((n_cols, d), BF16))
        out_specs.append(pl.BlockSpec((LANES, d), lambda h, i: (h * nt + i, 0)))
    res = list(pl.pallas_call(
        functools.partial(_branch_a_kernel, tc=tc, n=n, r=r, hd=hd, n_sub=n_sub,
                          has_v=want_v, has_side=transpose_b),
        grid=(nh, nt),
        in_specs=in_specs,
        out_specs=out_specs,
        out_shape=out_shape,
        scratch_shapes=[pltpu.VMEM((d, 3 * hd), BF16), pltpu.VMEM((CHUNK, CHUNK), BF16)],
        compiler_params=_params("parallel", "arbitrary"),
    )(*args))
    out_a = res.pop(0)
    v_rows = res.pop(0) if want_v else None
    w_bt = res.pop(0) if transpose_b else None
    return out_a, v_rows, w_bt


def _transpose_side_blocks(in_refs, out_refs):
    for w_ref, o_ref in zip(in_refs, out_refs):
        o_ref[...] = w_ref[...].T.astype(o_ref.dtype)


def _proj_b_kernel(w_ref, xn_ref, *refs):
    n_side = (len(refs) - 1) // 2
    o_ref = refs[n_side]
    _transpose_side_blocks(refs[:n_side], refs[n_side + 1:])
    for t in range(xn_ref.shape[0] // LANE_TILE):
        ls = slice(t * LANE_TILE, (t + 1) * LANE_TILE)
        xn = xn_ref[ls, :]
        for m in range(w_ref.shape[0] // ROW_CHUNK):
            sl = slice(m * ROW_CHUNK, (m + 1) * ROW_CHUNK)
            o_ref[sl, ls] = lax.dot_general(
                w_ref[sl, :], xn, (((1,), (1,)), ((), ())),
                preferred_element_type=F32).astype(o_ref.dtype)


def _proj_b(w_bt, xn, nkt, side=()):
    tc, nk, d = xn.shape
    rows = w_bt.shape[0]
    nkb = nk // nkt
    in_specs = [pl.BlockSpec((rows, d), lambda j, k: (0, 0), pipeline_mode=pl.Buffered(1)),
                pl.BlockSpec((None, nkt, d), lambda j, k: (j, k, 0))]
    out_specs = [pl.BlockSpec((None, rows, nkt), lambda j, k: (j, 0, k))]
    out_shape = [jax.ShapeDtypeStruct((tc, rows, nk), BF16)]
    for w, n_rows, row_block in side:
        n_cols = w.shape[1]
        assert tc * nkb * LANES == n_cols, "one 128-column block per grid step"
        in_specs.append(pl.BlockSpec((n_rows, LANES), lambda j, k, rb=row_block: (rb, j * nkb + k)))
        out_specs.append(pl.BlockSpec((LANES, n_rows), lambda j, k: (j * nkb + k, 0)))
        out_shape.append(jax.ShapeDtypeStruct((n_cols, n_rows), BF16))
    res = pl.pallas_call(
        _proj_b_kernel,
        grid=(tc, nkb),
        in_specs=in_specs,
        out_specs=out_specs,
        out_shape=out_shape,
        compiler_params=_params("parallel", "parallel"),
    )(w_bt, xn, *[w for w, _, _ in side])
    return res[0], tuple(res[1:])


def _shift_rows(x, d):
    if d % 8 == 0:
        return jnp.concatenate([jnp.zeros((d, x.shape[1]), x.dtype), x[:-d]], axis=0)
    row = lax.broadcasted_iota(jnp.int32, x.shape, 0)
    return jnp.where(row >= d, pltpu.roll(x, d, 0), 0.0)


def _ssm_kernel(x_ref, mp_ref, q_ref, ct_ref, dsk_ref, *rest, tc, gb, nk, fold, has_h0):
    if has_h0:
        h0r_ref, h0i_ref, y_ref, hfr_ref, hfi_ref = rest
    else:
        y_ref, hf_ref = rest
    rows = tc * GROUP
    half = q_ref.shape[2] // 2
    b_tc = tc.bit_length() - 1

    def one_group(g):
        r0 = g * GROUP if isinstance(g, int) else pl.multiple_of(g * GROUP, GROUP)
        if fold == 1:
            xg = x_ref[:, pl.ds(r0, GROUP), :]
        else:
            xg = jnp.stack([x_ref[j // fold, pl.ds(r0, GROUP), (j % fold) * nk:(j % fold + 1) * nk]
                            for j in range(tc)])
        res = jnp.dot(mp_ref[g], xg.reshape(rows, nk), preferred_element_type=F32)
        y_local = res[:rows]
        s_re = res[rows:rows + half]
        s_im = res[rows + half:]
        ct = ct_ref[g]
        if has_h0:
            h_re, h_im = h0r_ref[g], h0i_ref[g]
            a_re, a_im = ct[:, b_tc:b_tc + 1], ct[:, POW_COLS + b_tc:POW_COLS + b_tc + 1]
            hfr_ref[g] = a_re * h_re - a_im * h_im + s_re
            hfi_ref[g] = a_re * h_im + a_im * h_re + s_im
            h_in = jnp.concatenate([h_re, h_im], axis=0)
        else:
            tiles = []
            for pair in range(nk // (2 * LANES)):
                lo, mid, hi = 2 * pair * LANES, (2 * pair + 1) * LANES, (2 * pair + 2) * LANES
                xr = jnp.concatenate([s_re[:, lo:mid], s_re[:, mid:hi]], axis=0).T
                xi = jnp.concatenate([s_im[:, lo:mid], s_im[:, mid:hi]], axis=0).T
                for s in range(SCAN_STEPS):
                    cr, ci = ct[s:s + 1], ct[SCAN_ROWS + s:SCAN_ROWS + s + 1]
                    rr, ri = _shift_rows(xr, 1 << s), _shift_rows(xi, 1 << s)
                    xr, xi = xr + cr * rr - ci * ri, xi + cr * ri + ci * rr
                hf_ref[g, 2 * pair:2 * pair + 1, :] = xr[LANES - 1:]
                hf_ref[g, 2 * pair + 1:2 * pair + 2, :] = xi[LANES - 1:]
                hr_t = _shift_rows(xr, 1).T
                hi_t = _shift_rows(xi, 1).T
                tiles.append(jnp.concatenate([hr_t[:half], hi_t[:half]], axis=0))
                tiles.append(jnp.concatenate([hr_t[half:], hi_t[half:]], axis=0))
            h_in = jnp.concatenate(tiles, axis=1)
        y = y_local + jnp.dot(q_ref[g], h_in.astype(BF16), preferred_element_type=F32)
        dsk = jnp.concatenate([dsk_ref[pl.ds(r0, GROUP), :]] * (nk // LANES), axis=1)
        y3 = y.reshape(tc, GROUP, nk) + dsk[None] * xg.astype(F32)
        ya = jax.nn.gelu(y3).astype(y_ref.dtype)
        if fold == 1:
            y_ref[:, pl.ds(r0, GROUP), :] = ya
        else:
            for j in range(tc):
                y_ref[j // fold, pl.ds(r0, GROUP), (j % fold) * nk:(j % fold + 1) * nk] = ya[j]

    def body(gi, carry):
        for k in range(GROUPS_PER_TRIP):
            one_group(gi * GROUPS_PER_TRIP + k)
        return carry

    if gb == GROUPS_PER_TRIP:
        body(0, 0)
    else:
        lax.fori_loop(0, gb // GROUPS_PER_TRIP, body, 0)


def _ssm(zb, mp, q, ct, dsk, h0, gb, fold):
    tcf, _, nkf = zb.shape
    tc, nk = tcf * fold, nkf // fold
    ng, mrows, rows = mp.shape
    p2 = mrows - rows
    cb = ng * GROUP
    has_h0 = h0 is not None
    in_specs = [pl.BlockSpec((tcf, gb * GROUP, nkf), lambda i: (0, i, 0)),
                pl.BlockSpec((gb, mrows, rows), lambda i: (i, 0, 0)),
                pl.BlockSpec((gb, rows, p2), lambda i: (i, 0, 0)),
                pl.BlockSpec((gb,) + ct.shape[1:], lambda i: (i, 0, 0)),
                pl.BlockSpec((gb * GROUP, LANES), lambda i: (i, 0))]
    args = [zb, mp, q, ct, dsk]
    if has_h0:
        state_spec = pl.BlockSpec((gb, p2 // 2, nk), lambda i: (i, 0, 0))
        in_specs += [state_spec, state_spec]
        args += list(h0)
        hf_specs = [state_spec, state_spec]
        hf_shapes = [jax.ShapeDtypeStruct((ng, p2 // 2, nk), F32)] * 2
    else:
        hf_specs = [pl.BlockSpec((gb, nk // LANES, LANES), lambda i: (i, 0, 0))]
        hf_shapes = [jax.ShapeDtypeStruct((ng, nk // LANES, LANES), F32)]
    res = pl.pallas_call(
        functools.partial(_ssm_kernel, tc=tc, gb=gb, nk=nk, fold=fold, has_h0=has_h0),
        grid=(ng // gb,),
        in_specs=in_specs,
        out_specs=[pl.BlockSpec((tcf, gb * GROUP, nkf), lambda i: (0, i, 0))] + hf_specs,
        out_shape=[jax.ShapeDtypeStruct((tcf, cb, nkf), BF16)] + hf_shapes,
        compiler_params=_params("parallel"),
    )(*args)
    return res[0], tuple(res[1:])


def _glu_out_kernel(y_ref, gate_ref, wg_ref, bg_ref, wo_ref, *refs, has_cast):
    refs = list(refs)
    cast_in = refs.pop(0) if has_cast else None
    r_ref = refs.pop(0)
    if has_cast:
        cast_out = refs.pop(0)
        cast_out[...] = cast_in[...].astype(cast_out.dtype)
    ob_refs = refs
    for t, ob_ref in enumerate(ob_refs):
        ls = slice(t * LANE_TILE, (t + 1) * LANE_TILE)
        y = y_ref[:, ls]
        for m in range(wg_ref.shape[0] // ROW_CHUNK):
            sl = slice(m * ROW_CHUNK, (m + 1) * ROW_CHUNK)
            g = jnp.dot(wg_ref[sl, :], y, preferred_element_type=F32) + bg_ref[sl, :]
            ob = (y_ref[sl, ls].astype(F32) * jax.nn.sigmoid(g)
                  * jax.nn.silu(gate_ref[sl, ls].astype(F32)))
            ob_ref[sl, :] = ob.astype(ob_ref.dtype)
    for t, ob_ref in enumerate(ob_refs):
        ob = ob_ref[...]
        for m in range(wo_ref.shape[0] // ROW_CHUNK):
            sl = slice(m * ROW_CHUNK, (m + 1) * ROW_CHUNK)
            r_ref[t * LANE_TILE:(t + 1) * LANE_TILE, sl] = jnp.dot(
                wo_ref[sl, :], ob, preferred_element_type=F32).T


def _glu_out(y_t, zb, wg_t, bg, wo_t, nkt, cast_rows=None):
    tc, cb, nk = y_t.shape
    d = wo_t.shape[0]
    nkb = nk // nkt
    in_specs = [pl.BlockSpec((None, cb, nkt), lambda j, k: (j, 0, k)),
                pl.BlockSpec((None, cb, nkt), lambda j, k: (j, 1, k)),
                pl.BlockSpec((cb, cb), lambda j, k: (0, 0), pipeline_mode=pl.Buffered(1)),
                pl.BlockSpec((cb, LANE_TILE), lambda j, k: (0, 0), pipeline_mode=pl.Buffered(1)),
                pl.BlockSpec((d, cb), lambda j, k: (0, 0), pipeline_mode=pl.Buffered(1))]
    args = [y_t, zb, wg_t, bg, wo_t]
    out_specs = [pl.BlockSpec((None, nkt, d), lambda j, k: (j, k, 0))]
    out_shape = [jax.ShapeDtypeStruct((tc, nk, d), F32)]
    if cast_rows is not None:
        w, n_rows = cast_rows
        assert tc * nkb * LANES == n_rows, "one 128-row block per grid step"
        blk = pl.BlockSpec((LANES, w.shape[1]), lambda j, k: (j * nkb + k, 0))
        in_specs.append(blk)
        args.append(w)
        out_specs.append(blk)
        out_shape.append(jax.ShapeDtypeStruct((n_rows, w.shape[1]), BF16))
    res = pl.pallas_call(
        functools.partial(_glu_out_kernel, has_cast=cast_rows is not None),
        grid=(tc, nkb),
        in_specs=in_specs,
        out_specs=out_specs,
        out_shape=out_shape,
        scratch_shapes=[pltpu.VMEM((cb, LANE_TILE), BF16)] * (nkt // LANE_TILE),
        compiler_params=_params("parallel", "parallel"),
    )(*args)
    return res[0], (res[1] if cast_rows is not None else None)


def _out_norm_kernel(x_ref, oa_ref, r_ref, wo_ref, g_ref, y_ref, ms_ref, *, tc, n):
    ca = oa_ref.shape[-1]
    d = x_ref.shape[-1]
    nc = d // LANES
    p = jnp.dot(oa_ref[...].reshape(tc * n, ca), wo_ref[...], preferred_element_type=F32)
    for j in range(tc):
        s = p[j * n:(j + 1) * n] + r_ref[j]
        for c in range(nc):
            ms_ref[c, pl.ds(j, n, stride=tc), :] = s[:, c * LANES:(c + 1) * LANES]
    acc = x_ref[...] + jnp.concatenate([ms_ref[c] for c in range(nc)], axis=1)
    ms = jnp.mean(acc * acc, axis=-1, keepdims=True)
    y_ref[...] = acc * lax.rsqrt(ms + EPS) * g_ref[...]


def _out_norm(x2d, oa, r, wo_a, gf, n):
    tc, nk, ca = oa.shape
    t, d = x2d.shape
    return pl.pallas_call(
        functools.partial(_out_norm_kernel, tc=tc, n=n),
        grid=(nk // n,),
        in_specs=[pl.BlockSpec((n * tc, d), lambda i: (i, 0)),
                  pl.BlockSpec((tc, n, ca), lambda i: (0, i, 0)),
                  pl.BlockSpec((tc, n, d), lambda i: (0, i, 0)),
                  pl.BlockSpec((ca, d), lambda i: (0, 0), pipeline_mode=pl.Buffered(1)),
                  pl.BlockSpec((1, d), lambda i: (0, 0))],
        out_specs=pl.BlockSpec((n * tc, d), lambda i: (i, 0)),
        out_shape=jax.ShapeDtypeStruct((t, d), F32),
        scratch_shapes=[pltpu.VMEM((d // LANES, n * tc, LANES), F32)],
        compiler_params=_params("parallel"),
    )(x2d, oa, r, wo_a, gf.reshape(1, d))


def _discretise(a_re, a_im, log_dt):
    dt = jnp.exp(log_dt)[:, None]
    x_re, x_im = a_re * dt, a_im * dt
    n = (2.0 ** jnp.arange(N_POW, dtype=F32))[None, :, None]
    mag = jnp.exp(n * x_re[:, None, :])
    ang = n * x_im[:, None, :]
    pw_r, pw_i = mag * jnp.cos(ang), mag * jnp.sin(ang)
    lr, li = pw_r[:, 0], pw_i[:, 0]
    den = a_re * a_re + a_im * a_im
    nr, ni = lr - 1.0, li
    qr, qi = (nr * a_re + ni * a_im) / den, (ni * a_re - nr * a_im) / den
    cat = lambda a, b: jnp.concatenate([a, b], axis=-1)
    rowtab = jnp.stack([cat(lr, lr), cat(li, li), cat(qr, qi), cat(-qi, qr)], axis=1)
    zpad = jnp.zeros_like(pw_r[:, :POW_COLS - N_POW])
    lt = jnp.concatenate([pw_r, zpad, pw_i, zpad], axis=1).transpose(0, 2, 1)
    b0 = TC_PROMPT.bit_length() - 1
    sc_r, sc_i = pw_r[:, b0:b0 + SCAN_STEPS], pw_i[:, b0:b0 + SCAN_STEPS]
    zrow = jnp.zeros_like(cat(sc_r, sc_r)[:, :SCAN_ROWS - SCAN_STEPS])
    scan_rows = jnp.concatenate([cat(sc_r, sc_r), zrow, cat(sc_i, sc_i), zrow], axis=1)
    return rowtab, lt, scan_rows


def _ssm_tables_kernel(rt_ref, btr_ref, bti_ref, c2r_ref, c2i_ref, c2n_ref,
                       mp_ref, q_ref, mps_ref, qs_ref, *, gb, tc, tcs, unroll):
    rows = tc * GROUP
    rows_s = tcs * GROUP
    half = rt_ref.shape[2] // 2
    lane_k = lax.broadcasted_iota(jnp.int32, (GROUP, rows), 1)
    lane_q = lax.broadcasted_iota(jnp.int32, (1, 2 * half), 1)

    def one_group(g):
        rt = rt_ref[g]
        l_r, l_i, qu, qv = rt[0:1], rt[1:2], rt[2:3], rt[3:4]
        btr, bti = btr_ref[g], bti_ref[g]
        bb = btr * qu + bti * qv
        bs = btr * qv - bti * qu
        c2r, c2i = c2r_ref[g], c2i_ref[g]
        cur_r = jnp.ones((1, 2 * half), F32)
        cur_i = jnp.zeros((1, 2 * half), F32)
        et_blocks = []
        for n in range(tc):
            et_blocks.append(cur_r * bb + cur_i * bs)
            cur_r, cur_i = cur_r * l_r - cur_i * l_i, cur_r * l_i + cur_i * l_r
            u = jnp.where(lane_q < half, cur_r, -cur_i)
            v = jnp.where(lane_q < half, -cur_i, -cur_r)
            blk = (c2r * u + c2i * v).astype(q_ref.dtype)
            q_ref[g, n * GROUP:(n + 1) * GROUP, :] = blk
            if n < tcs:
                qs_ref[g, n * GROUP:(n + 1) * GROUP, :] = blk
        p = jnp.concatenate(et_blocks[::-1], axis=0).T
        mp_ref[g, rows:, :] = p.astype(mp_ref.dtype)
        mps_ref[g, rows_s:, :] = p[:, rows - rows_s:].astype(mps_ref.dtype)
        krow = jnp.dot(c2n_ref[g], p, precision=lax.Precision.HIGHEST, preferred_element_type=F32)
        for i in range(tc):
            sh = (tc - 1 - i) * GROUP
            blk = krow if sh == 0 else pltpu.roll(krow, rows - sh, 1)
            blk = jnp.where(lane_k < (i + 1) * GROUP, blk, 0.0)
            mp_ref[g, i * GROUP:(i + 1) * GROUP, :] = blk.astype(mp_ref.dtype)
            if i < tcs:
                mps_ref[g, i * GROUP:(i + 1) * GROUP, :] = blk[:, :rows_s].astype(mps_ref.dtype)

    def body(gi, carry):
        for k in range(unroll):
            one_group(gi * unroll + k)
        return carry

    if unroll == gb:
        body(0, 0)
    else:
        lax.fori_loop(0, gb // unroll, body, 0)


def _ssm_tables_job(rowtab, b_re, b_im, c_re, c_im, tcs, gb):
    ng, _, p2 = rowtab.shape
    tc = TC_PROMPT
    rows, rows_s = tc * GROUP, tcs * GROUP
    dup = lambda a: jnp.concatenate([a, a], axis=-1)
    args = (rowtab, dup(b_re.transpose(0, 2, 1)), dup(b_im.transpose(0, 2, 1)),
            dup(c_re), dup(c_im), jnp.concatenate([c_re, -c_im], axis=-1))
    shapes = [(ng, rows + p2, rows), (ng, rows, p2), (ng, rows_s + p2, rows_s), (ng, rows_s, p2)]
    in_specs = [pl.BlockSpec((gb,) + a.shape[1:], lambda i: (i, 0, 0)) for a in args]
    out_specs = [pl.BlockSpec((gb,) + s[1:], lambda i: (i, 0, 0)) for s in shapes]
    out_shape = [jax.ShapeDtypeStruct(s, BF16) for s in shapes]
    body = functools.partial(_ssm_tables_kernel, gb=gb, tc=tc, tcs=tcs, unroll=gb)
    return args, in_specs, out_specs, out_shape, body, ng // gb


def _mix_bias(b_s, hd, tc):
    nh = b_s.shape[0]
    r = CHUNK // tc
    if tc == TC_PROMPT:
        bm = b_s.reshape(nh, r, tc).transpose(0, 2, 1).reshape(nh, CHUNK)
    else:
        bm = jnp.repeat(b_s[:, :tc], r, axis=1)
    return jnp.broadcast_to(bm[:, :, None], (nh, CHUNK, hd))


def _tiles(tc, nk):
    fold = max(1, LANE_TILE // nk)
    return (NORM_ROWS // tc, min(2 * PROJ_ROWS, tc * nk) // tc, min(2 * LANE_TILE, fold * nk), fold)


def _layer(x2d, tc, h0, wts, ct, tables_job, shared):
    (g_norm, w_in, gv, w_s, bmix, d_skip, w_glu, b_glu, w_out, g_final) = wts
    cb = w_glu.shape[0]
    d = x2d.shape[1]
    nk = x2d.shape[0] // tc
    n_norm, n_a, nkt, fold = _tiles(tc, nk)
    gb = SSM_GROUPS_PER_STEP
    make_w_t = shared is None
    xn, tabs = _norm_permute(x2d, g_norm, tc, n_norm, tables_job if make_w_t else None)
    if make_w_t:
        mp, q, mp_s, q_s = tabs
    else:
        (w_bt, wg_t, wo_bt, wo_a), (mp, q) = shared
        mp_s, q_s = mp, q
    out_a, v_rows, made_bt = _branch_a(xn, w_in, gv, w_s, bmix, n_a, tc * n_a // PROJ_ROWS,
                                       h0 is not None, make_w_t)
    if make_w_t:
        w_bt = made_bt
    side = ((w_glu, cb, 0), (w_out, cb, 1)) if make_w_t else ()
    zb, made = _proj_b(w_bt, xn.reshape(tc // fold, fold * nk, d), nkt, side)
    if make_w_t:
        wg_t, wo_bt = made
    dsk = jnp.broadcast_to(d_skip[:, None], (cb, LANES))
    y_t, h_fin = _ssm(zb, mp, q, ct, dsk, h0, gb, fold)
    bg = jnp.broadcast_to(b_glu[:, None], (cb, LANE_TILE))
    ca = w_out.shape[0] - cb
    r, made_a = _glu_out(y_t, zb, wg_t, bg, wo_bt, nkt, (w_out, ca) if make_w_t else None)
    if make_w_t:
        wo_a = made_a
    y = _out_norm(x2d, out_a, r.reshape(tc, nk, d), wo_a, g_final, n_norm)
    return y, h_fin, v_rows, ((w_bt, wg_t, wo_bt, wo_a), (mp_s, q_s))


def kernel(x_prompt, x_sample, state_ssm_re, state_ssm_im, g_norm, w_in, g_v, w_s, b_s,
           a_re, a_im, log_dt, b_re, b_im, c_re, c_im, d_skip, w_glu, b_glu, w_out, g_final):
    depth = g_norm.shape[0]
    assert depth == 1, "single-layer step"
    bsz, seq, d = x_prompt.shape
    dbsz, dseq, _ = x_sample.shape
    nh = w_s.shape[1]
    ng, npst = a_re.shape[1:]
    cb = ng * GROUP
    ca = (w_in.shape[2] - 2 * cb) // 3
    hd = ca // nh
    assert seq // TC_PROMPT == LANES and dbsz == LANES and CHUNK % dseq == 0
    assert dseq & (dseq - 1) == 0 and dseq <= TC_PROMPT, "sample chunk must be a power of two"
    assert ca == cb == d

    l = 0
    gv = g_v[l].reshape(nh, 1, hd)
    rowtab, lt, scan_rows = _discretise(a_re[l], a_im[l], log_dt[l])
    tables_job = _ssm_tables_job(rowtab, b_re[l], b_im[l], c_re[l], c_im[l], dseq, SSM_GROUPS_PER_STEP)

    def weights(tc):
        return (g_norm[l], w_in[l], gv, w_s[l], _mix_bias(b_s[l], hd, tc),
                d_skip[l], w_glu[l], b_glu[l], w_out[l], g_final)

    assert bsz % 2 == 0
    tc_p, nk_p = TC_PROMPT, bsz * seq // TC_PROMPT
    y_p, hf_p, _, shared = _layer(x_prompt.reshape(nk_p * tc_p, d), tc_p, None,
                                  weights(tc_p), scan_rows, tables_job, None)
    hf_p = hf_p[0].reshape(ng, bsz // 2, 2, 2, npst)
    re_p = hf_p[:, :, 0].reshape(ng, bsz, npst).transpose(1, 0, 2)[None]
    im_p = hf_p[:, :, 1].reshape(ng, bsz, npst).transpose(1, 0, 2)[None]

    tc_s, nk_s = dseq, dbsz
    h0 = (state_ssm_re[l].transpose(1, 2, 0), state_ssm_im[l].transpose(1, 2, 0))
    y_s, hf_s, v_s, _ = _layer(x_sample.reshape(nk_s * tc_s, d), tc_s, h0,
                               weights(tc_s), lt, None, shared)
    re_s = hf_s[0].transpose(2, 0, 1)[None]
    im_s = hf_s[1].transpose(2, 0, 1)[None]
    v_out = v_s.transpose(1, 0, 2)[None]

    dt_p, dt_s = x_prompt.dtype, x_sample.dtype
    return (y_p.reshape(bsz, seq, d).astype(dt_p), y_s.reshape(dbsz, dseq, d).astype(dt_s),
            re_p.astype(dt_p), im_p.astype(dt_p), re_s.astype(dt_s), im_s.astype(dt_s),
            v_out.astype(dt_s))
```

```python
import functools

import jax
import jax.numpy as jnp
from jax import lax
from jax.experimental import pallas as pl
from jax.experimental.pallas import tpu as pltpu

EPS = 1e-6
LANES = 128
CHUNK = 128
GROUP = 16
TC_PROMPT = 16
ROW_CHUNK = 512
LANE_TILE = 256
PROJ_ROWS = 512
BRANCH_A_ROWS = 2048
NORM_ROWS = 512
SSM_GROUPS_PER_STEP = 8
POW_COLS = 16
SCAN_STEPS = 7
N_POW = 11
SCAN_ROWS = 8
GROUPS_PER_TRIP = 8
VMEM_LIMIT = 56 * 1024 * 1024

F32 = jnp.float32
BF16 = jnp.bfloat16


def _params(*sem):
    return pltpu.CompilerParams(dimension_semantics=sem, vmem_limit_bytes=VMEM_LIMIT)


def _norm_permute_kernel(x_ref, g_ref, *refs, tc, n, n_side_in, side_body):
    o_ref = refs[n_side_in]
    x = x_ref[...]
    ms = jnp.mean(x * x, axis=-1, keepdims=True)
    xn = (x * lax.rsqrt(ms + EPS) * g_ref[...]).astype(BF16)
    rows = n * tc
    dst = lax.broadcasted_iota(jnp.int32, (rows, rows), 0)
    src = lax.broadcasted_iota(jnp.int32, (rows, rows), 1)
    sel = (src == tc * (dst % n) + dst // n).astype(BF16)
    o_ref[...] = jnp.dot(sel, xn, preferred_element_type=F32).astype(o_ref.dtype).reshape(o_ref.shape)
    if side_body is not None:
        side_body(*refs[:n_side_in], *refs[n_side_in + 1:])


def _norm_permute(x2d, g, tc, n, side_job=None):
    t, d = x2d.shape
    nk = t // tc
    s_args, s_in, s_out, s_shape, s_body, s_steps = side_job or ((), [], [], [], None, nk // n)
    assert s_steps == nk // n, "the side job needs one block per grid step"
    res = pl.pallas_call(
        functools.partial(_norm_permute_kernel, tc=tc, n=n, n_side_in=len(s_args), side_body=s_body),
        grid=(nk // n,),
        in_specs=[pl.BlockSpec((n * tc, d), lambda i: (i, 0)),
                  pl.BlockSpec((1, d), lambda i: (0, 0))] + list(s_in),
        out_specs=[pl.BlockSpec((tc, n, d), lambda i: (0, i, 0))] + list(s_out),
        out_shape=[jax.ShapeDtypeStruct((tc, nk, d), BF16)] + list(s_shape),
        compiler_params=_params("parallel"),
    )(x2d, g.reshape(1, d), *s_args)
    return res[0], tuple(res[1:])


def _branch_a_kernel(*refs, tc, n, r, hd, n_sub, has_v, has_side):
    refs = list(refs)
    xn_ref, wu_ref, wv_ref, wg_ref, gv_ref, ws_ref, bmix_ref = refs[:7]
    del refs[:7]
    side_in = [refs.pop(0)] if has_side else []
    oa_ref = refs.pop(0)
    v_refs = [refs.pop(0)] if has_v else []
    side_out = [refs.pop(0)] if has_side else []
    w_ref, wm_ref = refs
    d = xn_ref.shape[-1]

    @pl.when(pl.program_id(1) == 0)
    def _():
        w_ref[:, :hd] = wu_ref[...].astype(BF16)
        w_ref[:, hd:2 * hd] = wv_ref[...].astype(BF16)
        w_ref[:, 2 * hd:] = wg_ref[...].astype(BF16)
        row = lax.broadcasted_iota(jnp.int32, (CHUNK, CHUNK), 0)
        col = lax.broadcasted_iota(jnp.int32, (CHUNK, CHUNK), 1)
        w_tril = jnp.where(row >= col, ws_ref[...], 0.0).astype(BF16)
        if tc == TC_PROMPT:
            time_of_row = tc * (row % r) + row // r
            same_seq = None
        else:
            time_of_row = row // r
            same_seq = (row % r) == (col % r)
        sel = (col == time_of_row).astype(BF16)
        sw = jnp.dot(sel, w_tril, preferred_element_type=F32).astype(BF16)
        wm = lax.dot_general(sw, sel, (((1,), (1,)), ((), ())), preferred_element_type=F32)
        if same_seq is not None:
            wm = jnp.where(same_seq, wm, 0.0)
        wm_ref[...] = wm.astype(BF16)

    ns = n // n_sub
    zs = [jnp.dot(xn_ref[:, k * ns:(k + 1) * ns, :].reshape(tc * ns, d), w_ref[...],
                  preferred_element_type=F32) for k in range(n_sub)]
    wm = wm_ref[...]
    bm = bmix_ref[...]
    for k, z in enumerate(zs):
        k0 = k * ns
        u = jax.nn.gelu(z[:, :hd])
        v = jax.nn.gelu(z[:, hd:2 * hd])
        ga = z[:, 2 * hd:]
        ms = jnp.mean(v * v, axis=-1, keepdims=True)
        vn3 = (v * lax.rsqrt(ms + EPS) * gv_ref[...]).reshape(tc, ns, hd)
        pre3 = (u * jax.nn.silu(ga)).reshape(tc, ns, hd)
        if v_refs:
            v_refs[0][:, k0:k0 + ns, :] = vn3
        for c in range(ns // r):
            vg = vn3[:, c * r:(c + 1) * r, :].reshape(CHUNK, hd)
            mixed = jnp.dot(wm, vg.astype(BF16), preferred_element_type=F32) + bm
            og = pre3[:, c * r:(c + 1) * r, :].reshape(CHUNK, hd) * mixed
            oa_ref[:, k0 + c * r:k0 + (c + 1) * r, :] = og.reshape(tc, r, hd).astype(oa_ref.dtype)
    _transpose_side_blocks(side_in, side_out)


def _branch_a(xn, w_in, gv, w_s, bmix, n, n_sub, want_v, transpose_b):
    tc, nk, d = xn.shape
    nh, _, hd = gv.shape
    r = CHUNK // tc
    nt = nk // n
    out_map = lambda h, i: (0, i, h)
    in_specs = [pl.BlockSpec((tc, n, d), lambda h, i: (0, i, 0)),
                pl.BlockSpec((d, hd), lambda h, i: (0, h)),
                pl.BlockSpec((d, hd), lambda h, i: (0, nh + h)),
                pl.BlockSpec((d, hd), lambda h, i: (0, 2 * nh + h)),
                pl.BlockSpec((None, 1, hd), lambda h, i: (h, 0, 0)),
                pl.BlockSpec((None, CHUNK, CHUNK), lambda h, i: (h, 0, 0)),
                pl.BlockSpec((None, CHUNK, hd), lambda h, i: (h, 0, 0))]
    args = [xn, w_in, w_in, w_in, gv, w_s, bmix]
    out_shape = [jax.ShapeDtypeStruct((tc, nk, nh * hd), BF16)]
    out_specs = [pl.BlockSpec((tc, n, hd), out_map)]
    if want_v:
        out_shape.append(jax.ShapeDtypeStruct((tc, nk, nh * hd), F32))
        out_specs.append(pl.BlockSpec((tc, n, hd), out_map))
    if transpose_b:
        col0 = 3 * nh * hd // LANES
        n_cols = w_in.shape[1] - 3 * nh * hd
        assert nh * nt * LANES == n_cols, "one 128-column block per grid step"
        in_specs.append(pl.BlockSpec((d, LANES), lambda h, i: (0, col0 + h * nt + i)))
        args.append(w_in)
        out_shape.append(jax.ShapeDtypeStruct((n_cols, d), BF16))
        out_specs.append(pl.BlockSpec((LANES, d), lambda h, i: (h * nt + i, 0)))
    res = list(pl.pallas_call(
        functools.partial(_branch_a_kernel, tc=tc, n=n, r=r, hd=hd, n_sub=n_sub,
                          has_v=want_v, has_side=transpose_b),
        grid=(nh, nt),
        in_specs=in_specs,
        out_specs=out_specs,
        out_shape=out_shape,
        scratch_shapes=[pltpu.VMEM((d, 3 * hd), BF16), pltpu.VMEM((CHUNK, CHUNK), BF16)],
        compiler_params=_params("parallel", "arbitrary"),
    )(*args))
    out_a = res.pop(0)
    v_rows = res.pop(0) if want_v else None
    w_bt = res.pop(0) if transpose_b else None
    return out_a, v_rows, w_bt


def _transpose_side_blocks(in_refs, out_refs):
    for w_ref, o_ref in zip(in_refs, out_refs):
        o_ref[...] = w_ref[...].T.astype(o_ref.dtype)


def _proj_b_kernel(w_ref, xn_ref, *refs):
    n_side = (len(refs) - 1) // 2
    o_ref = refs[n_side]
    _transpose_side_blocks(refs[:n_side], refs[n_side + 1:])
    for t in range(xn_ref.shape[0] // LANE_TILE):
        ls = slice(t * LANE_TILE, (t + 1) * LANE_TILE)
        xn = xn_ref[ls, :]
        for m in range(w_ref.shape[0] // ROW_CHUNK):
            sl = slice(m * ROW_CHUNK, (m + 1) * ROW_CHUNK)
            o_ref[sl, ls] = lax.dot_general(
                w_ref[sl, :], xn, (((1,), (1,)), ((), ())),
                preferred_element_type=F32).astype(o_ref.dtype)


def _proj_b(w_bt, xn, nkt, side=()):
    tc, nk, d = xn.shape
    rows = w_bt.shape[0]
    nkb = nk // nkt
    in_specs = [pl.BlockSpec((rows, d), lambda j, k: (0, 0), pipeline_mode=pl.Buffered(1)),
                pl.BlockSpec((None, nkt, d), lambda j, k: (j, k, 0))]
    out_specs = [pl.BlockSpec((None, rows, nkt), lambda j, k: (j, 0, k))]
    out_shape = [jax.ShapeDtypeStruct((tc, rows, nk), BF16)]
    for w, n_rows, row_block in side:
        n_cols = w.shape[1]
        assert tc * nkb * LANES == n_cols, "one 128-column block per grid step"
        in_specs.append(pl.BlockSpec((n_rows, LANES), lambda j, k, rb=row_block: (rb, j * nkb + k)))
        out_specs.append(pl.BlockSpec((LANES, n_rows), lambda j, k: (j * nkb + k, 0)))
        out_shape.append(jax.ShapeDtypeStruct((n_cols, n_rows), BF16))
    res = pl.pallas_call(
        _proj_b_kernel,
        grid=(tc, nkb),
        in_specs=in_specs,
        out_specs=out_specs,
        out_shape=out_shape,
        compiler_params=_params("parallel", "parallel"),
    )(w_bt, xn, *[w for w, _, _ in side])
    return res[0], tuple(res[1:])


def _shift_rows(x, d):
    if d % 8 == 0:
        return jnp.concatenate([jnp.zeros((d, x.shape[1]), x.dtype), x[:-d]], axis=0)
    row = lax.broadcasted_iota(jnp.int32, x.shape, 0)
    return jnp.where(row >= d, pltpu.roll(x, d, 0), 0.0)


def _ssm_kernel(x_ref, mp_ref, q_ref, ct_ref, dsk_ref, *rest, tc, gb, nk, fold, has_h0):
    if has_h0:
        h0r_ref, h0i_ref, y_ref, hfr_ref, hfi_ref = rest
    else:
        y_ref, hf_ref = rest
    rows = tc * GROUP
    half = q_ref.shape[2] // 2
    b_tc = tc.bit_length() - 1

    def one_group(g):
        r0 = g * GROUP if isinstance(g, int) else pl.multiple_of(g * GROUP, GROUP)
        if fold == 1:
            xg = x_ref[:, pl.ds(r0, GROUP), :]
        else:
            xg = jnp.stack([x_ref[j // fold, pl.ds(r0, GROUP), (j % fold) * nk:(j % fold + 1) * nk]
                            for j in range(tc)])
        res = jnp.dot(mp_ref[g], xg.reshape(rows, nk), preferred_element_type=F32)
        y_local = res[:rows]
        s_re = res[rows:rows + half]
        s_im = res[rows + half:]
        ct = ct_ref[g]
        if has_h0:
            h_re, h_im = h0r_ref[g], h0i_ref[g]
            a_re, a_im = ct[:, b_tc:b_tc + 1], ct[:, POW_COLS + b_tc:POW_COLS + b_tc + 1]
            hfr_ref[g] = a_re * h_re - a_im * h_im + s_re
            hfi_ref[g] = a_re * h_im + a_im * h_re + s_im
            h_in = jnp.concatenate([h_re, h_im], axis=0)
        else:
            tiles = []
            for pair in range(nk // (2 * LANES)):
                lo, mid, hi = 2 * pair * LANES, (2 * pair + 1) * LANES, (2 * pair + 2) * LANES
                xr = jnp.concatenate([s_re[:, lo:mid], s_re[:, mid:hi]], axis=0).T
                xi = jnp.concatenate([s_im[:, lo:mid], s_im[:, mid:hi]], axis=0).T
                for s in range(SCAN_STEPS):
                    cr, ci = ct[s:s + 1], ct[SCAN_ROWS + s:SCAN_ROWS + s + 1]
                    rr, ri = _shift_rows(xr, 1 << s), _shift_rows(xi, 1 << s)
                    xr, xi = xr + cr * rr - ci * ri, xi + cr * ri + ci * rr
                hf_ref[g, 2 * pair:2 * pair + 1, :] = xr[LANES - 1:]
                hf_ref[g, 2 * pair + 1:2 * pair + 2, :] = xi[LANES - 1:]
                hr_t = _shift_rows(xr, 1).T
                hi_t = _shift_rows(xi, 1).T
                tiles.append(jnp.concatenate([hr_t[:half], hi_t[:half]], axis=0))
                tiles.append(jnp.concatenate([hr_t[half:], hi_t[half:]], axis=0))
            h_in = jnp.concatenate(tiles, axis=1)
        y = y_local + jnp.dot(q_ref[g], h_in.astype(BF16), preferred_element_type=F32)
        dsk = jnp.concatenate([dsk_ref[pl.ds(r0, GROUP), :]] * (nk // LANES), axis=1)
        y3 = y.reshape(tc, GROUP, nk) + dsk[None] * xg.astype(F32)
        ya = jax.nn.gelu(y3).astype(y_ref.dtype)
        if fold == 1:
            y_ref[:, pl.ds(r0, GROUP), :] = ya
        else:
            for j in range(tc):
                y_ref[j // fold, pl.ds(r0, GROUP), (j % fold) * nk:(j % fold + 1) * nk] = ya[j]

    def body(gi, carry):
        for k in range(GROUPS_PER_TRIP):
            one_group(gi * GROUPS_PER_TRIP + k)
        return carry

    if gb == GROUPS_PER_TRIP:
        body(0, 0)
    else:
        lax.fori_loop(0, gb // GROUPS_PER_TRIP, body, 0)


def _ssm(zb, mp, q, ct, dsk, h0, gb, fold):
    tcf, _, nkf = zb.shape
    tc, nk = tcf * fold, nkf // fold
    ng, mrows, rows = mp.shape
    p2 = mrows - rows
    cb = ng * GROUP
    has_h0 = h0 is not None
    in_specs = [pl.BlockSpec((tcf, gb * GROUP, nkf), lambda i: (0, i, 0)),
                pl.BlockSpec((gb, mrows, rows), lambda i: (i, 0, 0)),
                pl.BlockSpec((gb, rows, p2), lambda i: (i, 0, 0)),
                pl.BlockSpec((gb,) + ct.shape[1:], lambda i: (i, 0, 0)),
                pl.BlockSpec((gb * GROUP, LANES), lambda i: (i, 0))]
    args = [zb, mp, q, ct, dsk]
    if has_h0:
        state_spec = pl.BlockSpec((gb, p2 // 2, nk), lambda i: (i, 0, 0))
        in_specs += [state_spec, state_spec]
        args += list(h0)
        hf_specs = [state_spec, state_spec]
        hf_shapes = [jax.ShapeDtypeStruct((ng, p2 // 2, nk), F32)] * 2
    else:
        hf_specs = [pl.BlockSpec((gb, nk // LANES, LANES), lambda i: (i, 0, 0))]
        hf_shapes = [jax.ShapeDtypeStruct((ng, nk // LANES, LANES), F32)]
    res = pl.pallas_call(
        functools.partial(_ssm_kernel, tc=tc, gb=gb, nk=nk, fold=fold, has_h0=has_h0),
        grid=(ng // gb,),
        in_specs=in_specs,
        out_specs=[pl.BlockSpec((tcf, gb * GROUP, nkf), lambda i: (0, i, 0))] + hf_specs,
        out_shape=[jax.ShapeDtypeStruct((tcf, cb, nkf), BF16)] + hf_shapes,
        compiler_params=_params("parallel"),
    )(*args)
    return res[0], tuple(res[1:])


def _glu_out_kernel(y_ref, gate_ref, wg_ref, bg_ref, wo_ref, *refs, has_cast):
    refs = list(refs)
    cast_in = refs.pop(0) if has_cast else None
    r_ref = refs.pop(0)
    if has_cast:
        cast_out = refs.pop(0)
        cast_out[...] = cast_in[...].astype(cast_out.dtype)
    ob_refs = refs
    for t, ob_ref in enumerate(ob_refs):
        ls = slice(t * LANE_TILE, (t + 1) * LANE_TILE)
        y = y_ref[:, ls]
        for m in range(wg_ref.shape[0] // ROW_CHUNK):
            sl = slice(m * ROW_CHUNK, (m + 1) * ROW_CHUNK)
            g = jnp.dot(wg_ref[sl, :], y, preferred_element_type=F32) + bg_ref[sl, :]
            ob = (y_ref[sl, ls].astype(F32) * jax.nn.sigmoid(g)
                  * jax.nn.silu(gate_ref[sl, ls].astype(F32)))
            ob_ref[sl, :] = ob.astype(ob_ref.dtype)
    for t, ob_ref in enumerate(ob_refs):
        ob = ob_ref[...]
        for m in range(wo_ref.shape[0] // ROW_CHUNK):
            sl = slice(m * ROW_CHUNK, (m + 1) * ROW_CHUNK)
            r_ref[t * LANE_TILE:(t + 1) * LANE_TILE, sl] = jnp.dot(
                wo_ref[sl, :], ob, preferred_element_type=F32).T


def _glu_out(y_t, zb, wg_t, bg, wo_t, nkt, cast_rows=None):
    tc, cb, nk = y_t.shape
    d = wo_t.shape[0]
    nkb = nk // nkt
    in_specs = [pl.BlockSpec((None, cb, nkt), lambda j, k: (j, 0, k)),
                pl.BlockSpec((None, cb, nkt), lambda j, k: (j, 1, k)),
                pl.BlockSpec((cb, cb), lambda j, k: (0, 0), pipeline_mode=pl.Buffered(1)),
                pl.BlockSpec((cb, LANE_TILE), lambda j, k: (0, 0), pipeline_mode=pl.Buffered(1)),
                pl.BlockSpec((d, cb), lambda j, k: (0, 0), pipeline_mode=pl.Buffered(1))]
    args = [y_t, zb, wg_t, bg, wo_t]
    out_specs = [pl.BlockSpec((None, nkt, d), lambda j, k: (j, k, 0))]
    out_shape = [jax.ShapeDtypeStruct((tc, nk, d), F32)]
    if cast_rows is not None:
        w, n_rows = cast_rows
        assert tc * nkb * LANES == n_rows, "one 128-row block per grid step"
        blk = pl.BlockSpec((LANES, w.shape[1]), lambda j, k: (j * nkb + k, 0))
        in_specs.append(blk)
        args.append(w)
        out_specs.append(blk)
        out_shape.append(jax.ShapeDtypeStruct((n_rows, w.shape[1]), BF16))
    res = pl.pallas_call(
        functools.partial(_glu_out_kernel, has_cast=cast_rows is not None),
        grid=(tc, nkb),
        in_specs=in_specs,
        out_specs=out_specs,
        out_shape=out_shape,
        scratch_shapes=[pltpu.VMEM((cb, LANE_TILE), BF16)] * (nkt // LANE_TILE),
        compiler_params=_params("parallel", "parallel"),
    )(*args)
    return res[0], (res[1] if cast_rows is not None else None)


def _out_norm_kernel(x_ref, oa_ref, r_ref, wo_ref, g_ref, y_ref, ms_ref, *, tc, n):
    ca = oa_ref.shape[-1]
    d = x_ref.shape[-1]
    nc = d // LANES
    p = jnp.dot(oa_ref[...].reshape(tc * n, ca), wo_ref[...], preferred_element_type=F32)
    for j in range(tc):
        s = p[j * n:(j + 1) * n] + r_ref[j]
        for c in range(nc):
            ms_ref[c, pl.ds(j, n, stride=tc), :] = s[:, c * LANES:(c + 1) * LANES]
    acc = x_ref[...] + jnp.concatenate([ms_ref[c] for c in range(nc)], axis=1)
    ms = jnp.mean(acc * acc, axis=-1, keepdims=True)
    y_ref[...] = acc * lax.rsqrt(ms + EPS) * g_ref[...]


def _out_norm(x2d, oa, r, wo_a, gf, n):
    tc, nk, ca = oa.shape
    t, d = x2d.shape
    return pl.pallas_call(
        functools.partial(_out_norm_kernel, tc=tc, n=n),
        grid=(nk // n,),
        in_specs=[pl.BlockSpec((n * tc, d), lambda i: (i, 0)),
                  pl.BlockSpec((tc, n, ca), lambda i: (0, i, 0)),
                  pl.BlockSpec((tc, n, d), lambda i: (0, i, 0)),
                  pl.BlockSpec((ca, d), lambda i: (0, 0), pipeline_mode=pl.Buffered(1)),
                  pl.BlockSpec((1, d), lambda i: (0, 0))],
        out_specs=pl.BlockSpec((n * tc, d), lambda i: (i, 0)),
        out_shape=jax.ShapeDtypeStruct((t, d), F32),
        scratch_shapes=[pltpu.VMEM((d // LANES, n * tc, LANES), F32)],
        compiler_params=_params("parallel"),
    )(x2d, oa, r, wo_a, gf.reshape(1, d))


def _discretise(a_re, a_im, log_dt):
    dt = jnp.exp(log_dt)[:, None]
    x_re, x_im = a_re * dt, a_im * dt
    n = (2.0 ** jnp.arange(N_POW, dtype=F32))[None, :, None]
    mag = jnp.exp(n * x_re[:, None, :])
    ang = n * x_im[:, None, :]
    pw_r, pw_i = mag * jnp.cos(ang), mag * jnp.sin(ang)
    lr, li = pw_r[:, 0], pw_i[:, 0]
    den = a_re * a_re + a_im * a_im
    nr, ni = lr - 1.0, li
    qr, qi = (nr * a_re + ni * a_im) / den, (ni * a_re - nr * a_im) / den
    cat = lambda a, b: jnp.concatenate([a, b], axis=-1)
    rowtab = jnp.stack([cat(lr, lr), cat(li, li), cat(qr, qi), cat(-qi, qr)], axis=1)
    zpad = jnp.zeros_like(pw_r[:, :POW_COLS - N_POW])
    lt = jnp.concatenate([pw_r, zpad, pw_i, zpad], axis=1).transpose(0, 2, 1)
    b0 = TC_PROMPT.bit_length() - 1
    sc_r, sc_i = pw_r[:, b0:b0 + SCAN_STEPS], pw_i[:, b0:b0 + SCAN_STEPS]
    zrow = jnp.zeros_like(cat(sc_r, sc_r)[:, :SCAN_ROWS - SCAN_STEPS])
    scan_rows = jnp.concatenate([cat(sc_r, sc_r), zrow, cat(sc_i, sc_i), zrow], axis=1)
    return rowtab, lt, scan_rows


def _ssm_tables_kernel(rt_ref, btr_ref, bti_ref, c2r_ref, c2i_ref, c2n_ref,
                       mp_ref, q_ref, mps_ref, qs_ref, *, gb, tc, tcs, unroll):
    rows = tc * GROUP
    rows_s = tcs * GROUP
    half = rt_ref.shape[2] // 2
    lane_k = lax.broadcasted_iota(jnp.int32, (GROUP, rows), 1)
    lane_q = lax.broadcasted_iota(jnp.int32, (1, 2 * half), 1)

    def one_group(g):
        rt = rt_ref[g]
        l_r, l_i, qu, qv = rt[0:1], rt[1:2], rt[2:3], rt[3:4]
        btr, bti = btr_ref[g], bti_ref[g]
        bb = btr * qu + bti * qv
        bs = btr * qv - bti * qu
        c2r, c2i = c2r_ref[g], c2i_ref[g]
        cur_r = jnp.ones((1, 2 * half), F32)
        cur_i = jnp.zeros((1, 2 * half), F32)
        et_blocks = []
        for n in range(tc):
            et_blocks.append(cur_r * bb + cur_i * bs)
            cur_r, cur_i = cur_r * l_r - cur_i * l_i, cur_r * l_i + cur_i * l_r
            u = jnp.where(lane_q < half, cur_r, -cur_i)
            v = jnp.where(lane_q < half, -cur_i, -cur_r)
            blk = (c2r * u + c2i * v).astype(q_ref.dtype)
            q_ref[g, n * GROUP:(n + 1) * GROUP, :] = blk
            if n < tcs:
                qs_ref[g, n * GROUP:(n + 1) * GROUP, :] = blk
        p = jnp.concatenate(et_blocks[::-1], axis=0).T
        mp_ref[g, rows:, :] = p.astype(mp_ref.dtype)
        mps_ref[g, rows_s:, :] = p[:, rows - rows_s:].astype(mps_ref.dtype)
        krow = jnp.dot(c2n_ref[g], p, precision=lax.Precision.HIGHEST, preferred_element_type=F32)
        for i in range(tc):
            sh = (tc - 1 - i) * GROUP
            blk = krow if sh == 0 else pltpu.roll(krow, rows - sh, 1)
            blk = jnp.where(lane_k < (i + 1) * GROUP, blk, 0.0)
            mp_ref[g, i * GROUP:(i + 1) * GROUP, :] = blk.astype(mp_ref.dtype)
            if i < tcs:
                mps_ref[g, i * GROUP:(i + 1) * GROUP, :] = blk[:, :rows_s].astype(mps_ref.dtype)

    def body(gi, carry):
        for k in range(unroll):
            one_group(gi * unroll + k)
        return carry

    if unroll == gb:
        body(0, 0)
    else:
        lax.fori_loop(0, gb // unroll, body, 0)


def _ssm_tables_job(rowtab, b_re, b_im, c_re, c_im, tcs, gb):
    ng, _, p2 = rowtab.shape
    tc = TC_PROMPT
    rows, rows_s = tc * GROUP, tcs * GROUP
    dup = lambda a: jnp.concatenate([a, a], axis=-1)
    args = (rowtab, dup(b_re.transpose(0, 2, 1)), dup(b_im.transpose(0, 2, 1)),
            dup(c_re), dup(c_im), jnp.concatenate([c_re, -c_im], axis=-1))
    shapes = [(ng, rows + p2, rows), (ng, rows, p2), (ng, rows_s + p2, rows_s), (ng, rows_s, p2)]
    in_specs = [pl.BlockSpec((gb,) + a.shape[1:], lambda i: (i, 0, 0)) for a in args]
    out_specs = [pl.BlockSpec((gb,) + s[1:], lambda i: (i, 0, 0)) for s in shapes]
    out_shape = [jax.ShapeDtypeStruct(s, BF16) for s in shapes]
    body = functools.partial(_ssm_tables_kernel, gb=gb, tc=tc, tcs=tcs, unroll=gb)
    return args, in_specs, out_specs, out_shape, body, ng // gb


def _mix_bias(b_s, hd, tc):
    nh = b_s.shape[0]
    r = CHUNK // tc
    if tc == TC_PROMPT:
        bm = b_s.reshape(nh, r, tc).transpose(0, 2, 1).reshape(nh, CHUNK)
    else:
        bm = jnp.repeat(b_s[:, :tc], r, axis=1)
    return jnp.broadcast_to(bm[:, :, None], (nh, CHUNK, hd))


def _tiles(tc, nk):
    fold = max(1, LANE_TILE // nk)
    return (NORM_ROWS // tc, min(BRANCH_A_ROWS, tc * nk) // tc, min(2 * LANE_TILE, fold * nk), fold)


def _layer(x2d, tc, h0, wts, ct, tables_job, shared):
    (g_norm, w_in, gv, w_s, bmix, d_skip, w_glu, b_glu, w_out, g_final) = wts
    cb = w_glu.shape[0]
    d = x2d.shape[1]
    nk = x2d.shape[0] // tc
    n_norm, n_a, nkt, fold = _tiles(tc, nk)
    gb = SSM_GROUPS_PER_STEP
    make_w_t = shared is None
    xn, tabs = _norm_permute(x2d, g_norm, tc, n_norm, tables_job if make_w_t else None)
    if make_w_t:
        mp, q, mp_s, q_s = tabs
    else:
        (w_bt, wg_t, wo_bt, wo_a), (mp, q) = shared
        mp_s, q_s = mp, q
    out_a, v_rows, made_bt = _branch_a(xn, w_in, gv, w_s, bmix, n_a, tc * n_a // PROJ_ROWS,
                                       h0 is not None, make_w_t)
    if make_w_t:
        w_bt = made_bt
    side = ((w_glu, cb, 0), (w_out, cb, 1)) if make_w_t else ()
    zb, made = _proj_b(w_bt, xn.reshape(tc // fold, fold * nk, d), nkt, side)
    if make_w_t:
        wg_t, wo_bt = made
    dsk = jnp.broadcast_to(d_skip[:, None], (cb, LANES))
    y_t, h_fin = _ssm(zb, mp, q, ct, dsk, h0, gb, fold)
    bg = jnp.broadcast_to(b_glu[:, None], (cb, LANE_TILE))
    ca = w_out.shape[0] - cb
    r, made_a = _glu_out(y_t, zb, wg_t, bg, wo_bt, nkt, (w_out, ca) if make_w_t else None)
    if make_w_t:
        wo_a = made_a
    y = _out_norm(x2d, out_a, r.reshape(tc, nk, d), wo_a, g_final, n_norm)
    return y, h_fin, v_rows, ((w_bt, wg_t, wo_bt, wo_a), (mp_s, q_s))


def kernel(x_prompt, x_sample, state_ssm_re, state_ssm_im, g_norm, w_in, g_v, w_s, b_s,
           a_re, a_im, log_dt, b_re, b_im, c_re, c_im, d_skip, w_glu, b_glu, w_out, g_final):
    depth = g_norm.shape[0]
    assert depth == 1, "single-layer step"
    bsz, seq, d = x_prompt.shape
    dbsz, dseq, _ = x_sample.shape
    nh = w_s.shape[1]
    ng, npst = a_re.shape[1:]
    cb = ng * GROUP
    ca = (w_in.shape[2] - 2 * cb) // 3
    hd = ca // nh
    assert seq // TC_PROMPT == LANES and dbsz == LANES and CHUNK % dseq == 0
    assert dseq & (dseq - 1) == 0 and dseq <= TC_PROMPT, "sample chunk must be a power of two"
    assert ca == cb == d

    l = 0
    gv = g_v[l].reshape(nh, 1, hd)
    rowtab, lt, scan_rows = _discretise(a_re[l], a_im[l], log_dt[l])
    tables_job = _ssm_tables_job(rowtab, b_re[l], b_im[l], c_re[l], c_im[l], dseq, SSM_GROUPS_PER_STEP)

    def weights(tc):
        return (g_norm[l], w_in[l], gv, w_s[l], _mix_bias(b_s[l], hd, tc),
                d_skip[l], w_glu[l], b_glu[l], w_out[l], g_final)

    assert bsz % 2 == 0
    tc_p, nk_p = TC_PROMPT, bsz * seq // TC_PROMPT
    y_p, hf_p, _, shared = _layer(x_prompt.reshape(nk_p * tc_p, d), tc_p, None,
                                  weights(tc_p), scan_rows, tables_job, None)
    hf_p = hf_p[0].reshape(ng, bsz // 2, 2, 2, npst)
    re_p = hf_p[:, :, 0].reshape(ng, bsz, npst).transpose(1, 0, 2)[None]
    im_p = hf_p[:, :, 1].reshape(ng, bsz, npst).transpose(1, 0, 2)[None]

    tc_s, nk_s = dseq, dbsz
    h0 = (state_ssm_re[l].transpose(1, 2, 0), state_ssm_im[l].transpose(1, 2, 0))
    y_s, hf_s, v_s, _ = _layer(x_sample.reshape(nk_s * tc_s, d), tc_s, h0,
                               weights(tc_s), lt, None, shared)
    re_s = hf_s[0].transpose(2, 0, 1)[None]
    im_s = hf_s[1].transpose(2, 0, 1)[None]
    v_out = v_s.transpose(1, 0, 2)[None]

    dt_p, dt_s = x_prompt.dtype, x_sample.dtype
    return (y_p.reshape(bsz, seq, d).astype(dt_p), y_s.reshape(dbsz, dseq, d).astype(dt_s),
            re_p.astype(dt_p), im_p.astype(dt_p), re_s.astype(dt_s), im_s.astype(dt_s),
            v_out.astype(dt_s))
```

```python
import functools

import jax
import jax.numpy as jnp
from jax import lax
from jax.experimental import pallas as pl
from jax.experimental.pallas import tpu as pltpu

EPS = 1e-6
LANES = 128
CHUNK = 128
GROUP = 16
TC_PROMPT = 16
ROW_CHUNK = 512
LANE_TILE = 256
PROJ_ROWS = 512
BRANCH_A_ROWS = 2048
NORM_ROWS = 512
SSM_GROUPS_PER_STEP = 8
POW_COLS = 16
SCAN_STEPS = 7
N_POW = 11
SCAN_ROWS = 8
GROUPS_PER_TRIP = 8
VMEM_LIMIT = 56 * 1024 * 1024

F32 = jnp.float32
BF16 = jnp.bfloat16


def _params(*sem):
    return pltpu.CompilerParams(dimension_semantics=sem, vmem_limit_bytes=VMEM_LIMIT)


def _norm_permute_kernel(x_ref, g_ref, *refs, tc, n, n_side_in, side_body):
    o_ref = refs[n_side_in]
    x = x_ref[...]
    ms = jnp.mean(x * x, axis=-1, keepdims=True)
    xn = (x * lax.rsqrt(ms + EPS) * g_ref[...]).astype(BF16)
    rows = n * tc
    dst = lax.broadcasted_iota(jnp.int32, (rows, rows), 0)
    src = lax.broadcasted_iota(jnp.int32, (rows, rows), 1)
    sel = (src == tc * (dst % n) + dst // n).astype(BF16)
    o_ref[...] = jnp.dot(sel, xn, preferred_element_type=F32).astype(o_ref.dtype).reshape(o_ref.shape)
    if side_body is not None:
        side_body(*refs[:n_side_in], *refs[n_side_in + 1:])


def _norm_permute(x2d, g, tc, n, side_job=None):
    t, d = x2d.shape
    nk = t // tc
    s_args, s_in, s_out, s_shape, s_body, s_steps = side_job or ((), [], [], [], None, nk // n)
    assert s_steps == nk // n, "the side job needs one block per grid step"
    res = pl.pallas_call(
        functools.partial(_norm_permute_kernel, tc=tc, n=n, n_side_in=len(s_args), side_body=s_body),
        grid=(nk // n,),
        in_specs=[pl.BlockSpec((n * tc, d), lambda i: (i, 0)),
                  pl.BlockSpec((1, d), lambda i: (0, 0))] + list(s_in),
        out_specs=[pl.BlockSpec((tc, n, d), lambda i: (0, i, 0))] + list(s_out),
        out_shape=[jax.ShapeDtypeStruct((tc, nk, d), BF16)] + list(s_shape),
        compiler_params=_params("parallel"),
    )(x2d, g.reshape(1, d), *s_args)
    return res[0], tuple(res[1:])


def _branch_a_kernel(*refs, tc, n, r, hd, n_sub, has_v, has_side):
    refs = list(refs)
    xn_ref, wu_ref, wv_ref, wg_ref, gv_ref, ws_ref, bmix_ref = refs[:7]
    del refs[:7]
    side_in = [refs.pop(0)] if has_side else []
    oa_ref = refs.pop(0)
    v_refs = [refs.pop(0)] if has_v else []
    side_out = [refs.pop(0)] if has_side else []
    w_ref, wm_ref = refs
    d = xn_ref.shape[-1]

    @pl.when(pl.program_id(1) == 0)
    def _():
        w_ref[:, :hd] = wu_ref[...].astype(BF16)
        w_ref[:, hd:2 * hd] = wv_ref[...].astype(BF16)
        w_ref[:, 2 * hd:] = wg_ref[...].astype(BF16)
        row = lax.broadcasted_iota(jnp.int32, (CHUNK, CHUNK), 0)
        col = lax.broadcasted_iota(jnp.int32, (CHUNK, CHUNK), 1)
        w_tril = jnp.where(row >= col, ws_ref[...], 0.0).astype(BF16)
        if tc == TC_PROMPT:
            time_of_row = tc * (row % r) + row // r
            same_seq = None
        else:
            time_of_row = row // r
            same_seq = (row % r) == (col % r)
        sel = (col == time_of_row).astype(BF16)
        sw = jnp.dot(sel, w_tril, preferred_element_type=F32).astype(BF16)
        wm = lax.dot_general(sw, sel, (((1,), (1,)), ((), ())), preferred_element_type=F32)
        if same_seq is not None:
            wm = jnp.where(same_seq, wm, 0.0)
        wm_ref[...] = wm.astype(BF16)

    ns = n // n_sub
    zs = [jnp.dot(xn_ref[:, k * ns:(k + 1) * ns, :].reshape(tc * ns, d), w_ref[...],
                  preferred_element_type=F32) for k in range(n_sub)]
    wm = wm_ref[...]
    bm = bmix_ref[...]
    for k, z in enumerate(zs):
        k0 = k * ns
        u = jax.nn.gelu(z[:, :hd])
        v = jax.nn.gelu(z[:, hd:2 * hd])
        ga = z[:, 2 * hd:]
        ms = jnp.mean(v * v, axis=-1, keepdims=True)
        vn3 = (v * lax.rsqrt(ms + EPS) * gv_ref[...]).reshape(tc, ns, hd)
        pre3 = (u * jax.nn.silu(ga)).reshape(tc, ns, hd)
        if v_refs:
            v_refs[0][:, k0:k0 + ns, :] = vn3
        for c in range(ns // r):
            vg = vn3[:, c * r:(c + 1) * r, :].reshape(CHUNK, hd)
            mixed = jnp.dot(wm, vg.astype(BF16), preferred_element_type=F32) + bm
            og = pre3[:, c * r:(c + 1) * r, :].reshape(CHUNK, hd) * mixed
            oa_ref[:, k0 + c * r:k0 + (c + 1) * r, :] = og.reshape(tc, r, hd).astype(oa_ref.dtype)
    _transpose_side_blocks(side_in, side_out)


def _branch_a(xn, w_in, gv, w_s, bmix, n, n_sub, want_v, transpose_b):
    tc, nk, d = xn.shape
    nh, _, hd = gv.shape
    r = CHUNK // tc
    nt = nk // n
    out_map = lambda h, i: (0, i, h)
    in_specs = [pl.BlockSpec((tc, n, d), lambda h, i: (0, i, 0)),
                pl.BlockSpec((d, hd), lambda h, i: (0, h)),
                pl.BlockSpec((d, hd), lambda h, i: (0, nh + h)),
                pl.BlockSpec((d, hd), lambda h, i: (0, 2 * nh + h)),
                pl.BlockSpec((None, 1, hd), lambda h, i: (h, 0, 0)),
                pl.BlockSpec((None, CHUNK, CHUNK), lambda h, i: (h, 0, 0)),
                pl.BlockSpec((None, CHUNK, hd), lambda h, i: (h, 0, 0))]
    args = [xn, w_in, w_in, w_in, gv, w_s, bmix]
    out_shape = [jax.ShapeDtypeStruct((tc, nk, nh * hd), BF16)]
    out_specs = [pl.BlockSpec((tc, n, hd), out_map)]
    if want_v:
        out_shape.append(jax.ShapeDtypeStruct((tc, nk, nh * hd), F32))
        out_specs.append(pl.BlockSpec((tc, n, hd), out_map))
    if transpose_b:
        col0 = 3 * nh * hd // LANES
        n_cols = w_in.shape[1] - 3 * nh * hd
        assert nh * nt * LANES == n_cols, "one 128-column block per grid step"
        in_specs.append(pl.BlockSpec((d, LANES), lambda h, i: (0, col0 + h * nt + i)))
        args.append(w_in)
        out_shape.append(jax.ShapeDtypeStruct((n_cols, d), BF16))
        out_specs.append(pl.BlockSpec((LANES, d), lambda h, i: (h * nt + i, 0)))
    res = list(pl.pallas_call(
        functools.partial(_branch_a_kernel, tc=tc, n=n, r=r, hd=hd, n_sub=n_sub,
                          has_v=want_v, has_side=transpose_b),
        grid=(nh, nt),
        in_specs=in_specs,
        out_specs=out_specs,
        out_shape=out_shape,
        scratch_shapes=[pltpu.VMEM((d, 3 * hd), BF16), pltpu.VMEM((CHUNK, CHUNK), BF16)],
        compiler_params=_params("parallel", "arbitrary"),
    )(*args))
    out_a = res.pop(0)
    v_rows = res.pop(0) if want_v else None
    w_bt = res.pop(0) if transpose_b else None
    return out_a, v_rows, w_bt


def _transpose_side_blocks(in_refs, out_refs):
    for w_ref, o_ref in zip(in_refs, out_refs):
        o_ref[...] = w_ref[...].T.astype(o_ref.dtype)


def _stream_maps(counts):
    starts = [sum(counts[:s]) for s in range(len(counts))]
    maps = [lambda j, st=st, c=c: jnp.clip(j - st, 0, c - 1) for st, c in zip(starts, counts)]
    owns = [lambda j, st=st, c=c: jnp.logical_and(j >= st, j < st + c) for st, c in zip(starts, counts)]
    return maps, owns


def _proj_b_kernel(w_ref, *refs, counts, n_side):
    ns = len(counts)
    xn_refs, side_in = refs[:ns], refs[ns:ns + n_side]
    o_refs, side_out = refs[ns + n_side:2 * ns + n_side], refs[2 * ns + n_side:]
    _, owns = _stream_maps(counts)
    j = pl.program_id(0)
    for s, (xn_ref, o_ref) in enumerate(zip(xn_refs, o_refs)):
        @pl.when(owns[s](j))
        def _(s=s, xn_ref=xn_ref, o_ref=o_ref):
            if s == 0:
                _transpose_side_blocks(side_in, side_out)
            for t in range(xn_ref.shape[0] // LANE_TILE):
                ls = slice(t * LANE_TILE, (t + 1) * LANE_TILE)
                xn = xn_ref[ls, :]
                for m in range(w_ref.shape[0] // ROW_CHUNK):
                    sl = slice(m * ROW_CHUNK, (m + 1) * ROW_CHUNK)
                    o_ref[sl, ls] = lax.dot_general(
                        w_ref[sl, :], xn, (((1,), (1,)), ((), ())),
                        preferred_element_type=F32).astype(o_ref.dtype)


def _proj_b(w_bt, xns, side=()):
    rows, d = w_bt.shape
    counts = [x.shape[0] for x in xns]
    maps, _ = _stream_maps(counts)
    in_specs = [pl.BlockSpec((rows, d), lambda j: (0, 0), pipeline_mode=pl.Buffered(1))]
    in_specs += [pl.BlockSpec((None, x.shape[1], d), lambda j, m=m: (m(j), 0, 0))
                 for x, m in zip(xns, maps)]
    out_specs = [pl.BlockSpec((None, rows, x.shape[1]), lambda j, m=m: (m(j), 0, 0))
                 for x, m in zip(xns, maps)]
    out_shape = [jax.ShapeDtypeStruct((x.shape[0], rows, x.shape[1]), BF16) for x in xns]
    for w, n_rows, row_block in side:
        n_cols = w.shape[1]
        assert counts[0] * LANES == n_cols, "one 128-column block per step of the first stream set"
        in_specs.append(pl.BlockSpec((n_rows, LANES), lambda j, rb=row_block: (rb, maps[0](j))))
        out_specs.append(pl.BlockSpec((LANES, n_rows), lambda j: (maps[0](j), 0)))
        out_shape.append(jax.ShapeDtypeStruct((n_cols, n_rows), BF16))
    res = pl.pallas_call(
        functools.partial(_proj_b_kernel, counts=counts, n_side=len(side)),
        grid=(sum(counts),),
        in_specs=in_specs,
        out_specs=out_specs,
        out_shape=out_shape,
        compiler_params=_params("arbitrary"),
    )(w_bt, *xns, *[w for w, _, _ in side])
    return tuple(res[:len(xns)]), tuple(res[len(xns):])


def _shift_rows(x, d):
    if d % 8 == 0:
        return jnp.concatenate([jnp.zeros((d, x.shape[1]), x.dtype), x[:-d]], axis=0)
    row = lax.broadcasted_iota(jnp.int32, x.shape, 0)
    return jnp.where(row >= d, pltpu.roll(x, d, 0), 0.0)


def _ssm_kernel(x_ref, mp_ref, q_ref, ct_ref, dsk_ref, *rest, tc, gb, nk, fold, has_h0):
    if has_h0:
        h0r_ref, h0i_ref, y_ref, hfr_ref, hfi_ref = rest
    else:
        y_ref, hf_ref = rest
    rows = tc * GROUP
    half = q_ref.shape[2] // 2
    b_tc = tc.bit_length() - 1

    def one_group(g):
        r0 = g * GROUP if isinstance(g, int) else pl.multiple_of(g * GROUP, GROUP)
        if fold == 1:
            xg = x_ref[:, pl.ds(r0, GROUP), :]
        else:
            xg = jnp.stack([x_ref[j // fold, pl.ds(r0, GROUP), (j % fold) * nk:(j % fold + 1) * nk]
                            for j in range(tc)])
        res = jnp.dot(mp_ref[g], xg.reshape(rows, nk), preferred_element_type=F32)
        y_local = res[:rows]
        s_re = res[rows:rows + half]
        s_im = res[rows + half:]
        ct = ct_ref[g]
        if has_h0:
            h_re, h_im = h0r_ref[g], h0i_ref[g]
            a_re, a_im = ct[:, b_tc:b_tc + 1], ct[:, POW_COLS + b_tc:POW_COLS + b_tc + 1]
            hfr_ref[g] = a_re * h_re - a_im * h_im + s_re
            hfi_ref[g] = a_re * h_im + a_im * h_re + s_im
            h_in = jnp.concatenate([h_re, h_im], axis=0)
        else:
            tiles = []
            for pair in range(nk // (2 * LANES)):
                lo, mid, hi = 2 * pair * LANES, (2 * pair + 1) * LANES, (2 * pair + 2) * LANES
                xr = jnp.concatenate([s_re[:, lo:mid], s_re[:, mid:hi]], axis=0).T
                xi = jnp.concatenate([s_im[:, lo:mid], s_im[:, mid:hi]], axis=0).T
                for s in range(SCAN_STEPS):
                    cr, ci = ct[s:s + 1], ct[SCAN_ROWS + s:SCAN_ROWS + s + 1]
                    rr, ri = _shift_rows(xr, 1 << s), _shift_rows(xi, 1 << s)
                    xr, xi = xr + cr * rr - ci * ri, xi + cr * ri + ci * rr
                hf_ref[g, 2 * pair:2 * pair + 1, :] = xr[LANES - 1:]
                hf_ref[g, 2 * pair + 1:2 * pair + 2, :] = xi[LANES - 1:]
                hr_t = _shift_rows(xr, 1).T
                hi_t = _shift_rows(xi, 1).T
                tiles.append(jnp.concatenate([hr_t[:half], hi_t[:half]], axis=0))
                tiles.append(jnp.concatenate([hr_t[half:], hi_t[half:]], axis=0))
            h_in = jnp.concatenate(tiles, axis=1)
        y = y_local + jnp.dot(q_ref[g], h_in.astype(BF16), preferred_element_type=F32)
        dsk = jnp.concatenate([dsk_ref[pl.ds(r0, GROUP), :]] * (nk // LANES), axis=1)
        y3 = y.reshape(tc, GROUP, nk) + dsk[None] * xg.astype(F32)
        ya = jax.nn.gelu(y3).astype(y_ref.dtype)
        if fold == 1:
            y_ref[:, pl.ds(r0, GROUP), :] = ya
        else:
            for j in range(tc):
                y_ref[j // fold, pl.ds(r0, GROUP), (j % fold) * nk:(j % fold + 1) * nk] = ya[j]

    def body(gi, carry):
        for k in range(GROUPS_PER_TRIP):
            one_group(gi * GROUPS_PER_TRIP + k)
        return carry

    if gb == GROUPS_PER_TRIP:
        body(0, 0)
    else:
        lax.fori_loop(0, gb // GROUPS_PER_TRIP, body, 0)


def _ssm(zb, mp, q, ct, dsk, h0, gb, fold):
    tcf, _, nkf = zb.shape
    tc, nk = tcf * fold, nkf // fold
    ng, mrows, rows = mp.shape
    p2 = mrows - rows
    cb = ng * GROUP
    has_h0 = h0 is not None
    in_specs = [pl.BlockSpec((tcf, gb * GROUP, nkf), lambda i: (0, i, 0)),
                pl.BlockSpec((gb, mrows, rows), lambda i: (i, 0, 0)),
                pl.BlockSpec((gb, rows, p2), lambda i: (i, 0, 0)),
                pl.BlockSpec((gb,) + ct.shape[1:], lambda i: (i, 0, 0)),
                pl.BlockSpec((gb * GROUP, LANES), lambda i: (i, 0))]
    args = [zb, mp, q, ct, dsk]
    if has_h0:
        state_spec = pl.BlockSpec((gb, p2 // 2, nk), lambda i: (i, 0, 0))
        in_specs += [state_spec, state_spec]
        args += list(h0)
        hf_specs = [state_spec, state_spec]
        hf_shapes = [jax.ShapeDtypeStruct((ng, p2 // 2, nk), F32)] * 2
    else:
        hf_specs = [pl.BlockSpec((gb, nk // LANES, LANES), lambda i: (i, 0, 0))]
        hf_shapes = [jax.ShapeDtypeStruct((ng, nk // LANES, LANES), F32)]
    res = pl.pallas_call(
        functools.partial(_ssm_kernel, tc=tc, gb=gb, nk=nk, fold=fold, has_h0=has_h0),
        grid=(ng // gb,),
        in_specs=in_specs,
        out_specs=[pl.BlockSpec((tcf, gb * GROUP, nkf), lambda i: (0, i, 0))] + hf_specs,
        out_shape=[jax.ShapeDtypeStruct((tcf, cb, nkf), BF16)] + hf_shapes,
        compiler_params=_params("parallel"),
    )(*args)
    return res[0], tuple(res[1:])


def _glu_out_kernel(wg_ref, bg_ref, wo_ref, *refs, counts, has_cast, n_tiles):
    ns = len(counts)
    refs = list(refs)
    y_refs, gate_refs = refs[:ns], refs[ns:2 * ns]
    del refs[:2 * ns]
    cast_in = refs.pop(0) if has_cast else None
    r_refs = refs[:ns]
    del refs[:ns]
    cast_out = refs.pop(0) if has_cast else None
    ob_refs = refs
    _, owns = _stream_maps(counts)
    j = pl.program_id(0)
    for s in range(ns):
        @pl.when(owns[s](j))
        def _(s=s, y_ref=y_refs[s], gate_ref=gate_refs[s], r_ref=r_refs[s]):
            if s == 0 and has_cast:
                cast_out[...] = cast_in[...].astype(cast_out.dtype)
            for t, ob_ref in enumerate(ob_refs[:n_tiles[s]]):
                ls = slice(t * LANE_TILE, (t + 1) * LANE_TILE)
                y = y_ref[:, ls]
                for m in range(wg_ref.shape[0] // ROW_CHUNK):
                    sl = slice(m * ROW_CHUNK, (m + 1) * ROW_CHUNK)
                    g = jnp.dot(wg_ref[sl, :], y, preferred_element_type=F32) + bg_ref[sl, :]
                    ob = (y_ref[sl, ls].astype(F32) * jax.nn.sigmoid(g)
                          * jax.nn.silu(gate_ref[sl, ls].astype(F32)))
                    ob_ref[sl, :] = ob.astype(ob_ref.dtype)
            for t, ob_ref in enumerate(ob_refs[:n_tiles[s]]):
                ob = ob_ref[...]
                for m in range(wo_ref.shape[0] // ROW_CHUNK):
                    sl = slice(m * ROW_CHUNK, (m + 1) * ROW_CHUNK)
                    r_ref[t * LANE_TILE:(t + 1) * LANE_TILE, sl] = jnp.dot(
                        wo_ref[sl, :], ob, preferred_element_type=F32).T


def _glu_out(y_ts, zbs, wg_t, bg, wo_t, cast_rows=None):
    cb = wg_t.shape[0]
    d = wo_t.shape[0]
    counts = [y.shape[0] for y in y_ts]
    maps, _ = _stream_maps(counts)
    const = lambda j: (0, 0)
    in_specs = [pl.BlockSpec((cb, cb), const, pipeline_mode=pl.Buffered(1)),
                pl.BlockSpec((cb, LANE_TILE), const, pipeline_mode=pl.Buffered(1)),
                pl.BlockSpec((d, cb), const, pipeline_mode=pl.Buffered(1))]
    in_specs += [pl.BlockSpec((None, cb, y.shape[2]), lambda j, m=m: (m(j), 0, 0)) for y, m in zip(y_ts, maps)]
    in_specs += [pl.BlockSpec((None, cb, y.shape[2]), lambda j, m=m: (m(j), 1, 0)) for y, m in zip(y_ts, maps)]
    args = [wg_t, bg, wo_t, *y_ts, *zbs]
    out_specs = [pl.BlockSpec((None, y.shape[2], d), lambda j, m=m: (m(j), 0, 0)) for y, m in zip(y_ts, maps)]
    out_shape = [jax.ShapeDtypeStruct((y.shape[0], y.shape[2], d), F32) for y in y_ts]
    if cast_rows is not None:
        w, n_rows = cast_rows
        assert counts[0] * LANES == n_rows, "one 128-row block per step of the first stream set"
        blk = pl.BlockSpec((LANES, w.shape[1]), lambda j: (maps[0](j), 0))
        in_specs.append(blk)
        args.append(w)
        out_specs.append(blk)
        out_shape.append(jax.ShapeDtypeStruct((n_rows, w.shape[1]), BF16))
    n_tiles = [y.shape[2] // LANE_TILE for y in y_ts]
    res = pl.pallas_call(
        functools.partial(_glu_out_kernel, counts=counts, has_cast=cast_rows is not None,
                          n_tiles=n_tiles),
        grid=(sum(counts),),
        in_specs=in_specs,
        out_specs=out_specs,
        out_shape=out_shape,
        scratch_shapes=[pltpu.VMEM((cb, LANE_TILE), BF16)] * max(n_tiles),
        compiler_params=_params("arbitrary"),
    )(*args)
    return tuple(res[:len(y_ts)]), (res[len(y_ts)] if cast_rows is not None else None)


def _out_norm_kernel(x_ref, oa_ref, r_ref, wo_ref, g_ref, y_ref, ms_ref, *, tc, n):
    ca = oa_ref.shape[-1]
    d = x_ref.shape[-1]
    nc = d // LANES
    p = jnp.dot(oa_ref[...].reshape(tc * n, ca), wo_ref[...], preferred_element_type=F32)
    for j in range(tc):
        s = p[j * n:(j + 1) * n] + r_ref[j]
        for c in range(nc):
            ms_ref[c, pl.ds(j, n, stride=tc), :] = s[:, c * LANES:(c + 1) * LANES]
    acc = x_ref[...] + jnp.concatenate([ms_ref[c] for c in range(nc)], axis=1)
    ms = jnp.mean(acc * acc, axis=-1, keepdims=True)
    y_ref[...] = acc * lax.rsqrt(ms + EPS) * g_ref[...]


def _out_norm(x2d, oa, r, wo_a, gf, n):
    tc, nk, ca = oa.shape
    t, d = x2d.shape
    return pl.pallas_call(
        functools.partial(_out_norm_kernel, tc=tc, n=n),
        grid=(nk // n,),
        in_specs=[pl.BlockSpec((n * tc, d), lambda i: (i, 0)),
                  pl.BlockSpec((tc, n, ca), lambda i: (0, i, 0)),
                  pl.BlockSpec((tc, n, d), lambda i: (0, i, 0)),
                  pl.BlockSpec((ca, d), lambda i: (0, 0), pipeline_mode=pl.Buffered(1)),
                  pl.BlockSpec((1, d), lambda i: (0, 0))],
        out_specs=pl.BlockSpec((n * tc, d), lambda i: (i, 0)),
        out_shape=jax.ShapeDtypeStruct((t, d), F32),
        scratch_shapes=[pltpu.VMEM((d // LANES, n * tc, LANES), F32)],
        compiler_params=_params("parallel"),
    )(x2d, oa, r, wo_a, gf.reshape(1, d))


def _discretise(a_re, a_im, log_dt):
    dt = jnp.exp(log_dt)[:, None]
    x_re, x_im = a_re * dt, a_im * dt
    n = (2.0 ** jnp.arange(N_POW, dtype=F32))[None, :, None]
    mag = jnp.exp(n * x_re[:, None, :])
    ang = n * x_im[:, None, :]
    pw_r, pw_i = mag * jnp.cos(ang), mag * jnp.sin(ang)
    lr, li = pw_r[:, 0], pw_i[:, 0]
    den = a_re * a_re + a_im * a_im
    nr, ni = lr - 1.0, li
    qr, qi = (nr * a_re + ni * a_im) / den, (ni * a_re - nr * a_im) / den
    cat = lambda a, b: jnp.concatenate([a, b], axis=-1)
    rowtab = jnp.stack([cat(lr, lr), cat(li, li), cat(qr, qi), cat(-qi, qr)], axis=1)
    zpad = jnp.zeros_like(pw_r[:, :POW_COLS - N_POW])
    lt = jnp.concatenate([pw_r, zpad, pw_i, zpad], axis=1).transpose(0, 2, 1)
    b0 = TC_PROMPT.bit_length() - 1
    sc_r, sc_i = pw_r[:, b0:b0 + SCAN_STEPS], pw_i[:, b0:b0 + SCAN_STEPS]
    zrow = jnp.zeros_like(cat(sc_r, sc_r)[:, :SCAN_ROWS - SCAN_STEPS])
    scan_rows = jnp.concatenate([cat(sc_r, sc_r), zrow, cat(sc_i, sc_i), zrow], axis=1)
    return rowtab, lt, scan_rows


def _ssm_tables_kernel(rt_ref, btr_ref, bti_ref, c2r_ref, c2i_ref, c2n_ref,
                       mp_ref, q_ref, mps_ref, qs_ref, *, gb, tc, tcs, unroll):
    rows = tc * GROUP
    rows_s = tcs * GROUP
    half = rt_ref.shape[2] // 2
    lane_k = lax.broadcasted_iota(jnp.int32, (GROUP, rows), 1)
    lane_q = lax.broadcasted_iota(jnp.int32, (1, 2 * half), 1)

    def one_group(g):
        rt = rt_ref[g]
        l_r, l_i, qu, qv = rt[0:1], rt[1:2], rt[2:3], rt[3:4]
        btr, bti = btr_ref[g], bti_ref[g]
        bb = btr * qu + bti * qv
        bs = btr * qv - bti * qu
        c2r, c2i = c2r_ref[g], c2i_ref[g]
        cur_r = jnp.ones((1, 2 * half), F32)
        cur_i = jnp.zeros((1, 2 * half), F32)
        et_blocks = []
        for n in range(tc):
            et_blocks.append(cur_r * bb + cur_i * bs)
            cur_r, cur_i = cur_r * l_r - cur_i * l_i, cur_r * l_i + cur_i * l_r
            u = jnp.where(lane_q < half, cur_r, -cur_i)
            v = jnp.where(lane_q < half, -cur_i, -cur_r)
            blk = (c2r * u + c2i * v).astype(q_ref.dtype)
            q_ref[g, n * GROUP:(n + 1) * GROUP, :] = blk
            if n < tcs:
                qs_ref[g, n * GROUP:(n + 1) * GROUP, :] = blk
        p = jnp.concatenate(et_blocks[::-1], axis=0).T
        mp_ref[g, rows:, :] = p.astype(mp_ref.dtype)
        mps_ref[g, rows_s:, :] = p[:, rows - rows_s:].astype(mps_ref.dtype)
        krow = jnp.dot(c2n_ref[g], p, precision=lax.Precision.HIGHEST, preferred_element_type=F32)
        for i in range(tc):
            sh = (tc - 1 - i) * GROUP
            blk = krow if sh == 0 else pltpu.roll(krow, rows - sh, 1)
            blk = jnp.where(lane_k < (i + 1) * GROUP, blk, 0.0)
            mp_ref[g, i * GROUP:(i + 1) * GROUP, :] = blk.astype(mp_ref.dtype)
            if i < tcs:
                mps_ref[g, i * GROUP:(i + 1) * GROUP, :] = blk[:, :rows_s].astype(mps_ref.dtype)

    def body(gi, carry):
        for k in range(unroll):
            one_group(gi * unroll + k)
        return carry

    if unroll == gb:
        body(0, 0)
    else:
        lax.fori_loop(0, gb // unroll, body, 0)


def _ssm_tables_job(rowtab, b_re, b_im, c_re, c_im, tcs, gb):
    ng, _, p2 = rowtab.shape
    tc = TC_PROMPT
    rows, rows_s = tc * GROUP, tcs * GROUP
    dup = lambda a: jnp.concatenate([a, a], axis=-1)
    args = (rowtab, dup(b_re.transpose(0, 2, 1)), dup(b_im.transpose(0, 2, 1)),
            dup(c_re), dup(c_im), jnp.concatenate([c_re, -c_im], axis=-1))
    shapes = [(ng, rows + p2, rows), (ng, rows, p2), (ng, rows_s + p2, rows_s), (ng, rows_s, p2)]
    in_specs = [pl.BlockSpec((gb,) + a.shape[1:], lambda i: (i, 0, 0)) for a in args]
    out_specs = [pl.BlockSpec((gb,) + s[1:], lambda i: (i, 0, 0)) for s in shapes]
    out_shape = [jax.ShapeDtypeStruct(s, BF16) for s in shapes]
    body = functools.partial(_ssm_tables_kernel, gb=gb, tc=tc, tcs=tcs, unroll=gb)
    return args, in_specs, out_specs, out_shape, body, ng // gb


def _mix_bias(b_s, hd, tc):
    nh = b_s.shape[0]
    r = CHUNK // tc
    if tc == TC_PROMPT:
        bm = b_s.reshape(nh, r, tc).transpose(0, 2, 1).reshape(nh, CHUNK)
    else:
        bm = jnp.repeat(b_s[:, :tc], r, axis=1)
    return jnp.broadcast_to(bm[:, :, None], (nh, CHUNK, hd))


def _tiles(tc, nk):
    fold = max(1, LANE_TILE // nk)
    return NORM_ROWS // tc, min(BRANCH_A_ROWS, tc * nk) // tc, fold


def kernel(x_prompt, x_sample, state_ssm_re, state_ssm_im, g_norm, w_in, g_v, w_s, b_s,
           a_re, a_im, log_dt, b_re, b_im, c_re, c_im, d_skip, w_glu, b_glu, w_out, g_final):
    depth = g_norm.shape[0]
    assert depth == 1, "single-layer step"
    bsz, seq, d = x_prompt.shape
    dbsz, dseq, _ = x_sample.shape
    nh = w_s.shape[1]
    ng, npst = a_re.shape[1:]
    cb = ng * GROUP
    ca = (w_in.shape[2] - 2 * cb) // 3
    hd = ca // nh
    assert seq // TC_PROMPT == LANES and dbsz == LANES and CHUNK % dseq == 0
    assert dseq & (dseq - 1) == 0 and dseq <= TC_PROMPT, "sample chunk must be a power of two"
    assert ca == cb == d

    l = 0
    gv = g_v[l].reshape(nh, 1, hd)
    rowtab, lt, scan_rows = _discretise(a_re[l], a_im[l], log_dt[l])
    tables_job = _ssm_tables_job(rowtab, b_re[l], b_im[l], c_re[l], c_im[l], dseq, SSM_GROUPS_PER_STEP)

    assert bsz % 2 == 0
    gb = SSM_GROUPS_PER_STEP
    tcs = (TC_PROMPT, dseq)
    nks = (bsz * seq // TC_PROMPT, dbsz)
    xs = (x_prompt.reshape(bsz * seq, d), x_sample.reshape(dbsz * dseq, d))
    tiles = [_tiles(tc, nk) for tc, nk in zip(tcs, nks)]
    xns, out_as, v_rows = [], [], []
    tabs = w_bt = None
    for s, (x2d, tc, (n_norm, n_a, _)) in enumerate(zip(xs, tcs, tiles)):
        xn, made = _norm_permute(x2d, g_norm[l], tc, n_norm, tables_job if s == 0 else None)
        tabs = made if s == 0 else tabs
        out_a, v, made_bt = _branch_a(xn, w_in[l], gv, w_s[l], _mix_bias(b_s[l], hd, tc), n_a,
                                      tc * n_a // PROJ_ROWS, s == 1, s == 0)
        w_bt = made_bt if s == 0 else w_bt
        xns.append(xn)
        out_as.append(out_a)
        v_rows.append(v)
    mp, q, mp_s, q_s = tabs
    folds = [t[2] for t in tiles]
    slabs = [xn.reshape(tc // f, f * nk, d) for xn, tc, nk, f in zip(xns, tcs, nks, folds)]
    zbs, (wg_t, wo_bt) = _proj_b(w_bt, slabs, ((w_glu[l], cb, 0), (w_out[l], cb, 1)))
    dsk = jnp.broadcast_to(d_skip[l][:, None], (cb, LANES))
    h0 = (state_ssm_re[l].transpose(1, 2, 0), state_ssm_im[l].transpose(1, 2, 0))
    y_p_t, hf_p = _ssm(zbs[0], mp, q, scan_rows, dsk, None, gb, folds[0])
    y_s_t, hf_s = _ssm(zbs[1], mp_s, q_s, lt, dsk, h0, gb, folds[1])
    bg = jnp.broadcast_to(b_glu[l][:, None], (cb, LANE_TILE))
    rs, wo_a = _glu_out((y_p_t, y_s_t), zbs, wg_t, bg, wo_bt, (w_out[l], ca))
    y_p, y_s = [_out_norm(x2d, out_a, r.reshape(tc, nk, d), wo_a, g_final, t[0])
                for x2d, out_a, r, tc, nk, t in zip(xs, out_as, rs, tcs, nks, tiles)]
    v_s = v_rows[1]

    hf_p = hf_p[0].reshape(ng, bsz // 2, 2, 2, npst)
    re_p = hf_p[:, :, 0].reshape(ng, bsz, npst).transpose(1, 0, 2)[None]
    im_p = hf_p[:, :, 1].reshape(ng, bsz, npst).transpose(1, 0, 2)[None]
    re_s = hf_s[0].transpose(2, 0, 1)[None]
    im_s = hf_s[1].transpose(2, 0, 1)[None]
    v_out = v_s.transpose(1, 0, 2)[None]

    dt_p, dt_s = x_prompt.dtype, x_sample.dtype
    return (y_p.reshape(bsz, seq, d).astype(dt_p), y_s.reshape(dbsz, dseq, d).astype(dt_s),
            re_p.astype(dt_p), im_p.astype(dt_p), re_s.astype(dt_s), im_s.astype(dt_s),
            v_out.astype(dt_s))
```

```python
import functools

import jax
import jax.numpy as jnp
from jax import lax
from jax.experimental import pallas as pl
from jax.experimental.pallas import tpu as pltpu

EPS = 1e-6
LANES = 128
CHUNK = 128
GROUP = 16
TC_PROMPT = 16
ROW_CHUNK = 512
LANE_TILE = 256
PROJ_ROWS = 512
BRANCH_A_ROWS = 2048
NORM_ROWS = 512
SSM_GROUPS_PER_STEP = 16
SSM_GROUPS_PER_STEP_1CHUNK = 32
POW_COLS = 16
SCAN_STEPS = 7
N_POW = 11
SCAN_ROWS = 8
GROUPS_PER_TRIP = 8
VMEM_LIMIT = 56 * 1024 * 1024

F32 = jnp.float32
BF16 = jnp.bfloat16


def _params(*sem):
    return pltpu.CompilerParams(dimension_semantics=sem, vmem_limit_bytes=VMEM_LIMIT)


def _norm_permute_kernel(x_ref, g_ref, *refs, tc, n, n_side_in, side_body):
    o_ref = refs[n_side_in]
    x = x_ref[...]
    ms = jnp.mean(x * x, axis=-1, keepdims=True)
    xn = (x * lax.rsqrt(ms + EPS) * g_ref[...]).astype(BF16)
    rows = n * tc
    dst = lax.broadcasted_iota(jnp.int32, (rows, rows), 0)
    src = lax.broadcasted_iota(jnp.int32, (rows, rows), 1)
    sel = (src == tc * (dst % n) + dst // n).astype(BF16)
    o_ref[...] = jnp.dot(sel, xn, preferred_element_type=F32).astype(o_ref.dtype).reshape(o_ref.shape)
    if side_body is not None:
        side_body(*refs[:n_side_in], *refs[n_side_in + 1:])


def _norm_permute(x2d, g, tc, n, side_job=None):
    t, d = x2d.shape
    nk = t // tc
    s_args, s_in, s_out, s_shape, s_body, s_steps = side_job or ((), [], [], [], None, nk // n)
    assert s_steps == nk // n, "the side job needs one block per grid step"
    res = pl.pallas_call(
        functools.partial(_norm_permute_kernel, tc=tc, n=n, n_side_in=len(s_args), side_body=s_body),
        grid=(nk // n,),
        in_specs=[pl.BlockSpec((n * tc, d), lambda i: (i, 0)),
                  pl.BlockSpec((1, d), lambda i: (0, 0))] + list(s_in),
        out_specs=[pl.BlockSpec((tc, n, d), lambda i: (0, i, 0))] + list(s_out),
        out_shape=[jax.ShapeDtypeStruct((tc, nk, d), BF16)] + list(s_shape),
        compiler_params=_params("parallel"),
    )(x2d, g.reshape(1, d), *s_args)
    return res[0], tuple(res[1:])


def _branch_a_kernel(*refs, tc, n, r, hd, n_sub, has_v, has_side):
    refs = list(refs)
    xn_ref, wu_ref, wv_ref, wg_ref, gv_ref, ws_ref, bmix_ref = refs[:7]
    del refs[:7]
    side_in = [refs.pop(0)] if has_side else []
    oa_ref = refs.pop(0)
    v_refs = [refs.pop(0)] if has_v else []
    side_out = [refs.pop(0)] if has_side else []
    w_ref, wm_ref = refs
    d = xn_ref.shape[-1]

    @pl.when(pl.program_id(1) == 0)
    def _():
        w_ref[:, :hd] = wu_ref[...].astype(BF16)
        w_ref[:, hd:2 * hd] = wv_ref[...].astype(BF16)
        w_ref[:, 2 * hd:] = wg_ref[...].astype(BF16)
        row = lax.broadcasted_iota(jnp.int32, (CHUNK, CHUNK), 0)
        col = lax.broadcasted_iota(jnp.int32, (CHUNK, CHUNK), 1)
        w_tril = jnp.where(row >= col, ws_ref[...], 0.0).astype(BF16)
        if tc == TC_PROMPT:
            time_of_row = tc * (row % r) + row // r
            same_seq = None
        else:
            time_of_row = row // r
            same_seq = (row % r) == (col % r)
        sel = (col == time_of_row).astype(BF16)
        sw = jnp.dot(sel, w_tril, preferred_element_type=F32).astype(BF16)
        wm = lax.dot_general(sw, sel, (((1,), (1,)), ((), ())), preferred_element_type=F32)
        if same_seq is not None:
            wm = jnp.where(same_seq, wm, 0.0)
        wm_ref[...] = wm.astype(BF16)

    ns = n // n_sub
    zs = [jnp.dot(xn_ref[:, k * ns:(k + 1) * ns, :].reshape(tc * ns, d), w_ref[...],
                  preferred_element_type=F32) for k in range(n_sub)]
    wm = wm_ref[...]
    bm = bmix_ref[...]
    for k, z in enumerate(zs):
        k0 = k * ns
        u = jax.nn.gelu(z[:, :hd])
        v = jax.nn.gelu(z[:, hd:2 * hd])
        ga = z[:, 2 * hd:]
        ms = jnp.mean(v * v, axis=-1, keepdims=True)
        vn3 = (v * lax.rsqrt(ms + EPS) * gv_ref[...]).reshape(tc, ns, hd)
        pre3 = (u * jax.nn.silu(ga)).reshape(tc, ns, hd)
        if v_refs:
            v_refs[0][:, k0:k0 + ns, :] = vn3
        for c in range(ns // r):
            vg = vn3[:, c * r:(c + 1) * r, :].reshape(CHUNK, hd)
            mixed = jnp.dot(wm, vg.astype(BF16), preferred_element_type=F32) + bm
            og = pre3[:, c * r:(c + 1) * r, :].reshape(CHUNK, hd) * mixed
            oa_ref[:, k0 + c * r:k0 + (c + 1) * r, :] = og.reshape(tc, r, hd).astype(oa_ref.dtype)
    _transpose_side_blocks(side_in, side_out)


def _branch_a(xn, w_in, gv, w_s, bmix, n, n_sub, want_v, transpose_b):
    tc, nk, d = xn.shape
    nh, _, hd = gv.shape
    r = CHUNK // tc
    nt = nk // n
    out_map = lambda h, i: (0, i, h)
    in_specs = [pl.BlockSpec((tc, n, d), lambda h, i: (0, i, 0)),
                pl.BlockSpec((d, hd), lambda h, i: (0, h)),
                pl.BlockSpec((d, hd), lambda h, i: (0, nh + h)),
                pl.BlockSpec((d, hd), lambda h, i: (0, 2 * nh + h)),
                pl.BlockSpec((None, 1, hd), lambda h, i: (h, 0, 0)),
                pl.BlockSpec((None, CHUNK, CHUNK), lambda h, i: (h, 0, 0)),
                pl.BlockSpec((None, CHUNK, hd), lambda h, i: (h, 0, 0))]
    args = [xn, w_in, w_in, w_in, gv, w_s, bmix]
    out_shape = [jax.ShapeDtypeStruct((tc, nk, nh * hd), BF16)]
    out_specs = [pl.BlockSpec((tc, n, hd), out_map)]
    if want_v:
        out_shape.append(jax.ShapeDtypeStruct((tc, nk, nh * hd), F32))
        out_specs.append(pl.BlockSpec((tc, n, hd), out_map))
    if transpose_b:
        col0 = 3 * nh * hd // LANES
        n_cols = w_in.shape[1] - 3 * nh * hd
        assert nh * nt * LANES == n_cols, "one 128-column block per grid step"
        in_specs.append(pl.BlockSpec((d, LANES), lambda h, i: (0, col0 + h * nt + i)))
        args.append(w_in)
        out_shape.append(jax.ShapeDtypeStruct((n_cols, d), BF16))
        out_specs.append(pl.BlockSpec((LANES, d), lambda h, i: (h * nt + i, 0)))
    res = list(pl.pallas_call(
        functools.partial(_branch_a_kernel, tc=tc, n=n, r=r, hd=hd, n_sub=n_sub,
                          has_v=want_v, has_side=transpose_b),
        grid=(nh, nt),
        in_specs=in_specs,
        out_specs=out_specs,
        out_shape=out_shape,
        scratch_shapes=[pltpu.VMEM((d, 3 * hd), BF16), pltpu.VMEM((CHUNK, CHUNK), BF16)],
        compiler_params=_params("parallel", "arbitrary"),
    )(*args))
    out_a = res.pop(0)
    v_rows = res.pop(0) if want_v else None
    w_bt = res.pop(0) if transpose_b else None
    return out_a, v_rows, w_bt


def _transpose_side_blocks(in_refs, out_refs):
    for w_ref, o_ref in zip(in_refs, out_refs):
        o_ref[...] = w_ref[...].T.astype(o_ref.dtype)


def _stream_maps(counts):
    starts = [sum(counts[:s]) for s in range(len(counts))]
    maps = [lambda j, st=st, c=c: jnp.clip(j - st, 0, c - 1) for st, c in zip(starts, counts)]
    owns = [lambda j, st=st, c=c: jnp.logical_and(j >= st, j < st + c) for st, c in zip(starts, counts)]
    return maps, owns


def _proj_b_kernel(w_ref, *refs, counts, n_side):
    ns = len(counts)
    xn_refs, side_in = refs[:ns], refs[ns:ns + n_side]
    o_refs, side_out = refs[ns + n_side:2 * ns + n_side], refs[2 * ns + n_side:]
    _, owns = _stream_maps(counts)
    j = pl.program_id(0)
    for s, (xn_ref, o_ref) in enumerate(zip(xn_refs, o_refs)):
        @pl.when(owns[s](j))
        def _(s=s, xn_ref=xn_ref, o_ref=o_ref):
            if s == 0:
                _transpose_side_blocks(side_in, side_out)
            for t in range(xn_ref.shape[0] // LANE_TILE):
                ls = slice(t * LANE_TILE, (t + 1) * LANE_TILE)
                xn = xn_ref[ls, :]
                for m in range(w_ref.shape[0] // ROW_CHUNK):
                    sl = slice(m * ROW_CHUNK, (m + 1) * ROW_CHUNK)
                    o_ref[sl, ls] = lax.dot_general(
                        w_ref[sl, :], xn, (((1,), (1,)), ((), ())),
                        preferred_element_type=F32).astype(o_ref.dtype)


def _proj_b(w_bt, xns, side=()):
    rows, d = w_bt.shape
    counts = [x.shape[0] for x in xns]
    maps, _ = _stream_maps(counts)
    in_specs = [pl.BlockSpec((rows, d), lambda j: (0, 0), pipeline_mode=pl.Buffered(1))]
    in_specs += [pl.BlockSpec((None, x.shape[1], d), lambda j, m=m: (m(j), 0, 0))
                 for x, m in zip(xns, maps)]
    out_specs = [pl.BlockSpec((None, rows, x.shape[1]), lambda j, m=m: (m(j), 0, 0))
                 for x, m in zip(xns, maps)]
    out_shape = [jax.ShapeDtypeStruct((x.shape[0], rows, x.shape[1]), BF16) for x in xns]
    for w, n_rows, row_block in side:
        n_cols = w.shape[1]
        assert counts[0] * LANES == n_cols, "one 128-column block per step of the first stream set"
        in_specs.append(pl.BlockSpec((n_rows, LANES), lambda j, rb=row_block: (rb, maps[0](j))))
        out_specs.append(pl.BlockSpec((LANES, n_rows), lambda j: (maps[0](j), 0)))
        out_shape.append(jax.ShapeDtypeStruct((n_cols, n_rows), BF16))
    res = pl.pallas_call(
        functools.partial(_proj_b_kernel, counts=counts, n_side=len(side)),
        grid=(sum(counts),),
        in_specs=in_specs,
        out_specs=out_specs,
        out_shape=out_shape,
        compiler_params=_params("arbitrary"),
    )(w_bt, *xns, *[w for w, _, _ in side])
    return tuple(res[:len(xns)]), tuple(res[len(xns):])


def _shift_rows(x, d):
    if d % 8 == 0:
        return jnp.concatenate([jnp.zeros((d, x.shape[1]), x.dtype), x[:-d]], axis=0)
    row = lax.broadcasted_iota(jnp.int32, x.shape, 0)
    return jnp.where(row >= d, pltpu.roll(x, d, 0), 0.0)


def _ssm_kernel(x_ref, mp_ref, q_ref, ct_ref, dsk_ref, *rest, tc, gb, nk, fold, has_h0):
    if has_h0:
        h0r_ref, h0i_ref, y_ref, hfr_ref, hfi_ref = rest
    else:
        y_ref, hf_ref = rest
    rows = tc * GROUP
    half = q_ref.shape[2] // 2
    b_tc = tc.bit_length() - 1

    def one_group(g):
        r0 = g * GROUP if isinstance(g, int) else pl.multiple_of(g * GROUP, GROUP)
        if fold == 1:
            xg = x_ref[:, pl.ds(r0, GROUP), :]
        else:
            xg = jnp.stack([x_ref[j // fold, pl.ds(r0, GROUP), (j % fold) * nk:(j % fold + 1) * nk]
                            for j in range(tc)])
        res = jnp.dot(mp_ref[g], xg.reshape(rows, nk), preferred_element_type=F32)
        y_local = res[:rows]
        s_re = res[rows:rows + half]
        s_im = res[rows + half:]
        ct = ct_ref[g]
        if has_h0:
            h_re, h_im = h0r_ref[g], h0i_ref[g]
            a_re, a_im = ct[:, b_tc:b_tc + 1], ct[:, POW_COLS + b_tc:POW_COLS + b_tc + 1]
            hfr_ref[g] = a_re * h_re - a_im * h_im + s_re
            hfi_ref[g] = a_re * h_im + a_im * h_re + s_im
            h_in = jnp.concatenate([h_re, h_im], axis=0)
        else:
            tiles = []
            for pair in range(nk // (2 * LANES)):
                lo, mid, hi = 2 * pair * LANES, (2 * pair + 1) * LANES, (2 * pair + 2) * LANES
                xr = jnp.concatenate([s_re[:, lo:mid], s_re[:, mid:hi]], axis=0).T
                xi = jnp.concatenate([s_im[:, lo:mid], s_im[:, mid:hi]], axis=0).T
                for s in range(SCAN_STEPS):
                    cr, ci = ct[s:s + 1], ct[SCAN_ROWS + s:SCAN_ROWS + s + 1]
                    rr, ri = _shift_rows(xr, 1 << s), _shift_rows(xi, 1 << s)
                    xr, xi = xr + cr * rr - ci * ri, xi + cr * ri + ci * rr
                hf_ref[g, 2 * pair:2 * pair + 1, :] = xr[LANES - 1:]
                hf_ref[g, 2 * pair + 1:2 * pair + 2, :] = xi[LANES - 1:]
                hr_t = _shift_rows(xr, 1).T
                hi_t = _shift_rows(xi, 1).T
                tiles.append(jnp.concatenate([hr_t[:half], hi_t[:half]], axis=0))
                tiles.append(jnp.concatenate([hr_t[half:], hi_t[half:]], axis=0))
            h_in = jnp.concatenate(tiles, axis=1)
        y = y_local + jnp.dot(q_ref[g], h_in.astype(BF16), preferred_element_type=F32)
        dsk = jnp.concatenate([dsk_ref[pl.ds(r0, GROUP), :]] * (nk // LANES), axis=1)
        y3 = y.reshape(tc, GROUP, nk) + dsk[None] * xg.astype(F32)
        ya = jax.nn.gelu(y3).astype(y_ref.dtype)
        if fold == 1:
            y_ref[:, pl.ds(r0, GROUP), :] = ya
        else:
            for j in range(tc):
                y_ref[j // fold, pl.ds(r0, GROUP), (j % fold) * nk:(j % fold + 1) * nk] = ya[j]

    def body(gi, carry):
        for k in range(GROUPS_PER_TRIP):
            one_group(gi * GROUPS_PER_TRIP + k)
        return carry

    if gb == GROUPS_PER_TRIP:
        body(0, 0)
    else:
        lax.fori_loop(0, gb // GROUPS_PER_TRIP, body, 0)


def _ssm(zb, mp, q, ct, dsk, h0, gb, fold):
    tcf, _, nkf = zb.shape
    tc, nk = tcf * fold, nkf // fold
    ng, mrows, rows = mp.shape
    p2 = mrows - rows
    cb = ng * GROUP
    has_h0 = h0 is not None
    in_specs = [pl.BlockSpec((tcf, gb * GROUP, nkf), lambda i: (0, i, 0)),
                pl.BlockSpec((gb, mrows, rows), lambda i: (i, 0, 0)),
                pl.BlockSpec((gb, rows, p2), lambda i: (i, 0, 0)),
                pl.BlockSpec((gb,) + ct.shape[1:], lambda i: (i, 0, 0)),
                pl.BlockSpec((gb * GROUP, LANES), lambda i: (i, 0))]
    args = [zb, mp, q, ct, dsk]
    if has_h0:
        state_spec = pl.BlockSpec((gb, p2 // 2, nk), lambda i: (i, 0, 0))
        in_specs += [state_spec, state_spec]
        args += list(h0)
        hf_specs = [state_spec, state_spec]
        hf_shapes = [jax.ShapeDtypeStruct((ng, p2 // 2, nk), F32)] * 2
    else:
        hf_specs = [pl.BlockSpec((gb, nk // LANES, LANES), lambda i: (i, 0, 0))]
        hf_shapes = [jax.ShapeDtypeStruct((ng, nk // LANES, LANES), F32)]
    res = pl.pallas_call(
        functools.partial(_ssm_kernel, tc=tc, gb=gb, nk=nk, fold=fold, has_h0=has_h0),
        grid=(ng // gb,),
        in_specs=in_specs,
        out_specs=[pl.BlockSpec((tcf, gb * GROUP, nkf), lambda i: (0, i, 0))] + hf_specs,
        out_shape=[jax.ShapeDtypeStruct((tcf, cb, nkf), BF16)] + hf_shapes,
        compiler_params=_params("parallel"),
    )(*args)
    return res[0], tuple(res[1:])


def _glu_out_kernel(wg_ref, bg_ref, wo_ref, *refs, counts, has_cast, n_tiles):
    ns = len(counts)
    refs = list(refs)
    y_refs, gate_refs = refs[:ns], refs[ns:2 * ns]
    del refs[:2 * ns]
    cast_in = refs.pop(0) if has_cast else None
    r_refs = refs[:ns]
    del refs[:ns]
    cast_out = refs.pop(0) if has_cast else None
    ob_refs = refs
    _, owns = _stream_maps(counts)
    j = pl.program_id(0)
    for s in range(ns):
        @pl.when(owns[s](j))
        def _(s=s, y_ref=y_refs[s], gate_ref=gate_refs[s], r_ref=r_refs[s]):
            if s == 0 and has_cast:
                cast_out[...] = cast_in[...].astype(cast_out.dtype)
            for t, ob_ref in enumerate(ob_refs[:n_tiles[s]]):
                ls = slice(t * LANE_TILE, (t + 1) * LANE_TILE)
                y = y_ref[:, ls]
                for m in range(wg_ref.shape[0] // ROW_CHUNK):
                    sl = slice(m * ROW_CHUNK, (m + 1) * ROW_CHUNK)
                    g = jnp.dot(wg_ref[sl, :], y, preferred_element_type=F32) + bg_ref[sl, :]
                    ob = (y_ref[sl, ls].astype(F32) * jax.nn.sigmoid(g)
                          * jax.nn.silu(gate_ref[sl, ls].astype(F32)))
                    ob_ref[sl, :] = ob.astype(ob_ref.dtype)
            for t, ob_ref in enumerate(ob_refs[:n_tiles[s]]):
                ob = ob_ref[...]
                for m in range(wo_ref.shape[0] // ROW_CHUNK):
                    sl = slice(m * ROW_CHUNK, (m + 1) * ROW_CHUNK)
                    r_ref[t * LANE_TILE:(t + 1) * LANE_TILE, sl] = jnp.dot(
                        wo_ref[sl, :], ob, preferred_element_type=F32).T


def _glu_out(y_ts, zbs, wg_t, bg, wo_t, cast_rows=None):
    cb = wg_t.shape[0]
    d = wo_t.shape[0]
    counts = [y.shape[0] for y in y_ts]
    maps, _ = _stream_maps(counts)
    const = lambda j: (0, 0)
    in_specs = [pl.BlockSpec((cb, cb), const, pipeline_mode=pl.Buffered(1)),
                pl.BlockSpec((cb, LANE_TILE), const, pipeline_mode=pl.Buffered(1)),
                pl.BlockSpec((d, cb), const, pipeline_mode=pl.Buffered(1))]
    in_specs += [pl.BlockSpec((None, cb, y.shape[2]), lambda j, m=m: (m(j), 0, 0)) for y, m in zip(y_ts, maps)]
    in_specs += [pl.BlockSpec((None, cb, y.shape[2]), lambda j, m=m: (m(j), 1, 0)) for y, m in zip(y_ts, maps)]
    args = [wg_t, bg, wo_t, *y_ts, *zbs]
    out_specs = [pl.BlockSpec((None, y.shape[2], d), lambda j, m=m: (m(j), 0, 0)) for y, m in zip(y_ts, maps)]
    out_shape = [jax.ShapeDtypeStruct((y.shape[0], y.shape[2], d), F32) for y in y_ts]
    if cast_rows is not None:
        w, n_rows = cast_rows
        assert counts[0] * LANES == n_rows, "one 128-row block per step of the first stream set"
        blk = pl.BlockSpec((LANES, w.shape[1]), lambda j: (maps[0](j), 0))
        in_specs.append(blk)
        args.append(w)
        out_specs.append(blk)
        out_shape.append(jax.ShapeDtypeStruct((n_rows, w.shape[1]), BF16))
    n_tiles = [y.shape[2] // LANE_TILE for y in y_ts]
    res = pl.pallas_call(
        functools.partial(_glu_out_kernel, counts=counts, has_cast=cast_rows is not None,
                          n_tiles=n_tiles),
        grid=(sum(counts),),
        in_specs=in_specs,
        out_specs=out_specs,
        out_shape=out_shape,
        scratch_shapes=[pltpu.VMEM((cb, LANE_TILE), BF16)] * max(n_tiles),
        compiler_params=_params("arbitrary"),
    )(*args)
    return tuple(res[:len(y_ts)]), (res[len(y_ts)] if cast_rows is not None else None)


def _out_norm_kernel(x_ref, oa_ref, r_ref, wo_ref, g_ref, y_ref, ms_ref, *, tc, n):
    ca = oa_ref.shape[-1]
    d = x_ref.shape[-1]
    nc = d // LANES
    p = jnp.dot(oa_ref[...].reshape(tc * n, ca), wo_ref[...], preferred_element_type=F32)
    for j in range(tc):
        s = p[j * n:(j + 1) * n] + r_ref[j]
        for c in range(nc):
            ms_ref[c, pl.ds(j, n, stride=tc), :] = s[:, c * LANES:(c + 1) * LANES]
    acc = x_ref[...] + jnp.concatenate([ms_ref[c] for c in range(nc)], axis=1)
    ms = jnp.mean(acc * acc, axis=-1, keepdims=True)
    y_ref[...] = acc * lax.rsqrt(ms + EPS) * g_ref[...]


def _out_norm(x2d, oa, r, wo_a, gf, n):
    tc, nk, ca = oa.shape
    t, d = x2d.shape
    return pl.pallas_call(
        functools.partial(_out_norm_kernel, tc=tc, n=n),
        grid=(nk // n,),
        in_specs=[pl.BlockSpec((n * tc, d), lambda i: (i, 0)),
                  pl.BlockSpec((tc, n, ca), lambda i: (0, i, 0)),
                  pl.BlockSpec((tc, n, d), lambda i: (0, i, 0)),
                  pl.BlockSpec((ca, d), lambda i: (0, 0), pipeline_mode=pl.Buffered(1)),
                  pl.BlockSpec((1, d), lambda i: (0, 0))],
        out_specs=pl.BlockSpec((n * tc, d), lambda i: (i, 0)),
        out_shape=jax.ShapeDtypeStruct((t, d), F32),
        scratch_shapes=[pltpu.VMEM((d // LANES, n * tc, LANES), F32)],
        compiler_params=_params("parallel"),
    )(x2d, oa, r, wo_a, gf.reshape(1, d))


def _discretise(a_re, a_im, log_dt):
    dt = jnp.exp(log_dt)[:, None]
    x_re, x_im = a_re * dt, a_im * dt
    n = (2.0 ** jnp.arange(N_POW, dtype=F32))[None, :, None]
    mag = jnp.exp(n * x_re[:, None, :])
    ang = n * x_im[:, None, :]
    pw_r, pw_i = mag * jnp.cos(ang), mag * jnp.sin(ang)
    lr, li = pw_r[:, 0], pw_i[:, 0]
    den = a_re * a_re + a_im * a_im
    nr, ni = lr - 1.0, li
    qr, qi = (nr * a_re + ni * a_im) / den, (ni * a_re - nr * a_im) / den
    cat = lambda a, b: jnp.concatenate([a, b], axis=-1)
    rowtab = jnp.stack([cat(lr, lr), cat(li, li), cat(qr, qi), cat(-qi, qr)], axis=1)
    zpad = jnp.zeros_like(pw_r[:, :POW_COLS - N_POW])
    lt = jnp.concatenate([pw_r, zpad, pw_i, zpad], axis=1).transpose(0, 2, 1)
    b0 = TC_PROMPT.bit_length() - 1
    sc_r, sc_i = pw_r[:, b0:b0 + SCAN_STEPS], pw_i[:, b0:b0 + SCAN_STEPS]
    zrow = jnp.zeros_like(cat(sc_r, sc_r)[:, :SCAN_ROWS - SCAN_STEPS])
    scan_rows = jnp.concatenate([cat(sc_r, sc_r), zrow, cat(sc_i, sc_i), zrow], axis=1)
    return rowtab, lt, scan_rows


def _ssm_tables_kernel(rt_ref, btr_ref, bti_ref, c2r_ref, c2i_ref, c2n_ref,
                       mp_ref, q_ref, mps_ref, qs_ref, *, gb, tc, tcs, unroll):
    rows = tc * GROUP
    rows_s = tcs * GROUP
    half = rt_ref.shape[2] // 2
    lane_k = lax.broadcasted_iota(jnp.int32, (GROUP, rows), 1)
    lane_q = lax.broadcasted_iota(jnp.int32, (1, 2 * half), 1)

    def one_group(g):
        rt = rt_ref[g]
        l_r, l_i, qu, qv = rt[0:1], rt[1:2], rt[2:3], rt[3:4]
        btr, bti = btr_ref[g], bti_ref[g]
        bb = btr * qu + bti * qv
        bs = btr * qv - bti * qu
        c2r, c2i = c2r_ref[g], c2i_ref[g]
        cur_r = jnp.ones((1, 2 * half), F32)
        cur_i = jnp.zeros((1, 2 * half), F32)
        et_blocks = []
        for n in range(tc):
            et_blocks.append(cur_r * bb + cur_i * bs)
            cur_r, cur_i = cur_r * l_r - cur_i * l_i, cur_r * l_i + cur_i * l_r
            u = jnp.where(lane_q < half, cur_r, -cur_i)
            v = jnp.where(lane_q < half, -cur_i, -cur_r)
            blk = (c2r * u + c2i * v).astype(q_ref.dtype)
            q_ref[g, n * GROUP:(n + 1) * GROUP, :] = blk
            if n < tcs:
                qs_ref[g, n * GROUP:(n + 1) * GROUP, :] = blk
        p = jnp.concatenate(et_blocks[::-1], axis=0).T
        mp_ref[g, rows:, :] = p.astype(mp_ref.dtype)
        mps_ref[g, rows_s:, :] = p[:, rows - rows_s:].astype(mps_ref.dtype)
        krow = jnp.dot(c2n_ref[g], p, precision=lax.Precision.HIGHEST, preferred_element_type=F32)
        for i in range(tc):
            sh = (tc - 1 - i) * GROUP
            blk = krow if sh == 0 else pltpu.roll(krow, rows - sh, 1)
            blk = jnp.where(lane_k < (i + 1) * GROUP, blk, 0.0)
            mp_ref[g, i * GROUP:(i + 1) * GROUP, :] = blk.astype(mp_ref.dtype)
            if i < tcs:
                mps_ref[g, i * GROUP:(i + 1) * GROUP, :] = blk[:, :rows_s].astype(mps_ref.dtype)

    def body(gi, carry):
        for k in range(unroll):
            one_group(gi * unroll + k)
        return carry

    if unroll == gb:
        body(0, 0)
    else:
        lax.fori_loop(0, gb // unroll, body, 0)


def _ssm_tables_job(rowtab, b_re, b_im, c_re, c_im, tcs, gb):
    ng, _, p2 = rowtab.shape
    tc = TC_PROMPT
    rows, rows_s = tc * GROUP, tcs * GROUP
    dup = lambda a: jnp.concatenate([a, a], axis=-1)
    args = (rowtab, dup(b_re.transpose(0, 2, 1)), dup(b_im.transpose(0, 2, 1)),
            dup(c_re), dup(c_im), jnp.concatenate([c_re, -c_im], axis=-1))
    shapes = [(ng, rows + p2, rows), (ng, rows, p2), (ng, rows_s + p2, rows_s), (ng, rows_s, p2)]
    in_specs = [pl.BlockSpec((gb,) + a.shape[1:], lambda i: (i, 0, 0)) for a in args]
    out_specs = [pl.BlockSpec((gb,) + s[1:], lambda i: (i, 0, 0)) for s in shapes]
    out_shape = [jax.ShapeDtypeStruct(s, BF16) for s in shapes]
    body = functools.partial(_ssm_tables_kernel, gb=gb, tc=tc, tcs=tcs, unroll=gb)
    return args, in_specs, out_specs, out_shape, body, ng // gb


def _mix_bias(b_s, hd, tc):
    nh = b_s.shape[0]
    r = CHUNK // tc
    if tc == TC_PROMPT:
        bm = b_s.reshape(nh, r, tc).transpose(0, 2, 1).reshape(nh, CHUNK)
    else:
        bm = jnp.repeat(b_s[:, :tc], r, axis=1)
    return jnp.broadcast_to(bm[:, :, None], (nh, CHUNK, hd))


def _tiles(tc, nk):
    fold = max(1, LANE_TILE // nk)
    return NORM_ROWS // tc, min(BRANCH_A_ROWS, tc * nk) // tc, fold


def kernel(x_prompt, x_sample, state_ssm_re, state_ssm_im, g_norm, w_in, g_v, w_s, b_s,
           a_re, a_im, log_dt, b_re, b_im, c_re, c_im, d_skip, w_glu, b_glu, w_out, g_final):
    depth = g_norm.shape[0]
    assert depth == 1, "single-layer step"
    bsz, seq, d = x_prompt.shape
    dbsz, dseq, _ = x_sample.shape
    nh = w_s.shape[1]
    ng, npst = a_re.shape[1:]
    cb = ng * GROUP
    ca = (w_in.shape[2] - 2 * cb) // 3
    hd = ca // nh
    assert seq // TC_PROMPT == LANES and dbsz == LANES and CHUNK % dseq == 0
    assert dseq & (dseq - 1) == 0 and dseq <= TC_PROMPT, "sample chunk must be a power of two"
    assert ca == cb == d

    l = 0
    gv = g_v[l].reshape(nh, 1, hd)
    rowtab, lt, scan_rows = _discretise(a_re[l], a_im[l], log_dt[l])

    assert bsz % 2 == 0
    tcs = (TC_PROMPT, dseq)
    nks = (bsz * seq // TC_PROMPT, dbsz)
    xs = (x_prompt.reshape(bsz * seq, d), x_sample.reshape(dbsz * dseq, d))
    tiles = [_tiles(tc, nk) for tc, nk in zip(tcs, nks)]
    tables_job = _ssm_tables_job(rowtab, b_re[l], b_im[l], c_re[l], c_im[l], dseq,
                                 ng // (nks[0] // tiles[0][0]))
    xns, out_as, v_rows = [], [], []
    tabs = w_bt = None
    for s, (x2d, tc, (n_norm, n_a, _)) in enumerate(zip(xs, tcs, tiles)):
        xn, made = _norm_permute(x2d, g_norm[l], tc, n_norm, tables_job if s == 0 else None)
        tabs = made if s == 0 else tabs
        out_a, v, made_bt = _branch_a(xn, w_in[l], gv, w_s[l], _mix_bias(b_s[l], hd, tc), n_a,
                                      tc * n_a // PROJ_ROWS, s == 1, s == 0)
        w_bt = made_bt if s == 0 else w_bt
        xns.append(xn)
        out_as.append(out_a)
        v_rows.append(v)
    mp, q, mp_s, q_s = tabs
    folds = [t[2] for t in tiles]
    slabs = [xn.reshape(tc // f, f * nk, d) for xn, tc, nk, f in zip(xns, tcs, nks, folds)]
    zbs, (wg_t, wo_bt) = _proj_b(w_bt, slabs, ((w_glu[l], cb, 0), (w_out[l], cb, 1)))
    dsk = jnp.broadcast_to(d_skip[l][:, None], (cb, LANES))
    h0 = (state_ssm_re[l].transpose(1, 2, 0), state_ssm_im[l].transpose(1, 2, 0))
    y_p_t, hf_p = _ssm(zbs[0], mp, q, scan_rows, dsk, None, SSM_GROUPS_PER_STEP, folds[0])
    y_s_t, hf_s = _ssm(zbs[1], mp_s, q_s, lt, dsk, h0, SSM_GROUPS_PER_STEP_1CHUNK, folds[1])
    bg = jnp.broadcast_to(b_glu[l][:, None], (cb, LANE_TILE))
    rs, wo_a = _glu_out((y_p_t, y_s_t), zbs, wg_t, bg, wo_bt, (w_out[l], ca))
    y_p, y_s = [_out_norm(x2d, out_a, r.reshape(tc, nk, d), wo_a, g_final, t[0])
                for x2d, out_a, r, tc, nk, t in zip(xs, out_as, rs, tcs, nks, tiles)]
    v_s = v_rows[1]

    hf_p = hf_p[0].reshape(ng, bsz // 2, 2, 2, npst)
    re_p = hf_p[:, :, 0].reshape(ng, bsz, npst).transpose(1, 0, 2)[None]
    im_p = hf_p[:, :, 1].reshape(ng, bsz, npst).transpose(1, 0, 2)[None]
    re_s = hf_s[0].transpose(2, 0, 1)[None]
    im_s = hf_s[1].transpose(2, 0, 1)[None]
    v_out = v_s.transpose(1, 0, 2)[None]

    dt_p, dt_s = x_prompt.dtype, x_sample.dtype
    return (y_p.reshape(bsz, seq, d).astype(dt_p), y_s.reshape(dbsz, dseq, d).astype(dt_s),
            re_p.astype(dt_p), im_p.astype(dt_p), re_s.astype(dt_s), im_s.astype(dt_s),
            v_out.astype(dt_s))
```

```python
import functools

import jax
import jax.numpy as jnp
from jax import lax
from jax.experimental import pallas as pl
from jax.experimental.pallas import tpu as pltpu

EPS = 1e-6
LANES = 128
CHUNK = 128
GROUP = 16
TC_PROMPT = 16
ROW_CHUNK = 512
LANE_TILE = 256
PROJ_ROWS = 512
BRANCH_A_ROWS = 2048
NORM_ROWS = 512
OUT_ROWS = 256
SSM_GROUPS_PER_STEP = 8
SSM_GROUPS_PER_STEP_1CHUNK = 32
POW_COLS = 16
SCAN_STEPS = 7
N_POW = 11
SCAN_ROWS = 8
GROUPS_PER_TRIP = 8
VMEM_LIMIT = 56 * 1024 * 1024

F32 = jnp.float32
BF16 = jnp.bfloat16


def _params(*sem):
    return pltpu.CompilerParams(dimension_semantics=sem, vmem_limit_bytes=VMEM_LIMIT)


def _norm_permute_kernel(x_ref, g_ref, *refs, tc, n, n_side_in, side_body):
    o_ref = refs[n_side_in]
    x = x_ref[...]
    ms = jnp.mean(x * x, axis=-1, keepdims=True)
    xn = (x * lax.rsqrt(ms + EPS) * g_ref[...]).astype(BF16)
    rows = n * tc
    dst = lax.broadcasted_iota(jnp.int32, (rows, rows), 0)
    src = lax.broadcasted_iota(jnp.int32, (rows, rows), 1)
    sel = (src == tc * (dst % n) + dst // n).astype(BF16)
    o_ref[...] = jnp.dot(sel, xn, preferred_element_type=F32).astype(o_ref.dtype).reshape(o_ref.shape)
    if side_body is not None:
        side_body(*refs[:n_side_in], *refs[n_side_in + 1:])


def _norm_permute(x2d, g, tc, n, side_job=None):
    t, d = x2d.shape
    nk = t // tc
    s_args, s_in, s_out, s_shape, s_body, s_steps = side_job or ((), [], [], [], None, nk // n)
    assert s_steps == nk // n, "the side job needs one block per grid step"
    res = pl.pallas_call(
        functools.partial(_norm_permute_kernel, tc=tc, n=n, n_side_in=len(s_args), side_body=s_body),
        grid=(nk // n,),
        in_specs=[pl.BlockSpec((n * tc, d), lambda i: (i, 0)),
                  pl.BlockSpec((1, d), lambda i: (0, 0))] + list(s_in),
        out_specs=[pl.BlockSpec((tc, n, d), lambda i: (0, i, 0))] + list(s_out),
        out_shape=[jax.ShapeDtypeStruct((tc, nk, d), BF16)] + list(s_shape),
        compiler_params=_params("parallel"),
    )(x2d, g.reshape(1, d), *s_args)
    return res[0], tuple(res[1:])


def _branch_a_kernel(*refs, tc, n, r, hd, n_sub, has_v, has_side):
    refs = list(refs)
    xn_ref, wu_ref, wv_ref, wg_ref, gv_ref, ws_ref, bmix_ref = refs[:7]
    del refs[:7]
    side_in = [refs.pop(0)] if has_side else []
    oa_ref = refs.pop(0)
    v_refs = [refs.pop(0)] if has_v else []
    side_out = [refs.pop(0)] if has_side else []
    w_ref, wm_ref = refs
    d = xn_ref.shape[-1]

    @pl.when(pl.program_id(1) == 0)
    def _():
        w_ref[:, :hd] = wu_ref[...].astype(BF16)
        w_ref[:, hd:2 * hd] = wv_ref[...].astype(BF16)
        w_ref[:, 2 * hd:] = wg_ref[...].astype(BF16)
        row = lax.broadcasted_iota(jnp.int32, (CHUNK, CHUNK), 0)
        col = lax.broadcasted_iota(jnp.int32, (CHUNK, CHUNK), 1)
        w_tril = jnp.where(row >= col, ws_ref[...], 0.0).astype(BF16)
        if tc == TC_PROMPT:
            time_of_row = tc * (row % r) + row // r
            same_seq = None
        else:
            time_of_row = row // r
            same_seq = (row % r) == (col % r)
        sel = (col == time_of_row).astype(BF16)
        sw = jnp.dot(sel, w_tril, preferred_element_type=F32).astype(BF16)
        wm = lax.dot_general(sw, sel, (((1,), (1,)), ((), ())), preferred_element_type=F32)
        if same_seq is not None:
            wm = jnp.where(same_seq, wm, 0.0)
        wm_ref[...] = wm.astype(BF16)

    ns = n // n_sub
    zs = [jnp.dot(xn_ref[:, k * ns:(k + 1) * ns, :].reshape(tc * ns, d), w_ref[...],
                  preferred_element_type=F32) for k in range(n_sub)]
    wm = wm_ref[...]
    bm = bmix_ref[...]
    for k, z in enumerate(zs):
        k0 = k * ns
        u = jax.nn.gelu(z[:, :hd])
        v = jax.nn.gelu(z[:, hd:2 * hd])
        ga = z[:, 2 * hd:]
        ms = jnp.mean(v * v, axis=-1, keepdims=True)
        vn3 = (v * lax.rsqrt(ms + EPS) * gv_ref[...]).reshape(tc, ns, hd)
        pre3 = (u * jax.nn.silu(ga)).reshape(tc, ns, hd)
        if v_refs:
            v_refs[0][:, k0:k0 + ns, :] = vn3
        for c in range(ns // r):
            vg = vn3[:, c * r:(c + 1) * r, :].reshape(CHUNK, hd)
            mixed = jnp.dot(wm, vg.astype(BF16), preferred_element_type=F32) + bm
            og = pre3[:, c * r:(c + 1) * r, :].reshape(CHUNK, hd) * mixed
            oa_ref[:, k0 + c * r:k0 + (c + 1) * r, :] = og.reshape(tc, r, hd).astype(oa_ref.dtype)
    _transpose_side_blocks(side_in, side_out)


def _branch_a(xn, w_in, gv, w_s, bmix, n, n_sub, want_v, transpose_b):
    tc, nk, d = xn.shape
    nh, _, hd = gv.shape
    r = CHUNK // tc
    nt = nk // n
    out_map = lambda h, i: (0, i, h)
    in_specs = [pl.BlockSpec((tc, n, d), lambda h, i: (0, i, 0)),
                pl.BlockSpec((d, hd), lambda h, i: (0, h)),
                pl.BlockSpec((d, hd), lambda h, i: (0, nh + h)),
                pl.BlockSpec((d, hd), lambda h, i: (0, 2 * nh + h)),
                pl.BlockSpec((None, 1, hd), lambda h, i: (h, 0, 0)),
                pl.BlockSpec((None, CHUNK, CHUNK), lambda h, i: (h, 0, 0)),
                pl.BlockSpec((None, CHUNK, hd), lambda h, i: (h, 0, 0))]
    args = [xn, w_in, w_in, w_in, gv, w_s, bmix]
    out_shape = [jax.ShapeDtypeStruct((tc, nk, nh * hd), BF16)]
    out_specs = [pl.BlockSpec((tc, n, hd), out_map)]
    if want_v:
        out_shape.append(jax.ShapeDtypeStruct((tc, nk, nh * hd), F32))
        out_specs.append(pl.BlockSpec((tc, n, hd), out_map))
    if transpose_b:
        col0 = 3 * nh * hd // LANES
        n_cols = w_in.shape[1] - 3 * nh * hd
        assert nh * nt * LANES == n_cols, "one 128-column block per grid step"
        in_specs.append(pl.BlockSpec((d, LANES), lambda h, i: (0, col0 + h * nt + i)))
        args.append(w_in)
        out_shape.append(jax.ShapeDtypeStruct((n_cols, d), BF16))
        out_specs.append(pl.BlockSpec((LANES, d), lambda h, i: (h * nt + i, 0)))
    res = list(pl.pallas_call(
        functools.partial(_branch_a_kernel, tc=tc, n=n, r=r, hd=hd, n_sub=n_sub,
                          has_v=want_v, has_side=transpose_b),
        grid=(nh, nt),
        in_specs=in_specs,
        out_specs=out_specs,
        out_shape=out_shape,
        scratch_shapes=[pltpu.VMEM((d, 3 * hd), BF16), pltpu.VMEM((CHUNK, CHUNK), BF16)],
        compiler_params=_params("parallel", "arbitrary"),
    )(*args))
    out_a = res.pop(0)
    v_rows = res.pop(0) if want_v else None
    w_bt = res.pop(0) if transpose_b else None
    return out_a, v_rows, w_bt


def _transpose_side_blocks(in_refs, out_refs):
    for w_ref, o_ref in zip(in_refs, out_refs):
        o_ref[...] = w_ref[...].T.astype(o_ref.dtype)


def _stream_maps(counts):
    starts = [sum(counts[:s]) for s in range(len(counts))]
    maps = [lambda j, st=st, c=c: jnp.clip(j - st, 0, c - 1) for st, c in zip(starts, counts)]
    owns = [lambda j, st=st, c=c: jnp.logical_and(j >= st, j < st + c) for st, c in zip(starts, counts)]
    return maps, owns


def _proj_b_kernel(w_ref, *refs, counts, n_side):
    ns = len(counts)
    xn_refs, side_in = refs[:ns], refs[ns:ns + n_side]
    o_refs, side_out = refs[ns + n_side:2 * ns + n_side], refs[2 * ns + n_side:]
    _, owns = _stream_maps(counts)
    j = pl.program_id(0)
    for s, (xn_ref, o_ref) in enumerate(zip(xn_refs, o_refs)):
        @pl.when(owns[s](j))
        def _(s=s, xn_ref=xn_ref, o_ref=o_ref):
            if s == 0:
                _transpose_side_blocks(side_in, side_out)
            for t in range(xn_ref.shape[0] // LANE_TILE):
                ls = slice(t * LANE_TILE, (t + 1) * LANE_TILE)
                xn = xn_ref[ls, :]
                for m in range(w_ref.shape[0] // ROW_CHUNK):
                    sl = slice(m * ROW_CHUNK, (m + 1) * ROW_CHUNK)
                    o_ref[sl, ls] = lax.dot_general(
                        w_ref[sl, :], xn, (((1,), (1,)), ((), ())),
                        preferred_element_type=F32).astype(o_ref.dtype)


def _proj_b(w_bt, xns, side=()):
    rows, d = w_bt.shape
    counts = [x.shape[0] for x in xns]
    maps, _ = _stream_maps(counts)
    in_specs = [pl.BlockSpec((rows, d), lambda j: (0, 0), pipeline_mode=pl.Buffered(1))]
    in_specs += [pl.BlockSpec((None, x.shape[1], d), lambda j, m=m: (m(j), 0, 0))
                 for x, m in zip(xns, maps)]
    out_specs = [pl.BlockSpec((None, rows, x.shape[1]), lambda j, m=m: (m(j), 0, 0))
                 for x, m in zip(xns, maps)]
    out_shape = [jax.ShapeDtypeStruct((x.shape[0], rows, x.shape[1]), BF16) for x in xns]
    for w, n_rows, row_block in side:
        n_cols = w.shape[1]
        assert counts[0] * LANES == n_cols, "one 128-column block per step of the first stream set"
        in_specs.append(pl.BlockSpec((n_rows, LANES), lambda j, rb=row_block: (rb, maps[0](j))))
        out_specs.append(pl.BlockSpec((LANES, n_rows), lambda j: (maps[0](j), 0)))
        out_shape.append(jax.ShapeDtypeStruct((n_cols, n_rows), BF16))
    res = pl.pallas_call(
        functools.partial(_proj_b_kernel, counts=counts, n_side=len(side)),
        grid=(sum(counts),),
        in_specs=in_specs,
        out_specs=out_specs,
        out_shape=out_shape,
        compiler_params=_params("arbitrary"),
    )(w_bt, *xns, *[w for w, _, _ in side])
    return tuple(res[:len(xns)]), tuple(res[len(xns):])


def _shift_rows(x, d):
    if d % 8 == 0:
        return jnp.concatenate([jnp.zeros((d, x.shape[1]), x.dtype), x[:-d]], axis=0)
    row = lax.broadcasted_iota(jnp.int32, x.shape, 0)
    return jnp.where(row >= d, pltpu.roll(x, d, 0), 0.0)


def _ssm_kernel(x_ref, mp_ref, q_ref, ct_ref, dsk_ref, *rest, tc, gb, nk, fold, has_h0):
    if has_h0:
        h0r_ref, h0i_ref, y_ref, hfr_ref, hfi_ref = rest
    else:
        y_ref, hf_ref = rest
    rows = tc * GROUP
    half = q_ref.shape[2] // 2
    b_tc = tc.bit_length() - 1

    def one_group(g):
        r0 = g * GROUP if isinstance(g, int) else pl.multiple_of(g * GROUP, GROUP)
        if fold == 1:
            xg = x_ref[:, pl.ds(r0, GROUP), :]
        else:
            xg = jnp.stack([x_ref[j // fold, pl.ds(r0, GROUP), (j % fold) * nk:(j % fold + 1) * nk]
                            for j in range(tc)])
        res = jnp.dot(mp_ref[g], xg.reshape(rows, nk), preferred_element_type=F32)
        y_local = res[:rows]
        s_re = res[rows:rows + half]
        s_im = res[rows + half:]
        ct = ct_ref[g]
        if has_h0:
            h_re, h_im = h0r_ref[g], h0i_ref[g]
            a_re, a_im = ct[:, b_tc:b_tc + 1], ct[:, POW_COLS + b_tc:POW_COLS + b_tc + 1]
            hfr_ref[g] = a_re * h_re - a_im * h_im + s_re
            hfi_ref[g] = a_re * h_im + a_im * h_re + s_im
            h_in = jnp.concatenate([h_re, h_im], axis=0)
        else:
            tiles = []
            for pair in range(nk // (2 * LANES)):
                lo, mid, hi = 2 * pair * LANES, (2 * pair + 1) * LANES, (2 * pair + 2) * LANES
                xr = jnp.concatenate([s_re[:, lo:mid], s_re[:, mid:hi]], axis=0).T
                xi = jnp.concatenate([s_im[:, lo:mid], s_im[:, mid:hi]], axis=0).T
                for s in range(SCAN_STEPS):
                    cr, ci = ct[s:s + 1], ct[SCAN_ROWS + s:SCAN_ROWS + s + 1]
                    rr, ri = _shift_rows(xr, 1 << s), _shift_rows(xi, 1 << s)
                    xr, xi = xr + cr * rr - ci * ri, xi + cr * ri + ci * rr
                hf_ref[g, 2 * pair:2 * pair + 1, :] = xr[LANES - 1:]
                hf_ref[g, 2 * pair + 1:2 * pair + 2, :] = xi[LANES - 1:]
                hr_t = _shift_rows(xr, 1).T
                hi_t = _shift_rows(xi, 1).T
                tiles.append(jnp.concatenate([hr_t[:half], hi_t[:half]], axis=0))
                tiles.append(jnp.concatenate([hr_t[half:], hi_t[half:]], axis=0))
            h_in = jnp.concatenate(tiles, axis=1)
        y = y_local + jnp.dot(q_ref[g], h_in.astype(BF16), preferred_element_type=F32)
        dsk = jnp.concatenate([dsk_ref[pl.ds(r0, GROUP), :]] * (nk // LANES), axis=1)
        y3 = y.reshape(tc, GROUP, nk) + dsk[None] * xg.astype(F32)
        ya = jax.nn.gelu(y3).astype(y_ref.dtype)
        if fold == 1:
            y_ref[:, pl.ds(r0, GROUP), :] = ya
        else:
            for j in range(tc):
                y_ref[j // fold, pl.ds(r0, GROUP), (j % fold) * nk:(j % fold + 1) * nk] = ya[j]

    def body(gi, carry):
        for k in range(GROUPS_PER_TRIP):
            one_group(gi * GROUPS_PER_TRIP + k)
        return carry

    if gb == GROUPS_PER_TRIP:
        body(0, 0)
    else:
        lax.fori_loop(0, gb // GROUPS_PER_TRIP, body, 0)


def _ssm(zb, mp, q, ct, dsk, h0, gb, fold):
    tcf, _, nkf = zb.shape
    tc, nk = tcf * fold, nkf // fold
    ng, mrows, rows = mp.shape
    p2 = mrows - rows
    cb = ng * GROUP
    has_h0 = h0 is not None
    in_specs = [pl.BlockSpec((tcf, gb * GROUP, nkf), lambda i: (0, i, 0)),
                pl.BlockSpec((gb, mrows, rows), lambda i: (i, 0, 0)),
                pl.BlockSpec((gb, rows, p2), lambda i: (i, 0, 0)),
                pl.BlockSpec((gb,) + ct.shape[1:], lambda i: (i, 0, 0)),
                pl.BlockSpec((gb * GROUP, LANES), lambda i: (i, 0))]
    args = [zb, mp, q, ct, dsk]
    if has_h0:
        state_spec = pl.BlockSpec((gb, p2 // 2, nk), lambda i: (i, 0, 0))
        in_specs += [state_spec, state_spec]
        args += list(h0)
        hf_specs = [state_spec, state_spec]
        hf_shapes = [jax.ShapeDtypeStruct((ng, p2 // 2, nk), F32)] * 2
    else:
        hf_specs = [pl.BlockSpec((gb, nk // LANES, LANES), lambda i: (i, 0, 0))]
        hf_shapes = [jax.ShapeDtypeStruct((ng, nk // LANES, LANES), F32)]
    res = pl.pallas_call(
        functools.partial(_ssm_kernel, tc=tc, gb=gb, nk=nk, fold=fold, has_h0=has_h0),
        grid=(ng // gb,),
        in_specs=in_specs,
        out_specs=[pl.BlockSpec((tcf, gb * GROUP, nkf), lambda i: (0, i, 0))] + hf_specs,
        out_shape=[jax.ShapeDtypeStruct((tcf, cb, nkf), BF16)] + hf_shapes,
        compiler_params=_params("parallel"),
    )(*args)
    return res[0], tuple(res[1:])


def _glu_out_kernel(wg_ref, bg_ref, wo_ref, *refs, counts, has_cast, n_tiles):
    ns = len(counts)
    refs = list(refs)
    y_refs, gate_refs = refs[:ns], refs[ns:2 * ns]
    del refs[:2 * ns]
    cast_in = refs.pop(0) if has_cast else None
    r_refs = refs[:ns]
    del refs[:ns]
    cast_out = refs.pop(0) if has_cast else None
    ob_refs = refs
    _, owns = _stream_maps(counts)
    j = pl.program_id(0)
    for s in range(ns):
        @pl.when(owns[s](j))
        def _(s=s, y_ref=y_refs[s], gate_ref=gate_refs[s], r_ref=r_refs[s]):
            if s == 0 and has_cast:
                cast_out[...] = cast_in[...].astype(cast_out.dtype)
            for t, ob_ref in enumerate(ob_refs[:n_tiles[s]]):
                ls = slice(t * LANE_TILE, (t + 1) * LANE_TILE)
                y = y_ref[:, ls]
                for m in range(wg_ref.shape[0] // ROW_CHUNK):
                    sl = slice(m * ROW_CHUNK, (m + 1) * ROW_CHUNK)
                    g = jnp.dot(wg_ref[sl, :], y, preferred_element_type=F32) + bg_ref[sl, :]
                    ob = (y_ref[sl, ls].astype(F32) * jax.nn.sigmoid(g)
                          * jax.nn.silu(gate_ref[sl, ls].astype(F32)))
                    ob_ref[sl, :] = ob.astype(ob_ref.dtype)
            for t, ob_ref in enumerate(ob_refs[:n_tiles[s]]):
                ob = ob_ref[...]
                for m in range(wo_ref.shape[0] // ROW_CHUNK):
                    sl = slice(m * ROW_CHUNK, (m + 1) * ROW_CHUNK)
                    r_ref[t * LANE_TILE:(t + 1) * LANE_TILE, sl] = jnp.dot(
                        wo_ref[sl, :], ob, preferred_element_type=F32).T


def _glu_out(y_ts, zbs, wg_t, bg, wo_t, cast_rows=None):
    cb = wg_t.shape[0]
    d = wo_t.shape[0]
    counts = [y.shape[0] for y in y_ts]
    maps, _ = _stream_maps(counts)
    const = lambda j: (0, 0)
    in_specs = [pl.BlockSpec((cb, cb), const, pipeline_mode=pl.Buffered(1)),
                pl.BlockSpec((cb, LANE_TILE), const, pipeline_mode=pl.Buffered(1)),
                pl.BlockSpec((d, cb), const, pipeline_mode=pl.Buffered(1))]
    in_specs += [pl.BlockSpec((None, cb, y.shape[2]), lambda j, m=m: (m(j), 0, 0)) for y, m in zip(y_ts, maps)]
    in_specs += [pl.BlockSpec((None, cb, y.shape[2]), lambda j, m=m: (m(j), 1, 0)) for y, m in zip(y_ts, maps)]
    args = [wg_t, bg, wo_t, *y_ts, *zbs]
    out_specs = [pl.BlockSpec((None, y.shape[2], d), lambda j, m=m: (m(j), 0, 0)) for y, m in zip(y_ts, maps)]
    out_shape = [jax.ShapeDtypeStruct((y.shape[0], y.shape[2], d), F32) for y in y_ts]
    if cast_rows is not None:
        w, n_rows = cast_rows
        assert counts[0] * LANES == n_rows, "one 128-row block per step of the first stream set"
        blk = pl.BlockSpec((LANES, w.shape[1]), lambda j: (maps[0](j), 0))
        in_specs.append(blk)
        args.append(w)
        out_specs.append(blk)
        out_shape.append(jax.ShapeDtypeStruct((n_rows, w.shape[1]), BF16))
    n_tiles = [y.shape[2] // LANE_TILE for y in y_ts]
    res = pl.pallas_call(
        functools.partial(_glu_out_kernel, counts=counts, has_cast=cast_rows is not None,
                          n_tiles=n_tiles),
        grid=(sum(counts),),
        in_specs=in_specs,
        out_specs=out_specs,
        out_shape=out_shape,
        scratch_shapes=[pltpu.VMEM((cb, LANE_TILE), BF16)] * max(n_tiles),
        compiler_params=_params("arbitrary"),
    )(*args)
    return tuple(res[:len(y_ts)]), (res[len(y_ts)] if cast_rows is not None else None)


def _out_norm_kernel(x_ref, oa_ref, r_ref, wo_ref, g_ref, y_ref, ms_ref, *, tc, n):
    ca = oa_ref.shape[-1]
    d = x_ref.shape[-1]
    nc = d // LANES
    p = jnp.dot(oa_ref[...].reshape(tc * n, ca), wo_ref[...], preferred_element_type=F32)
    for j in range(tc):
        s = p[j * n:(j + 1) * n] + r_ref[j]
        for c in range(nc):
            ms_ref[c, pl.ds(j, n, stride=tc), :] = s[:, c * LANES:(c + 1) * LANES]
    acc = x_ref[...] + jnp.concatenate([ms_ref[c] for c in range(nc)], axis=1)
    ms = jnp.mean(acc * acc, axis=-1, keepdims=True)
    y_ref[...] = acc * lax.rsqrt(ms + EPS) * g_ref[...]


def _out_norm(x2d, oa, r, wo_a, gf, n):
    tc, nk, ca = oa.shape
    t, d = x2d.shape
    return pl.pallas_call(
        functools.partial(_out_norm_kernel, tc=tc, n=n),
        grid=(nk // n,),
        in_specs=[pl.BlockSpec((n * tc, d), lambda i: (i, 0)),
                  pl.BlockSpec((tc, n, ca), lambda i: (0, i, 0)),
                  pl.BlockSpec((tc, n, d), lambda i: (0, i, 0)),
                  pl.BlockSpec((ca, d), lambda i: (0, 0), pipeline_mode=pl.Buffered(1)),
                  pl.BlockSpec((1, d), lambda i: (0, 0))],
        out_specs=pl.BlockSpec((n * tc, d), lambda i: (i, 0)),
        out_shape=jax.ShapeDtypeStruct((t, d), F32),
        scratch_shapes=[pltpu.VMEM((d // LANES, n * tc, LANES), F32)],
        compiler_params=_params("parallel"),
    )(x2d, oa, r, wo_a, gf.reshape(1, d))


def _discretise(a_re, a_im, log_dt):
    dt = jnp.exp(log_dt)[:, None]
    x_re, x_im = a_re * dt, a_im * dt
    n = (2.0 ** jnp.arange(N_POW, dtype=F32))[None, :, None]
    mag = jnp.exp(n * x_re[:, None, :])
    ang = n * x_im[:, None, :]
    pw_r, pw_i = mag * jnp.cos(ang), mag * jnp.sin(ang)
    lr, li = pw_r[:, 0], pw_i[:, 0]
    den = a_re * a_re + a_im * a_im
    nr, ni = lr - 1.0, li
    qr, qi = (nr * a_re + ni * a_im) / den, (ni * a_re - nr * a_im) / den
    cat = lambda a, b: jnp.concatenate([a, b], axis=-1)
    rowtab = jnp.stack([cat(lr, lr), cat(li, li), cat(qr, qi), cat(-qi, qr)], axis=1)
    zpad = jnp.zeros_like(pw_r[:, :POW_COLS - N_POW])
    lt = jnp.concatenate([pw_r, zpad, pw_i, zpad], axis=1).transpose(0, 2, 1)
    b0 = TC_PROMPT.bit_length() - 1
    sc_r, sc_i = pw_r[:, b0:b0 + SCAN_STEPS], pw_i[:, b0:b0 + SCAN_STEPS]
    zrow = jnp.zeros_like(cat(sc_r, sc_r)[:, :SCAN_ROWS - SCAN_STEPS])
    scan_rows = jnp.concatenate([cat(sc_r, sc_r), zrow, cat(sc_i, sc_i), zrow], axis=1)
    return rowtab, lt, scan_rows


def _ssm_tables_kernel(rt_ref, btr_ref, bti_ref, c2r_ref, c2i_ref, c2n_ref,
                       mp_ref, q_ref, mps_ref, qs_ref, *, gb, tc, tcs, unroll):
    rows = tc * GROUP
    rows_s = tcs * GROUP
    half = rt_ref.shape[2] // 2
    lane_k = lax.broadcasted_iota(jnp.int32, (GROUP, rows), 1)
    lane_q = lax.broadcasted_iota(jnp.int32, (1, 2 * half), 1)

    def one_group(g):
        rt = rt_ref[g]
        l_r, l_i, qu, qv = rt[0:1], rt[1:2], rt[2:3], rt[3:4]
        btr, bti = btr_ref[g], bti_ref[g]
        bb = btr * qu + bti * qv
        bs = btr * qv - bti * qu
        c2r, c2i = c2r_ref[g], c2i_ref[g]
        cur_r = jnp.ones((1, 2 * half), F32)
        cur_i = jnp.zeros((1, 2 * half), F32)
        et_blocks = []
        for n in range(tc):
            et_blocks.append(cur_r * bb + cur_i * bs)
            cur_r, cur_i = cur_r * l_r - cur_i * l_i, cur_r * l_i + cur_i * l_r
            u = jnp.where(lane_q < half, cur_r, -cur_i)
            v = jnp.where(lane_q < half, -cur_i, -cur_r)
            blk = (c2r * u + c2i * v).astype(q_ref.dtype)
            q_ref[g, n * GROUP:(n + 1) * GROUP, :] = blk
            if n < tcs:
                qs_ref[g, n * GROUP:(n + 1) * GROUP, :] = blk
        p = jnp.concatenate(et_blocks[::-1], axis=0).T
        mp_ref[g, rows:, :] = p.astype(mp_ref.dtype)
        mps_ref[g, rows_s:, :] = p[:, rows - rows_s:].astype(mps_ref.dtype)
        krow = jnp.dot(c2n_ref[g], p, precision=lax.Precision.HIGHEST, preferred_element_type=F32)
        for i in range(tc):
            sh = (tc - 1 - i) * GROUP
            blk = krow if sh == 0 else pltpu.roll(krow, rows - sh, 1)
            blk = jnp.where(lane_k < (i + 1) * GROUP, blk, 0.0)
            mp_ref[g, i * GROUP:(i + 1) * GROUP, :] = blk.astype(mp_ref.dtype)
            if i < tcs:
                mps_ref[g, i * GROUP:(i + 1) * GROUP, :] = blk[:, :rows_s].astype(mps_ref.dtype)

    def body(gi, carry):
        for k in range(unroll):
            one_group(gi * unroll + k)
        return carry

    if unroll == gb:
        body(0, 0)
    else:
        lax.fori_loop(0, gb // unroll, body, 0)


def _ssm_tables_job(rowtab, b_re, b_im, c_re, c_im, tcs, gb):
    ng, _, p2 = rowtab.shape
    tc = TC_PROMPT
    rows, rows_s = tc * GROUP, tcs * GROUP
    dup = lambda a: jnp.concatenate([a, a], axis=-1)
    args = (rowtab, dup(b_re.transpose(0, 2, 1)), dup(b_im.transpose(0, 2, 1)),
            dup(c_re), dup(c_im), jnp.concatenate([c_re, -c_im], axis=-1))
    shapes = [(ng, rows + p2, rows), (ng, rows, p2), (ng, rows_s + p2, rows_s), (ng, rows_s, p2)]
    in_specs = [pl.BlockSpec((gb,) + a.shape[1:], lambda i: (i, 0, 0)) for a in args]
    out_specs = [pl.BlockSpec((gb,) + s[1:], lambda i: (i, 0, 0)) for s in shapes]
    out_shape = [jax.ShapeDtypeStruct(s, BF16) for s in shapes]
    body = functools.partial(_ssm_tables_kernel, gb=gb, tc=tc, tcs=tcs, unroll=gb)
    return args, in_specs, out_specs, out_shape, body, ng // gb


def _mix_bias(b_s, hd, tc):
    nh = b_s.shape[0]
    r = CHUNK // tc
    if tc == TC_PROMPT:
        bm = b_s.reshape(nh, r, tc).transpose(0, 2, 1).reshape(nh, CHUNK)
    else:
        bm = jnp.repeat(b_s[:, :tc], r, axis=1)
    return jnp.broadcast_to(bm[:, :, None], (nh, CHUNK, hd))


def _tiles(tc, nk):
    fold = max(1, LANE_TILE // nk)
    return NORM_ROWS // tc, min(BRANCH_A_ROWS, tc * nk) // tc, fold


def kernel(x_prompt, x_sample, state_ssm_re, state_ssm_im, g_norm, w_in, g_v, w_s, b_s,
           a_re, a_im, log_dt, b_re, b_im, c_re, c_im, d_skip, w_glu, b_glu, w_out, g_final):
    depth = g_norm.shape[0]
    assert depth == 1, "single-layer step"
    bsz, seq, d = x_prompt.shape
    dbsz, dseq, _ = x_sample.shape
    nh = w_s.shape[1]
    ng, npst = a_re.shape[1:]
    cb = ng * GROUP
    ca = (w_in.shape[2] - 2 * cb) // 3
    hd = ca // nh
    assert seq // TC_PROMPT == LANES and dbsz == LANES and CHUNK % dseq == 0
    assert dseq & (dseq - 1) == 0 and dseq <= TC_PROMPT, "sample chunk must be a power of two"
    assert ca == cb == d

    l = 0
    gv = g_v[l].reshape(nh, 1, hd)
    rowtab, lt, scan_rows = _discretise(a_re[l], a_im[l], log_dt[l])

    assert bsz % 2 == 0
    tcs = (TC_PROMPT, dseq)
    nks = (bsz * seq // TC_PROMPT, dbsz)
    xs = (x_prompt.reshape(bsz * seq, d), x_sample.reshape(dbsz * dseq, d))
    tiles = [_tiles(tc, nk) for tc, nk in zip(tcs, nks)]
    tables_job = _ssm_tables_job(rowtab, b_re[l], b_im[l], c_re[l], c_im[l], dseq,
                                 ng // (nks[0] // tiles[0][0]))
    xns, out_as, v_rows = [], [], []
    tabs = w_bt = None
    for s, (x2d, tc, (n_norm, n_a, _)) in enumerate(zip(xs, tcs, tiles)):
        xn, made = _norm_permute(x2d, g_norm[l], tc, n_norm, tables_job if s == 0 else None)
        tabs = made if s == 0 else tabs
        out_a, v, made_bt = _branch_a(xn, w_in[l], gv, w_s[l], _mix_bias(b_s[l], hd, tc), n_a,
                                      tc * n_a // PROJ_ROWS, s == 1, s == 0)
        w_bt = made_bt if s == 0 else w_bt
        xns.append(xn)
        out_as.append(out_a)
        v_rows.append(v)
    mp, q, mp_s, q_s = tabs
    folds = [t[2] for t in tiles]
    slabs = [xn.reshape(tc // f, f * nk, d) for xn, tc, nk, f in zip(xns, tcs, nks, folds)]
    zbs, (wg_t, wo_bt) = _proj_b(w_bt, slabs, ((w_glu[l], cb, 0), (w_out[l], cb, 1)))
    dsk = jnp.broadcast_to(d_skip[l][:, None], (cb, LANES))
    h0 = (state_ssm_re[l].transpose(1, 2, 0), state_ssm_im[l].transpose(1, 2, 0))
    y_p_t, hf_p = _ssm(zbs[0], mp, q, scan_rows, dsk, None, SSM_GROUPS_PER_STEP, folds[0])
    y_s_t, hf_s = _ssm(zbs[1], mp_s, q_s, lt, dsk, h0, SSM_GROUPS_PER_STEP_1CHUNK, folds[1])
    bg = jnp.broadcast_to(b_glu[l][:, None], (cb, LANE_TILE))
    rs, wo_a = _glu_out((y_p_t, y_s_t), zbs, wg_t, bg, wo_bt, (w_out[l], ca))
    y_p, y_s = [_out_norm(x2d, out_a, r.reshape(tc, nk, d), wo_a, g_final, OUT_ROWS // tc)
                for x2d, out_a, r, tc, nk in zip(xs, out_as, rs, tcs, nks)]
    v_s = v_rows[1]

    hf_p = hf_p[0].reshape(ng, bsz // 2, 2, 2, npst)
    re_p = hf_p[:, :, 0].reshape(ng, bsz, npst).transpose(1, 0, 2)[None]
    im_p = hf_p[:, :, 1].reshape(ng, bsz, npst).transpose(1, 0, 2)[None]
    re_s = hf_s[0].transpose(2, 0, 1)[None]
    im_s = hf_s[1].transpose(2, 0, 1)[None]
    v_out = v_s.transpose(1, 0, 2)[None]

    dt_p, dt_s = x_prompt.dtype, x_sample.dtype
    return (y_p.reshape(bsz, seq, d).astype(dt_p), y_s.reshape(dbsz, dseq, d).astype(dt_s),
            re_p.astype(dt_p), im_p.astype(dt_p), re_s.astype(dt_s), im_s.astype(dt_s),
            v_out.astype(dt_s))
```

```python
import functools

import jax
import jax.numpy as jnp
from jax import lax
from jax.experimental import pallas as pl
from jax.experimental.pallas import tpu as pltpu

EPS = 1e-6
LANES = 128
CHUNK = 128
GROUP = 16
TC_PROMPT = 16
ROW_CHUNK = 512
LANE_TILE = 256
PROJ_ROWS = 512
BRANCH_A_ROWS = 2048
NORM_ROWS = 512
OUT_ROWS = 512
SSM_GROUPS_PER_STEP = 8
SSM_GROUPS_PER_STEP_1CHUNK = 32
SCAN_STEPS = 7
SCAN_ROWS = 8
GROUPS_PER_TRIP = 8
VMEM_LIMIT = 56 * 1024 * 1024

F32 = jnp.float32
BF16 = jnp.bfloat16


def _params(*sem):
    return pltpu.CompilerParams(dimension_semantics=sem, vmem_limit_bytes=VMEM_LIMIT)


def _norm_permute_kernel(x_ref, g_ref, *refs, tc, n, n_side_in, side_body):
    o_ref = refs[n_side_in]
    x = x_ref[...]
    ms = jnp.mean(x * x, axis=-1, keepdims=True)
    xn = (x * lax.rsqrt(ms + EPS) * g_ref[...]).astype(BF16)
    rows = n * tc
    dst = lax.broadcasted_iota(jnp.int32, (rows, rows), 0)
    src = lax.broadcasted_iota(jnp.int32, (rows, rows), 1)
    sel = (src == tc * (dst % n) + dst // n).astype(BF16)
    o_ref[...] = jnp.dot(sel, xn, preferred_element_type=F32).astype(o_ref.dtype).reshape(o_ref.shape)
    if side_body is not None:
        side_body(*refs[:n_side_in], *refs[n_side_in + 1:])


def _norm_permute(x2d, g, tc, n, side_job=None):
    t, d = x2d.shape
    nk = t // tc
    s_args, s_in, s_out, s_shape, s_body, s_steps = side_job or ((), [], [], [], None, nk // n)
    assert s_steps == nk // n, "the side job needs one block per grid step"
    res = pl.pallas_call(
        functools.partial(_norm_permute_kernel, tc=tc, n=n, n_side_in=len(s_args), side_body=s_body),
        grid=(nk // n,),
        in_specs=[pl.BlockSpec((n * tc, d), lambda i: (i, 0)),
                  pl.BlockSpec((1, d), lambda i: (0, 0))] + list(s_in),
        out_specs=[pl.BlockSpec((tc, n, d), lambda i: (0, i, 0))] + list(s_out),
        out_shape=[jax.ShapeDtypeStruct((tc, nk, d), BF16)] + list(s_shape),
        compiler_params=_params("parallel"),
    )(x2d, g.reshape(1, d), *s_args)
    return res[0], tuple(res[1:])


def _branch_a_kernel(*refs, tc, n, r, hd, n_sub, has_v, has_side):
    refs = list(refs)
    xn_ref, wu_ref, wv_ref, wg_ref, gv_ref, ws_ref, bmix_ref = refs[:7]
    del refs[:7]
    side_in = [refs.pop(0)] if has_side else []
    oa_ref = refs.pop(0)
    v_refs = [refs.pop(0)] if has_v else []
    side_out = [refs.pop(0)] if has_side else []
    w_ref, wm_ref = refs
    d = xn_ref.shape[-1]

    @pl.when(pl.program_id(1) == 0)
    def _():
        w_ref[:, :hd] = wu_ref[...].astype(BF16)
        w_ref[:, hd:2 * hd] = wv_ref[...].astype(BF16)
        w_ref[:, 2 * hd:] = wg_ref[...].astype(BF16)
        row = lax.broadcasted_iota(jnp.int32, (CHUNK, CHUNK), 0)
        col = lax.broadcasted_iota(jnp.int32, (CHUNK, CHUNK), 1)
        w_tril = jnp.where(row >= col, ws_ref[...], 0.0).astype(BF16)
        if tc == TC_PROMPT:
            time_of_row = tc * (row % r) + row // r
            same_seq = None
        else:
            time_of_row = row // r
            same_seq = (row % r) == (col % r)
        sel = (col == time_of_row).astype(BF16)
        sw = jnp.dot(sel, w_tril, preferred_element_type=F32).astype(BF16)
        wm = lax.dot_general(sw, sel, (((1,), (1,)), ((), ())), preferred_element_type=F32)
        if same_seq is not None:
            wm = jnp.where(same_seq, wm, 0.0)
        wm_ref[...] = wm.astype(BF16)

    ns = n // n_sub
    zs = [jnp.dot(xn_ref[:, k * ns:(k + 1) * ns, :].reshape(tc * ns, d), w_ref[...],
                  preferred_element_type=F32) for k in range(n_sub)]
    wm = wm_ref[...]
    bm = bmix_ref[...]
    for k, z in enumerate(zs):
        k0 = k * ns
        u = jax.nn.gelu(z[:, :hd])
        v = jax.nn.gelu(z[:, hd:2 * hd])
        ga = z[:, 2 * hd:]
        ms = jnp.mean(v * v, axis=-1, keepdims=True)
        vn3 = (v * lax.rsqrt(ms + EPS) * gv_ref[...]).reshape(tc, ns, hd)
        pre3 = (u * jax.nn.silu(ga)).reshape(tc, ns, hd)
        if v_refs:
            v_refs[0][:, k0:k0 + ns, :] = vn3
        for c in range(ns // r):
            vg = vn3[:, c * r:(c + 1) * r, :].reshape(CHUNK, hd)
            mixed = jnp.dot(wm, vg.astype(BF16), preferred_element_type=F32) + bm
            og = pre3[:, c * r:(c + 1) * r, :].reshape(CHUNK, hd) * mixed
            oa_ref[:, k0 + c * r:k0 + (c + 1) * r, :] = og.reshape(tc, r, hd).astype(oa_ref.dtype)
    _transpose_side_blocks(side_in, side_out)


def _branch_a(xn, w_in, gv, w_s, bmix, n, n_sub, want_v, transpose_b):
    tc, nk, d = xn.shape
    nh, _, hd = gv.shape
    r = CHUNK // tc
    nt = nk // n
    out_map = lambda h, i: (0, i, h)
    in_specs = [pl.BlockSpec((tc, n, d), lambda h, i: (0, i, 0)),
                pl.BlockSpec((d, hd), lambda h, i: (0, h)),
                pl.BlockSpec((d, hd), lambda h, i: (0, nh + h)),
                pl.BlockSpec((d, hd), lambda h, i: (0, 2 * nh + h)),
                pl.BlockSpec((None, 1, hd), lambda h, i: (h, 0, 0)),
                pl.BlockSpec((None, CHUNK, CHUNK), lambda h, i: (h, 0, 0)),
                pl.BlockSpec((None, CHUNK, hd), lambda h, i: (h, 0, 0))]
    args = [xn, w_in, w_in, w_in, gv, w_s, bmix]
    out_shape = [jax.ShapeDtypeStruct((tc, nk, nh * hd), BF16)]
    out_specs = [pl.BlockSpec((tc, n, hd), out_map)]
    if want_v:
        out_shape.append(jax.ShapeDtypeStruct((tc, nk, nh * hd), F32))
        out_specs.append(pl.BlockSpec((tc, n, hd), out_map))
    if transpose_b:
        col0 = 3 * nh * hd // LANES
        n_cols = w_in.shape[1] - 3 * nh * hd
        assert nh * nt * LANES == n_cols, "one 128-column block per grid step"
        in_specs.append(pl.BlockSpec((d, LANES), lambda h, i: (0, col0 + h * nt + i)))
        args.append(w_in)
        out_shape.append(jax.ShapeDtypeStruct((n_cols, d), BF16))
        out_specs.append(pl.BlockSpec((LANES, d), lambda h, i: (h * nt + i, 0)))
    res = list(pl.pallas_call(
        functools.partial(_branch_a_kernel, tc=tc, n=n, r=r, hd=hd, n_sub=n_sub,
                          has_v=want_v, has_side=transpose_b),
        grid=(nh, nt),
        in_specs=in_specs,
        out_specs=out_specs,
        out_shape=out_shape,
        scratch_shapes=[pltpu.VMEM((d, 3 * hd), BF16), pltpu.VMEM((CHUNK, CHUNK), BF16)],
        compiler_params=_params("parallel", "arbitrary"),
    )(*args))
    out_a = res.pop(0)
    v_rows = res.pop(0) if want_v else None
    w_bt = res.pop(0) if transpose_b else None
    return out_a, v_rows, w_bt


def _transpose_side_blocks(in_refs, out_refs):
    for w_ref, o_ref in zip(in_refs, out_refs):
        o_ref[...] = w_ref[...].T.astype(o_ref.dtype)


def _stream_maps(counts):
    starts = [sum(counts[:s]) for s in range(len(counts))]
    maps = [lambda j, st=st, c=c: jnp.clip(j - st, 0, c - 1) for st, c in zip(starts, counts)]
    owns = [lambda j, st=st, c=c: jnp.logical_and(j >= st, j < st + c) for st, c in zip(starts, counts)]
    return maps, owns


def _proj_b_kernel(w_ref, *refs, counts, n_side):
    ns = len(counts)
    xn_refs, side_in = refs[:ns], refs[ns:ns + n_side]
    o_refs, side_out = refs[ns + n_side:2 * ns + n_side], refs[2 * ns + n_side:]
    _, owns = _stream_maps(counts)
    j = pl.program_id(0)
    for s, (xn_ref, o_ref) in enumerate(zip(xn_refs, o_refs)):
        @pl.when(owns[s](j))
        def _(s=s, xn_ref=xn_ref, o_ref=o_ref):
            if s == 0:
                _transpose_side_blocks(side_in, side_out)
            for t in range(xn_ref.shape[0] // LANE_TILE):
                ls = slice(t * LANE_TILE, (t + 1) * LANE_TILE)
                xn = xn_ref[ls, :]
                for m in range(w_ref.shape[0] // ROW_CHUNK):
                    sl = slice(m * ROW_CHUNK, (m + 1) * ROW_CHUNK)
                    o_ref[sl, ls] = lax.dot_general(
                        w_ref[sl, :], xn, (((1,), (1,)), ((), ())),
                        preferred_element_type=F32).astype(o_ref.dtype)


def _proj_b(w_bt, xns, side=()):
    rows, d = w_bt.shape
    counts = [x.shape[0] for x in xns]
    maps, _ = _stream_maps(counts)
    in_specs = [pl.BlockSpec((rows, d), lambda j: (0, 0), pipeline_mode=pl.Buffered(1))]
    in_specs += [pl.BlockSpec((None, x.shape[1], d), lambda j, m=m: (m(j), 0, 0))
                 for x, m in zip(xns, maps)]
    out_specs = [pl.BlockSpec((None, rows, x.shape[1]), lambda j, m=m: (m(j), 0, 0))
                 for x, m in zip(xns, maps)]
    out_shape = [jax.ShapeDtypeStruct((x.shape[0], rows, x.shape[1]), BF16) for x in xns]
    for w, n_rows, row_block in side:
        n_cols = w.shape[1]
        assert counts[0] * LANES == n_cols, "one 128-column block per step of the first stream set"
        in_specs.append(pl.BlockSpec((n_rows, LANES), lambda j, rb=row_block: (rb, maps[0](j))))
        out_specs.append(pl.BlockSpec((LANES, n_rows), lambda j: (maps[0](j), 0)))
        out_shape.append(jax.ShapeDtypeStruct((n_cols, n_rows), BF16))
    res = pl.pallas_call(
        functools.partial(_proj_b_kernel, counts=counts, n_side=len(side)),
        grid=(sum(counts),),
        in_specs=in_specs,
        out_specs=out_specs,
        out_shape=out_shape,
        compiler_params=_params("arbitrary"),
    )(w_bt, *xns, *[w for w, _, _ in side])
    return tuple(res[:len(xns)]), tuple(res[len(xns):])


def _shift_rows(x, d):
    if d % 8 == 0:
        return jnp.concatenate([jnp.zeros((d, x.shape[1]), x.dtype), x[:-d]], axis=0)
    row = lax.broadcasted_iota(jnp.int32, x.shape, 0)
    return jnp.where(row >= d, pltpu.roll(x, d, 0), 0.0)


def _ssm_kernel(x_ref, mp_ref, q_ref, ct_ref, dsk_ref, *rest, tc, gb, nk, fold, has_h0):
    if has_h0:
        h0r_ref, h0i_ref, y_ref, hfr_ref, hfi_ref = rest
    else:
        y_ref, hf_ref = rest
    rows = tc * GROUP
    half = q_ref.shape[2] // 2

    def one_group(g):
        r0 = g * GROUP if isinstance(g, int) else pl.multiple_of(g * GROUP, GROUP)
        if fold == 1:
            xg = x_ref[:, pl.ds(r0, GROUP), :]
        else:
            xg = jnp.stack([x_ref[j // fold, pl.ds(r0, GROUP), (j % fold) * nk:(j % fold + 1) * nk]
                            for j in range(tc)])
        res = jnp.dot(mp_ref[g], xg.reshape(rows, nk), preferred_element_type=F32)
        y_local = res[:rows]
        s_re = res[rows:rows + half]
        s_im = res[rows + half:]
        ct = ct_ref[g]
        if has_h0:
            h_re, h_im = h0r_ref[g], h0i_ref[g]
            a_re, a_im = ct[:, 0:1], ct[:, 1:2]
            hfr_ref[g] = a_re * h_re - a_im * h_im + s_re
            hfi_ref[g] = a_re * h_im + a_im * h_re + s_im
            h_in = jnp.concatenate([h_re, h_im], axis=0)
        else:
            tiles = []
            for pair in range(nk // (2 * LANES)):
                lo, mid, hi = 2 * pair * LANES, (2 * pair + 1) * LANES, (2 * pair + 2) * LANES
                xr = jnp.concatenate([s_re[:, lo:mid], s_re[:, mid:hi]], axis=0).T
                xi = jnp.concatenate([s_im[:, lo:mid], s_im[:, mid:hi]], axis=0).T
                for s in range(SCAN_STEPS):
                    cr, ci = ct[s:s + 1], ct[SCAN_ROWS + s:SCAN_ROWS + s + 1]
                    rr, ri = _shift_rows(xr, 1 << s), _shift_rows(xi, 1 << s)
                    xr, xi = xr + cr * rr - ci * ri, xi + cr * ri + ci * rr
                hf_ref[g, 2 * pair:2 * pair + 1, :] = xr[LANES - 1:]
                hf_ref[g, 2 * pair + 1:2 * pair + 2, :] = xi[LANES - 1:]
                hr_t = _shift_rows(xr, 1).T
                hi_t = _shift_rows(xi, 1).T
                tiles.append(jnp.concatenate([hr_t[:half], hi_t[:half]], axis=0))
                tiles.append(jnp.concatenate([hr_t[half:], hi_t[half:]], axis=0))
            h_in = jnp.concatenate(tiles, axis=1)
        y = y_local + jnp.dot(q_ref[g], h_in.astype(BF16), preferred_element_type=F32)
        dsk = jnp.concatenate([dsk_ref[pl.ds(r0, GROUP), :]] * (nk // LANES), axis=1)
        y3 = y.reshape(tc, GROUP, nk) + dsk[None] * xg.astype(F32)
        ya = jax.nn.gelu(y3).astype(y_ref.dtype)
        if fold == 1:
            y_ref[:, pl.ds(r0, GROUP), :] = ya
        else:
            for j in range(tc):
                y_ref[j // fold, pl.ds(r0, GROUP), (j % fold) * nk:(j % fold + 1) * nk] = ya[j]

    def body(gi, carry):
        for k in range(GROUPS_PER_TRIP):
            one_group(gi * GROUPS_PER_TRIP + k)
        return carry

    if gb == GROUPS_PER_TRIP:
        body(0, 0)
    else:
        lax.fori_loop(0, gb // GROUPS_PER_TRIP, body, 0)


def _ssm(zb, mp, q, ct, dsk, h0, gb, fold):
    tcf, _, nkf = zb.shape
    tc, nk = tcf * fold, nkf // fold
    ng, mrows, rows = mp.shape
    p2 = mrows - rows
    cb = ng * GROUP
    has_h0 = h0 is not None
    in_specs = [pl.BlockSpec((tcf, gb * GROUP, nkf), lambda i: (0, i, 0)),
                pl.BlockSpec((gb, mrows, rows), lambda i: (i, 0, 0)),
                pl.BlockSpec((gb, rows, p2), lambda i: (i, 0, 0)),
                pl.BlockSpec((gb,) + ct.shape[1:], lambda i: (i, 0, 0)),
                pl.BlockSpec((gb * GROUP, LANES), lambda i: (i, 0))]
    args = [zb, mp, q, ct, dsk]
    if has_h0:
        state_spec = pl.BlockSpec((gb, p2 // 2, nk), lambda i: (i, 0, 0))
        in_specs += [state_spec, state_spec]
        args += list(h0)
        hf_specs = [state_spec, state_spec]
        hf_shapes = [jax.ShapeDtypeStruct((ng, p2 // 2, nk), F32)] * 2
    else:
        hf_specs = [pl.BlockSpec((gb, nk // LANES, LANES), lambda i: (i, 0, 0))]
        hf_shapes = [jax.ShapeDtypeStruct((ng, nk // LANES, LANES), F32)]
    res = pl.pallas_call(
        functools.partial(_ssm_kernel, tc=tc, gb=gb, nk=nk, fold=fold, has_h0=has_h0),
        grid=(ng // gb,),
        in_specs=in_specs,
        out_specs=[pl.BlockSpec((tcf, gb * GROUP, nkf), lambda i: (0, i, 0))] + hf_specs,
        out_shape=[jax.ShapeDtypeStruct((tcf, cb, nkf), BF16)] + hf_shapes,
        compiler_params=_params("parallel"),
    )(*args)
    return res[0], tuple(res[1:])


def _glu_out_kernel(wg_ref, bg_ref, wo_ref, *refs, counts, has_cast, n_tiles):
    ns = len(counts)
    refs = list(refs)
    y_refs, gate_refs = refs[:ns], refs[ns:2 * ns]
    del refs[:2 * ns]
    cast_in = refs.pop(0) if has_cast else None
    r_refs = refs[:ns]
    del refs[:ns]
    cast_out = refs.pop(0) if has_cast else None
    ob_refs = refs
    _, owns = _stream_maps(counts)
    j = pl.program_id(0)
    for s in range(ns):
        @pl.when(owns[s](j))
        def _(s=s, y_ref=y_refs[s], gate_ref=gate_refs[s], r_ref=r_refs[s]):
            if s == 0 and has_cast:
                cast_out[...] = cast_in[...].astype(cast_out.dtype)
            for t, ob_ref in enumerate(ob_refs[:n_tiles[s]]):
                ls = slice(t * LANE_TILE, (t + 1) * LANE_TILE)
                y = y_ref[:, ls]
                for m in range(wg_ref.shape[0] // ROW_CHUNK):
                    sl = slice(m * ROW_CHUNK, (m + 1) * ROW_CHUNK)
                    g = jnp.dot(wg_ref[sl, :], y, preferred_element_type=F32) + bg_ref[sl, :]
                    ob = (y_ref[sl, ls].astype(F32) * jax.nn.sigmoid(g)
                          * jax.nn.silu(gate_ref[sl, ls].astype(F32)))
                    ob_ref[sl, :] = ob.astype(ob_ref.dtype)
            for t, ob_ref in enumerate(ob_refs[:n_tiles[s]]):
                ob = ob_ref[...]
                for m in range(wo_ref.shape[0] // ROW_CHUNK):
                    sl = slice(m * ROW_CHUNK, (m + 1) * ROW_CHUNK)
                    r_ref[t * LANE_TILE:(t + 1) * LANE_TILE, sl] = jnp.dot(
                        wo_ref[sl, :], ob, preferred_element_type=F32).T


def _glu_out(y_ts, zbs, wg_t, bg, wo_t, cast_rows=None):
    cb = wg_t.shape[0]
    d = wo_t.shape[0]
    counts = [y.shape[0] for y in y_ts]
    maps, _ = _stream_maps(counts)
    const = lambda j: (0, 0)
    in_specs = [pl.BlockSpec((cb, cb), const, pipeline_mode=pl.Buffered(1)),
                pl.BlockSpec((cb, LANE_TILE), const, pipeline_mode=pl.Buffered(1)),
                pl.BlockSpec((d, cb), const, pipeline_mode=pl.Buffered(1))]
    in_specs += [pl.BlockSpec((None, cb, y.shape[2]), lambda j, m=m: (m(j), 0, 0)) for y, m in zip(y_ts, maps)]
    in_specs += [pl.BlockSpec((None, cb, y.shape[2]), lambda j, m=m: (m(j), 1, 0)) for y, m in zip(y_ts, maps)]
    args = [wg_t, bg, wo_t, *y_ts, *zbs]
    out_specs = [pl.BlockSpec((None, y.shape[2], d), lambda j, m=m: (m(j), 0, 0)) for y, m in zip(y_ts, maps)]
    out_shape = [jax.ShapeDtypeStruct((y.shape[0], y.shape[2], d), F32) for y in y_ts]
    if cast_rows is not None:
        w, n_rows = cast_rows
        assert counts[0] * LANES == n_rows, "one 128-row block per step of the first stream set"
        blk = pl.BlockSpec((LANES, w.shape[1]), lambda j: (maps[0](j), 0))
        in_specs.append(blk)
        args.append(w)
        out_specs.append(blk)
        out_shape.append(jax.ShapeDtypeStruct((n_rows, w.shape[1]), BF16))
    n_tiles = [y.shape[2] // LANE_TILE for y in y_ts]
    res = pl.pallas_call(
        functools.partial(_glu_out_kernel, counts=counts, has_cast=cast_rows is not None,
                          n_tiles=n_tiles),
        grid=(sum(counts),),
        in_specs=in_specs,
        out_specs=out_specs,
        out_shape=out_shape,
        scratch_shapes=[pltpu.VMEM((cb, LANE_TILE), BF16)] * max(n_tiles),
        compiler_params=_params("arbitrary"),
    )(*args)
    return tuple(res[:len(y_ts)]), (res[len(y_ts)] if cast_rows is not None else None)


def _out_norm_kernel(x_ref, oa_ref, r_ref, wo_ref, g_ref, y_ref, ms_ref, *, tc, n):
    ca = oa_ref.shape[-1]
    d = x_ref.shape[-1]
    nc = d // LANES
    p = jnp.dot(oa_ref[...].reshape(tc * n, ca), wo_ref[...], preferred_element_type=F32)
    for j in range(tc):
        s = p[j * n:(j + 1) * n] + r_ref[j]
        for c in range(nc):
            ms_ref[c, pl.ds(j, n, stride=tc), :] = s[:, c * LANES:(c + 1) * LANES]
    acc = x_ref[...] + jnp.concatenate([ms_ref[c] for c in range(nc)], axis=1)
    ms = jnp.mean(acc * acc, axis=-1, keepdims=True)
    y_ref[...] = acc * lax.rsqrt(ms + EPS) * g_ref[...]


def _out_norm(x2d, oa, r, wo_a, gf, n):
    tc, nk, ca = oa.shape
    t, d = x2d.shape
    return pl.pallas_call(
        functools.partial(_out_norm_kernel, tc=tc, n=n),
        grid=(nk // n,),
        in_specs=[pl.BlockSpec((n * tc, d), lambda i: (i, 0)),
                  pl.BlockSpec((tc, n, ca), lambda i: (0, i, 0)),
                  pl.BlockSpec((tc, n, d), lambda i: (0, i, 0)),
                  pl.BlockSpec((ca, d), lambda i: (0, 0), pipeline_mode=pl.Buffered(1)),
                  pl.BlockSpec((1, d), lambda i: (0, 0))],
        out_specs=pl.BlockSpec((n * tc, d), lambda i: (i, 0)),
        out_shape=jax.ShapeDtypeStruct((t, d), F32),
        scratch_shapes=[pltpu.VMEM((d // LANES, n * tc, LANES), F32)],
        compiler_params=_params("parallel"),
    )(x2d, oa, r, wo_a, gf.reshape(1, d))


def _discretise(a_re, a_im, log_dt, tc_sample):
    dt = jnp.exp(log_dt)[:, None]
    x_re, x_im = a_re * dt, a_im * dt
    exps = [1.0] + [float(TC_PROMPT << s) for s in range(SCAN_STEPS)] + [float(tc_sample)]
    n = jnp.asarray(exps, F32)[None, :, None]
    mag = jnp.exp(n * x_re[:, None, :])
    ang = n * x_im[:, None, :]
    pw_r, pw_i = mag * jnp.cos(ang), mag * jnp.sin(ang)
    lr, li = pw_r[:, 0], pw_i[:, 0]
    den = a_re * a_re + a_im * a_im
    nr, ni = lr - 1.0, li
    qr, qi = (nr * a_re + ni * a_im) / den, (ni * a_re - nr * a_im) / den
    cat = lambda a, b: jnp.concatenate([a, b], axis=-1)
    rowtab = jnp.stack([cat(lr, lr), cat(li, li), cat(qr, qi), cat(-qi, qr)], axis=1)
    a_cols = jnp.stack([pw_r[:, -1], pw_i[:, -1]], axis=-1)
    sc_r, sc_i = pw_r[:, 1:1 + SCAN_STEPS], pw_i[:, 1:1 + SCAN_STEPS]
    zrow = jnp.zeros_like(cat(sc_r, sc_r)[:, :SCAN_ROWS - SCAN_STEPS])
    scan_rows = jnp.concatenate([cat(sc_r, sc_r), zrow, cat(sc_i, sc_i), zrow], axis=1)
    return rowtab, a_cols, scan_rows


def _ssm_tables_kernel(rt_ref, btr_ref, bti_ref, c2r_ref, c2i_ref, c2n_ref,
                       mp_ref, q_ref, mps_ref, qs_ref, *, gb, tc, tcs, unroll):
    rows = tc * GROUP
    rows_s = tcs * GROUP
    half = rt_ref.shape[2] // 2
    lane_k = lax.broadcasted_iota(jnp.int32, (GROUP, rows), 1)
    lane_q = lax.broadcasted_iota(jnp.int32, (1, 2 * half), 1)

    def one_group(g):
        rt = rt_ref[g]
        l_r, l_i, qu, qv = rt[0:1], rt[1:2], rt[2:3], rt[3:4]
        btr, bti = btr_ref[g], bti_ref[g]
        bb = btr * qu + bti * qv
        bs = btr * qv - bti * qu
        c2r, c2i = c2r_ref[g], c2i_ref[g]
        cur_r = jnp.ones((1, 2 * half), F32)
        cur_i = jnp.zeros((1, 2 * half), F32)
        et_blocks = []
        for n in range(tc):
            et_blocks.append(cur_r * bb + cur_i * bs)
            cur_r, cur_i = cur_r * l_r - cur_i * l_i, cur_r * l_i + cur_i * l_r
            u = jnp.where(lane_q < half, cur_r, -cur_i)
            v = jnp.where(lane_q < half, -cur_i, -cur_r)
            blk = (c2r * u + c2i * v).astype(q_ref.dtype)
            q_ref[g, n * GROUP:(n + 1) * GROUP, :] = blk
            if n < tcs:
                qs_ref[g, n * GROUP:(n + 1) * GROUP, :] = blk
        p = jnp.concatenate(et_blocks[::-1], axis=0).T
        mp_ref[g, rows:, :] = p.astype(mp_ref.dtype)
        mps_ref[g, rows_s:, :] = p[:, rows - rows_s:].astype(mps_ref.dtype)
        krow = jnp.dot(c2n_ref[g], p, precision=lax.Precision.HIGHEST, preferred_element_type=F32)
        for i in range(tc):
            sh = (tc - 1 - i) * GROUP
            blk = krow if sh == 0 else pltpu.roll(krow, rows - sh, 1)
            blk = jnp.where(lane_k < (i + 1) * GROUP, blk, 0.0)
            mp_ref[g, i * GROUP:(i + 1) * GROUP, :] = blk.astype(mp_ref.dtype)
            if i < tcs:
                mps_ref[g, i * GROUP:(i + 1) * GROUP, :] = blk[:, :rows_s].astype(mps_ref.dtype)

    def body(gi, carry):
        for k in range(unroll):
            one_group(gi * unroll + k)
        return carry

    if unroll == gb:
        body(0, 0)
    else:
        lax.fori_loop(0, gb // unroll, body, 0)


def _ssm_tables_job(rowtab, b_re, b_im, c_re, c_im, tcs, gb):
    ng, _, p2 = rowtab.shape
    tc = TC_PROMPT
    rows, rows_s = tc * GROUP, tcs * GROUP
    dup = lambda a: jnp.concatenate([a, a], axis=-1)
    args = (rowtab, dup(b_re.transpose(0, 2, 1)), dup(b_im.transpose(0, 2, 1)),
            dup(c_re), dup(c_im), jnp.concatenate([c_re, -c_im], axis=-1))
    shapes = [(ng, rows + p2, rows), (ng, rows, p2), (ng, rows_s + p2, rows_s), (ng, rows_s, p2)]
    in_specs = [pl.BlockSpec((gb,) + a.shape[1:], lambda i: (i, 0, 0)) for a in args]
    out_specs = [pl.BlockSpec((gb,) + s[1:], lambda i: (i, 0, 0)) for s in shapes]
    out_shape = [jax.ShapeDtypeStruct(s, BF16) for s in shapes]
    body = functools.partial(_ssm_tables_kernel, gb=gb, tc=tc, tcs=tcs, unroll=gb)
    return args, in_specs, out_specs, out_shape, body, ng // gb


def _mix_bias(b_s, hd, tc):
    nh = b_s.shape[0]
    r = CHUNK // tc
    if tc == TC_PROMPT:
        bm = b_s.reshape(nh, r, tc).transpose(0, 2, 1).reshape(nh, CHUNK)
    else:
        bm = jnp.repeat(b_s[:, :tc], r, axis=1)
    return jnp.broadcast_to(bm[:, :, None], (nh, CHUNK, hd))


def _tiles(tc, nk):
    fold = max(1, LANE_TILE // nk)
    return NORM_ROWS // tc, min(BRANCH_A_ROWS, tc * nk) // tc, fold


def kernel(x_prompt, x_sample, state_ssm_re, state_ssm_im, g_norm, w_in, g_v, w_s, b_s,
           a_re, a_im, log_dt, b_re, b_im, c_re, c_im, d_skip, w_glu, b_glu, w_out, g_final):
    depth = g_norm.shape[0]
    assert depth == 1, "single-layer step"
    bsz, seq, d = x_prompt.shape
    dbsz, dseq, _ = x_sample.shape
    nh = w_s.shape[1]
    ng, npst = a_re.shape[1:]
    cb = ng * GROUP
    ca = (w_in.shape[2] - 2 * cb) // 3
    hd = ca // nh
    assert seq // TC_PROMPT == LANES and dbsz == LANES and CHUNK % dseq == 0
    assert dseq <= TC_PROMPT
    assert ca == cb == d

    l = 0
    gv = g_v[l].reshape(nh, 1, hd)
    rowtab, a_cols, scan_rows = _discretise(a_re[l], a_im[l], log_dt[l], dseq)

    assert bsz % 2 == 0
    tcs = (TC_PROMPT, dseq)
    nks = (bsz * seq // TC_PROMPT, dbsz)
    xs = (x_prompt.reshape(bsz * seq, d), x_sample.reshape(dbsz * dseq, d))
    tiles = [_tiles(tc, nk) for tc, nk in zip(tcs, nks)]
    tables_job = _ssm_tables_job(rowtab, b_re[l], b_im[l], c_re[l], c_im[l], dseq,
                                 ng // (nks[0] // tiles[0][0]))
    xns, out_as, v_rows = [], [], []
    tabs = w_bt = None
    for s, (x2d, tc, (n_norm, n_a, _)) in enumerate(zip(xs, tcs, tiles)):
        xn, made = _norm_permute(x2d, g_norm[l], tc, n_norm, tables_job if s == 0 else None)
        tabs = made if s == 0 else tabs
        out_a, v, made_bt = _branch_a(xn, w_in[l], gv, w_s[l], _mix_bias(b_s[l], hd, tc), n_a,
                                      tc * n_a // PROJ_ROWS, s == 1, s == 0)
        w_bt = made_bt if s == 0 else w_bt
        xns.append(xn)
        out_as.append(out_a)
        v_rows.append(v)
    mp, q, mp_s, q_s = tabs
    folds = [t[2] for t in tiles]
    slabs = [xn.reshape(tc // f, f * nk, d) for xn, tc, nk, f in zip(xns, tcs, nks, folds)]
    zbs, (wg_t, wo_bt) = _proj_b(w_bt, slabs, ((w_glu[l], cb, 0), (w_out[l], cb, 1)))
    dsk = jnp.broadcast_to(d_skip[l][:, None], (cb, LANES))
    h0 = (state_ssm_re[l].transpose(1, 2, 0), state_ssm_im[l].transpose(1, 2, 0))
    y_p_t, hf_p = _ssm(zbs[0], mp, q, scan_rows, dsk, None, SSM_GROUPS_PER_STEP, folds[0])
    y_s_t, hf_s = _ssm(zbs[1], mp_s, q_s, a_cols, dsk, h0, SSM_GROUPS_PER_STEP_1CHUNK, folds[1])
    bg = jnp.broadcast_to(b_glu[l][:, None], (cb, LANE_TILE))
    rs, wo_a = _glu_out((y_p_t, y_s_t), zbs, wg_t, bg, wo_bt, (w_out[l], ca))
    y_p, y_s = [_out_norm(x2d, out_a, r.reshape(tc, nk, d), wo_a, g_final, OUT_ROWS // tc)
                for x2d, out_a, r, tc, nk in zip(xs, out_as, rs, tcs, nks)]
    v_s = v_rows[1]

    hf_p = hf_p[0].reshape(ng, bsz // 2, 2, 2, npst)
    re_p = hf_p[:, :, 0].reshape(ng, bsz, npst).transpose(1, 0, 2)[None]
    im_p = hf_p[:, :, 1].reshape(ng, bsz, npst).transpose(1, 0, 2)[None]
    re_s = hf_s[0].transpose(2, 0, 1)[None]
    im_s = hf_s[1].transpose(2, 0, 1)[None]
    v_out = v_s.transpose(1, 0, 2)[None]

    dt_p, dt_s = x_prompt.dtype, x_sample.dtype
    return (y_p.reshape(bsz, seq, d).astype(dt_p), y_s.reshape(dbsz, dseq, d).astype(dt_s),
            re_p.astype(dt_p), im_p.astype(dt_p), re_s.astype(dt_s), im_s.astype(dt_s),
            v_out.astype(dt_s))
```

```python
import functools

import jax
import jax.numpy as jnp
from jax import lax
from jax.experimental import pallas as pl
from jax.experimental.pallas import tpu as pltpu

EPS = 1e-6
LANES = 128
CHUNK = 128
GROUP = 16
TC_PROMPT = 16
ROW_CHUNK = 512
LANE_TILE = 256
PROJ_ROWS = 512
BRANCH_A_ROWS = 2048
NORM_ROWS = 512
OUT_ROWS = 512
OUT_NORM_INPUT_BUFFERS = 3
SSM_GROUPS_PER_STEP = 8
SSM_GROUPS_PER_STEP_1CHUNK = 32
POW_COLS = 16
SCAN_STEPS = 7
N_POW = 11
SCAN_ROWS = 8
GROUPS_PER_TRIP = 8
VMEM_LIMIT = 56 * 1024 * 1024

F32 = jnp.float32
BF16 = jnp.bfloat16


def _params(*sem):
    return pltpu.CompilerParams(dimension_semantics=sem, vmem_limit_bytes=VMEM_LIMIT)


def _norm_permute_kernel(x_ref, g_ref, *refs, tc, n, n_side_in, side_body):
    o_ref = refs[n_side_in]
    x = x_ref[...]
    ms = jnp.mean(x * x, axis=-1, keepdims=True)
    xn = (x * lax.rsqrt(ms + EPS) * g_ref[...]).astype(BF16)
    rows = n * tc
    dst = lax.broadcasted_iota(jnp.int32, (rows, rows), 0)
    src = lax.broadcasted_iota(jnp.int32, (rows, rows), 1)
    sel = (src == tc * (dst % n) + dst // n).astype(BF16)
    o_ref[...] = jnp.dot(sel, xn, preferred_element_type=F32).astype(o_ref.dtype).reshape(o_ref.shape)
    if side_body is not None:
        side_body(*refs[:n_side_in], *refs[n_side_in + 1:])


def _norm_permute(x2d, g, tc, n, side_job=None):
    t, d = x2d.shape
    nk = t // tc
    s_args, s_in, s_out, s_shape, s_body, s_steps = side_job or ((), [], [], [], None, nk // n)
    assert s_steps == nk // n, "the side job needs one block per grid step"
    res = pl.pallas_call(
        functools.partial(_norm_permute_kernel, tc=tc, n=n, n_side_in=len(s_args), side_body=s_body),
        grid=(nk // n,),
        in_specs=[pl.BlockSpec((n * tc, d), lambda i: (i, 0)),
                  pl.BlockSpec((1, d), lambda i: (0, 0))] + list(s_in),
        out_specs=[pl.BlockSpec((tc, n, d), lambda i: (0, i, 0))] + list(s_out),
        out_shape=[jax.ShapeDtypeStruct((tc, nk, d), BF16)] + list(s_shape),
        compiler_params=_params("parallel"),
    )(x2d, g.reshape(1, d), *s_args)
    return res[0], tuple(res[1:])


def _branch_a_kernel(*refs, tc, n, r, hd, n_sub, has_v, has_side):
    refs = list(refs)
    xn_ref, wu_ref, wv_ref, wg_ref, gv_ref, ws_ref, bmix_ref = refs[:7]
    del refs[:7]
    side_in = [refs.pop(0)] if has_side else []
    oa_ref = refs.pop(0)
    v_refs = [refs.pop(0)] if has_v else []
    side_out = [refs.pop(0)] if has_side else []
    w_ref, wm_ref = refs
    d = xn_ref.shape[-1]

    @pl.when(pl.program_id(1) == 0)
    def _():
        w_ref[:, :hd] = wu_ref[...].astype(BF16)
        w_ref[:, hd:2 * hd] = wv_ref[...].astype(BF16)
        w_ref[:, 2 * hd:] = wg_ref[...].astype(BF16)
        row = lax.broadcasted_iota(jnp.int32, (CHUNK, CHUNK), 0)
        col = lax.broadcasted_iota(jnp.int32, (CHUNK, CHUNK), 1)
        w_tril = jnp.where(row >= col, ws_ref[...], 0.0).astype(BF16)
        if tc == TC_PROMPT:
            time_of_row = tc * (row % r) + row // r
            same_seq = None
        else:
            time_of_row = row // r
            same_seq = (row % r) == (col % r)
        sel = (col == time_of_row).astype(BF16)
        sw = jnp.dot(sel, w_tril, preferred_element_type=F32).astype(BF16)
        wm = lax.dot_general(sw, sel, (((1,), (1,)), ((), ())), preferred_element_type=F32)
        if same_seq is not None:
            wm = jnp.where(same_seq, wm, 0.0)
        wm_ref[...] = wm.astype(BF16)

    ns = n // n_sub
    zs = [jnp.dot(xn_ref[:, k * ns:(k + 1) * ns, :].reshape(tc * ns, d), w_ref[...],
                  preferred_element_type=F32) for k in range(n_sub)]
    wm = wm_ref[...]
    bm = bmix_ref[...]
    for k, z in enumerate(zs):
        k0 = k * ns
        u = jax.nn.gelu(z[:, :hd])
        v = jax.nn.gelu(z[:, hd:2 * hd])
        ga = z[:, 2 * hd:]
        ms = jnp.mean(v * v, axis=-1, keepdims=True)
        vn3 = (v * lax.rsqrt(ms + EPS) * gv_ref[...]).reshape(tc, ns, hd)
        pre3 = (u * jax.nn.silu(ga)).reshape(tc, ns, hd)
        if v_refs:
            v_refs[0][:, k0:k0 + ns, :] = vn3
        for c in range(ns // r):
            vg = vn3[:, c * r:(c + 1) * r, :].reshape(CHUNK, hd)
            mixed = jnp.dot(wm, vg.astype(BF16), preferred_element_type=F32) + bm
            og = pre3[:, c * r:(c + 1) * r, :].reshape(CHUNK, hd) * mixed
            oa_ref[:, k0 + c * r:k0 + (c + 1) * r, :] = og.reshape(tc, r, hd).astype(oa_ref.dtype)
    _transpose_side_blocks(side_in, side_out)


def _branch_a(xn, w_in, gv, w_s, bmix, n, n_sub, want_v, transpose_b):
    tc, nk, d = xn.shape
    nh, _, hd = gv.shape
    r = CHUNK // tc
    nt = nk // n
    out_map = lambda h, i: (0, i, h)
    in_specs = [pl.BlockSpec((tc, n, d), lambda h, i: (0, i, 0)),
                pl.BlockSpec((d, hd), lambda h, i: (0, h)),
                pl.BlockSpec((d, hd), lambda h, i: (0, nh + h)),
                pl.BlockSpec((d, hd), lambda h, i: (0, 2 * nh + h)),
                pl.BlockSpec((None, 1, hd), lambda h, i: (h, 0, 0)),
                pl.BlockSpec((None, CHUNK, CHUNK), lambda h, i: (h, 0, 0)),
                pl.BlockSpec((None, CHUNK, hd), lambda h, i: (h, 0, 0))]
    args = [xn, w_in, w_in, w_in, gv, w_s, bmix]
    out_shape = [jax.ShapeDtypeStruct((tc, nk, nh * hd), BF16)]
    out_specs = [pl.BlockSpec((tc, n, hd), out_map)]
    if want_v:
        out_shape.append(jax.ShapeDtypeStruct((tc, nk, nh * hd), F32))
        out_specs.append(pl.BlockSpec((tc, n, hd), out_map))
    if transpose_b:
        col0 = 3 * nh * hd // LANES
        n_cols = w_in.shape[1] - 3 * nh * hd
        assert nh * nt * LANES == n_cols, "one 128-column block per grid step"
        in_specs.append(pl.BlockSpec((d, LANES), lambda h, i: (0, col0 + h * nt + i)))
        args.append(w_in)
        out_shape.append(jax.ShapeDtypeStruct((n_cols, d), BF16))
        out_specs.append(pl.BlockSpec((LANES, d), lambda h, i: (h * nt + i, 0)))
    res = list(pl.pallas_call(
        functools.partial(_branch_a_kernel, tc=tc, n=n, r=r, hd=hd, n_sub=n_sub,
                          has_v=want_v, has_side=transpose_b),
        grid=(nh, nt),
        in_specs=in_specs,
        out_specs=out_specs,
        out_shape=out_shape,
        scratch_shapes=[pltpu.VMEM((d, 3 * hd), BF16), pltpu.VMEM((CHUNK, CHUNK), BF16)],
        compiler_params=_params("parallel", "arbitrary"),
    )(*args))
    out_a = res.pop(0)
    v_rows = res.pop(0) if want_v else None
    w_bt = res.pop(0) if transpose_b else None
    return out_a, v_rows, w_bt


def _transpose_side_blocks(in_refs, out_refs):
    for w_ref, o_ref in zip(in_refs, out_refs):
        o_ref[...] = w_ref[...].T.astype(o_ref.dtype)


def _stream_maps(counts):
    starts = [sum(counts[:s]) for s in range(len(counts))]
    maps = [lambda j, st=st, c=c: jnp.clip(j - st, 0, c - 1) for st, c in zip(starts, counts)]
    owns = [lambda j, st=st, c=c: jnp.logical_and(j >= st, j < st + c) for st, c in zip(starts, counts)]
    return maps, owns


def _proj_b_kernel(w_ref, *refs, counts, n_side):
    ns = len(counts)
    xn_refs, side_in = refs[:ns], refs[ns:ns + n_side]
    o_refs, side_out = refs[ns + n_side:2 * ns + n_side], refs[2 * ns + n_side:]
    _, owns = _stream_maps(counts)
    j = pl.program_id(0)
    for s, (xn_ref, o_ref) in enumerate(zip(xn_refs, o_refs)):
        @pl.when(owns[s](j))
        def _(s=s, xn_ref=xn_ref, o_ref=o_ref):
            if s == 0:
                _transpose_side_blocks(side_in, side_out)
            for t in range(xn_ref.shape[0] // LANE_TILE):
                ls = slice(t * LANE_TILE, (t + 1) * LANE_TILE)
                xn = xn_ref[ls, :]
                for m in range(w_ref.shape[0] // ROW_CHUNK):
                    sl = slice(m * ROW_CHUNK, (m + 1) * ROW_CHUNK)
                    o_ref[sl, ls] = lax.dot_general(
                        w_ref[sl, :], xn, (((1,), (1,)), ((), ())),
                        preferred_element_type=F32).astype(o_ref.dtype)


def _proj_b(w_bt, xns, side=()):
    rows, d = w_bt.shape
    counts = [x.shape[0] for x in xns]
    maps, _ = _stream_maps(counts)
    in_specs = [pl.BlockSpec((rows, d), lambda j: (0, 0), pipeline_mode=pl.Buffered(1))]
    in_specs += [pl.BlockSpec((None, x.shape[1], d), lambda j, m=m: (m(j), 0, 0))
                 for x, m in zip(xns, maps)]
    out_specs = [pl.BlockSpec((None, rows, x.shape[1]), lambda j, m=m: (m(j), 0, 0))
                 for x, m in zip(xns, maps)]
    out_shape = [jax.ShapeDtypeStruct((x.shape[0], rows, x.shape[1]), BF16) for x in xns]
    for w, n_rows, row_block in side:
        n_cols = w.shape[1]
        assert counts[0] * LANES == n_cols, "one 128-column block per step of the first stream set"
        in_specs.append(pl.BlockSpec((n_rows, LANES), lambda j, rb=row_block: (rb, maps[0](j))))
        out_specs.append(pl.BlockSpec((LANES, n_rows), lambda j: (maps[0](j), 0)))
        out_shape.append(jax.ShapeDtypeStruct((n_cols, n_rows), BF16))
    res = pl.pallas_call(
        functools.partial(_proj_b_kernel, counts=counts, n_side=len(side)),
        grid=(sum(counts),),
        in_specs=in_specs,
        out_specs=out_specs,
        out_shape=out_shape,
        compiler_params=_params("arbitrary"),
    )(w_bt, *xns, *[w for w, _, _ in side])
    return tuple(res[:len(xns)]), tuple(res[len(xns):])


def _shift_rows(x, d):
    if d % 8 == 0:
        return jnp.concatenate([jnp.zeros((d, x.shape[1]), x.dtype), x[:-d]], axis=0)
    row = lax.broadcasted_iota(jnp.int32, x.shape, 0)
    return jnp.where(row >= d, pltpu.roll(x, d, 0), 0.0)


def _ssm_kernel(x_ref, mp_ref, q_ref, ct_ref, dsk_ref, *rest, tc, gb, nk, fold, has_h0):
    if has_h0:
        h0r_ref, h0i_ref, y_ref, hfr_ref, hfi_ref = rest
    else:
        y_ref, hf_ref = rest
    rows = tc * GROUP
    half = q_ref.shape[2] // 2
    b_tc = tc.bit_length() - 1

    def one_group(g):
        r0 = g * GROUP if isinstance(g, int) else pl.multiple_of(g * GROUP, GROUP)
        if fold == 1:
            xg = x_ref[:, pl.ds(r0, GROUP), :]
        else:
            xg = jnp.stack([x_ref[j // fold, pl.ds(r0, GROUP), (j % fold) * nk:(j % fold + 1) * nk]
                            for j in range(tc)])
        res = jnp.dot(mp_ref[g], xg.reshape(rows, nk), preferred_element_type=F32)
        y_local = res[:rows]
        s_re = res[rows:rows + half]
        s_im = res[rows + half:]
        ct = ct_ref[g]
        if has_h0:
            h_re, h_im = h0r_ref[g], h0i_ref[g]
            a_re, a_im = ct[:, b_tc:b_tc + 1], ct[:, POW_COLS + b_tc:POW_COLS + b_tc + 1]
            hfr_ref[g] = a_re * h_re - a_im * h_im + s_re
            hfi_ref[g] = a_re * h_im + a_im * h_re + s_im
            h_in = jnp.concatenate([h_re, h_im], axis=0)
        else:
            tiles = []
            for pair in range(nk // (2 * LANES)):
                lo, mid, hi = 2 * pair * LANES, (2 * pair + 1) * LANES, (2 * pair + 2) * LANES
                xr = jnp.concatenate([s_re[:, lo:mid], s_re[:, mid:hi]], axis=0).T
                xi = jnp.concatenate([s_im[:, lo:mid], s_im[:, mid:hi]], axis=0).T
                for s in range(SCAN_STEPS):
                    cr, ci = ct[s:s + 1], ct[SCAN_ROWS + s:SCAN_ROWS + s + 1]
                    rr, ri = _shift_rows(xr, 1 << s), _shift_rows(xi, 1 << s)
                    xr, xi = xr + cr * rr - ci * ri, xi + cr * ri + ci * rr
                hf_ref[g, 2 * pair:2 * pair + 1, :] = xr[LANES - 1:]
                hf_ref[g, 2 * pair + 1:2 * pair + 2, :] = xi[LANES - 1:]
                hr_t = _shift_rows(xr, 1).T
                hi_t = _shift_rows(xi, 1).T
                tiles.append(jnp.concatenate([hr_t[:half], hi_t[:half]], axis=0))
                tiles.append(jnp.concatenate([hr_t[half:], hi_t[half:]], axis=0))
            h_in = jnp.concatenate(tiles, axis=1)
        y = y_local + jnp.dot(q_ref[g], h_in.astype(BF16), preferred_element_type=F32)
        dsk = jnp.concatenate([dsk_ref[pl.ds(r0, GROUP), :]] * (nk // LANES), axis=1)
        y3 = y.reshape(tc, GROUP, nk) + dsk[None] * xg.astype(F32)
        ya = jax.nn.gelu(y3).astype(y_ref.dtype)
        if fold == 1:
            y_ref[:, pl.ds(r0, GROUP), :] = ya
        else:
            for j in range(tc):
                y_ref[j // fold, pl.ds(r0, GROUP), (j % fold) * nk:(j % fold + 1) * nk] = ya[j]

    def body(gi, carry):
        for k in range(GROUPS_PER_TRIP):
            one_group(gi * GROUPS_PER_TRIP + k)
        return carry

    if gb == GROUPS_PER_TRIP:
        body(0, 0)
    else:
        lax.fori_loop(0, gb // GROUPS_PER_TRIP, body, 0)


def _ssm(zb, mp, q, ct, dsk, h0, gb, fold):
    tcf, _, nkf = zb.shape
    tc, nk = tcf * fold, nkf // fold
    ng, mrows, rows = mp.shape
    p2 = mrows - rows
    cb = ng * GROUP
    has_h0 = h0 is not None
    in_specs = [pl.BlockSpec((tcf, gb * GROUP, nkf), lambda i: (0, i, 0)),
                pl.BlockSpec((gb, mrows, rows), lambda i: (i, 0, 0)),
                pl.BlockSpec((gb, rows, p2), lambda i: (i, 0, 0)),
                pl.BlockSpec((gb,) + ct.shape[1:], lambda i: (i, 0, 0)),
                pl.BlockSpec((gb * GROUP, LANES), lambda i: (i, 0))]
    args = [zb, mp, q, ct, dsk]
    if has_h0:
        state_spec = pl.BlockSpec((gb, p2 // 2, nk), lambda i: (i, 0, 0))
        in_specs += [state_spec, state_spec]
        args += list(h0)
        hf_specs = [state_spec, state_spec]
        hf_shapes = [jax.ShapeDtypeStruct((ng, p2 // 2, nk), F32)] * 2
    else:
        hf_specs = [pl.BlockSpec((gb, nk // LANES, LANES), lambda i: (i, 0, 0))]
        hf_shapes = [jax.ShapeDtypeStruct((ng, nk // LANES, LANES), F32)]
    res = pl.pallas_call(
        functools.partial(_ssm_kernel, tc=tc, gb=gb, nk=nk, fold=fold, has_h0=has_h0),
        grid=(ng // gb,),
        in_specs=in_specs,
        out_specs=[pl.BlockSpec((tcf, gb * GROUP, nkf), lambda i: (0, i, 0))] + hf_specs,
        out_shape=[jax.ShapeDtypeStruct((tcf, cb, nkf), BF16)] + hf_shapes,
        compiler_params=_params("parallel"),
    )(*args)
    return res[0], tuple(res[1:])


def _glu_out_kernel(wg_ref, bg_ref, wo_ref, *refs, counts, has_cast, n_tiles):
    ns = len(counts)
    refs = list(refs)
    y_refs, gate_refs = refs[:ns], refs[ns:2 * ns]
    del refs[:2 * ns]
    cast_in = refs.pop(0) if has_cast else None
    r_refs = refs[:ns]
    del refs[:ns]
    cast_out = refs.pop(0) if has_cast else None
    ob_refs = refs
    _, owns = _stream_maps(counts)
    j = pl.program_id(0)
    for s in range(ns):
        @pl.when(owns[s](j))
        def _(s=s, y_ref=y_refs[s], gate_ref=gate_refs[s], r_ref=r_refs[s]):
            if s == 0 and has_cast:
                cast_out[...] = cast_in[...].astype(cast_out.dtype)
            for t, ob_ref in enumerate(ob_refs[:n_tiles[s]]):
                ls = slice(t * LANE_TILE, (t + 1) * LANE_TILE)
                y = y_ref[:, ls]
                for m in range(wg_ref.shape[0] // ROW_CHUNK):
                    sl = slice(m * ROW_CHUNK, (m + 1) * ROW_CHUNK)
                    g = jnp.dot(wg_ref[sl, :], y, preferred_element_type=F32) + bg_ref[sl, :]
                    ob = (y_ref[sl, ls].astype(F32) * jax.nn.sigmoid(g)
                          * jax.nn.silu(gate_ref[sl, ls].astype(F32)))
                    ob_ref[sl, :] = ob.astype(ob_ref.dtype)
            for t, ob_ref in enumerate(ob_refs[:n_tiles[s]]):
                ob = ob_ref[...]
                for m in range(wo_ref.shape[0] // ROW_CHUNK):
                    sl = slice(m * ROW_CHUNK, (m + 1) * ROW_CHUNK)
                    r_ref[t * LANE_TILE:(t + 1) * LANE_TILE, sl] = jnp.dot(
                        wo_ref[sl, :], ob, preferred_element_type=F32).T


def _glu_out(y_ts, zbs, wg_t, bg, wo_t, cast_rows=None):
    cb = wg_t.shape[0]
    d = wo_t.shape[0]
    counts = [y.shape[0] for y in y_ts]
    maps, _ = _stream_maps(counts)
    const = lambda j: (0, 0)
    in_specs = [pl.BlockSpec((cb, cb), const, pipeline_mode=pl.Buffered(1)),
                pl.BlockSpec((cb, LANE_TILE), const, pipeline_mode=pl.Buffered(1)),
                pl.BlockSpec((d, cb), const, pipeline_mode=pl.Buffered(1))]
    in_specs += [pl.BlockSpec((None, cb, y.shape[2]), lambda j, m=m: (m(j), 0, 0)) for y, m in zip(y_ts, maps)]
    in_specs += [pl.BlockSpec((None, cb, y.shape[2]), lambda j, m=m: (m(j), 1, 0)) for y, m in zip(y_ts, maps)]
    args = [wg_t, bg, wo_t, *y_ts, *zbs]
    out_specs = [pl.BlockSpec((None, y.shape[2], d), lambda j, m=m: (m(j), 0, 0)) for y, m in zip(y_ts, maps)]
    out_shape = [jax.ShapeDtypeStruct((y.shape[0], y.shape[2], d), F32) for y in y_ts]
    if cast_rows is not None:
        w, n_rows = cast_rows
        assert counts[0] * LANES == n_rows, "one 128-row block per step of the first stream set"
        blk = pl.BlockSpec((LANES, w.shape[1]), lambda j: (maps[0](j), 0))
        in_specs.append(blk)
        args.append(w)
        out_specs.append(blk)
        out_shape.append(jax.ShapeDtypeStruct((n_rows, w.shape[1]), BF16))
    n_tiles = [y.shape[2] // LANE_TILE for y in y_ts]
    res = pl.pallas_call(
        functools.partial(_glu_out_kernel, counts=counts, has_cast=cast_rows is not None,
                          n_tiles=n_tiles),
        grid=(sum(counts),),
        in_specs=in_specs,
        out_specs=out_specs,
        out_shape=out_shape,
        scratch_shapes=[pltpu.VMEM((cb, LANE_TILE), BF16)] * max(n_tiles),
        compiler_params=_params("arbitrary"),
    )(*args)
    return tuple(res[:len(y_ts)]), (res[len(y_ts)] if cast_rows is not None else None)


def _out_norm_kernel(x_ref, oa_ref, r_ref, wo_ref, g_ref, y_ref, ms_ref, *, tc, n):
    ca = oa_ref.shape[-1]
    d = x_ref.shape[-1]
    nc = d // LANES
    p = jnp.dot(oa_ref[...].reshape(tc * n, ca), wo_ref[...], preferred_element_type=F32)
    for j in range(tc):
        s = p[j * n:(j + 1) * n] + r_ref[j]
        for c in range(nc):
            ms_ref[c, pl.ds(j, n, stride=tc), :] = s[:, c * LANES:(c + 1) * LANES]
    acc = x_ref[...] + jnp.concatenate([ms_ref[c] for c in range(nc)], axis=1)
    ms = jnp.mean(acc * acc, axis=-1, keepdims=True)
    y_ref[...] = acc * lax.rsqrt(ms + EPS) * g_ref[...]


def _out_norm(x2d, oa, r, wo_a, gf, n):
    tc, nk, ca = oa.shape
    t, d = x2d.shape
    body = functools.partial(_out_norm_kernel, tc=tc, n=n)

    def outer(x_hbm, oa_hbm, r_hbm, wo_ref, g_ref, y_hbm, ms_ref):
        def step(x_ref, oa_ref, r_ref, y_ref):
            body(x_ref, oa_ref, r_ref, wo_ref, g_ref, y_ref, ms_ref)

        buf = pl.Buffered(OUT_NORM_INPUT_BUFFERS)
        pltpu.emit_pipeline(
            step, grid=(nk // n,),
            in_specs=[pl.BlockSpec((n * tc, d), lambda i: (i, 0), pipeline_mode=buf),
                      pl.BlockSpec((tc, n, ca), lambda i: (0, i, 0), pipeline_mode=buf),
                      pl.BlockSpec((tc, n, d), lambda i: (0, i, 0), pipeline_mode=buf)],
            out_specs=[pl.BlockSpec((n * tc, d), lambda i: (i, 0))],
        )(x_hbm, oa_hbm, r_hbm, y_hbm)

    hbm = pl.BlockSpec(memory_space=pl.ANY)
    vmem = pl.BlockSpec(memory_space=pltpu.VMEM)
    return pl.pallas_call(
        outer,
        in_specs=[hbm, hbm, hbm, vmem, vmem],
        out_specs=hbm,
        out_shape=jax.ShapeDtypeStruct((t, d), F32),
        scratch_shapes=[pltpu.VMEM((d // LANES, n * tc, LANES), F32)],
        compiler_params=pltpu.CompilerParams(vmem_limit_bytes=VMEM_LIMIT),
    )(x2d, oa, r, wo_a, gf.reshape(1, d))


def _discretise(a_re, a_im, log_dt):
    dt = jnp.exp(log_dt)[:, None]
    x_re, x_im = a_re * dt, a_im * dt
    n = (2.0 ** jnp.arange(N_POW, dtype=F32))[None, :, None]
    mag = jnp.exp(n * x_re[:, None, :])
    ang = n * x_im[:, None, :]
    pw_r, pw_i = mag * jnp.cos(ang), mag * jnp.sin(ang)
    lr, li = pw_r[:, 0], pw_i[:, 0]
    den = a_re * a_re + a_im * a_im
    nr, ni = lr - 1.0, li
    qr, qi = (nr * a_re + ni * a_im) / den, (ni * a_re - nr * a_im) / den
    cat = lambda a, b: jnp.concatenate([a, b], axis=-1)
    rowtab = jnp.stack([cat(lr, lr), cat(li, li), cat(qr, qi), cat(-qi, qr)], axis=1)
    zpad = jnp.zeros_like(pw_r[:, :POW_COLS - N_POW])
    lt = jnp.concatenate([pw_r, zpad, pw_i, zpad], axis=1).transpose(0, 2, 1)
    b0 = TC_PROMPT.bit_length() - 1
    sc_r, sc_i = pw_r[:, b0:b0 + SCAN_STEPS], pw_i[:, b0:b0 + SCAN_STEPS]
    zrow = jnp.zeros_like(cat(sc_r, sc_r)[:, :SCAN_ROWS - SCAN_STEPS])
    scan_rows = jnp.concatenate([cat(sc_r, sc_r), zrow, cat(sc_i, sc_i), zrow], axis=1)
    return rowtab, lt, scan_rows


def _ssm_tables_kernel(rt_ref, btr_ref, bti_ref, c2r_ref, c2i_ref, c2n_ref,
                       mp_ref, q_ref, mps_ref, qs_ref, *, gb, tc, tcs, unroll):
    rows = tc * GROUP
    rows_s = tcs * GROUP
    half = rt_ref.shape[2] // 2
    lane_k = lax.broadcasted_iota(jnp.int32, (GROUP, rows), 1)
    lane_q = lax.broadcasted_iota(jnp.int32, (1, 2 * half), 1)

    def one_group(g):
        rt = rt_ref[g]
        l_r, l_i, qu, qv = rt[0:1], rt[1:2], rt[2:3], rt[3:4]
        btr, bti = btr_ref[g], bti_ref[g]
        bb = btr * qu + bti * qv
        bs = btr * qv - bti * qu
        c2r, c2i = c2r_ref[g], c2i_ref[g]
        cur_r = jnp.ones((1, 2 * half), F32)
        cur_i = jnp.zeros((1, 2 * half), F32)
        et_blocks = []
        for n in range(tc):
            et_blocks.append(cur_r * bb + cur_i * bs)
            cur_r, cur_i = cur_r * l_r - cur_i * l_i, cur_r * l_i + cur_i * l_r
            u = jnp.where(lane_q < half, cur_r, -cur_i)
            v = jnp.where(lane_q < half, -cur_i, -cur_r)
            blk = (c2r * u + c2i * v).astype(q_ref.dtype)
            q_ref[g, n * GROUP:(n + 1) * GROUP, :] = blk
            if n < tcs:
                qs_ref[g, n * GROUP:(n + 1) * GROUP, :] = blk
        p = jnp.concatenate(et_blocks[::-1], axis=0).T
        mp_ref[g, rows:, :] = p.astype(mp_ref.dtype)
        mps_ref[g, rows_s:, :] = p[:, rows - rows_s:].astype(mps_ref.dtype)
        krow = jnp.dot(c2n_ref[g], p, precision=lax.Precision.HIGHEST, preferred_element_type=F32)
        for i in range(tc):
            sh = (tc - 1 - i) * GROUP
            blk = krow if sh == 0 else pltpu.roll(krow, rows - sh, 1)
            blk = jnp.where(lane_k < (i + 1) * GROUP, blk, 0.0)
            mp_ref[g, i * GROUP:(i + 1) * GROUP, :] = blk.astype(mp_ref.dtype)
            if i < tcs:
                mps_ref[g, i * GROUP:(i + 1) * GROUP, :] = blk[:, :rows_s].astype(mps_ref.dtype)

    def body(gi, carry):
        for k in range(unroll):
            one_group(gi * unroll + k)
        return carry

    if unroll == gb:
        body(0, 0)
    else:
        lax.fori_loop(0, gb // unroll, body, 0)


def _ssm_tables_job(rowtab, b_re, b_im, c_re, c_im, tcs, gb):
    ng, _, p2 = rowtab.shape
    tc = TC_PROMPT
    rows, rows_s = tc * GROUP, tcs * GROUP
    dup = lambda a: jnp.concatenate([a, a], axis=-1)
    args = (rowtab, dup(b_re.transpose(0, 2, 1)), dup(b_im.transpose(0, 2, 1)),
            dup(c_re), dup(c_im), jnp.concatenate([c_re, -c_im], axis=-1))
    shapes = [(ng, rows + p2, rows), (ng, rows, p2), (ng, rows_s + p2, rows_s), (ng, rows_s, p2)]
    in_specs = [pl.BlockSpec((gb,) + a.shape[1:], lambda i: (i, 0, 0)) for a in args]
    out_specs = [pl.BlockSpec((gb,) + s[1:], lambda i: (i, 0, 0)) for s in shapes]
    out_shape = [jax.ShapeDtypeStruct(s, BF16) for s in shapes]
    body = functools.partial(_ssm_tables_kernel, gb=gb, tc=tc, tcs=tcs, unroll=gb)
    return args, in_specs, out_specs, out_shape, body, ng // gb


def _mix_bias(b_s, hd, tc):
    nh = b_s.shape[0]
    r = CHUNK // tc
    if tc == TC_PROMPT:
        bm = b_s.reshape(nh, r, tc).transpose(0, 2, 1).reshape(nh, CHUNK)
    else:
        bm = jnp.repeat(b_s[:, :tc], r, axis=1)
    return jnp.broadcast_to(bm[:, :, None], (nh, CHUNK, hd))


def _tiles(tc, nk):
    fold = max(1, LANE_TILE // nk)
    return NORM_ROWS // tc, min(BRANCH_A_ROWS, tc * nk) // tc, fold


def kernel(x_prompt, x_sample, state_ssm_re, state_ssm_im, g_norm, w_in, g_v, w_s, b_s,
           a_re, a_im, log_dt, b_re, b_im, c_re, c_im, d_skip, w_glu, b_glu, w_out, g_final):
    depth = g_norm.shape[0]
    assert depth == 1, "single-layer step"
    bsz, seq, d = x_prompt.shape
    dbsz, dseq, _ = x_sample.shape
    nh = w_s.shape[1]
    ng, npst = a_re.shape[1:]
    cb = ng * GROUP
    ca = (w_in.shape[2] - 2 * cb) // 3
    hd = ca // nh
    assert seq // TC_PROMPT == LANES and dbsz == LANES and CHUNK % dseq == 0
    assert dseq & (dseq - 1) == 0 and dseq <= TC_PROMPT, "sample chunk must be a power of two"
    assert ca == cb == d

    l = 0
    gv = g_v[l].reshape(nh, 1, hd)
    rowtab, lt, scan_rows = _discretise(a_re[l], a_im[l], log_dt[l])

    assert bsz % 2 == 0
    tcs = (TC_PROMPT, dseq)
    nks = (bsz * seq // TC_PROMPT, dbsz)
    xs = (x_prompt.reshape(bsz * seq, d), x_sample.reshape(dbsz * dseq, d))
    tiles = [_tiles(tc, nk) for tc, nk in zip(tcs, nks)]
    tables_job = _ssm_tables_job(rowtab, b_re[l], b_im[l], c_re[l], c_im[l], dseq,
                                 ng // (nks[0] // tiles[0][0]))
    xns, out_as, v_rows = [], [], []
    tabs = w_bt = None
    for s, (x2d, tc, (n_norm, n_a, _)) in enumerate(zip(xs, tcs, tiles)):
        xn, made = _norm_permute(x2d, g_norm[l], tc, n_norm, tables_job if s == 0 else None)
        tabs = made if s == 0 else tabs
        out_a, v, made_bt = _branch_a(xn, w_in[l], gv, w_s[l], _mix_bias(b_s[l], hd, tc), n_a,
                                      tc * n_a // PROJ_ROWS, s == 1, s == 0)
        w_bt = made_bt if s == 0 else w_bt
        xns.append(xn)
        out_as.append(out_a)
        v_rows.append(v)
    mp, q, mp_s, q_s = tabs
    folds = [t[2] for t in tiles]
    slabs = [xn.reshape(tc // f, f * nk, d) for xn, tc, nk, f in zip(xns, tcs, nks, folds)]
    zbs, (wg_t, wo_bt) = _proj_b(w_bt, slabs, ((w_glu[l], cb, 0), (w_out[l], cb, 1)))
    dsk = jnp.broadcast_to(d_skip[l][:, None], (cb, LANES))
    h0 = (state_ssm_re[l].transpose(1, 2, 0), state_ssm_im[l].transpose(1, 2, 0))
    y_p_t, hf_p = _ssm(zbs[0], mp, q, scan_rows, dsk, None, SSM_GROUPS_PER_STEP, folds[0])
    y_s_t, hf_s = _ssm(zbs[1], mp_s, q_s, lt, dsk, h0, SSM_GROUPS_PER_STEP_1CHUNK, folds[1])
    bg = jnp.broadcast_to(b_glu[l][:, None], (cb, LANE_TILE))
    rs, wo_a = _glu_out((y_p_t, y_s_t), zbs, wg_t, bg, wo_bt, (w_out[l], ca))
    y_p, y_s = [_out_norm(x2d, out_a, r.reshape(tc, nk, d), wo_a, g_final, OUT_ROWS // tc)
                for x2d, out_a, r, tc, nk in zip(xs, out_as, rs, tcs, nks)]
    v_s = v_rows[1]

    hf_p = hf_p[0].reshape(ng, bsz // 2, 2, 2, npst)
    re_p = hf_p[:, :, 0].reshape(ng, bsz, npst).transpose(1, 0, 2)[None]
    im_p = hf_p[:, :, 1].reshape(ng, bsz, npst).transpose(1, 0, 2)[None]
    re_s = hf_s[0].transpose(2, 0, 1)[None]
    im_s = hf_s[1].transpose(2, 0, 1)[None]
    v_out = v_s.transpose(1, 0, 2)[None]

    dt_p, dt_s = x_prompt.dtype, x_sample.dtype
    return (y_p.reshape(bsz, seq, d).astype(dt_p), y_s.reshape(dbsz, dseq, d).astype(dt_s),
            re_p.astype(dt_p), im_p.astype(dt_p), re_s.astype(dt_s), im_s.astype(dt_s),
            v_out.astype(dt_s))
```

```python
import functools

import jax
import jax.numpy as jnp
from jax import lax
from jax.experimental import pallas as pl
from jax.experimental.pallas import tpu as pltpu

EPS = 1e-6
LANES = 128
CHUNK = 128
GROUP = 16
TC_PROMPT = 16
ROW_CHUNK = 512
LANE_TILE = 256
PROJ_ROWS = 512
BRANCH_A_ROWS = 2048
NORM_ROWS = 512
OUT_ROWS = 512
UNPERMUTE_PITCH = 24
SSM_GROUPS_PER_STEP = 8
SSM_GROUPS_PER_STEP_1CHUNK = 32
POW_COLS = 16
SCAN_STEPS = 7
N_POW = 11
SCAN_ROWS = 8
GROUPS_PER_TRIP = 8
VMEM_LIMIT = 56 * 1024 * 1024

F32 = jnp.float32
BF16 = jnp.bfloat16


def _params(*sem):
    return pltpu.CompilerParams(dimension_semantics=sem, vmem_limit_bytes=VMEM_LIMIT)


def _norm_permute_kernel(x_ref, g_ref, *refs, tc, n, n_side_in, side_body):
    o_ref = refs[n_side_in]
    x = x_ref[...]
    ms = jnp.mean(x * x, axis=-1, keepdims=True)
    xn = (x * lax.rsqrt(ms + EPS) * g_ref[...]).astype(BF16)
    rows = n * tc
    dst = lax.broadcasted_iota(jnp.int32, (rows, rows), 0)
    src = lax.broadcasted_iota(jnp.int32, (rows, rows), 1)
    sel = (src == tc * (dst % n) + dst // n).astype(BF16)
    o_ref[...] = jnp.dot(sel, xn, preferred_element_type=F32).astype(o_ref.dtype).reshape(o_ref.shape)
    if side_body is not None:
        side_body(*refs[:n_side_in], *refs[n_side_in + 1:])


def _norm_permute(x2d, g, tc, n, side_job=None):
    t, d = x2d.shape
    nk = t // tc
    s_args, s_in, s_out, s_shape, s_body, s_steps = side_job or ((), [], [], [], None, nk // n)
    assert s_steps == nk // n, "the side job needs one block per grid step"
    res = pl.pallas_call(
        functools.partial(_norm_permute_kernel, tc=tc, n=n, n_side_in=len(s_args), side_body=s_body),
        grid=(nk // n,),
        in_specs=[pl.BlockSpec((n * tc, d), lambda i: (i, 0)),
                  pl.BlockSpec((1, d), lambda i: (0, 0))] + list(s_in),
        out_specs=[pl.BlockSpec((tc, n, d), lambda i: (0, i, 0))] + list(s_out),
        out_shape=[jax.ShapeDtypeStruct((tc, nk, d), BF16)] + list(s_shape),
        compiler_params=_params("parallel"),
    )(x2d, g.reshape(1, d), *s_args)
    return res[0], tuple(res[1:])


def _branch_a_kernel(*refs, tc, n, r, hd, n_sub, has_v, has_side):
    refs = list(refs)
    xn_ref, wu_ref, wv_ref, wg_ref, gv_ref, ws_ref, bmix_ref = refs[:7]
    del refs[:7]
    side_in = [refs.pop(0)] if has_side else []
    oa_ref = refs.pop(0)
    v_refs = [refs.pop(0)] if has_v else []
    side_out = [refs.pop(0)] if has_side else []
    w_ref, wm_ref = refs
    d = xn_ref.shape[-1]

    @pl.when(pl.program_id(1) == 0)
    def _():
        w_ref[:, :hd] = wu_ref[...].astype(BF16)
        w_ref[:, hd:2 * hd] = wv_ref[...].astype(BF16)
        w_ref[:, 2 * hd:] = wg_ref[...].astype(BF16)
        row = lax.broadcasted_iota(jnp.int32, (CHUNK, CHUNK), 0)
        col = lax.broadcasted_iota(jnp.int32, (CHUNK, CHUNK), 1)
        w_tril = jnp.where(row >= col, ws_ref[...], 0.0).astype(BF16)
        if tc == TC_PROMPT:
            time_of_row = tc * (row % r) + row // r
            same_seq = None
        else:
            time_of_row = row // r
            same_seq = (row % r) == (col % r)
        sel = (col == time_of_row).astype(BF16)
        sw = jnp.dot(sel, w_tril, preferred_element_type=F32).astype(BF16)
        wm = lax.dot_general(sw, sel, (((1,), (1,)), ((), ())), preferred_element_type=F32)
        if same_seq is not None:
            wm = jnp.where(same_seq, wm, 0.0)
        wm_ref[...] = wm.astype(BF16)

    ns = n // n_sub
    zs = [jnp.dot(xn_ref[:, k * ns:(k + 1) * ns, :].reshape(tc * ns, d), w_ref[...],
                  preferred_element_type=F32) for k in range(n_sub)]
    wm = wm_ref[...]
    bm = bmix_ref[...]
    for k, z in enumerate(zs):
        k0 = k * ns
        u = jax.nn.gelu(z[:, :hd])
        v = jax.nn.gelu(z[:, hd:2 * hd])
        ga = z[:, 2 * hd:]
        ms = jnp.mean(v * v, axis=-1, keepdims=True)
        vn3 = (v * lax.rsqrt(ms + EPS) * gv_ref[...]).reshape(tc, ns, hd)
        pre3 = (u * jax.nn.silu(ga)).reshape(tc, ns, hd)
        if v_refs:
            v_refs[0][:, k0:k0 + ns, :] = vn3
        for c in range(ns // r):
            vg = vn3[:, c * r:(c + 1) * r, :].reshape(CHUNK, hd)
            mixed = jnp.dot(wm, vg.astype(BF16), preferred_element_type=F32) + bm
            og = pre3[:, c * r:(c + 1) * r, :].reshape(CHUNK, hd) * mixed
            oa_ref[:, k0 + c * r:k0 + (c + 1) * r, :] = og.reshape(tc, r, hd).astype(oa_ref.dtype)
    _transpose_side_blocks(side_in, side_out)


def _branch_a(xn, w_in, gv, w_s, bmix, n, n_sub, want_v, transpose_b):
    tc, nk, d = xn.shape
    nh, _, hd = gv.shape
    r = CHUNK // tc
    nt = nk // n
    out_map = lambda h, i: (0, i, h)
    in_specs = [pl.BlockSpec((tc, n, d), lambda h, i: (0, i, 0)),
                pl.BlockSpec((d, hd), lambda h, i: (0, h)),
                pl.BlockSpec((d, hd), lambda h, i: (0, nh + h)),
                pl.BlockSpec((d, hd), lambda h, i: (0, 2 * nh + h)),
                pl.BlockSpec((None, 1, hd), lambda h, i: (h, 0, 0)),
                pl.BlockSpec((None, CHUNK, CHUNK), lambda h, i: (h, 0, 0)),
                pl.BlockSpec((None, CHUNK, hd), lambda h, i: (h, 0, 0))]
    args = [xn, w_in, w_in, w_in, gv, w_s, bmix]
    out_shape = [jax.ShapeDtypeStruct((tc, nk, nh * hd), BF16)]
    out_specs = [pl.BlockSpec((tc, n, hd), out_map)]
    if want_v:
        out_shape.append(jax.ShapeDtypeStruct((tc, nk, nh * hd), F32))
        out_specs.append(pl.BlockSpec((tc, n, hd), out_map))
    if transpose_b:
        col0 = 3 * nh * hd // LANES
        n_cols = w_in.shape[1] - 3 * nh * hd
        assert nh * nt * LANES == n_cols, "one 128-column block per grid step"
        in_specs.append(pl.BlockSpec((d, LANES), lambda h, i: (0, col0 + h * nt + i)))
        args.append(w_in)
        out_shape.append(jax.ShapeDtypeStruct((n_cols, d), BF16))
        out_specs.append(pl.BlockSpec((LANES, d), lambda h, i: (h * nt + i, 0)))
    res = list(pl.pallas_call(
        functools.partial(_branch_a_kernel, tc=tc, n=n, r=r, hd=hd, n_sub=n_sub,
                          has_v=want_v, has_side=transpose_b),
        grid=(nh, nt),
        in_specs=in_specs,
        out_specs=out_specs,
        out_shape=out_shape,
        scratch_shapes=[pltpu.VMEM((d, 3 * hd), BF16), pltpu.VMEM((CHUNK, CHUNK), BF16)],
        compiler_params=_params("parallel", "arbitrary"),
    )(*args))
    out_a = res.pop(0)
    v_rows = res.pop(0) if want_v else None
    w_bt = res.pop(0) if transpose_b else None
    return out_a, v_rows, w_bt


def _transpose_side_blocks(in_refs, out_refs):
    for w_ref, o_ref in zip(in_refs, out_refs):
        o_ref[...] = w_ref[...].T.astype(o_ref.dtype)


def _stream_maps(counts):
    starts = [sum(counts[:s]) for s in range(len(counts))]
    maps = [lambda j, st=st, c=c: jnp.clip(j - st, 0, c - 1) for st, c in zip(starts, counts)]
    owns = [lambda j, st=st, c=c: jnp.logical_and(j >= st, j < st + c) for st, c in zip(starts, counts)]
    return maps, owns


def _proj_b_kernel(w_ref, *refs, counts, n_side):
    ns = len(counts)
    xn_refs, side_in = refs[:ns], refs[ns:ns + n_side]
    o_refs, side_out = refs[ns + n_side:2 * ns + n_side], refs[2 * ns + n_side:]
    _, owns = _stream_maps(counts)
    j = pl.program_id(0)
    for s, (xn_ref, o_ref) in enumerate(zip(xn_refs, o_refs)):
        @pl.when(owns[s](j))
        def _(s=s, xn_ref=xn_ref, o_ref=o_ref):
            if s == 0:
                _transpose_side_blocks(side_in, side_out)
            for t in range(xn_ref.shape[0] // LANE_TILE):
                ls = slice(t * LANE_TILE, (t + 1) * LANE_TILE)
                xn = xn_ref[ls, :]
                for m in range(w_ref.shape[0] // ROW_CHUNK):
                    sl = slice(m * ROW_CHUNK, (m + 1) * ROW_CHUNK)
                    o_ref[sl, ls] = lax.dot_general(
                        w_ref[sl, :], xn, (((1,), (1,)), ((), ())),
                        preferred_element_type=F32).astype(o_ref.dtype)


def _proj_b(w_bt, xns, side=()):
    rows, d = w_bt.shape
    counts = [x.shape[0] for x in xns]
    maps, _ = _stream_maps(counts)
    in_specs = [pl.BlockSpec((rows, d), lambda j: (0, 0), pipeline_mode=pl.Buffered(1))]
    in_specs += [pl.BlockSpec((None, x.shape[1], d), lambda j, m=m: (m(j), 0, 0))
                 for x, m in zip(xns, maps)]
    out_specs = [pl.BlockSpec((None, rows, x.shape[1]), lambda j, m=m: (m(j), 0, 0))
                 for x, m in zip(xns, maps)]
    out_shape = [jax.ShapeDtypeStruct((x.shape[0], rows, x.shape[1]), BF16) for x in xns]
    for w, n_rows, row_block in side:
        n_cols = w.shape[1]
        assert counts[0] * LANES == n_cols, "one 128-column block per step of the first stream set"
        in_specs.append(pl.BlockSpec((n_rows, LANES), lambda j, rb=row_block: (rb, maps[0](j))))
        out_specs.append(pl.BlockSpec((LANES, n_rows), lambda j: (maps[0](j), 0)))
        out_shape.append(jax.ShapeDtypeStruct((n_cols, n_rows), BF16))
    res = pl.pallas_call(
        functools.partial(_proj_b_kernel, counts=counts, n_side=len(side)),
        grid=(sum(counts),),
        in_specs=in_specs,
        out_specs=out_specs,
        out_shape=out_shape,
        compiler_params=_params("arbitrary"),
    )(w_bt, *xns, *[w for w, _, _ in side])
    return tuple(res[:len(xns)]), tuple(res[len(xns):])


def _shift_rows(x, d):
    if d % 8 == 0:
        return jnp.concatenate([jnp.zeros((d, x.shape[1]), x.dtype), x[:-d]], axis=0)
    row = lax.broadcasted_iota(jnp.int32, x.shape, 0)
    return jnp.where(row >= d, pltpu.roll(x, d, 0), 0.0)


def _ssm_kernel(x_ref, mp_ref, q_ref, ct_ref, dsk_ref, *rest, tc, gb, nk, fold, has_h0):
    if has_h0:
        h0r_ref, h0i_ref, y_ref, hfr_ref, hfi_ref = rest
    else:
        y_ref, hf_ref = rest
    rows = tc * GROUP
    half = q_ref.shape[2] // 2
    b_tc = tc.bit_length() - 1

    def one_group(g):
        r0 = g * GROUP if isinstance(g, int) else pl.multiple_of(g * GROUP, GROUP)
        if fold == 1:
            xg = x_ref[:, pl.ds(r0, GROUP), :]
        else:
            xg = jnp.stack([x_ref[j // fold, pl.ds(r0, GROUP), (j % fold) * nk:(j % fold + 1) * nk]
                            for j in range(tc)])
        res = jnp.dot(mp_ref[g], xg.reshape(rows, nk), preferred_element_type=F32)
        y_local = res[:rows]
        s_re = res[rows:rows + half]
        s_im = res[rows + half:]
        ct = ct_ref[g]
        if has_h0:
            h_re, h_im = h0r_ref[g], h0i_ref[g]
            a_re, a_im = ct[:, b_tc:b_tc + 1], ct[:, POW_COLS + b_tc:POW_COLS + b_tc + 1]
            hfr_ref[g] = a_re * h_re - a_im * h_im + s_re
            hfi_ref[g] = a_re * h_im + a_im * h_re + s_im
            h_in = jnp.concatenate([h_re, h_im], axis=0)
        else:
            tiles = []
            for pair in range(nk // (2 * LANES)):
                lo, mid, hi = 2 * pair * LANES, (2 * pair + 1) * LANES, (2 * pair + 2) * LANES
                xr = jnp.concatenate([s_re[:, lo:mid], s_re[:, mid:hi]], axis=0).T
                xi = jnp.concatenate([s_im[:, lo:mid], s_im[:, mid:hi]], axis=0).T
                for s in range(SCAN_STEPS):
                    cr, ci = ct[s:s + 1], ct[SCAN_ROWS + s:SCAN_ROWS + s + 1]
                    rr, ri = _shift_rows(xr, 1 << s), _shift_rows(xi, 1 << s)
                    xr, xi = xr + cr * rr - ci * ri, xi + cr * ri + ci * rr
                hf_ref[g, 2 * pair:2 * pair + 1, :] = xr[LANES - 1:]
                hf_ref[g, 2 * pair + 1:2 * pair + 2, :] = xi[LANES - 1:]
                hr_t = _shift_rows(xr, 1).T
                hi_t = _shift_rows(xi, 1).T
                tiles.append(jnp.concatenate([hr_t[:half], hi_t[:half]], axis=0))
                tiles.append(jnp.concatenate([hr_t[half:], hi_t[half:]], axis=0))
            h_in = jnp.concatenate(tiles, axis=1)
        y = y_local + jnp.dot(q_ref[g], h_in.astype(BF16), preferred_element_type=F32)
        dsk = jnp.concatenate([dsk_ref[pl.ds(r0, GROUP), :]] * (nk // LANES), axis=1)
        y3 = y.reshape(tc, GROUP, nk) + dsk[None] * xg.astype(F32)
        ya = jax.nn.gelu(y3).astype(y_ref.dtype)
        if fold == 1:
            y_ref[:, pl.ds(r0, GROUP), :] = ya
        else:
            for j in range(tc):
                y_ref[j // fold, pl.ds(r0, GROUP), (j % fold) * nk:(j % fold + 1) * nk] = ya[j]

    def body(gi, carry):
        for k in range(GROUPS_PER_TRIP):
            one_group(gi * GROUPS_PER_TRIP + k)
        return carry

    if gb == GROUPS_PER_TRIP:
        body(0, 0)
    else:
        lax.fori_loop(0, gb // GROUPS_PER_TRIP, body, 0)


def _ssm(zb, mp, q, ct, dsk, h0, gb, fold):
    tcf, _, nkf = zb.shape
    tc, nk = tcf * fold, nkf // fold
    ng, mrows, rows = mp.shape
    p2 = mrows - rows
    cb = ng * GROUP
    has_h0 = h0 is not None
    in_specs = [pl.BlockSpec((tcf, gb * GROUP, nkf), lambda i: (0, i, 0)),
                pl.BlockSpec((gb, mrows, rows), lambda i: (i, 0, 0)),
                pl.BlockSpec((gb, rows, p2), lambda i: (i, 0, 0)),
                pl.BlockSpec((gb,) + ct.shape[1:], lambda i: (i, 0, 0)),
                pl.BlockSpec((gb * GROUP, LANES), lambda i: (i, 0))]
    args = [zb, mp, q, ct, dsk]
    if has_h0:
        state_spec = pl.BlockSpec((gb, p2 // 2, nk), lambda i: (i, 0, 0))
        in_specs += [state_spec, state_spec]
        args += list(h0)
        hf_specs = [state_spec, state_spec]
        hf_shapes = [jax.ShapeDtypeStruct((ng, p2 // 2, nk), F32)] * 2
    else:
        hf_specs = [pl.BlockSpec((gb, nk // LANES, LANES), lambda i: (i, 0, 0))]
        hf_shapes = [jax.ShapeDtypeStruct((ng, nk // LANES, LANES), F32)]
    res = pl.pallas_call(
        functools.partial(_ssm_kernel, tc=tc, gb=gb, nk=nk, fold=fold, has_h0=has_h0),
        grid=(ng // gb,),
        in_specs=in_specs,
        out_specs=[pl.BlockSpec((tcf, gb * GROUP, nkf), lambda i: (0, i, 0))] + hf_specs,
        out_shape=[jax.ShapeDtypeStruct((tcf, cb, nkf), BF16)] + hf_shapes,
        compiler_params=_params("parallel"),
    )(*args)
    return res[0], tuple(res[1:])


def _glu_out_kernel(wg_ref, bg_ref, wo_ref, *refs, counts, has_cast, n_tiles):
    ns = len(counts)
    refs = list(refs)
    y_refs, gate_refs = refs[:ns], refs[ns:2 * ns]
    del refs[:2 * ns]
    cast_in = refs.pop(0) if has_cast else None
    r_refs = refs[:ns]
    del refs[:ns]
    cast_out = refs.pop(0) if has_cast else None
    ob_refs = refs
    _, owns = _stream_maps(counts)
    j = pl.program_id(0)
    for s in range(ns):
        @pl.when(owns[s](j))
        def _(s=s, y_ref=y_refs[s], gate_ref=gate_refs[s], r_ref=r_refs[s]):
            if s == 0 and has_cast:
                cast_out[...] = cast_in[...].astype(cast_out.dtype)
            for t, ob_ref in enumerate(ob_refs[:n_tiles[s]]):
                ls = slice(t * LANE_TILE, (t + 1) * LANE_TILE)
                y = y_ref[:, ls]
                for m in range(wg_ref.shape[0] // ROW_CHUNK):
                    sl = slice(m * ROW_CHUNK, (m + 1) * ROW_CHUNK)
                    g = jnp.dot(wg_ref[sl, :], y, preferred_element_type=F32) + bg_ref[sl, :]
                    ob = (y_ref[sl, ls].astype(F32) * jax.nn.sigmoid(g)
                          * jax.nn.silu(gate_ref[sl, ls].astype(F32)))
                    ob_ref[sl, :] = ob.astype(ob_ref.dtype)
            for t, ob_ref in enumerate(ob_refs[:n_tiles[s]]):
                ob = ob_ref[...]
                for m in range(wo_ref.shape[0] // ROW_CHUNK):
                    sl = slice(m * ROW_CHUNK, (m + 1) * ROW_CHUNK)
                    r_ref[t * LANE_TILE:(t + 1) * LANE_TILE, sl] = jnp.dot(
                        wo_ref[sl, :], ob, preferred_element_type=F32).T


def _glu_out(y_ts, zbs, wg_t, bg, wo_t, cast_rows=None):
    cb = wg_t.shape[0]
    d = wo_t.shape[0]
    counts = [y.shape[0] for y in y_ts]
    maps, _ = _stream_maps(counts)
    const = lambda j: (0, 0)
    in_specs = [pl.BlockSpec((cb, cb), const, pipeline_mode=pl.Buffered(1)),
                pl.BlockSpec((cb, LANE_TILE), const, pipeline_mode=pl.Buffered(1)),
                pl.BlockSpec((d, cb), const, pipeline_mode=pl.Buffered(1))]
    in_specs += [pl.BlockSpec((None, cb, y.shape[2]), lambda j, m=m: (m(j), 0, 0)) for y, m in zip(y_ts, maps)]
    in_specs += [pl.BlockSpec((None, cb, y.shape[2]), lambda j, m=m: (m(j), 1, 0)) for y, m in zip(y_ts, maps)]
    args = [wg_t, bg, wo_t, *y_ts, *zbs]
    out_specs = [pl.BlockSpec((None, y.shape[2], d), lambda j, m=m: (m(j), 0, 0)) for y, m in zip(y_ts, maps)]
    out_shape = [jax.ShapeDtypeStruct((y.shape[0], y.shape[2], d), F32) for y in y_ts]
    if cast_rows is not None:
        w, n_rows = cast_rows
        assert counts[0] * LANES == n_rows, "one 128-row block per step of the first stream set"
        blk = pl.BlockSpec((LANES, w.shape[1]), lambda j: (maps[0](j), 0))
        in_specs.append(blk)
        args.append(w)
        out_specs.append(blk)
        out_shape.append(jax.ShapeDtypeStruct((n_rows, w.shape[1]), BF16))
    n_tiles = [y.shape[2] // LANE_TILE for y in y_ts]
    res = pl.pallas_call(
        functools.partial(_glu_out_kernel, counts=counts, has_cast=cast_rows is not None,
                          n_tiles=n_tiles),
        grid=(sum(counts),),
        in_specs=in_specs,
        out_specs=out_specs,
        out_shape=out_shape,
        scratch_shapes=[pltpu.VMEM((cb, LANE_TILE), BF16)] * max(n_tiles),
        compiler_params=_params("arbitrary"),
    )(*args)
    return tuple(res[:len(y_ts)]), (res[len(y_ts)] if cast_rows is not None else None)


def _out_norm_kernel(x_ref, oa_ref, r_ref, wo_ref, g_ref, y_ref, ms_ref, *, tc, n):
    ca = oa_ref.shape[-1]
    d = x_ref.shape[-1]
    nc = d // LANES
    p = jnp.dot(oa_ref[...].reshape(tc * n, ca), wo_ref[...], preferred_element_type=F32)
    pitch = ms_ref.shape[1] // n
    for j in range(tc):
        s = p[j * n:(j + 1) * n] + r_ref[j]
        for c in range(nc):
            ms_ref[c, pl.ds(j, n, stride=pitch), :] = s[:, c * LANES:(c + 1) * LANES]
    mixed = [ms_ref[c].reshape(n, pitch, LANES)[:, :tc, :].reshape(n * tc, LANES) for c in range(nc)]
    acc = x_ref[...] + jnp.concatenate(mixed, axis=1)
    ms = jnp.mean(acc * acc, axis=-1, keepdims=True)
    y_ref[...] = acc * lax.rsqrt(ms + EPS) * g_ref[...]


def _out_norm(x2d, oa, r, wo_a, gf, n):
    tc, nk, ca = oa.shape
    t, d = x2d.shape
    return pl.pallas_call(
        functools.partial(_out_norm_kernel, tc=tc, n=n),
        grid=(nk // n,),
        in_specs=[pl.BlockSpec((n * tc, d), lambda i: (i, 0)),
                  pl.BlockSpec((tc, n, ca), lambda i: (0, i, 0)),
                  pl.BlockSpec((tc, n, d), lambda i: (0, i, 0)),
                  pl.BlockSpec((ca, d), lambda i: (0, 0), pipeline_mode=pl.Buffered(1)),
                  pl.BlockSpec((1, d), lambda i: (0, 0))],
        out_specs=pl.BlockSpec((n * tc, d), lambda i: (i, 0)),
        out_shape=jax.ShapeDtypeStruct((t, d), F32),
        scratch_shapes=[pltpu.VMEM((d // LANES, n * UNPERMUTE_PITCH, LANES), F32)],
        compiler_params=_params("parallel"),
    )(x2d, oa, r, wo_a, gf.reshape(1, d))


def _discretise(a_re, a_im, log_dt):
    dt = jnp.exp(log_dt)[:, None]
    x_re, x_im = a_re * dt, a_im * dt
    n = (2.0 ** jnp.arange(N_POW, dtype=F32))[None, :, None]
    mag = jnp.exp(n * x_re[:, None, :])
    ang = n * x_im[:, None, :]
    pw_r, pw_i = mag * jnp.cos(ang), mag * jnp.sin(ang)
    lr, li = pw_r[:, 0], pw_i[:, 0]
    den = a_re * a_re + a_im * a_im
    nr, ni = lr - 1.0, li
    qr, qi = (nr * a_re + ni * a_im) / den, (ni * a_re - nr * a_im) / den
    cat = lambda a, b: jnp.concatenate([a, b], axis=-1)
    rowtab = jnp.stack([cat(lr, lr), cat(li, li), cat(qr, qi), cat(-qi, qr)], axis=1)
    zpad = jnp.zeros_like(pw_r[:, :POW_COLS - N_POW])
    lt = jnp.concatenate([pw_r, zpad, pw_i, zpad], axis=1).transpose(0, 2, 1)
    b0 = TC_PROMPT.bit_length() - 1
    sc_r, sc_i = pw_r[:, b0:b0 + SCAN_STEPS], pw_i[:, b0:b0 + SCAN_STEPS]
    zrow = jnp.zeros_like(cat(sc_r, sc_r)[:, :SCAN_ROWS - SCAN_STEPS])
    scan_rows = jnp.concatenate([cat(sc_r, sc_r), zrow, cat(sc_i, sc_i), zrow], axis=1)
    return rowtab, lt, scan_rows


def _ssm_tables_kernel(rt_ref, btr_ref, bti_ref, c2r_ref, c2i_ref, c2n_ref,
                       mp_ref, q_ref, mps_ref, qs_ref, *, gb, tc, tcs, unroll):
    rows = tc * GROUP
    rows_s = tcs * GROUP
    half = rt_ref.shape[2] // 2
    lane_k = lax.broadcasted_iota(jnp.int32, (GROUP, rows), 1)
    lane_q = lax.broadcasted_iota(jnp.int32, (1, 2 * half), 1)

    def one_group(g):
        rt = rt_ref[g]
        l_r, l_i, qu, qv = rt[0:1], rt[1:2], rt[2:3], rt[3:4]
        btr, bti = btr_ref[g], bti_ref[g]
        bb = btr * qu + bti * qv
        bs = btr * qv - bti * qu
        c2r, c2i = c2r_ref[g], c2i_ref[g]
        cur_r = jnp.ones((1, 2 * half), F32)
        cur_i = jnp.zeros((1, 2 * half), F32)
        et_blocks = []
        for n in range(tc):
            et_blocks.append(cur_r * bb + cur_i * bs)
            cur_r, cur_i = cur_r * l_r - cur_i * l_i, cur_r * l_i + cur_i * l_r
            u = jnp.where(lane_q < half, cur_r, -cur_i)
            v = jnp.where(lane_q < half, -cur_i, -cur_r)
            blk = (c2r * u + c2i * v).astype(q_ref.dtype)
            q_ref[g, n * GROUP:(n + 1) * GROUP, :] = blk
            if n < tcs:
                qs_ref[g, n * GROUP:(n + 1) * GROUP, :] = blk
        p = jnp.concatenate(et_blocks[::-1], axis=0).T
        mp_ref[g, rows:, :] = p.astype(mp_ref.dtype)
        mps_ref[g, rows_s:, :] = p[:, rows - rows_s:].astype(mps_ref.dtype)
        krow = jnp.dot(c2n_ref[g], p, precision=lax.Precision.HIGHEST, preferred_element_type=F32)
        for i in range(tc):
            sh = (tc - 1 - i) * GROUP
            blk = krow if sh == 0 else pltpu.roll(krow, rows - sh, 1)
            blk = jnp.where(lane_k < (i + 1) * GROUP, blk, 0.0)
            mp_ref[g, i * GROUP:(i + 1) * GROUP, :] = blk.astype(mp_ref.dtype)
            if i < tcs:
                mps_ref[g, i * GROUP:(i + 1) * GROUP, :] = blk[:, :rows_s].astype(mps_ref.dtype)

    def body(gi, carry):
        for k in range(unroll):
            one_group(gi * unroll + k)
        return carry

    if unroll == gb:
        body(0, 0)
    else:
        lax.fori_loop(0, gb // unroll, body, 0)


def _ssm_tables_job(rowtab, b_re, b_im, c_re, c_im, tcs, gb):
    ng, _, p2 = rowtab.shape
    tc = TC_PROMPT
    rows, rows_s = tc * GROUP, tcs * GROUP
    dup = lambda a: jnp.concatenate([a, a], axis=-1)
    args = (rowtab, dup(b_re.transpose(0, 2, 1)), dup(b_im.transpose(0, 2, 1)),
            dup(c_re), dup(c_im), jnp.concatenate([c_re, -c_im], axis=-1))
    shapes = [(ng, rows + p2, rows), (ng, rows, p2), (ng, rows_s + p2, rows_s), (ng, rows_s, p2)]
    in_specs = [pl.BlockSpec((gb,) + a.shape[1:], lambda i: (i, 0, 0)) for a in args]
    out_specs = [pl.BlockSpec((gb,) + s[1:], lambda i: (i, 0, 0)) for s in shapes]
    out_shape = [jax.ShapeDtypeStruct(s, BF16) for s in shapes]
    body = functools.partial(_ssm_tables_kernel, gb=gb, tc=tc, tcs=tcs, unroll=gb)
    return args, in_specs, out_specs, out_shape, body, ng // gb


def _mix_bias(b_s, hd, tc):
    nh = b_s.shape[0]
    r = CHUNK // tc
    if tc == TC_PROMPT:
        bm = b_s.reshape(nh, r, tc).transpose(0, 2, 1).reshape(nh, CHUNK)
    else:
        bm = jnp.repeat(b_s[:, :tc], r, axis=1)
    return jnp.broadcast_to(bm[:, :, None], (nh, CHUNK, hd))


def _tiles(tc, nk):
    fold = max(1, LANE_TILE // nk)
    return NORM_ROWS // tc, min(BRANCH_A_ROWS, tc * nk) // tc, fold


def kernel(x_prompt, x_sample, state_ssm_re, state_ssm_im, g_norm, w_in, g_v, w_s, b_s,
           a_re, a_im, log_dt, b_re, b_im, c_re, c_im, d_skip, w_glu, b_glu, w_out, g_final):
    depth = g_norm.shape[0]
    assert depth == 1, "single-layer step"
    bsz, seq, d = x_prompt.shape
    dbsz, dseq, _ = x_sample.shape
    nh = w_s.shape[1]
    ng, npst = a_re.shape[1:]
    cb = ng * GROUP
    ca = (w_in.shape[2] - 2 * cb) // 3
    hd = ca // nh
    assert seq // TC_PROMPT == LANES and dbsz == LANES and CHUNK % dseq == 0
    assert dseq & (dseq - 1) == 0 and dseq <= TC_PROMPT, "sample chunk must be a power of two"
    assert ca == cb == d

    l = 0
    gv = g_v[l].reshape(nh, 1, hd)
    rowtab, lt, scan_rows = _discretise(a_re[l], a_im[l], log_dt[l])

    assert bsz % 2 == 0
    tcs = (TC_PROMPT, dseq)
    nks = (bsz * seq // TC_PROMPT, dbsz)
    xs = (x_prompt.reshape(bsz * seq, d), x_sample.reshape(dbsz * dseq, d))
    tiles = [_tiles(tc, nk) for tc, nk in zip(tcs, nks)]
    tables_job = _ssm_tables_job(rowtab, b_re[l], b_im[l], c_re[l], c_im[l], dseq,
                                 ng // (nks[0] // tiles[0][0]))
    xns, out_as, v_rows = [], [], []
    tabs = w_bt = None
    for s, (x2d, tc, (n_norm, n_a, _)) in enumerate(zip(xs, tcs, tiles)):
        xn, made = _norm_permute(x2d, g_norm[l], tc, n_norm, tables_job if s == 0 else None)
        tabs = made if s == 0 else tabs
        out_a, v, made_bt = _branch_a(xn, w_in[l], gv, w_s[l], _mix_bias(b_s[l], hd, tc), n_a,
                                      tc * n_a // PROJ_ROWS, s == 1, s == 0)
        w_bt = made_bt if s == 0 else w_bt
        xns.append(xn)
        out_as.append(out_a)
        v_rows.append(v)
    mp, q, mp_s, q_s = tabs
    folds = [t[2] for t in tiles]
    slabs = [xn.reshape(tc // f, f * nk, d) for xn, tc, nk, f in zip(xns, tcs, nks, folds)]
    zbs, (wg_t, wo_bt) = _proj_b(w_bt, slabs, ((w_glu[l], cb, 0), (w_out[l], cb, 1)))
    dsk = jnp.broadcast_to(d_skip[l][:, None], (cb, LANES))
    h0 = (state_ssm_re[l].transpose(1, 2, 0), state_ssm_im[l].transpose(1, 2, 0))
    y_p_t, hf_p = _ssm(zbs[0], mp, q, scan_rows, dsk, None, SSM_GROUPS_PER_STEP, folds[0])
    y_s_t, hf_s = _ssm(zbs[1], mp_s, q_s, lt, dsk, h0, SSM_GROUPS_PER_STEP_1CHUNK, folds[1])
    bg = jnp.broadcast_to(b_glu[l][:, None], (cb, LANE_TILE))
    rs, wo_a = _glu_out((y_p_t, y_s_t), zbs, wg_t, bg, wo_bt, (w_out[l], ca))
    y_p, y_s = [_out_norm(x2d, out_a, r.reshape(tc, nk, d), wo_a, g_final, OUT_ROWS // tc)
                for x2d, out_a, r, tc, nk in zip(xs, out_as, rs, tcs, nks)]
    v_s = v_rows[1]

    hf_p = hf_p[0].reshape(ng, bsz // 2, 2, 2, npst)
    re_p = hf_p[:, :, 0].reshape(ng, bsz, npst).transpose(1, 0, 2)[None]
    im_p = hf_p[:, :, 1].reshape(ng, bsz, npst).transpose(1, 0, 2)[None]
    re_s = hf_s[0].transpose(2, 0, 1)[None]
    im_s = hf_s[1].transpose(2, 0, 1)[None]
    v_out = v_s.transpose(1, 0, 2)[None]

    dt_p, dt_s = x_prompt.dtype, x_sample.dtype
    return (y_p.reshape(bsz, seq, d).astype(dt_p), y_s.reshape(dbsz, dseq, d).astype(dt_s),
            re_p.astype(dt_p), im_p.astype(dt_p), re_s.astype(dt_s), im_s.astype(dt_s),
            v_out.astype(dt_s))
```

```python
import functools

import jax
import jax.numpy as jnp
from jax import lax
from jax.experimental import pallas as pl
from jax.experimental.pallas import tpu as pltpu

EPS = 1e-6
LANES = 128
CHUNK = 128
GROUP = 16
TC_PROMPT = 16
ROW_CHUNK = 512
LANE_TILE = 256
PROJ_ROWS = 512
BRANCH_A_ROWS = 2048
NORM_ROWS = 512
OUT_ROWS = 512
UNPERMUTE_PITCH = 24
SSM_GROUPS_PER_STEP = 8
SSM_GROUPS_PER_STEP_1CHUNK = 32
POW_COLS = 16
SCAN_STEPS = 7
N_POW = 11
SCAN_ROWS = 8
GROUPS_PER_TRIP = 8
VMEM_LIMIT = 56 * 1024 * 1024

F32 = jnp.float32
BF16 = jnp.bfloat16


def _params(*sem):
    return pltpu.CompilerParams(dimension_semantics=sem, vmem_limit_bytes=VMEM_LIMIT)


def _norm_permute_kernel(x_ref, g_ref, *refs, tc, n, n_side_in, side_body):
    o_ref = refs[n_side_in]
    x = x_ref[...]
    ms = jnp.mean(x * x, axis=-1, keepdims=True)
    xn = (x * lax.rsqrt(ms + EPS) * g_ref[...]).astype(BF16)
    rows = n * tc
    dst = lax.broadcasted_iota(jnp.int32, (rows, rows), 0)
    src = lax.broadcasted_iota(jnp.int32, (rows, rows), 1)
    sel = (src == tc * (dst % n) + dst // n).astype(BF16)
    o_ref[...] = jnp.dot(sel, xn, preferred_element_type=F32).astype(o_ref.dtype).reshape(o_ref.shape)
    if side_body is not None:
        side_body(*refs[:n_side_in], *refs[n_side_in + 1:])


def _norm_permute(x2d, g, tc, n, side_job=None):
    t, d = x2d.shape
    nk = t // tc
    s_args, s_in, s_out, s_shape, s_body, s_steps = side_job or ((), [], [], [], None, nk // n)
    assert s_steps == nk // n, "the side job needs one block per grid step"
    res = pl.pallas_call(
        functools.partial(_norm_permute_kernel, tc=tc, n=n, n_side_in=len(s_args), side_body=s_body),
        grid=(nk // n,),
        in_specs=[pl.BlockSpec((n * tc, d), lambda i: (i, 0)),
                  pl.BlockSpec((1, d), lambda i: (0, 0))] + list(s_in),
        out_specs=[pl.BlockSpec((tc, n, d), lambda i: (0, i, 0))] + list(s_out),
        out_shape=[jax.ShapeDtypeStruct((tc, nk, d), BF16)] + list(s_shape),
        compiler_params=_params("parallel"),
    )(x2d, g.reshape(1, d), *s_args)
    return res[0], tuple(res[1:])


def _branch_a_kernel(*refs, tc, n, r, hd, n_sub, has_v, nh, fold):
    refs = list(refs)
    xn_ref, wu_ref, wv_ref, wg_ref, gv_ref, ws_ref, bmix_ref = refs[:7]
    del refs[:7]
    oa_ref = refs.pop(0)
    v_refs = [refs.pop(0)] if has_v else []
    zb_ref = refs.pop(0)
    w_ref, wm_ref, wt_ref = refs
    d = xn_ref.shape[-1]
    is_head = pl.program_id(0) < nh
    is_b = jnp.logical_not(is_head)

    @pl.when(jnp.logical_and(is_b, pl.program_id(1) == 0))
    def _():
        wt_ref[:hd, :] = wu_ref[...].T.astype(BF16)
        wt_ref[hd:, :] = wv_ref[...].T.astype(BF16)

    @pl.when(is_b)
    def _():
        js = LANE_TILE // n
        per = js // fold
        width = fold * n
        for k in range(tc // js):
            z = lax.dot_general(wt_ref[...], xn_ref[k * js:(k + 1) * js].reshape(js * n, d),
                                (((1,), (1,)), ((), ())), preferred_element_type=F32)
            for q in range(per):
                zb_ref[k * per + q] = z[:, q * width:(q + 1) * width].astype(zb_ref.dtype)

    @pl.when(jnp.logical_and(is_head, pl.program_id(1) == 0))
    def _():
        w_ref[:, :hd] = wu_ref[...].astype(BF16)
        w_ref[:, hd:2 * hd] = wv_ref[...].astype(BF16)
        w_ref[:, 2 * hd:] = wg_ref[...].astype(BF16)
        row = lax.broadcasted_iota(jnp.int32, (CHUNK, CHUNK), 0)
        col = lax.broadcasted_iota(jnp.int32, (CHUNK, CHUNK), 1)
        w_tril = jnp.where(row >= col, ws_ref[...], 0.0).astype(BF16)
        if tc == TC_PROMPT:
            time_of_row = tc * (row % r) + row // r
            same_seq = None
        else:
            time_of_row = row // r
            same_seq = (row % r) == (col % r)
        sel = (col == time_of_row).astype(BF16)
        sw = jnp.dot(sel, w_tril, preferred_element_type=F32).astype(BF16)
        wm = lax.dot_general(sw, sel, (((1,), (1,)), ((), ())), preferred_element_type=F32)
        if same_seq is not None:
            wm = jnp.where(same_seq, wm, 0.0)
        wm_ref[...] = wm.astype(BF16)

    @pl.when(is_head)
    def _():
        ns = n // n_sub
        zs = [jnp.dot(xn_ref[:, k * ns:(k + 1) * ns, :].reshape(tc * ns, d), w_ref[...],
                      preferred_element_type=F32) for k in range(n_sub)]
        wm = wm_ref[...]
        bm = bmix_ref[...]
        for k, z in enumerate(zs):
            k0 = k * ns
            u = jax.nn.gelu(z[:, :hd])
            v = jax.nn.gelu(z[:, hd:2 * hd])
            ga = z[:, 2 * hd:]
            ms = jnp.mean(v * v, axis=-1, keepdims=True)
            vn3 = (v * lax.rsqrt(ms + EPS) * gv_ref[...]).reshape(tc, ns, hd)
            pre3 = (u * jax.nn.silu(ga)).reshape(tc, ns, hd)
            if v_refs:
                v_refs[0][:, k0:k0 + ns, :] = vn3
            for c in range(ns // r):
                vg = vn3[:, c * r:(c + 1) * r, :].reshape(CHUNK, hd)
                mixed = jnp.dot(wm, vg.astype(BF16), preferred_element_type=F32) + bm
                og = pre3[:, c * r:(c + 1) * r, :].reshape(CHUNK, hd) * mixed
                oa_ref[:, k0 + c * r:k0 + (c + 1) * r, :] = og.reshape(tc, r, hd).astype(oa_ref.dtype)


def _branch_a(xn, w_in, gv, w_s, bmix, n, n_sub, want_v, fold):
    tc, nk, d = xn.shape
    nh, _, hd = gv.shape
    r = CHUNK // tc
    nt = nk // n
    n_bcols = w_in.shape[1] - 3 * nh * hd
    nb = n_bcols // (2 * hd)
    assert nb * 2 * hd == n_bcols and LANE_TILE % n == 0 and (LANE_TILE // n) % fold == 0
    head = lambda h: jnp.minimum(h, nh - 1)
    blk = lambda h: jnp.maximum(h - nh, 0)
    out_map = lambda h, i: (0, jnp.where(h < nh, i, nt - 1), head(h))
    in_specs = [pl.BlockSpec((tc, n, d), lambda h, i: (0, i, 0)),
                pl.BlockSpec((d, hd), lambda h, i: (0, jnp.where(h < nh, h, 3 * nh + 2 * (h - nh)))),
                pl.BlockSpec((d, hd), lambda h, i: (0, jnp.where(h < nh, nh + h, 3 * nh + 2 * (h - nh) + 1))),
                pl.BlockSpec((d, hd), lambda h, i: (0, 2 * nh + head(h))),
                pl.BlockSpec((None, 1, hd), lambda h, i: (head(h), 0, 0)),
                pl.BlockSpec((None, CHUNK, CHUNK), lambda h, i: (head(h), 0, 0)),
                pl.BlockSpec((None, CHUNK, hd), lambda h, i: (head(h), 0, 0))]
    args = [xn, w_in, w_in, w_in, gv, w_s, bmix]
    out_shape = [jax.ShapeDtypeStruct((tc, nk, nh * hd), BF16)]
    out_specs = [pl.BlockSpec((tc, n, hd), out_map)]
    if want_v:
        out_shape.append(jax.ShapeDtypeStruct((tc, nk, nh * hd), F32))
        out_specs.append(pl.BlockSpec((tc, n, hd), out_map))
    out_shape.append(jax.ShapeDtypeStruct((tc // fold, n_bcols, fold * nk), BF16))
    out_specs.append(pl.BlockSpec((tc // fold, 2 * hd, fold * n),
                                  lambda h, i: (0, blk(h), jnp.where(h < nh, 0, i))))
    res = list(pl.pallas_call(
        functools.partial(_branch_a_kernel, tc=tc, n=n, r=r, hd=hd, n_sub=n_sub,
                          has_v=want_v, nh=nh, fold=fold),
        grid=(nh + nb, nt),
        in_specs=in_specs,
        out_specs=out_specs,
        out_shape=out_shape,
        scratch_shapes=[pltpu.VMEM((d, 3 * hd), BF16), pltpu.VMEM((CHUNK, CHUNK), BF16),
                        pltpu.VMEM((2 * hd, d), BF16)],
        compiler_params=_params("arbitrary", "arbitrary"),
    )(*args))
    out_a = res.pop(0)
    v_rows = res.pop(0) if want_v else None
    return out_a, v_rows, res.pop(0)


def _transpose_side_blocks(in_refs, out_refs):
    for w_ref, o_ref in zip(in_refs, out_refs):
        o_ref[...] = w_ref[...].T.astype(o_ref.dtype)


def _stream_maps(counts):
    starts = [sum(counts[:s]) for s in range(len(counts))]
    maps = [lambda j, st=st, c=c: jnp.clip(j - st, 0, c - 1) for st, c in zip(starts, counts)]
    owns = [lambda j, st=st, c=c: jnp.logical_and(j >= st, j < st + c) for st, c in zip(starts, counts)]
    return maps, owns


def _proj_b_kernel(w_ref, *refs, counts, n_side):
    ns = len(counts)
    xn_refs, side_in = refs[:ns], refs[ns:ns + n_side]
    o_refs, side_out = refs[ns + n_side:2 * ns + n_side], refs[2 * ns + n_side:]
    _, owns = _stream_maps(counts)
    j = pl.program_id(0)
    for s, (xn_ref, o_ref) in enumerate(zip(xn_refs, o_refs)):
        @pl.when(owns[s](j))
        def _(s=s, xn_ref=xn_ref, o_ref=o_ref):
            if s == 0:
                _transpose_side_blocks(side_in, side_out)
            for t in range(xn_ref.shape[0] // LANE_TILE):
                ls = slice(t * LANE_TILE, (t + 1) * LANE_TILE)
                xn = xn_ref[ls, :]
                for m in range(w_ref.shape[0] // ROW_CHUNK):
                    sl = slice(m * ROW_CHUNK, (m + 1) * ROW_CHUNK)
                    o_ref[sl, ls] = lax.dot_general(
                        w_ref[sl, :], xn, (((1,), (1,)), ((), ())),
                        preferred_element_type=F32).astype(o_ref.dtype)


def _proj_b(w_bt, xns, side=()):
    rows, d = w_bt.shape
    counts = [x.shape[0] for x in xns]
    maps, _ = _stream_maps(counts)
    in_specs = [pl.BlockSpec((rows, d), lambda j: (0, 0), pipeline_mode=pl.Buffered(1))]
    in_specs += [pl.BlockSpec((None, x.shape[1], d), lambda j, m=m: (m(j), 0, 0))
                 for x, m in zip(xns, maps)]
    out_specs = [pl.BlockSpec((None, rows, x.shape[1]), lambda j, m=m: (m(j), 0, 0))
                 for x, m in zip(xns, maps)]
    out_shape = [jax.ShapeDtypeStruct((x.shape[0], rows, x.shape[1]), BF16) for x in xns]
    for w, n_rows, row_block in side:
        n_cols = w.shape[1]
        assert counts[0] * LANES == n_cols, "one 128-column block per step of the first stream set"
        in_specs.append(pl.BlockSpec((n_rows, LANES), lambda j, rb=row_block: (rb, maps[0](j))))
        out_specs.append(pl.BlockSpec((LANES, n_rows), lambda j: (maps[0](j), 0)))
        out_shape.append(jax.ShapeDtypeStruct((n_cols, n_rows), BF16))
    res = pl.pallas_call(
        functools.partial(_proj_b_kernel, counts=counts, n_side=len(side)),
        grid=(sum(counts),),
        in_specs=in_specs,
        out_specs=out_specs,
        out_shape=out_shape,
        compiler_params=_params("arbitrary"),
    )(w_bt, *xns, *[w for w, _, _ in side])
    return tuple(res[:len(xns)]), tuple(res[len(xns):])


def _shift_rows(x, d):
    if d % 8 == 0:
        return jnp.concatenate([jnp.zeros((d, x.shape[1]), x.dtype), x[:-d]], axis=0)
    row = lax.broadcasted_iota(jnp.int32, x.shape, 0)
    return jnp.where(row >= d, pltpu.roll(x, d, 0), 0.0)


def _ssm_kernel(x_ref, mp_ref, q_ref, ct_ref, dsk_ref, *rest, tc, gb, nk, fold, has_h0, n_side):
    n_h0 = 2 if has_h0 else 0
    side_in, side_out = rest[n_h0:n_h0 + n_side], rest[len(rest) - n_side:]
    rest = rest[:n_h0] + rest[n_h0 + n_side:len(rest) - n_side]
    _transpose_side_blocks(side_in, side_out)
    if has_h0:
        h0r_ref, h0i_ref, y_ref, hfr_ref, hfi_ref = rest
    else:
        y_ref, hf_ref = rest
    rows = tc * GROUP
    half = q_ref.shape[2] // 2
    b_tc = tc.bit_length() - 1

    def one_group(g):
        r0 = g * GROUP if isinstance(g, int) else pl.multiple_of(g * GROUP, GROUP)
        if fold == 1:
            xg = x_ref[:, pl.ds(r0, GROUP), :]
        else:
            xg = jnp.stack([x_ref[j // fold, pl.ds(r0, GROUP), (j % fold) * nk:(j % fold + 1) * nk]
                            for j in range(tc)])
        res = jnp.dot(mp_ref[g], xg.reshape(rows, nk), preferred_element_type=F32)
        y_local = res[:rows]
        s_re = res[rows:rows + half]
        s_im = res[rows + half:]
        ct = ct_ref[g]
        if has_h0:
            h_re, h_im = h0r_ref[g], h0i_ref[g]
            a_re, a_im = ct[:, b_tc:b_tc + 1], ct[:, POW_COLS + b_tc:POW_COLS + b_tc + 1]
            hfr_ref[g] = a_re * h_re - a_im * h_im + s_re
            hfi_ref[g] = a_re * h_im + a_im * h_re + s_im
            h_in = jnp.concatenate([h_re, h_im], axis=0)
        else:
            tiles = []
            for pair in range(nk // (2 * LANES)):
                lo, mid, hi = 2 * pair * LANES, (2 * pair + 1) * LANES, (2 * pair + 2) * LANES
                xr = jnp.concatenate([s_re[:, lo:mid], s_re[:, mid:hi]], axis=0).T
                xi = jnp.concatenate([s_im[:, lo:mid], s_im[:, mid:hi]], axis=0).T
                for s in range(SCAN_STEPS):
                    cr, ci = ct[s:s + 1], ct[SCAN_ROWS + s:SCAN_ROWS + s + 1]
                    rr, ri = _shift_rows(xr, 1 << s), _shift_rows(xi, 1 << s)
                    xr, xi = xr + cr * rr - ci * ri, xi + cr * ri + ci * rr
                hf_ref[g, 2 * pair:2 * pair + 1, :] = xr[LANES - 1:]
                hf_ref[g, 2 * pair + 1:2 * pair + 2, :] = xi[LANES - 1:]
                hr_t = _shift_rows(xr, 1).T
                hi_t = _shift_rows(xi, 1).T
                tiles.append(jnp.concatenate([hr_t[:half], hi_t[:half]], axis=0))
                tiles.append(jnp.concatenate([hr_t[half:], hi_t[half:]], axis=0))
            h_in = jnp.concatenate(tiles, axis=1)
        y = y_local + jnp.dot(q_ref[g], h_in.astype(BF16), preferred_element_type=F32)
        dsk = jnp.concatenate([dsk_ref[pl.ds(r0, GROUP), :]] * (nk // LANES), axis=1)
        y3 = y.reshape(tc, GROUP, nk) + dsk[None] * xg.astype(F32)
        ya = jax.nn.gelu(y3).astype(y_ref.dtype)
        if fold == 1:
            y_ref[:, pl.ds(r0, GROUP), :] = ya
        else:
            for j in range(tc):
                y_ref[j // fold, pl.ds(r0, GROUP), (j % fold) * nk:(j % fold + 1) * nk] = ya[j]

    def body(gi, carry):
        for k in range(GROUPS_PER_TRIP):
            one_group(gi * GROUPS_PER_TRIP + k)
        return carry

    if gb == GROUPS_PER_TRIP:
        body(0, 0)
    else:
        lax.fori_loop(0, gb // GROUPS_PER_TRIP, body, 0)


def _ssm(zb, mp, q, ct, dsk, h0, gb, fold, side=()):
    tcf, _, nkf = zb.shape
    tc, nk = tcf * fold, nkf // fold
    ng, mrows, rows = mp.shape
    p2 = mrows - rows
    cb = ng * GROUP
    has_h0 = h0 is not None
    in_specs = [pl.BlockSpec((tcf, gb * GROUP, nkf), lambda i: (0, i, 0)),
                pl.BlockSpec((gb, mrows, rows), lambda i: (i, 0, 0)),
                pl.BlockSpec((gb, rows, p2), lambda i: (i, 0, 0)),
                pl.BlockSpec((gb,) + ct.shape[1:], lambda i: (i, 0, 0)),
                pl.BlockSpec((gb * GROUP, LANES), lambda i: (i, 0))]
    args = [zb, mp, q, ct, dsk]
    if has_h0:
        state_spec = pl.BlockSpec((gb, p2 // 2, nk), lambda i: (i, 0, 0))
        in_specs += [state_spec, state_spec]
        args += list(h0)
        hf_specs = [state_spec, state_spec]
        hf_shapes = [jax.ShapeDtypeStruct((ng, p2 // 2, nk), F32)] * 2
    else:
        hf_specs = [pl.BlockSpec((gb, nk // LANES, LANES), lambda i: (i, 0, 0))]
        hf_shapes = [jax.ShapeDtypeStruct((ng, nk // LANES, LANES), F32)]
    side_specs, side_shapes = [], []
    for w, n_rows, row_block in side:
        n_cols = w.shape[1]
        assert (ng // gb) * LANES == n_cols, "one 128-column block per grid step"
        in_specs.append(pl.BlockSpec((n_rows, LANES), lambda i, rb=row_block: (rb, i)))
        args.append(w)
        side_specs.append(pl.BlockSpec((LANES, n_rows), lambda i: (i, 0)))
        side_shapes.append(jax.ShapeDtypeStruct((n_cols, n_rows), BF16))
    res = pl.pallas_call(
        functools.partial(_ssm_kernel, tc=tc, gb=gb, nk=nk, fold=fold, has_h0=has_h0,
                          n_side=len(side)),
        grid=(ng // gb,),
        in_specs=in_specs,
        out_specs=[pl.BlockSpec((tcf, gb * GROUP, nkf), lambda i: (0, i, 0))] + hf_specs + side_specs,
        out_shape=[jax.ShapeDtypeStruct((tcf, cb, nkf), BF16)] + hf_shapes + side_shapes,
        compiler_params=_params("parallel"),
    )(*args)
    n_main = len(res) - len(side)
    return res[0], tuple(res[1:n_main]), tuple(res[n_main:])


def _glu_out_kernel(wg_ref, bg_ref, wo_ref, *refs, counts, has_cast, n_tiles):
    ns = len(counts)
    refs = list(refs)
    y_refs, gate_refs = refs[:ns], refs[ns:2 * ns]
    del refs[:2 * ns]
    cast_in = refs.pop(0) if has_cast else None
    r_refs = refs[:ns]
    del refs[:ns]
    cast_out = refs.pop(0) if has_cast else None
    ob_refs = refs
    _, owns = _stream_maps(counts)
    j = pl.program_id(0)
    for s in range(ns):
        @pl.when(owns[s](j))
        def _(s=s, y_ref=y_refs[s], gate_ref=gate_refs[s], r_ref=r_refs[s]):
            if s == 0 and has_cast:
                cast_out[...] = cast_in[...].astype(cast_out.dtype)
            for t, ob_ref in enumerate(ob_refs[:n_tiles[s]]):
                ls = slice(t * LANE_TILE, (t + 1) * LANE_TILE)
                y = y_ref[:, ls]
                for m in range(wg_ref.shape[0] // ROW_CHUNK):
                    sl = slice(m * ROW_CHUNK, (m + 1) * ROW_CHUNK)
                    g = jnp.dot(wg_ref[sl, :], y, preferred_element_type=F32) + bg_ref[sl, :]
                    ob = (y_ref[sl, ls].astype(F32) * jax.nn.sigmoid(g)
                          * jax.nn.silu(gate_ref[sl, ls].astype(F32)))
                    ob_ref[sl, :] = ob.astype(ob_ref.dtype)
            for t, ob_ref in enumerate(ob_refs[:n_tiles[s]]):
                ob = ob_ref[...]
                for m in range(wo_ref.shape[0] // ROW_CHUNK):
                    sl = slice(m * ROW_CHUNK, (m + 1) * ROW_CHUNK)
                    r_ref[t * LANE_TILE:(t + 1) * LANE_TILE, sl] = jnp.dot(
                        wo_ref[sl, :], ob, preferred_element_type=F32).T


def _glu_out(y_ts, zbs, wg_t, bg, wo_t, cast_rows=None):
    cb = wg_t.shape[0]
    d = wo_t.shape[0]
    counts = [y.shape[0] for y in y_ts]
    maps, _ = _stream_maps(counts)
    const = lambda j: (0, 0)
    in_specs = [pl.BlockSpec((cb, cb), const, pipeline_mode=pl.Buffered(1)),
                pl.BlockSpec((cb, LANE_TILE), const, pipeline_mode=pl.Buffered(1)),
                pl.BlockSpec((d, cb), const, pipeline_mode=pl.Buffered(1))]
    in_specs += [pl.BlockSpec((None, cb, y.shape[2]), lambda j, m=m: (m(j), 0, 0)) for y, m in zip(y_ts, maps)]
    in_specs += [pl.BlockSpec((None, cb, y.shape[2]), lambda j, m=m: (m(j), 1, 0)) for y, m in zip(y_ts, maps)]
    args = [wg_t, bg, wo_t, *y_ts, *zbs]
    out_specs = [pl.BlockSpec((None, y.shape[2], d), lambda j, m=m: (m(j), 0, 0)) for y, m in zip(y_ts, maps)]
    out_shape = [jax.ShapeDtypeStruct((y.shape[0], y.shape[2], d), F32) for y in y_ts]
    if cast_rows is not None:
        w, n_rows = cast_rows
        assert counts[0] * LANES == n_rows, "one 128-row block per step of the first stream set"
        blk = pl.BlockSpec((LANES, w.shape[1]), lambda j: (maps[0](j), 0))
        in_specs.append(blk)
        args.append(w)
        out_specs.append(blk)
        out_shape.append(jax.ShapeDtypeStruct((n_rows, w.shape[1]), BF16))
    n_tiles = [y.shape[2] // LANE_TILE for y in y_ts]
    res = pl.pallas_call(
        functools.partial(_glu_out_kernel, counts=counts, has_cast=cast_rows is not None,
                          n_tiles=n_tiles),
        grid=(sum(counts),),
        in_specs=in_specs,
        out_specs=out_specs,
        out_shape=out_shape,
        scratch_shapes=[pltpu.VMEM((cb, LANE_TILE), BF16)] * max(n_tiles),
        compiler_params=_params("arbitrary"),
    )(*args)
    return tuple(res[:len(y_ts)]), (res[len(y_ts)] if cast_rows is not None else None)


def _out_norm_kernel(x_ref, oa_ref, r_ref, wo_ref, g_ref, y_ref, ms_ref, *, tc, n):
    ca = oa_ref.shape[-1]
    d = x_ref.shape[-1]
    nc = d // LANES
    p = jnp.dot(oa_ref[...].reshape(tc * n, ca), wo_ref[...], preferred_element_type=F32)
    pitch = ms_ref.shape[1] // n
    for j in range(tc):
        s = p[j * n:(j + 1) * n] + r_ref[j]
        for c in range(nc):
            ms_ref[c, pl.ds(j, n, stride=pitch), :] = s[:, c * LANES:(c + 1) * LANES]
    mixed = [ms_ref[c].reshape(n, pitch, LANES)[:, :tc, :].reshape(n * tc, LANES) for c in range(nc)]
    acc = x_ref[...] + jnp.concatenate(mixed, axis=1)
    ms = jnp.mean(acc * acc, axis=-1, keepdims=True)
    y_ref[...] = acc * lax.rsqrt(ms + EPS) * g_ref[...]


def _out_norm(x2d, oa, r, wo_a, gf, n):
    tc, nk, ca = oa.shape
    t, d = x2d.shape
    return pl.pallas_call(
        functools.partial(_out_norm_kernel, tc=tc, n=n),
        grid=(nk // n,),
        in_specs=[pl.BlockSpec((n * tc, d), lambda i: (i, 0)),
                  pl.BlockSpec((tc, n, ca), lambda i: (0, i, 0)),
                  pl.BlockSpec((tc, n, d), lambda i: (0, i, 0)),
                  pl.BlockSpec((ca, d), lambda i: (0, 0), pipeline_mode=pl.Buffered(1)),
                  pl.BlockSpec((1, d), lambda i: (0, 0))],
        out_specs=pl.BlockSpec((n * tc, d), lambda i: (i, 0)),
        out_shape=jax.ShapeDtypeStruct((t, d), F32),
        scratch_shapes=[pltpu.VMEM((d // LANES, n * UNPERMUTE_PITCH, LANES), F32)],
        compiler_params=_params("parallel"),
    )(x2d, oa, r, wo_a, gf.reshape(1, d))


def _discretise(a_re, a_im, log_dt):
    dt = jnp.exp(log_dt)[:, None]
    x_re, x_im = a_re * dt, a_im * dt
    n = (2.0 ** jnp.arange(N_POW, dtype=F32))[None, :, None]
    mag = jnp.exp(n * x_re[:, None, :])
    ang = n * x_im[:, None, :]
    pw_r, pw_i = mag * jnp.cos(ang), mag * jnp.sin(ang)
    lr, li = pw_r[:, 0], pw_i[:, 0]
    den = a_re * a_re + a_im * a_im
    nr, ni = lr - 1.0, li
    qr, qi = (nr * a_re + ni * a_im) / den, (ni * a_re - nr * a_im) / den
    cat = lambda a, b: jnp.concatenate([a, b], axis=-1)
    rowtab = jnp.stack([cat(lr, lr), cat(li, li), cat(qr, qi), cat(-qi, qr)], axis=1)
    zpad = jnp.zeros_like(pw_r[:, :POW_COLS - N_POW])
    lt = jnp.concatenate([pw_r, zpad, pw_i, zpad], axis=1).transpose(0, 2, 1)
    b0 = TC_PROMPT.bit_length() - 1
    sc_r, sc_i = pw_r[:, b0:b0 + SCAN_STEPS], pw_i[:, b0:b0 + SCAN_STEPS]
    zrow = jnp.zeros_like(cat(sc_r, sc_r)[:, :SCAN_ROWS - SCAN_STEPS])
    scan_rows = jnp.concatenate([cat(sc_r, sc_r), zrow, cat(sc_i, sc_i), zrow], axis=1)
    return rowtab, lt, scan_rows


def _ssm_tables_kernel(rt_ref, btr_ref, bti_ref, c2r_ref, c2i_ref, c2n_ref,
                       mp_ref, q_ref, mps_ref, qs_ref, *, gb, tc, tcs, unroll):
    rows = tc * GROUP
    rows_s = tcs * GROUP
    half = rt_ref.shape[2] // 2
    lane_k = lax.broadcasted_iota(jnp.int32, (GROUP, rows), 1)
    lane_q = lax.broadcasted_iota(jnp.int32, (1, 2 * half), 1)

    def one_group(g):
        rt = rt_ref[g]
        l_r, l_i, qu, qv = rt[0:1], rt[1:2], rt[2:3], rt[3:4]
        btr, bti = btr_ref[g], bti_ref[g]
        bb = btr * qu + bti * qv
        bs = btr * qv - bti * qu
        c2r, c2i = c2r_ref[g], c2i_ref[g]
        cur_r = jnp.ones((1, 2 * half), F32)
        cur_i = jnp.zeros((1, 2 * half), F32)
        et_blocks = []
        for n in range(tc):
            et_blocks.append(cur_r * bb + cur_i * bs)
            cur_r, cur_i = cur_r * l_r - cur_i * l_i, cur_r * l_i + cur_i * l_r
            u = jnp.where(lane_q < half, cur_r, -cur_i)
            v = jnp.where(lane_q < half, -cur_i, -cur_r)
            blk = (c2r * u + c2i * v).astype(q_ref.dtype)
            q_ref[g, n * GROUP:(n + 1) * GROUP, :] = blk
            if n < tcs:
                qs_ref[g, n * GROUP:(n + 1) * GROUP, :] = blk
        p = jnp.concatenate(et_blocks[::-1], axis=0).T
        mp_ref[g, rows:, :] = p.astype(mp_ref.dtype)
        mps_ref[g, rows_s:, :] = p[:, rows - rows_s:].astype(mps_ref.dtype)
        krow = jnp.dot(c2n_ref[g], p, precision=lax.Precision.HIGHEST, preferred_element_type=F32)
        for i in range(tc):
            sh = (tc - 1 - i) * GROUP
            blk = krow if sh == 0 else pltpu.roll(krow, rows - sh, 1)
            blk = jnp.where(lane_k < (i + 1) * GROUP, blk, 0.0)
            mp_ref[g, i * GROUP:(i + 1) * GROUP, :] = blk.astype(mp_ref.dtype)
            if i < tcs:
                mps_ref[g, i * GROUP:(i + 1) * GROUP, :] = blk[:, :rows_s].astype(mps_ref.dtype)

    def body(gi, carry):
        for k in range(unroll):
            one_group(gi * unroll + k)
        return carry

    if unroll == gb:
        body(0, 0)
    else:
        lax.fori_loop(0, gb // unroll, body, 0)


def _ssm_tables_job(rowtab, b_re, b_im, c_re, c_im, tcs, gb):
    ng, _, p2 = rowtab.shape
    tc = TC_PROMPT
    rows, rows_s = tc * GROUP, tcs * GROUP
    dup = lambda a: jnp.concatenate([a, a], axis=-1)
    args = (rowtab, dup(b_re.transpose(0, 2, 1)), dup(b_im.transpose(0, 2, 1)),
            dup(c_re), dup(c_im), jnp.concatenate([c_re, -c_im], axis=-1))
    shapes = [(ng, rows + p2, rows), (ng, rows, p2), (ng, rows_s + p2, rows_s), (ng, rows_s, p2)]
    in_specs = [pl.BlockSpec((gb,) + a.shape[1:], lambda i: (i, 0, 0)) for a in args]
    out_specs = [pl.BlockSpec((gb,) + s[1:], lambda i: (i, 0, 0)) for s in shapes]
    out_shape = [jax.ShapeDtypeStruct(s, BF16) for s in shapes]
    body = functools.partial(_ssm_tables_kernel, gb=gb, tc=tc, tcs=tcs, unroll=gb)
    return args, in_specs, out_specs, out_shape, body, ng // gb


def _mix_bias(b_s, hd, tc):
    nh = b_s.shape[0]
    r = CHUNK // tc
    if tc == TC_PROMPT:
        bm = b_s.reshape(nh, r, tc).transpose(0, 2, 1).reshape(nh, CHUNK)
    else:
        bm = jnp.repeat(b_s[:, :tc], r, axis=1)
    return jnp.broadcast_to(bm[:, :, None], (nh, CHUNK, hd))


def _tiles(tc, nk):
    fold = max(1, LANE_TILE // nk)
    return NORM_ROWS // tc, min(BRANCH_A_ROWS, tc * nk) // tc, fold


def kernel(x_prompt, x_sample, state_ssm_re, state_ssm_im, g_norm, w_in, g_v, w_s, b_s,
           a_re, a_im, log_dt, b_re, b_im, c_re, c_im, d_skip, w_glu, b_glu, w_out, g_final):
    depth = g_norm.shape[0]
    assert depth == 1, "single-layer step"
    bsz, seq, d = x_prompt.shape
    dbsz, dseq, _ = x_sample.shape
    nh = w_s.shape[1]
    ng, npst = a_re.shape[1:]
    cb = ng * GROUP
    ca = (w_in.shape[2] - 2 * cb) // 3
    hd = ca // nh
    assert seq // TC_PROMPT == LANES and dbsz == LANES and CHUNK % dseq == 0
    assert dseq & (dseq - 1) == 0 and dseq <= TC_PROMPT, "sample chunk must be a power of two"
    assert ca == cb == d

    l = 0
    gv = g_v[l].reshape(nh, 1, hd)
    rowtab, lt, scan_rows = _discretise(a_re[l], a_im[l], log_dt[l])

    assert bsz % 2 == 0
    tcs = (TC_PROMPT, dseq)
    nks = (bsz * seq // TC_PROMPT, dbsz)
    xs = (x_prompt.reshape(bsz * seq, d), x_sample.reshape(dbsz * dseq, d))
    tiles = [_tiles(tc, nk) for tc, nk in zip(tcs, nks)]
    tables_job = _ssm_tables_job(rowtab, b_re[l], b_im[l], c_re[l], c_im[l], dseq,
                                 ng // (nks[0] // tiles[0][0]))
    out_as, v_rows, zbs = [], [], []
    tabs = None
    folds = [t[2] for t in tiles]
    for s, (x2d, tc, (n_norm, n_a, fold)) in enumerate(zip(xs, tcs, tiles)):
        xn, made = _norm_permute(x2d, g_norm[l], tc, n_norm, tables_job if s == 0 else None)
        tabs = made if s == 0 else tabs
        out_a, v, zb = _branch_a(xn, w_in[l], gv, w_s[l], _mix_bias(b_s[l], hd, tc), n_a,
                                 tc * n_a // PROJ_ROWS, s == 1, fold)
        out_as.append(out_a)
        v_rows.append(v)
        zbs.append(zb)
    mp, q, mp_s, q_s = tabs
    dsk = jnp.broadcast_to(d_skip[l][:, None], (cb, LANES))
    h0 = (state_ssm_re[l].transpose(1, 2, 0), state_ssm_im[l].transpose(1, 2, 0))
    y_p_t, hf_p, (wg_t, wo_bt) = _ssm(zbs[0], mp, q, scan_rows, dsk, None, SSM_GROUPS_PER_STEP,
                                      folds[0], ((w_glu[l], cb, 0), (w_out[l], cb, 1)))
    y_s_t, hf_s, _ = _ssm(zbs[1], mp_s, q_s, lt, dsk, h0, SSM_GROUPS_PER_STEP_1CHUNK, folds[1])
    bg = jnp.broadcast_to(b_glu[l][:, None], (cb, LANE_TILE))
    rs, wo_a = _glu_out((y_p_t, y_s_t), zbs, wg_t, bg, wo_bt, (w_out[l], ca))
    y_p, y_s = [_out_norm(x2d, out_a, r.reshape(tc, nk, d), wo_a, g_final, OUT_ROWS // tc)
                for x2d, out_a, r, tc, nk in zip(xs, out_as, rs, tcs, nks)]
    v_s = v_rows[1]

    hf_p = hf_p[0].reshape(ng, bsz // 2, 2, 2, npst)
    re_p = hf_p[:, :, 0].reshape(ng, bsz, npst).transpose(1, 0, 2)[None]
    im_p = hf_p[:, :, 1].reshape(ng, bsz, npst).transpose(1, 0, 2)[None]
    re_s = hf_s[0].transpose(2, 0, 1)[None]
    im_s = hf_s[1].transpose(2, 0, 1)[None]
    v_out = v_s.transpose(1, 0, 2)[None]

    dt_p, dt_s = x_prompt.dtype, x_sample.dtype
    return (y_p.reshape(bsz, seq, d).astype(dt_p), y_s.reshape(dbsz, dseq, d).astype(dt_s),
            re_p.astype(dt_p), im_p.astype(dt_p), re_s.astype(dt_s), im_s.astype(dt_s),
            v_out.astype(dt_s))
```

```python
import functools

import jax
import jax.numpy as jnp
from jax import lax
from jax.experimental import pallas as pl
from jax.experimental.pallas import tpu as pltpu

EPS = 1e-6
LANES = 128
CHUNK = 128
GROUP = 16
TC_PROMPT = 16
ROW_CHUNK = 512
LANE_TILE = 256
PROJ_ROWS = 512
BRANCH_A_ROWS = 2048
NORM_ROWS = 512
OUT_ROWS = 512
UNPERMUTE_PITCH = 24
SSM_GROUPS_PER_STEP = 8
SSM_GROUPS_PER_STEP_1CHUNK = 32
POW_COLS = 16
SCAN_STEPS = 7
N_POW = 11
SCAN_ROWS = 8
GROUPS_PER_TRIP = 8
VMEM_LIMIT = 56 * 1024 * 1024

F32 = jnp.float32
BF16 = jnp.bfloat16


def _params(*sem):
    return pltpu.CompilerParams(dimension_semantics=sem, vmem_limit_bytes=VMEM_LIMIT)


def _norm_permute_kernel(x_ref, g_ref, *refs, tc, n, n_side_in, side_body):
    o_ref = refs[n_side_in]
    x = x_ref[...]
    ms = jnp.mean(x * x, axis=-1, keepdims=True)
    xn = (x * lax.rsqrt(ms + EPS) * g_ref[...]).astype(BF16)
    rows = n * tc
    dst = lax.broadcasted_iota(jnp.int32, (rows, rows), 0)
    src = lax.broadcasted_iota(jnp.int32, (rows, rows), 1)
    sel = (src == tc * (dst % n) + dst // n).astype(BF16)
    o_ref[...] = jnp.dot(sel, xn, preferred_element_type=F32).astype(o_ref.dtype).reshape(o_ref.shape)
    if side_body is not None:
        side_body(*refs[:n_side_in], *refs[n_side_in + 1:])


def _norm_permute(x2d, g, tc, n, side_job=None):
    t, d = x2d.shape
    nk = t // tc
    s_args, s_in, s_out, s_shape, s_body, s_steps = side_job or ((), [], [], [], None, nk // n)
    assert s_steps == nk // n, "the side job needs one block per grid step"
    res = pl.pallas_call(
        functools.partial(_norm_permute_kernel, tc=tc, n=n, n_side_in=len(s_args), side_body=s_body),
        grid=(nk // n,),
        in_specs=[pl.BlockSpec((n * tc, d), lambda i: (i, 0)),
                  pl.BlockSpec((1, d), lambda i: (0, 0))] + list(s_in),
        out_specs=[pl.BlockSpec((tc, n, d), lambda i: (0, i, 0))] + list(s_out),
        out_shape=[jax.ShapeDtypeStruct((tc, nk, d), BF16)] + list(s_shape),
        compiler_params=_params("parallel"),
    )(x2d, g.reshape(1, d), *s_args)
    return res[0], tuple(res[1:])


def _branch_a_kernel(*refs, tc, n, r, hd, n_sub, has_v, has_side):
    refs = list(refs)
    xn_ref, wu_ref, wv_ref, wg_ref, gv_ref, ws_ref, bmix_ref = refs[:7]
    del refs[:7]
    side_in = [refs.pop(0)] if has_side else []
    oa_ref = refs.pop(0)
    v_refs = [refs.pop(0)] if has_v else []
    side_out = [refs.pop(0)] if has_side else []
    w_ref, wm_ref = refs
    d = xn_ref.shape[-1]

    @pl.when(pl.program_id(1) == 0)
    def _():
        w_ref[:, :hd] = wu_ref[...].astype(BF16)
        w_ref[:, hd:2 * hd] = wv_ref[...].astype(BF16)
        w_ref[:, 2 * hd:] = wg_ref[...].astype(BF16)
        row = lax.broadcasted_iota(jnp.int32, (CHUNK, CHUNK), 0)
        col = lax.broadcasted_iota(jnp.int32, (CHUNK, CHUNK), 1)
        w_tril = jnp.where(row >= col, ws_ref[...], 0.0).astype(BF16)
        if tc == TC_PROMPT:
            time_of_row = tc * (row % r) + row // r
            same_seq = None
        else:
            time_of_row = row // r
            same_seq = (row % r) == (col % r)
        sel = (col == time_of_row).astype(BF16)
        sw = jnp.dot(sel, w_tril, preferred_element_type=F32).astype(BF16)
        wm = lax.dot_general(sw, sel, (((1,), (1,)), ((), ())), preferred_element_type=F32)
        if same_seq is not None:
            wm = jnp.where(same_seq, wm, 0.0)
        wm_ref[...] = wm.astype(BF16)

    ns = n // n_sub
    zs = [jnp.dot(xn_ref[:, k * ns:(k + 1) * ns, :].reshape(tc * ns, d), w_ref[...],
                  preferred_element_type=F32) for k in range(n_sub)]
    wm = wm_ref[...]
    bm = bmix_ref[...]
    for k, z in enumerate(zs):
        k0 = k * ns
        u = jax.nn.gelu(z[:, :hd])
        v = jax.nn.gelu(z[:, hd:2 * hd])
        ga = z[:, 2 * hd:]
        ms = jnp.mean(v * v, axis=-1, keepdims=True)
        vn3 = (v * lax.rsqrt(ms + EPS) * gv_ref[...]).reshape(tc, ns, hd)
        pre3 = (u * jax.nn.silu(ga)).reshape(tc, ns, hd)
        if v_refs:
            v_refs[0][:, k0:k0 + ns, :] = vn3
        for c in range(ns // r):
            vg = vn3[:, c * r:(c + 1) * r, :].reshape(CHUNK, hd)
            mixed = jnp.dot(wm, vg.astype(BF16), preferred_element_type=F32) + bm
            og = pre3[:, c * r:(c + 1) * r, :].reshape(CHUNK, hd) * mixed
            oa_ref[:, k0 + c * r:k0 + (c + 1) * r, :] = og.reshape(tc, r, hd).astype(oa_ref.dtype)
    _transpose_side_blocks(side_in, side_out)


def _branch_a(xn, w_in, gv, w_s, bmix, n, n_sub, want_v, transpose_b):
    tc, nk, d = xn.shape
    nh, _, hd = gv.shape
    r = CHUNK // tc
    nt = nk // n
    out_map = lambda h, i: (0, i, h)
    in_specs = [pl.BlockSpec((tc, n, d), lambda h, i: (0, i, 0)),
                pl.BlockSpec((d, hd), lambda h, i: (0, h)),
                pl.BlockSpec((d, hd), lambda h, i: (0, nh + h)),
                pl.BlockSpec((d, hd), lambda h, i: (0, 2 * nh + h)),
                pl.BlockSpec((None, 1, hd), lambda h, i: (h, 0, 0)),
                pl.BlockSpec((None, CHUNK, CHUNK), lambda h, i: (h, 0, 0)),
                pl.BlockSpec((None, CHUNK, hd), lambda h, i: (h, 0, 0))]
    args = [xn, w_in, w_in, w_in, gv, w_s, bmix]
    out_shape = [jax.ShapeDtypeStruct((tc, nk, nh * hd), BF16)]
    out_specs = [pl.BlockSpec((tc, n, hd), out_map)]
    if want_v:
        out_shape.append(jax.ShapeDtypeStruct((tc, nk, nh * hd), F32))
        out_specs.append(pl.BlockSpec((tc, n, hd), out_map))
    if transpose_b:
        col0 = 3 * nh * hd // LANES
        n_cols = w_in.shape[1] - 3 * nh * hd
        assert nh * nt * LANES == n_cols, "one 128-column block per grid step"
        in_specs.append(pl.BlockSpec((d, LANES), lambda h, i: (0, col0 + h * nt + i)))
        args.append(w_in)
        out_shape.append(jax.ShapeDtypeStruct((n_cols, d), BF16))
        out_specs.append(pl.BlockSpec((LANES, d), lambda h, i: (h * nt + i, 0)))
    res = list(pl.pallas_call(
        functools.partial(_branch_a_kernel, tc=tc, n=n, r=r, hd=hd, n_sub=n_sub,
                          has_v=want_v, has_side=transpose_b),
        grid=(nh, nt),
        in_specs=in_specs,
        out_specs=out_specs,
        out_shape=out_shape,
        scratch_shapes=[pltpu.VMEM((d, 3 * hd), BF16), pltpu.VMEM((CHUNK, CHUNK), BF16)],
        compiler_params=_params("parallel", "arbitrary"),
    )(*args))
    out_a = res.pop(0)
    v_rows = res.pop(0) if want_v else None
    w_bt = res.pop(0) if transpose_b else None
    return out_a, v_rows, w_bt


def _transpose_side_blocks(in_refs, out_refs):
    for w_ref, o_ref in zip(in_refs, out_refs):
        o_ref[...] = w_ref[...].T.astype(o_ref.dtype)


def _stream_maps(counts):
    starts = [sum(counts[:s]) for s in range(len(counts))]
    maps = [lambda j, st=st, c=c: jnp.clip(j - st, 0, c - 1) for st, c in zip(starts, counts)]
    owns = [lambda j, st=st, c=c: jnp.logical_and(j >= st, j < st + c) for st, c in zip(starts, counts)]
    return maps, owns


def _proj_b_kernel(w_ref, *refs, counts, n_side):
    ns = len(counts)
    xn_refs, side_in = refs[:ns], refs[ns:ns + n_side]
    o_refs, side_out = refs[ns + n_side:2 * ns + n_side], refs[2 * ns + n_side:]
    _, owns = _stream_maps(counts)
    j = pl.program_id(0)
    for s, (xn_ref, o_ref) in enumerate(zip(xn_refs, o_refs)):
        @pl.when(owns[s](j))
        def _(s=s, xn_ref=xn_ref, o_ref=o_ref):
            if s == 0:
                _transpose_side_blocks(side_in, side_out)
            for t in range(xn_ref.shape[0] // LANE_TILE):
                ls = slice(t * LANE_TILE, (t + 1) * LANE_TILE)
                xn = xn_ref[ls, :]
                for m in range(w_ref.shape[0] // ROW_CHUNK):
                    sl = slice(m * ROW_CHUNK, (m + 1) * ROW_CHUNK)
                    o_ref[sl, ls] = lax.dot_general(
                        w_ref[sl, :], xn, (((1,), (1,)), ((), ())),
                        preferred_element_type=F32).astype(o_ref.dtype)


def _proj_b(w_bt, xns, side=()):
    rows, d = w_bt.shape
    counts = [x.shape[0] for x in xns]
    maps, _ = _stream_maps(counts)
    in_specs = [pl.BlockSpec((rows, d), lambda j: (0, 0), pipeline_mode=pl.Buffered(1))]
    in_specs += [pl.BlockSpec((None, x.shape[1], d), lambda j, m=m: (m(j), 0, 0))
                 for x, m in zip(xns, maps)]
    out_specs = [pl.BlockSpec((None, rows, x.shape[1]), lambda j, m=m: (m(j), 0, 0))
                 for x, m in zip(xns, maps)]
    out_shape = [jax.ShapeDtypeStruct((x.shape[0], rows, x.shape[1]), BF16) for x in xns]
    for w, n_rows, row_block in side:
        n_cols = w.shape[1]
        assert counts[0] * LANES == n_cols, "one 128-column block per step of the first stream set"
        in_specs.append(pl.BlockSpec((n_rows, LANES), lambda j, rb=row_block: (rb, maps[0](j))))
        out_specs.append(pl.BlockSpec((LANES, n_rows), lambda j: (maps[0](j), 0)))
        out_shape.append(jax.ShapeDtypeStruct((n_cols, n_rows), BF16))
    res = pl.pallas_call(
        functools.partial(_proj_b_kernel, counts=counts, n_side=len(side)),
        grid=(sum(counts),),
        in_specs=in_specs,
        out_specs=out_specs,
        out_shape=out_shape,
        compiler_params=_params("arbitrary"),
    )(w_bt, *xns, *[w for w, _, _ in side])
    return tuple(res[:len(xns)]), tuple(res[len(xns):])


def _shift_rows(x, d):
    if d % 8 == 0:
        return jnp.concatenate([jnp.zeros((d, x.shape[1]), x.dtype), x[:-d]], axis=0)
    row = lax.broadcasted_iota(jnp.int32, x.shape, 0)
    return jnp.where(row >= d, pltpu.roll(x, d, 0), 0.0)


def _ssm_kernel(x_ref, mp_ref, q_ref, ct_ref, dsk_ref, *rest, tc, gb, nk, fold, has_h0):
    rows = tc * GROUP
    half = q_ref.shape[2] // 2
    if has_h0:
        h0r_io, h0i_io, y_ref, hfr_io, hfi_io, h0r_ref, h0i_ref, hfr_ref, hfi_ref = rest
        for t in range(gb * half // LANES):
            ts = slice(t * LANES, (t + 1) * LANES)
            h0r_ref[ts, :] = h0r_io[:, ts].T
            h0i_ref[ts, :] = h0i_io[:, ts].T
    else:
        y_ref, hf_ref = rest
    b_tc = tc.bit_length() - 1

    def one_group(g):
        r0 = g * GROUP if isinstance(g, int) else pl.multiple_of(g * GROUP, GROUP)
        if fold == 1:
            xg = x_ref[:, pl.ds(r0, GROUP), :]
        else:
            xg = jnp.stack([x_ref[j // fold, pl.ds(r0, GROUP), (j % fold) * nk:(j % fold + 1) * nk]
                            for j in range(tc)])
        res = jnp.dot(mp_ref[g], xg.reshape(rows, nk), preferred_element_type=F32)
        y_local = res[:rows]
        s_re = res[rows:rows + half]
        s_im = res[rows + half:]
        ct = ct_ref[g]
        if has_h0:
            ps = pl.ds(g * half if isinstance(g, int) else pl.multiple_of(g * half, half), half)
            h_re, h_im = h0r_ref[ps, :], h0i_ref[ps, :]
            a_re, a_im = ct[:, b_tc:b_tc + 1], ct[:, POW_COLS + b_tc:POW_COLS + b_tc + 1]
            hfr_ref[ps, :] = a_re * h_re - a_im * h_im + s_re
            hfi_ref[ps, :] = a_re * h_im + a_im * h_re + s_im
            h_in = jnp.concatenate([h_re, h_im], axis=0)
        else:
            tiles = []
            for pair in range(nk // (2 * LANES)):
                lo, mid, hi = 2 * pair * LANES, (2 * pair + 1) * LANES, (2 * pair + 2) * LANES
                xr = jnp.concatenate([s_re[:, lo:mid], s_re[:, mid:hi]], axis=0).T
                xi = jnp.concatenate([s_im[:, lo:mid], s_im[:, mid:hi]], axis=0).T
                for s in range(SCAN_STEPS):
                    cr, ci = ct[s:s + 1], ct[SCAN_ROWS + s:SCAN_ROWS + s + 1]
                    rr, ri = _shift_rows(xr, 1 << s), _shift_rows(xi, 1 << s)
                    xr, xi = xr + cr * rr - ci * ri, xi + cr * ri + ci * rr
                hf_ref[g, 2 * pair:2 * pair + 1, :] = xr[LANES - 1:]
                hf_ref[g, 2 * pair + 1:2 * pair + 2, :] = xi[LANES - 1:]
                hr_t = _shift_rows(xr, 1).T
                hi_t = _shift_rows(xi, 1).T
                tiles.append(jnp.concatenate([hr_t[:half], hi_t[:half]], axis=0))
                tiles.append(jnp.concatenate([hr_t[half:], hi_t[half:]], axis=0))
            h_in = jnp.concatenate(tiles, axis=1)
        y = y_local + jnp.dot(q_ref[g], h_in.astype(BF16), preferred_element_type=F32)
        dsk = jnp.concatenate([dsk_ref[pl.ds(r0, GROUP), :]] * (nk // LANES), axis=1)
        y3 = y.reshape(tc, GROUP, nk) + dsk[None] * xg.astype(F32)
        ya = jax.nn.gelu(y3).astype(y_ref.dtype)
        if fold == 1:
            y_ref[:, pl.ds(r0, GROUP), :] = ya
        else:
            for j in range(tc):
                y_ref[j // fold, pl.ds(r0, GROUP), (j % fold) * nk:(j % fold + 1) * nk] = ya[j]

    def body(gi, carry):
        for k in range(GROUPS_PER_TRIP):
            one_group(gi * GROUPS_PER_TRIP + k)
        return carry

    if gb == GROUPS_PER_TRIP:
        body(0, 0)
    else:
        lax.fori_loop(0, gb // GROUPS_PER_TRIP, body, 0)
    if has_h0:
        for t in range(gb * half // LANES):
            ts = slice(t * LANES, (t + 1) * LANES)
            hfr_io[:, ts] = hfr_ref[ts, :].T
            hfi_io[:, ts] = hfi_ref[ts, :].T


def _ssm(zb, mp, q, ct, dsk, h0, gb, fold):
    tcf, _, nkf = zb.shape
    tc, nk = tcf * fold, nkf // fold
    ng, mrows, rows = mp.shape
    p2 = mrows - rows
    cb = ng * GROUP
    has_h0 = h0 is not None
    in_specs = [pl.BlockSpec((tcf, gb * GROUP, nkf), lambda i: (0, i, 0)),
                pl.BlockSpec((gb, mrows, rows), lambda i: (i, 0, 0)),
                pl.BlockSpec((gb, rows, p2), lambda i: (i, 0, 0)),
                pl.BlockSpec((gb,) + ct.shape[1:], lambda i: (i, 0, 0)),
                pl.BlockSpec((gb * GROUP, LANES), lambda i: (i, 0))]
    args = [zb, mp, q, ct, dsk]
    scratch = []
    if has_h0:
        half = p2 // 2
        assert nk == LANES and (gb * half) % LANES == 0
        state_spec = pl.BlockSpec((nk, gb * half), lambda i: (0, i))
        in_specs += [state_spec, state_spec]
        args += list(h0)
        hf_specs = [state_spec, state_spec]
        hf_shapes = [jax.ShapeDtypeStruct((nk, ng * half), F32)] * 2
        scratch = [pltpu.VMEM((gb * half, nk), F32)] * 4
    else:
        hf_specs = [pl.BlockSpec((gb, nk // LANES, LANES), lambda i: (i, 0, 0))]
        hf_shapes = [jax.ShapeDtypeStruct((ng, nk // LANES, LANES), F32)]
    res = pl.pallas_call(
        functools.partial(_ssm_kernel, tc=tc, gb=gb, nk=nk, fold=fold, has_h0=has_h0),
        grid=(ng // gb,),
        in_specs=in_specs,
        out_specs=[pl.BlockSpec((tcf, gb * GROUP, nkf), lambda i: (0, i, 0))] + hf_specs,
        out_shape=[jax.ShapeDtypeStruct((tcf, cb, nkf), BF16)] + hf_shapes,
        scratch_shapes=scratch,
        compiler_params=_params("parallel"),
    )(*args)
    return res[0], tuple(res[1:])


def _glu_out_kernel(wg_ref, bg_ref, wo_ref, *refs, counts, has_cast, n_tiles):
    ns = len(counts)
    refs = list(refs)
    y_refs, gate_refs = refs[:ns], refs[ns:2 * ns]
    del refs[:2 * ns]
    cast_in = refs.pop(0) if has_cast else None
    r_refs = refs[:ns]
    del refs[:ns]
    cast_out = refs.pop(0) if has_cast else None
    ob_refs = refs
    _, owns = _stream_maps(counts)
    j = pl.program_id(0)
    for s in range(ns):
        @pl.when(owns[s](j))
        def _(s=s, y_ref=y_refs[s], gate_ref=gate_refs[s], r_ref=r_refs[s]):
            if s == 0 and has_cast:
                cast_out[...] = cast_in[...].astype(cast_out.dtype)
            for t, ob_ref in enumerate(ob_refs[:n_tiles[s]]):
                ls = slice(t * LANE_TILE, (t + 1) * LANE_TILE)
                y = y_ref[:, ls]
                for m in range(wg_ref.shape[0] // ROW_CHUNK):
                    sl = slice(m * ROW_CHUNK, (m + 1) * ROW_CHUNK)
                    g = jnp.dot(wg_ref[sl, :], y, preferred_element_type=F32) + bg_ref[sl, :]
                    ob = (y_ref[sl, ls].astype(F32) * jax.nn.sigmoid(g)
                          * jax.nn.silu(gate_ref[sl, ls].astype(F32)))
                    ob_ref[sl, :] = ob.astype(ob_ref.dtype)
            for t, ob_ref in enumerate(ob_refs[:n_tiles[s]]):
                ob = ob_ref[...]
                for m in range(wo_ref.shape[0] // ROW_CHUNK):
                    sl = slice(m * ROW_CHUNK, (m + 1) * ROW_CHUNK)
                    r_ref[t * LANE_TILE:(t + 1) * LANE_TILE, sl] = jnp.dot(
                        wo_ref[sl, :], ob, preferred_element_type=F32).T


def _glu_out(y_ts, zbs, wg_t, bg, wo_t, cast_rows=None):
    cb = wg_t.shape[0]
    d = wo_t.shape[0]
    counts = [y.shape[0] for y in y_ts]
    maps, _ = _stream_maps(counts)
    const = lambda j: (0, 0)
    in_specs = [pl.BlockSpec((cb, cb), const, pipeline_mode=pl.Buffered(1)),
                pl.BlockSpec((cb, LANE_TILE), const, pipeline_mode=pl.Buffered(1)),
                pl.BlockSpec((d, cb), const, pipeline_mode=pl.Buffered(1))]
    in_specs += [pl.BlockSpec((None, cb, y.shape[2]), lambda j, m=m: (m(j), 0, 0)) for y, m in zip(y_ts, maps)]
    in_specs += [pl.BlockSpec((None, cb, y.shape[2]), lambda j, m=m: (m(j), 1, 0)) for y, m in zip(y_ts, maps)]
    args = [wg_t, bg, wo_t, *y_ts, *zbs]
    out_specs = [pl.BlockSpec((None, y.shape[2], d), lambda j, m=m: (m(j), 0, 0)) for y, m in zip(y_ts, maps)]
    out_shape = [jax.ShapeDtypeStruct((y.shape[0], y.shape[2], d), F32) for y in y_ts]
    if cast_rows is not None:
        w, n_rows = cast_rows
        assert counts[0] * LANES == n_rows, "one 128-row block per step of the first stream set"
        blk = pl.BlockSpec((LANES, w.shape[1]), lambda j: (maps[0](j), 0))
        in_specs.append(blk)
        args.append(w)
        out_specs.append(blk)
        out_shape.append(jax.ShapeDtypeStruct((n_rows, w.shape[1]), BF16))
    n_tiles = [y.shape[2] // LANE_TILE for y in y_ts]
    res = pl.pallas_call(
        functools.partial(_glu_out_kernel, counts=counts, has_cast=cast_rows is not None,
                          n_tiles=n_tiles),
        grid=(sum(counts),),
        in_specs=in_specs,
        out_specs=out_specs,
        out_shape=out_shape,
        scratch_shapes=[pltpu.VMEM((cb, LANE_TILE), BF16)] * max(n_tiles),
        compiler_params=_params("arbitrary"),
    )(*args)
    return tuple(res[:len(y_ts)]), (res[len(y_ts)] if cast_rows is not None else None)


def _out_norm_kernel(x_ref, oa_ref, r_ref, wo_ref, g_ref, y_ref, ms_ref, *, tc, n):
    ca = oa_ref.shape[-1]
    d = x_ref.shape[-1]
    nc = d // LANES
    p = jnp.dot(oa_ref[...].reshape(tc * n, ca), wo_ref[...], preferred_element_type=F32)
    pitch = ms_ref.shape[1] // n
    for j in range(tc):
        s = p[j * n:(j + 1) * n] + r_ref[j]
        for c in range(nc):
            ms_ref[c, pl.ds(j, n, stride=pitch), :] = s[:, c * LANES:(c + 1) * LANES]
    mixed = [ms_ref[c].reshape(n, pitch, LANES)[:, :tc, :].reshape(n * tc, LANES) for c in range(nc)]
    acc = x_ref[...] + jnp.concatenate(mixed, axis=1)
    ms = jnp.mean(acc * acc, axis=-1, keepdims=True)
    y_ref[...] = acc * lax.rsqrt(ms + EPS) * g_ref[...]


def _out_norm(x2d, oa, r, wo_a, gf, n):
    tc, nk, ca = oa.shape
    t, d = x2d.shape
    return pl.pallas_call(
        functools.partial(_out_norm_kernel, tc=tc, n=n),
        grid=(nk // n,),
        in_specs=[pl.BlockSpec((n * tc, d), lambda i: (i, 0)),
                  pl.BlockSpec((tc, n, ca), lambda i: (0, i, 0)),
                  pl.BlockSpec((tc, n, d), lambda i: (0, i, 0)),
                  pl.BlockSpec((ca, d), lambda i: (0, 0), pipeline_mode=pl.Buffered(1)),
                  pl.BlockSpec((1, d), lambda i: (0, 0))],
        out_specs=pl.BlockSpec((n * tc, d), lambda i: (i, 0)),
        out_shape=jax.ShapeDtypeStruct((t, d), F32),
        scratch_shapes=[pltpu.VMEM((d // LANES, n * UNPERMUTE_PITCH, LANES), F32)],
        compiler_params=_params("parallel"),
    )(x2d, oa, r, wo_a, gf.reshape(1, d))


def _discretise(a_re, a_im, log_dt):
    dt = jnp.exp(log_dt)[:, None]
    x_re, x_im = a_re * dt, a_im * dt
    n = (2.0 ** jnp.arange(N_POW, dtype=F32))[None, :, None]
    mag = jnp.exp(n * x_re[:, None, :])
    ang = n * x_im[:, None, :]
    pw_r, pw_i = mag * jnp.cos(ang), mag * jnp.sin(ang)
    lr, li = pw_r[:, 0], pw_i[:, 0]
    den = a_re * a_re + a_im * a_im
    nr, ni = lr - 1.0, li
    qr, qi = (nr * a_re + ni * a_im) / den, (ni * a_re - nr * a_im) / den
    cat = lambda a, b: jnp.concatenate([a, b], axis=-1)
    rowtab = jnp.stack([cat(lr, lr), cat(li, li), cat(qr, qi), cat(-qi, qr)], axis=1)
    zpad = jnp.zeros_like(pw_r[:, :POW_COLS - N_POW])
    lt = jnp.concatenate([pw_r, zpad, pw_i, zpad], axis=1).transpose(0, 2, 1)
    b0 = TC_PROMPT.bit_length() - 1
    sc_r, sc_i = pw_r[:, b0:b0 + SCAN_STEPS], pw_i[:, b0:b0 + SCAN_STEPS]
    zrow = jnp.zeros_like(cat(sc_r, sc_r)[:, :SCAN_ROWS - SCAN_STEPS])
    scan_rows = jnp.concatenate([cat(sc_r, sc_r), zrow, cat(sc_i, sc_i), zrow], axis=1)
    return rowtab, lt, scan_rows


def _ssm_tables_kernel(rt_ref, btr_ref, bti_ref, c2r_ref, c2i_ref, c2n_ref,
                       mp_ref, q_ref, mps_ref, qs_ref, *, gb, tc, tcs, unroll):
    rows = tc * GROUP
    rows_s = tcs * GROUP
    half = rt_ref.shape[2] // 2
    lane_k = lax.broadcasted_iota(jnp.int32, (GROUP, rows), 1)
    lane_q = lax.broadcasted_iota(jnp.int32, (1, 2 * half), 1)

    def one_group(g):
        rt = rt_ref[g]
        l_r, l_i, qu, qv = rt[0:1], rt[1:2], rt[2:3], rt[3:4]
        btr, bti = btr_ref[g], bti_ref[g]
        bb = btr * qu + bti * qv
        bs = btr * qv - bti * qu
        c2r, c2i = c2r_ref[g], c2i_ref[g]
        cur_r = jnp.ones((1, 2 * half), F32)
        cur_i = jnp.zeros((1, 2 * half), F32)
        et_blocks = []
        for n in range(tc):
            et_blocks.append(cur_r * bb + cur_i * bs)
            cur_r, cur_i = cur_r * l_r - cur_i * l_i, cur_r * l_i + cur_i * l_r
            u = jnp.where(lane_q < half, cur_r, -cur_i)
            v = jnp.where(lane_q < half, -cur_i, -cur_r)
            blk = (c2r * u + c2i * v).astype(q_ref.dtype)
            q_ref[g, n * GROUP:(n + 1) * GROUP, :] = blk
            if n < tcs:
                qs_ref[g, n * GROUP:(n + 1) * GROUP, :] = blk
        p = jnp.concatenate(et_blocks[::-1], axis=0).T
        mp_ref[g, rows:, :] = p.astype(mp_ref.dtype)
        mps_ref[g, rows_s:, :] = p[:, rows - rows_s:].astype(mps_ref.dtype)
        krow = jnp.dot(c2n_ref[g], p, precision=lax.Precision.HIGHEST, preferred_element_type=F32)
        for i in range(tc):
            sh = (tc - 1 - i) * GROUP
            blk = krow if sh == 0 else pltpu.roll(krow, rows - sh, 1)
            blk = jnp.where(lane_k < (i + 1) * GROUP, blk, 0.0)
            mp_ref[g, i * GROUP:(i + 1) * GROUP, :] = blk.astype(mp_ref.dtype)
            if i < tcs:
                mps_ref[g, i * GROUP:(i + 1) * GROUP, :] = blk[:, :rows_s].astype(mps_ref.dtype)

    def body(gi, carry):
        for k in range(unroll):
            one_group(gi * unroll + k)
        return carry

    if unroll == gb:
        body(0, 0)
    else:
        lax.fori_loop(0, gb // unroll, body, 0)


def _ssm_tables_job(rowtab, b_re, b_im, c_re, c_im, tcs, gb):
    ng, _, p2 = rowtab.shape
    tc = TC_PROMPT
    rows, rows_s = tc * GROUP, tcs * GROUP
    dup = lambda a: jnp.concatenate([a, a], axis=-1)
    args = (rowtab, dup(b_re.transpose(0, 2, 1)), dup(b_im.transpose(0, 2, 1)),
            dup(c_re), dup(c_im), jnp.concatenate([c_re, -c_im], axis=-1))
    shapes = [(ng, rows + p2, rows), (ng, rows, p2), (ng, rows_s + p2, rows_s), (ng, rows_s, p2)]
    in_specs = [pl.BlockSpec((gb,) + a.shape[1:], lambda i: (i, 0, 0)) for a in args]
    out_specs = [pl.BlockSpec((gb,) + s[1:], lambda i: (i, 0, 0)) for s in shapes]
    out_shape = [jax.ShapeDtypeStruct(s, BF16) for s in shapes]
    body = functools.partial(_ssm_tables_kernel, gb=gb, tc=tc, tcs=tcs, unroll=gb)
    return args, in_specs, out_specs, out_shape, body, ng // gb


def _mix_bias(b_s, hd, tc):
    nh = b_s.shape[0]
    r = CHUNK // tc
    if tc == TC_PROMPT:
        bm = b_s.reshape(nh, r, tc).transpose(0, 2, 1).reshape(nh, CHUNK)
    else:
        bm = jnp.repeat(b_s[:, :tc], r, axis=1)
    return jnp.broadcast_to(bm[:, :, None], (nh, CHUNK, hd))


def _tiles(tc, nk):
    fold = max(1, LANE_TILE // nk)
    return NORM_ROWS // tc, min(BRANCH_A_ROWS, tc * nk) // tc, fold


def kernel(x_prompt, x_sample, state_ssm_re, state_ssm_im, g_norm, w_in, g_v, w_s, b_s,
           a_re, a_im, log_dt, b_re, b_im, c_re, c_im, d_skip, w_glu, b_glu, w_out, g_final):
    depth = g_norm.shape[0]
    assert depth == 1, "single-layer step"
    bsz, seq, d = x_prompt.shape
    dbsz, dseq, _ = x_sample.shape
    nh = w_s.shape[1]
    ng, npst = a_re.shape[1:]
    cb = ng * GROUP
    ca = (w_in.shape[2] - 2 * cb) // 3
    hd = ca // nh
    assert seq // TC_PROMPT == LANES and dbsz == LANES and CHUNK % dseq == 0
    assert dseq & (dseq - 1) == 0 and dseq <= TC_PROMPT, "sample chunk must be a power of two"
    assert ca == cb == d

    l = 0
    gv = g_v[l].reshape(nh, 1, hd)
    rowtab, lt, scan_rows = _discretise(a_re[l], a_im[l], log_dt[l])

    assert bsz % 2 == 0
    tcs = (TC_PROMPT, dseq)
    nks = (bsz * seq // TC_PROMPT, dbsz)
    xs = (x_prompt.reshape(bsz * seq, d), x_sample.reshape(dbsz * dseq, d))
    tiles = [_tiles(tc, nk) for tc, nk in zip(tcs, nks)]
    tables_job = _ssm_tables_job(rowtab, b_re[l], b_im[l], c_re[l], c_im[l], dseq,
                                 ng // (nks[0] // tiles[0][0]))
    xns, out_as, v_rows = [], [], []
    tabs = w_bt = None
    for s, (x2d, tc, (n_norm, n_a, _)) in enumerate(zip(xs, tcs, tiles)):
        xn, made = _norm_permute(x2d, g_norm[l], tc, n_norm, tables_job if s == 0 else None)
        tabs = made if s == 0 else tabs
        out_a, v, made_bt = _branch_a(xn, w_in[l], gv, w_s[l], _mix_bias(b_s[l], hd, tc), n_a,
                                      tc * n_a // PROJ_ROWS, s == 1, s == 0)
        w_bt = made_bt if s == 0 else w_bt
        xns.append(xn)
        out_as.append(out_a)
        v_rows.append(v)
    mp, q, mp_s, q_s = tabs
    folds = [t[2] for t in tiles]
    slabs = [xn.reshape(tc // f, f * nk, d) for xn, tc, nk, f in zip(xns, tcs, nks, folds)]
    zbs, (wg_t, wo_bt) = _proj_b(w_bt, slabs, ((w_glu[l], cb, 0), (w_out[l], cb, 1)))
    dsk = jnp.broadcast_to(d_skip[l][:, None], (cb, LANES))
    h0 = (state_ssm_re[l].reshape(dbsz, ng * npst), state_ssm_im[l].reshape(dbsz, ng * npst))
    y_p_t, hf_p = _ssm(zbs[0], mp, q, scan_rows, dsk, None, SSM_GROUPS_PER_STEP, folds[0])
    y_s_t, hf_s = _ssm(zbs[1], mp_s, q_s, lt, dsk, h0, SSM_GROUPS_PER_STEP_1CHUNK, folds[1])
    bg = jnp.broadcast_to(b_glu[l][:, None], (cb, LANE_TILE))
    rs, wo_a = _glu_out((y_p_t, y_s_t), zbs, wg_t, bg, wo_bt, (w_out[l], ca))
    y_p, y_s = [_out_norm(x2d, out_a, r.reshape(tc, nk, d), wo_a, g_final, OUT_ROWS // tc)
                for x2d, out_a, r, tc, nk in zip(xs, out_as, rs, tcs, nks)]
    v_s = v_rows[1]

    hf_p = hf_p[0].reshape(ng, bsz // 2, 2, 2, npst)
    re_p = hf_p[:, :, 0].reshape(ng, bsz, npst).transpose(1, 0, 2)[None]
    im_p = hf_p[:, :, 1].reshape(ng, bsz, npst).transpose(1, 0, 2)[None]
    re_s = hf_s[0].reshape(1, dbsz, ng, npst)
    im_s = hf_s[1].reshape(1, dbsz, ng, npst)
    v_out = v_s.transpose(1, 0, 2)[None]

    dt_p, dt_s = x_prompt.dtype, x_sample.dtype
    return (y_p.reshape(bsz, seq, d).astype(dt_p), y_s.reshape(dbsz, dseq, d).astype(dt_s),
            re_p.astype(dt_p), im_p.astype(dt_p), re_s.astype(dt_s), im_s.astype(dt_s),
            v_out.astype(dt_s))
```

```python
import functools

import jax
import jax.numpy as jnp
from jax import lax
from jax.experimental import pallas as pl
from jax.experimental.pallas import tpu as pltpu

EPS = 1e-6
LANES = 128
CHUNK = 128
GROUP = 16
TC_PROMPT = 16
ROW_CHUNK = 512
LANE_TILE = 256
PROJ_ROWS = 512
BRANCH_A_ROWS = 2048
NORM_ROWS = 512
OUT_ROWS = 512
UNPERMUTE_PITCH = 24
SSM_GROUPS_PER_STEP = 8
SSM_GROUPS_PER_STEP_1CHUNK = 32
POW_COLS = 16
SCAN_STEPS = 7
N_POW = 11
SCAN_ROWS = 8
GROUPS_PER_TRIP = 8
VMEM_LIMIT = 56 * 1024 * 1024

F32 = jnp.float32
BF16 = jnp.bfloat16


def _params(*sem):
    return pltpu.CompilerParams(dimension_semantics=sem, vmem_limit_bytes=VMEM_LIMIT)


def _norm_permute_kernel(x_ref, g_ref, *refs, tc, n, n_side_in, side_body):
    o_ref = refs[n_side_in]
    x = x_ref[...]
    ms = jnp.mean(x * x, axis=-1, keepdims=True)
    xn = (x * lax.rsqrt(ms + EPS) * g_ref[...]).astype(BF16)
    rows = n * tc
    dst = lax.broadcasted_iota(jnp.int32, (rows, rows), 0)
    src = lax.broadcasted_iota(jnp.int32, (rows, rows), 1)
    sel = (src == tc * (dst % n) + dst // n).astype(BF16)
    o_ref[...] = jnp.dot(sel, xn, preferred_element_type=F32).astype(o_ref.dtype).reshape(o_ref.shape)
    if side_body is not None:
        side_body(*refs[:n_side_in], *refs[n_side_in + 1:])


def _norm_permute(x2d, g, tc, n, side_job=None):
    t, d = x2d.shape
    nk = t // tc
    s_args, s_in, s_out, s_shape, s_body, s_steps = side_job or ((), [], [], [], None, nk // n)
    assert s_steps == nk // n, "the side job needs one block per grid step"
    res = pl.pallas_call(
        functools.partial(_norm_permute_kernel, tc=tc, n=n, n_side_in=len(s_args), side_body=s_body),
        grid=(nk // n,),
        in_specs=[pl.BlockSpec((n * tc, d), lambda i: (i, 0)),
                  pl.BlockSpec((1, d), lambda i: (0, 0))] + list(s_in),
        out_specs=[pl.BlockSpec((tc, n, d), lambda i: (0, i, 0))] + list(s_out),
        out_shape=[jax.ShapeDtypeStruct((tc, nk, d), BF16)] + list(s_shape),
        compiler_params=_params("parallel"),
    )(x2d, g.reshape(1, d), *s_args)
    return res[0], tuple(res[1:])


def _branch_a_kernel(*refs, tc, n, r, hd, n_sub, has_v, has_side):
    refs = list(refs)
    xn_ref, wu_ref, wv_ref, wg_ref, gv_ref, ws_ref, bmix_ref = refs[:7]
    del refs[:7]
    side_in = [refs.pop(0)] if has_side else []
    oa_ref = refs.pop(0)
    v_refs = [refs.pop(0)] if has_v else []
    side_out = [refs.pop(0)] if has_side else []
    w_ref, wm_ref = refs
    d = xn_ref.shape[-1]

    @pl.when(pl.program_id(1) == 0)
    def _():
        w_ref[:, :hd] = wu_ref[...].astype(BF16)
        w_ref[:, hd:2 * hd] = wv_ref[...].astype(BF16)
        w_ref[:, 2 * hd:] = wg_ref[...].astype(BF16)
        row = lax.broadcasted_iota(jnp.int32, (CHUNK, CHUNK), 0)
        col = lax.broadcasted_iota(jnp.int32, (CHUNK, CHUNK), 1)
        w_tril = jnp.where(row >= col, ws_ref[...], 0.0).astype(BF16)
        if tc == TC_PROMPT:
            time_of_row = tc * (row % r) + row // r
            same_seq = None
        else:
            time_of_row = row // r
            same_seq = (row % r) == (col % r)
        sel = (col == time_of_row).astype(BF16)
        sw = jnp.dot(sel, w_tril, preferred_element_type=F32).astype(BF16)
        wm = lax.dot_general(sw, sel, (((1,), (1,)), ((), ())), preferred_element_type=F32)
        if same_seq is not None:
            wm = jnp.where(same_seq, wm, 0.0)
        wm_ref[...] = wm.astype(BF16)

    ns = n // n_sub
    zs = [jnp.dot(xn_ref[:, k * ns:(k + 1) * ns, :].reshape(tc * ns, d), w_ref[...],
                  preferred_element_type=F32) for k in range(n_sub)]
    wm = wm_ref[...]
    bm = bmix_ref[...]
    for k, z in enumerate(zs):
        k0 = k * ns
        u = jax.nn.gelu(z[:, :hd])
        v = jax.nn.gelu(z[:, hd:2 * hd])
        ga = z[:, 2 * hd:]
        ms = jnp.mean(v * v, axis=-1, keepdims=True)
        vn3 = (v * lax.rsqrt(ms + EPS) * gv_ref[...]).reshape(tc, ns, hd)
        pre3 = (u * jax.nn.silu(ga)).reshape(tc, ns, hd)
        if v_refs:
            v_refs[0][:, k0:k0 + ns, :] = vn3
        for c in range(ns // r):
            vg = vn3[:, c * r:(c + 1) * r, :].reshape(CHUNK, hd)
            mixed = jnp.dot(wm, vg.astype(BF16), preferred_element_type=F32) + bm
            og = pre3[:, c * r:(c + 1) * r, :].reshape(CHUNK, hd) * mixed
            oa_ref[:, k0 + c * r:k0 + (c + 1) * r, :] = og.reshape(tc, r, hd).astype(oa_ref.dtype)
    _transpose_side_blocks(side_in, side_out)


def _branch_a(xn, w_in, gv, w_s, bmix, n, n_sub, want_v, transpose_b):
    tc, nk, d = xn.shape
    nh, _, hd = gv.shape
    r = CHUNK // tc
    nt = nk // n
    out_map = lambda h, i: (0, i, h)
    in_specs = [pl.BlockSpec((tc, n, d), lambda h, i: (0, i, 0)),
                pl.BlockSpec((d, hd), lambda h, i: (0, h)),
                pl.BlockSpec((d, hd), lambda h, i: (0, nh + h)),
                pl.BlockSpec((d, hd), lambda h, i: (0, 2 * nh + h)),
                pl.BlockSpec((None, 1, hd), lambda h, i: (h, 0, 0)),
                pl.BlockSpec((None, CHUNK, CHUNK), lambda h, i: (h, 0, 0)),
                pl.BlockSpec((None, CHUNK, hd), lambda h, i: (h, 0, 0))]
    args = [xn, w_in, w_in, w_in, gv, w_s, bmix]
    out_shape = [jax.ShapeDtypeStruct((tc, nk, nh * hd), BF16)]
    out_specs = [pl.BlockSpec((tc, n, hd), out_map)]
    if want_v:
        out_shape.append(jax.ShapeDtypeStruct((tc, nk, nh * hd), F32))
        out_specs.append(pl.BlockSpec((tc, n, hd), out_map))
    if transpose_b:
        col0 = 3 * nh * hd // LANES
        n_cols = w_in.shape[1] - 3 * nh * hd
        assert nh * nt * LANES == n_cols, "one 128-column block per grid step"
        in_specs.append(pl.BlockSpec((d, LANES), lambda h, i: (0, col0 + h * nt + i)))
        args.append(w_in)
        out_shape.append(jax.ShapeDtypeStruct((n_cols, d), BF16))
        out_specs.append(pl.BlockSpec((LANES, d), lambda h, i: (h * nt + i, 0)))
    res = list(pl.pallas_call(
        functools.partial(_branch_a_kernel, tc=tc, n=n, r=r, hd=hd, n_sub=n_sub,
                          has_v=want_v, has_side=transpose_b),
        grid=(nh, nt),
        in_specs=in_specs,
        out_specs=out_specs,
        out_shape=out_shape,
        scratch_shapes=[pltpu.VMEM((d, 3 * hd), BF16), pltpu.VMEM((CHUNK, CHUNK), BF16)],
        compiler_params=_params("parallel", "arbitrary"),
    )(*args))
    out_a = res.pop(0)
    v_rows = res.pop(0) if want_v else None
    w_bt = res.pop(0) if transpose_b else None
    return out_a, v_rows, w_bt


def _transpose_side_blocks(in_refs, out_refs):
    for w_ref, o_ref in zip(in_refs, out_refs):
        o_ref[...] = w_ref[...].T.astype(o_ref.dtype)


def _stream_maps(counts):
    starts = [sum(counts[:s]) for s in range(len(counts))]
    maps = [lambda j, st=st, c=c: jnp.clip(j - st, 0, c - 1) for st, c in zip(starts, counts)]
    owns = [lambda j, st=st, c=c: jnp.logical_and(j >= st, j < st + c) for st, c in zip(starts, counts)]
    return maps, owns


def _row_chunk_copies(w_hbm, w_ref, sem):
    chunks = [pl.ds(m * ROW_CHUNK, ROW_CHUNK) for m in range(w_ref.shape[0] // ROW_CHUNK)]
    return [pltpu.make_async_copy(w_hbm.at[sl], w_ref.at[sl], sem.at[m]) for m, sl in enumerate(chunks)]


def _proj_b_kernel(w_hbm, *refs, counts, n_side):
    ns = len(counts)
    xn_refs, side_in = refs[:ns], refs[ns:ns + n_side]
    o_refs, side_out = refs[ns + n_side:2 * ns + n_side], refs[2 * ns + n_side:2 * ns + 2 * n_side]
    w_ref, sem = refs[2 * ns + 2 * n_side:]
    _, owns = _stream_maps(counts)
    j = pl.program_id(0)

    def body(s, xn_ref, o_ref, first):
        copies = _row_chunk_copies(w_hbm, w_ref, sem) if first else []
        for cp in copies:
            cp.start()
        if s == 0:
            _transpose_side_blocks(side_in, side_out)
        for t in range(xn_ref.shape[0] // LANE_TILE):
            ls = slice(t * LANE_TILE, (t + 1) * LANE_TILE)
            xn = xn_ref[ls, :]
            for m in range(w_ref.shape[0] // ROW_CHUNK):
                if first and t == 0:
                    copies[m].wait()
                sl = slice(m * ROW_CHUNK, (m + 1) * ROW_CHUNK)
                o_ref[sl, ls] = lax.dot_general(
                    w_ref[sl, :], xn, (((1,), (1,)), ((), ())),
                    preferred_element_type=F32).astype(o_ref.dtype)

    pl.when(j == 0)(lambda: body(0, xn_refs[0], o_refs[0], True))
    for s, (xn_ref, o_ref) in enumerate(zip(xn_refs, o_refs)):
        own = jnp.logical_and(owns[s](j), j > 0)
        pl.when(own)(lambda s=s, xn_ref=xn_ref, o_ref=o_ref: body(s, xn_ref, o_ref, False))


def _proj_b(w_bt, xns, side=()):
    rows, d = w_bt.shape
    counts = [x.shape[0] for x in xns]
    maps, _ = _stream_maps(counts)
    in_specs = [pl.BlockSpec(memory_space=pl.ANY)]
    in_specs += [pl.BlockSpec((None, x.shape[1], d), lambda j, m=m: (m(j), 0, 0))
                 for x, m in zip(xns, maps)]
    out_specs = [pl.BlockSpec((None, rows, x.shape[1]), lambda j, m=m: (m(j), 0, 0))
                 for x, m in zip(xns, maps)]
    out_shape = [jax.ShapeDtypeStruct((x.shape[0], rows, x.shape[1]), BF16) for x in xns]
    for w, n_rows, row_block in side:
        n_cols = w.shape[1]
        assert counts[0] * LANES == n_cols, "one 128-column block per step of the first stream set"
        in_specs.append(pl.BlockSpec((n_rows, LANES), lambda j, rb=row_block: (rb, maps[0](j))))
        out_specs.append(pl.BlockSpec((LANES, n_rows), lambda j: (maps[0](j), 0)))
        out_shape.append(jax.ShapeDtypeStruct((n_cols, n_rows), BF16))
    res = pl.pallas_call(
        functools.partial(_proj_b_kernel, counts=counts, n_side=len(side)),
        grid=(sum(counts),),
        in_specs=in_specs,
        out_specs=out_specs,
        out_shape=out_shape,
        scratch_shapes=[pltpu.VMEM((rows, d), BF16), pltpu.SemaphoreType.DMA((rows // ROW_CHUNK,))],
        compiler_params=_params("arbitrary"),
    )(w_bt, *xns, *[w for w, _, _ in side])
    return tuple(res[:len(xns)]), tuple(res[len(xns):])


def _shift_rows(x, d):
    if d % 8 == 0:
        return jnp.concatenate([jnp.zeros((d, x.shape[1]), x.dtype), x[:-d]], axis=0)
    row = lax.broadcasted_iota(jnp.int32, x.shape, 0)
    return jnp.where(row >= d, pltpu.roll(x, d, 0), 0.0)


def _ssm_kernel(x_ref, mp_ref, q_ref, ct_ref, dsk_ref, *rest, tc, gb, nk, fold, has_h0):
    if has_h0:
        h0r_ref, h0i_ref, y_ref, hfr_ref, hfi_ref = rest
    else:
        y_ref, hf_ref = rest
    rows = tc * GROUP
    half = q_ref.shape[2] // 2
    b_tc = tc.bit_length() - 1

    def one_group(g):
        r0 = g * GROUP if isinstance(g, int) else pl.multiple_of(g * GROUP, GROUP)
        if fold == 1:
            xg = x_ref[:, pl.ds(r0, GROUP), :]
        else:
            xg = jnp.stack([x_ref[j // fold, pl.ds(r0, GROUP), (j % fold) * nk:(j % fold + 1) * nk]
                            for j in range(tc)])
        res = jnp.dot(mp_ref[g], xg.reshape(rows, nk), preferred_element_type=F32)
        y_local = res[:rows]
        s_re = res[rows:rows + half]
        s_im = res[rows + half:]
        ct = ct_ref[g]
        if has_h0:
            h_re, h_im = h0r_ref[g], h0i_ref[g]
            a_re, a_im = ct[:, b_tc:b_tc + 1], ct[:, POW_COLS + b_tc:POW_COLS + b_tc + 1]
            hfr_ref[g] = a_re * h_re - a_im * h_im + s_re
            hfi_ref[g] = a_re * h_im + a_im * h_re + s_im
            h_in = jnp.concatenate([h_re, h_im], axis=0)
        else:
            tiles = []
            for pair in range(nk // (2 * LANES)):
                lo, mid, hi = 2 * pair * LANES, (2 * pair + 1) * LANES, (2 * pair + 2) * LANES
                xr = jnp.concatenate([s_re[:, lo:mid], s_re[:, mid:hi]], axis=0).T
                xi = jnp.concatenate([s_im[:, lo:mid], s_im[:, mid:hi]], axis=0).T
                for s in range(SCAN_STEPS):
                    cr, ci = ct[s:s + 1], ct[SCAN_ROWS + s:SCAN_ROWS + s + 1]
                    rr, ri = _shift_rows(xr, 1 << s), _shift_rows(xi, 1 << s)
                    xr, xi = xr + cr * rr - ci * ri, xi + cr * ri + ci * rr
                hf_ref[g, 2 * pair:2 * pair + 1, :] = xr[LANES - 1:]
                hf_ref[g, 2 * pair + 1:2 * pair + 2, :] = xi[LANES - 1:]
                hr_t = _shift_rows(xr, 1).T
                hi_t = _shift_rows(xi, 1).T
                tiles.append(jnp.concatenate([hr_t[:half], hi_t[:half]], axis=0))
                tiles.append(jnp.concatenate([hr_t[half:], hi_t[half:]], axis=0))
            h_in = jnp.concatenate(tiles, axis=1)
        y = y_local + jnp.dot(q_ref[g], h_in.astype(BF16), preferred_element_type=F32)
        dsk = jnp.concatenate([dsk_ref[pl.ds(r0, GROUP), :]] * (nk // LANES), axis=1)
        y3 = y.reshape(tc, GROUP, nk) + dsk[None] * xg.astype(F32)
        ya = jax.nn.gelu(y3).astype(y_ref.dtype)
        if fold == 1:
            y_ref[:, pl.ds(r0, GROUP), :] = ya
        else:
            for j in range(tc):
                y_ref[j // fold, pl.ds(r0, GROUP), (j % fold) * nk:(j % fold + 1) * nk] = ya[j]

    def body(gi, carry):
        for k in range(GROUPS_PER_TRIP):
            one_group(gi * GROUPS_PER_TRIP + k)
        return carry

    if gb == GROUPS_PER_TRIP:
        body(0, 0)
    else:
        lax.fori_loop(0, gb // GROUPS_PER_TRIP, body, 0)


def _ssm(zb, mp, q, ct, dsk, h0, gb, fold):
    tcf, _, nkf = zb.shape
    tc, nk = tcf * fold, nkf // fold
    ng, mrows, rows = mp.shape
    p2 = mrows - rows
    cb = ng * GROUP
    has_h0 = h0 is not None
    in_specs = [pl.BlockSpec((tcf, gb * GROUP, nkf), lambda i: (0, i, 0)),
                pl.BlockSpec((gb, mrows, rows), lambda i: (i, 0, 0)),
                pl.BlockSpec((gb, rows, p2), lambda i: (i, 0, 0)),
                pl.BlockSpec((gb,) + ct.shape[1:], lambda i: (i, 0, 0)),
                pl.BlockSpec((gb * GROUP, LANES), lambda i: (i, 0))]
    args = [zb, mp, q, ct, dsk]
    if has_h0:
        state_spec = pl.BlockSpec((gb, p2 // 2, nk), lambda i: (i, 0, 0))
        in_specs += [state_spec, state_spec]
        args += list(h0)
        hf_specs = [state_spec, state_spec]
        hf_shapes = [jax.ShapeDtypeStruct((ng, p2 // 2, nk), F32)] * 2
    else:
        hf_specs = [pl.BlockSpec((gb, nk // LANES, LANES), lambda i: (i, 0, 0))]
        hf_shapes = [jax.ShapeDtypeStruct((ng, nk // LANES, LANES), F32)]
    res = pl.pallas_call(
        functools.partial(_ssm_kernel, tc=tc, gb=gb, nk=nk, fold=fold, has_h0=has_h0),
        grid=(ng // gb,),
        in_specs=in_specs,
        out_specs=[pl.BlockSpec((tcf, gb * GROUP, nkf), lambda i: (0, i, 0))] + hf_specs,
        out_shape=[jax.ShapeDtypeStruct((tcf, cb, nkf), BF16)] + hf_shapes,
        compiler_params=_params("parallel"),
    )(*args)
    return res[0], tuple(res[1:])


def _glu_out_kernel(wg_hbm, bg_ref, wo_hbm, *refs, counts, has_cast, n_tiles):
    ns = len(counts)
    refs = list(refs)
    y_refs, gate_refs = refs[:ns], refs[ns:2 * ns]
    del refs[:2 * ns]
    cast_in = refs.pop(0) if has_cast else None
    r_refs = refs[:ns]
    del refs[:ns]
    cast_out = refs.pop(0) if has_cast else None
    wg_ref, wo_ref, sem_g, sem_o = refs[-4:]
    ob_refs = refs[:-4]
    _, owns = _stream_maps(counts)
    j = pl.program_id(0)

    def body(s, first):
        y_ref, gate_ref, r_ref = y_refs[s], gate_refs[s], r_refs[s]
        copies_g = _row_chunk_copies(wg_hbm, wg_ref, sem_g) if first else []
        copies_o = _row_chunk_copies(wo_hbm, wo_ref, sem_o) if first else []
        for cp in copies_g + copies_o:
            cp.start()
        if s == 0 and has_cast:
            cast_out[...] = cast_in[...].astype(cast_out.dtype)
        for t, ob_ref in enumerate(ob_refs[:n_tiles[s]]):
            ls = slice(t * LANE_TILE, (t + 1) * LANE_TILE)
            y = y_ref[:, ls]
            for m in range(wg_ref.shape[0] // ROW_CHUNK):
                if first and t == 0:
                    copies_g[m].wait()
                sl = slice(m * ROW_CHUNK, (m + 1) * ROW_CHUNK)
                g = jnp.dot(wg_ref[sl, :], y, preferred_element_type=F32) + bg_ref[sl, :]
                ob = (y_ref[sl, ls].astype(F32) * jax.nn.sigmoid(g)
                      * jax.nn.silu(gate_ref[sl, ls].astype(F32)))
                ob_ref[sl, :] = ob.astype(ob_ref.dtype)
        for t, ob_ref in enumerate(ob_refs[:n_tiles[s]]):
            ob = ob_ref[...]
            for m in range(wo_ref.shape[0] // ROW_CHUNK):
                if first and t == 0:
                    copies_o[m].wait()
                sl = slice(m * ROW_CHUNK, (m + 1) * ROW_CHUNK)
                r_ref[t * LANE_TILE:(t + 1) * LANE_TILE, sl] = jnp.dot(
                    wo_ref[sl, :], ob, preferred_element_type=F32).T.astype(r_ref.dtype)

    pl.when(j == 0)(lambda: body(0, True))
    for s in range(ns):
        pl.when(jnp.logical_and(owns[s](j), j > 0))(lambda s=s: body(s, False))


def _glu_out(y_ts, zbs, wg_t, bg, wo_t, cast_rows=None):
    cb = wg_t.shape[0]
    d = wo_t.shape[0]
    counts = [y.shape[0] for y in y_ts]
    maps, _ = _stream_maps(counts)
    const = lambda j: (0, 0)
    in_specs = [pl.BlockSpec(memory_space=pl.ANY),
                pl.BlockSpec((cb, LANE_TILE), const, pipeline_mode=pl.Buffered(1)),
                pl.BlockSpec(memory_space=pl.ANY)]
    in_specs += [pl.BlockSpec((None, cb, y.shape[2]), lambda j, m=m: (m(j), 0, 0)) for y, m in zip(y_ts, maps)]
    in_specs += [pl.BlockSpec((None, cb, y.shape[2]), lambda j, m=m: (m(j), 1, 0)) for y, m in zip(y_ts, maps)]
    args = [wg_t, bg, wo_t, *y_ts, *zbs]
    out_specs = [pl.BlockSpec((None, y.shape[2], d), lambda j, m=m: (m(j), 0, 0)) for y, m in zip(y_ts, maps)]
    out_shape = [jax.ShapeDtypeStruct((y.shape[0], y.shape[2], d), BF16) for y in y_ts]
    if cast_rows is not None:
        w, n_rows = cast_rows
        assert counts[0] * LANES == n_rows, "one 128-row block per step of the first stream set"
        blk = pl.BlockSpec((LANES, w.shape[1]), lambda j: (maps[0](j), 0))
        in_specs.append(blk)
        args.append(w)
        out_specs.append(blk)
        out_shape.append(jax.ShapeDtypeStruct((n_rows, w.shape[1]), BF16))
    n_tiles = [y.shape[2] // LANE_TILE for y in y_ts]
    res = pl.pallas_call(
        functools.partial(_glu_out_kernel, counts=counts, has_cast=cast_rows is not None,
                          n_tiles=n_tiles),
        grid=(sum(counts),),
        in_specs=in_specs,
        out_specs=out_specs,
        out_shape=out_shape,
        scratch_shapes=[pltpu.VMEM((cb, LANE_TILE), BF16)] * max(n_tiles) + [
            pltpu.VMEM(wg_t.shape, BF16), pltpu.VMEM(wo_t.shape, BF16),
            pltpu.SemaphoreType.DMA((cb // ROW_CHUNK,)), pltpu.SemaphoreType.DMA((d // ROW_CHUNK,))],
        compiler_params=_params("arbitrary"),
    )(*args)
    return tuple(res[:len(y_ts)]), (res[len(y_ts)] if cast_rows is not None else None)


def _out_norm_kernel(x_ref, oa_ref, r_ref, wo_ref, g_ref, y_ref, ms_ref, *, tc, n):
    ca = oa_ref.shape[-1]
    d = x_ref.shape[-1]
    nc = d // LANES
    p = jnp.dot(oa_ref[...].reshape(tc * n, ca), wo_ref[...], preferred_element_type=F32)
    pitch = ms_ref.shape[1] // n
    for j in range(tc):
        s = p[j * n:(j + 1) * n] + r_ref[j].astype(F32)
        for c in range(nc):
            ms_ref[c, pl.ds(j, n, stride=pitch), :] = s[:, c * LANES:(c + 1) * LANES]
    mixed = [ms_ref[c].reshape(n, pitch, LANES)[:, :tc, :].reshape(n * tc, LANES) for c in range(nc)]
    acc = x_ref[...] + jnp.concatenate(mixed, axis=1)
    ms = jnp.mean(acc * acc, axis=-1, keepdims=True)
    y_ref[...] = acc * lax.rsqrt(ms + EPS) * g_ref[...]


def _out_norm(x2d, oa, r, wo_a, gf, n):
    tc, nk, ca = oa.shape
    t, d = x2d.shape
    return pl.pallas_call(
        functools.partial(_out_norm_kernel, tc=tc, n=n),
        grid=(nk // n,),
        in_specs=[pl.BlockSpec((n * tc, d), lambda i: (i, 0)),
                  pl.BlockSpec((tc, n, ca), lambda i: (0, i, 0)),
                  pl.BlockSpec((tc, n, d), lambda i: (0, i, 0)),
                  pl.BlockSpec((ca, d), lambda i: (0, 0), pipeline_mode=pl.Buffered(1)),
                  pl.BlockSpec((1, d), lambda i: (0, 0))],
        out_specs=pl.BlockSpec((n * tc, d), lambda i: (i, 0)),
        out_shape=jax.ShapeDtypeStruct((t, d), F32),
        scratch_shapes=[pltpu.VMEM((d // LANES, n * UNPERMUTE_PITCH, LANES), F32)],
        compiler_params=_params("parallel"),
    )(x2d, oa, r, wo_a, gf.reshape(1, d))


def _discretise(a_re, a_im, log_dt):
    dt = jnp.exp(log_dt)[:, None]
    x_re, x_im = a_re * dt, a_im * dt
    n = (2.0 ** jnp.arange(N_POW, dtype=F32))[None, :, None]
    mag = jnp.exp(n * x_re[:, None, :])
    ang = n * x_im[:, None, :]
    pw_r, pw_i = mag * jnp.cos(ang), mag * jnp.sin(ang)
    lr, li = pw_r[:, 0], pw_i[:, 0]
    den = a_re * a_re + a_im * a_im
    nr, ni = lr - 1.0, li
    qr, qi = (nr * a_re + ni * a_im) / den, (ni * a_re - nr * a_im) / den
    cat = lambda a, b: jnp.concatenate([a, b], axis=-1)
    rowtab = jnp.stack([cat(lr, lr), cat(li, li), cat(qr, qi), cat(-qi, qr)], axis=1)
    zpad = jnp.zeros_like(pw_r[:, :POW_COLS - N_POW])
    lt = jnp.concatenate([pw_r, zpad, pw_i, zpad], axis=1).transpose(0, 2, 1)
    b0 = TC_PROMPT.bit_length() - 1
    sc_r, sc_i = pw_r[:, b0:b0 + SCAN_STEPS], pw_i[:, b0:b0 + SCAN_STEPS]
    zrow = jnp.zeros_like(cat(sc_r, sc_r)[:, :SCAN_ROWS - SCAN_STEPS])
    scan_rows = jnp.concatenate([cat(sc_r, sc_r), zrow, cat(sc_i, sc_i), zrow], axis=1)
    return rowtab, lt, scan_rows


def _ssm_tables_kernel(rt_ref, btr_ref, bti_ref, c2r_ref, c2i_ref, c2n_ref,
                       mp_ref, q_ref, mps_ref, qs_ref, *, gb, tc, tcs, unroll):
    rows = tc * GROUP
    rows_s = tcs * GROUP
    half = rt_ref.shape[2] // 2
    lane_k = lax.broadcasted_iota(jnp.int32, (GROUP, rows), 1)
    lane_q = lax.broadcasted_iota(jnp.int32, (1, 2 * half), 1)

    def one_group(g):
        rt = rt_ref[g]
        l_r, l_i, qu, qv = rt[0:1], rt[1:2], rt[2:3], rt[3:4]
        btr, bti = btr_ref[g], bti_ref[g]
        bb = btr * qu + bti * qv
        bs = btr * qv - bti * qu
        c2r, c2i = c2r_ref[g], c2i_ref[g]
        cur_r = jnp.ones((1, 2 * half), F32)
        cur_i = jnp.zeros((1, 2 * half), F32)
        et_blocks = []
        for n in range(tc):
            et_blocks.append(cur_r * bb + cur_i * bs)
            cur_r, cur_i = cur_r * l_r - cur_i * l_i, cur_r * l_i + cur_i * l_r
            u = jnp.where(lane_q < half, cur_r, -cur_i)
            v = jnp.where(lane_q < half, -cur_i, -cur_r)
            blk = (c2r * u + c2i * v).astype(q_ref.dtype)
            q_ref[g, n * GROUP:(n + 1) * GROUP, :] = blk
            if n < tcs:
                qs_ref[g, n * GROUP:(n + 1) * GROUP, :] = blk
        p = jnp.concatenate(et_blocks[::-1], axis=0).T
        mp_ref[g, rows:, :] = p.astype(mp_ref.dtype)
        mps_ref[g, rows_s:, :] = p[:, rows - rows_s:].astype(mps_ref.dtype)
        krow = jnp.dot(c2n_ref[g], p, precision=lax.Precision.HIGHEST, preferred_element_type=F32)
        for i in range(tc):
            sh = (tc - 1 - i) * GROUP
            blk = krow if sh == 0 else pltpu.roll(krow, rows - sh, 1)
            blk = jnp.where(lane_k < (i + 1) * GROUP, blk, 0.0)
            mp_ref[g, i * GROUP:(i + 1) * GROUP, :] = blk.astype(mp_ref.dtype)
            if i < tcs:
                mps_ref[g, i * GROUP:(i + 1) * GROUP, :] = blk[:, :rows_s].astype(mps_ref.dtype)

    def body(gi, carry):
        for k in range(unroll):
            one_group(gi * unroll + k)
        return carry

    if unroll == gb:
        body(0, 0)
    else:
        lax.fori_loop(0, gb // unroll, body, 0)


def _ssm_tables_job(rowtab, b_re, b_im, c_re, c_im, tcs, gb):
    ng, _, p2 = rowtab.shape
    tc = TC_PROMPT
    rows, rows_s = tc * GROUP, tcs * GROUP
    dup = lambda a: jnp.concatenate([a, a], axis=-1)
    args = (rowtab, dup(b_re.transpose(0, 2, 1)), dup(b_im.transpose(0, 2, 1)),
            dup(c_re), dup(c_im), jnp.concatenate([c_re, -c_im], axis=-1))
    shapes = [(ng, rows + p2, rows), (ng, rows, p2), (ng, rows_s + p2, rows_s), (ng, rows_s, p2)]
    in_specs = [pl.BlockSpec((gb,) + a.shape[1:], lambda i: (i, 0, 0)) for a in args]
    out_specs = [pl.BlockSpec((gb,) + s[1:], lambda i: (i, 0, 0)) for s in shapes]
    out_shape = [jax.ShapeDtypeStruct(s, BF16) for s in shapes]
    body = functools.partial(_ssm_tables_kernel, gb=gb, tc=tc, tcs=tcs, unroll=gb)
    return args, in_specs, out_specs, out_shape, body, ng // gb


def _mix_bias(b_s, hd, tc):
    nh = b_s.shape[0]
    r = CHUNK // tc
    if tc == TC_PROMPT:
        bm = b_s.reshape(nh, r, tc).transpose(0, 2, 1).reshape(nh, CHUNK)
    else:
        bm = jnp.repeat(b_s[:, :tc], r, axis=1)
    return jnp.broadcast_to(bm[:, :, None], (nh, CHUNK, hd))


def _tiles(tc, nk):
    fold = max(1, LANE_TILE // nk)
    return NORM_ROWS // tc, min(BRANCH_A_ROWS, tc * nk) // tc, fold


def kernel(x_prompt, x_sample, state_ssm_re, state_ssm_im, g_norm, w_in, g_v, w_s, b_s,
           a_re, a_im, log_dt, b_re, b_im, c_re, c_im, d_skip, w_glu, b_glu, w_out, g_final):
    depth = g_norm.shape[0]
    assert depth == 1, "single-layer step"
    bsz, seq, d = x_prompt.shape
    dbsz, dseq, _ = x_sample.shape
    nh = w_s.shape[1]
    ng, npst = a_re.shape[1:]
    cb = ng * GROUP
    ca = (w_in.shape[2] - 2 * cb) // 3
    hd = ca // nh
    assert seq // TC_PROMPT == LANES and dbsz == LANES and CHUNK % dseq == 0
    assert dseq & (dseq - 1) == 0 and dseq <= TC_PROMPT, "sample chunk must be a power of two"
    assert ca == cb == d

    l = 0
    gv = g_v[l].reshape(nh, 1, hd)
    rowtab, lt, scan_rows = _discretise(a_re[l], a_im[l], log_dt[l])

    assert bsz % 2 == 0
    tcs = (TC_PROMPT, dseq)
    nks = (bsz * seq // TC_PROMPT, dbsz)
    xs = (x_prompt.reshape(bsz * seq, d), x_sample.reshape(dbsz * dseq, d))
    tiles = [_tiles(tc, nk) for tc, nk in zip(tcs, nks)]
    tables_job = _ssm_tables_job(rowtab, b_re[l], b_im[l], c_re[l], c_im[l], dseq,
                                 ng // (nks[0] // tiles[0][0]))
    xns, out_as, v_rows = [], [], []
    tabs = w_bt = None
    for s, (x2d, tc, (n_norm, n_a, _)) in enumerate(zip(xs, tcs, tiles)):
        xn, made = _norm_permute(x2d, g_norm[l], tc, n_norm, tables_job if s == 0 else None)
        tabs = made if s == 0 else tabs
        out_a, v, made_bt = _branch_a(xn, w_in[l], gv, w_s[l], _mix_bias(b_s[l], hd, tc), n_a,
                                      tc * n_a // PROJ_ROWS, s == 1, s == 0)
        w_bt = made_bt if s == 0 else w_bt
        xns.append(xn)
        out_as.append(out_a)
        v_rows.append(v)
    mp, q, mp_s, q_s = tabs
    folds = [t[2] for t in tiles]
    slabs = [xn.reshape(tc // f, f * nk, d) for xn, tc, nk, f in zip(xns, tcs, nks, folds)]
    zbs, (wg_t, wo_bt) = _proj_b(w_bt, slabs, ((w_glu[l], cb, 0), (w_out[l], cb, 1)))
    dsk = jnp.broadcast_to(d_skip[l][:, None], (cb, LANES))
    h0 = (state_ssm_re[l].transpose(1, 2, 0), state_ssm_im[l].transpose(1, 2, 0))
    y_p_t, hf_p = _ssm(zbs[0], mp, q, scan_rows, dsk, None, SSM_GROUPS_PER_STEP, folds[0])
    y_s_t, hf_s = _ssm(zbs[1], mp_s, q_s, lt, dsk, h0, SSM_GROUPS_PER_STEP_1CHUNK, folds[1])
    bg = jnp.broadcast_to(b_glu[l][:, None], (cb, LANE_TILE))
    rs, wo_a = _glu_out((y_p_t, y_s_t), zbs, wg_t, bg, wo_bt, (w_out[l], ca))
    y_p, y_s = [_out_norm(x2d, out_a, r.reshape(tc, nk, d), wo_a, g_final, OUT_ROWS // tc)
                for x2d, out_a, r, tc, nk in zip(xs, out_as, rs, tcs, nks)]
    v_s = v_rows[1]

    hf_p = hf_p[0].reshape(ng, bsz // 2, 2, 2, npst)
    re_p = hf_p[:, :, 0].reshape(ng, bsz, npst).transpose(1, 0, 2)[None]
    im_p = hf_p[:, :, 1].reshape(ng, bsz, npst).transpose(1, 0, 2)[None]
    re_s = hf_s[0].transpose(2, 0, 1)[None]
    im_s = hf_s[1].transpose(2, 0, 1)[None]
    v_out = v_s.transpose(1, 0, 2)[None]

    dt_p, dt_s = x_prompt.dtype, x_sample.dtype
    return (y_p.reshape(bsz, seq, d).astype(dt_p), y_s.reshape(dbsz, dseq, d).astype(dt_s),
            re_p.astype(dt_p), im_p.astype(dt_p), re_s.astype(dt_s), im_s.astype(dt_s),
            v_out.astype(dt_s))
```

```python
import functools

import jax
import jax.numpy as jnp
from jax import lax
from jax.experimental import pallas as pl
from jax.experimental.pallas import tpu as pltpu

EPS = 1e-6
LANES = 128
CHUNK = 128
GROUP = 16
TC_PROMPT = 16
ROW_CHUNK = 512
LANE_TILE = 256
PROJ_ROWS = 512
BRANCH_A_ROWS = 2048
NORM_ROWS = 512
OUT_ROWS = 512
UNPERMUTE_PITCH = 24
SSM_GROUPS_PER_STEP = 8
SSM_GROUPS_PER_STEP_1CHUNK = 32
POW_COLS = 16
SCAN_STEPS = 7
N_POW = 11
SCAN_ROWS = 8
GROUPS_PER_TRIP = 8
VMEM_LIMIT = 56 * 1024 * 1024

F32 = jnp.float32
BF16 = jnp.bfloat16


def _params(*sem):
    return pltpu.CompilerParams(dimension_semantics=sem, vmem_limit_bytes=VMEM_LIMIT)


def _norm_permute_kernel(x_ref, g_ref, *refs, tc, n, n_side_in, side_body):
    o_ref = refs[n_side_in]
    x = x_ref[...]
    ms = jnp.mean(x * x, axis=-1, keepdims=True)
    xn = (x * lax.rsqrt(ms + EPS) * g_ref[...]).astype(BF16)
    rows = n * tc
    dst = lax.broadcasted_iota(jnp.int32, (rows, rows), 0)
    src = lax.broadcasted_iota(jnp.int32, (rows, rows), 1)
    sel = (src == tc * (dst % n) + dst // n).astype(BF16)
    o_ref[...] = jnp.dot(sel, xn, preferred_element_type=F32).astype(o_ref.dtype).reshape(o_ref.shape)
    if side_body is not None:
        side_body(*refs[:n_side_in], *refs[n_side_in + 1:])


def _norm_permute(x2d, g, tc, n, side_job=None):
    t, d = x2d.shape
    nk = t // tc
    s_args, s_in, s_out, s_shape, s_body, s_steps = side_job or ((), [], [], [], None, nk // n)
    assert s_steps == nk // n, "the side job needs one block per grid step"
    res = pl.pallas_call(
        functools.partial(_norm_permute_kernel, tc=tc, n=n, n_side_in=len(s_args), side_body=s_body),
        grid=(nk // n,),
        in_specs=[pl.BlockSpec((n * tc, d), lambda i: (i, 0)),
                  pl.BlockSpec((1, d), lambda i: (0, 0))] + list(s_in),
        out_specs=[pl.BlockSpec((tc, n, d), lambda i: (0, i, 0))] + list(s_out),
        out_shape=[jax.ShapeDtypeStruct((tc, nk, d), BF16)] + list(s_shape),
        compiler_params=_params("parallel"),
    )(x2d, g.reshape(1, d), *s_args)
    return res[0], tuple(res[1:])


def _branch_a_kernel(*refs, tc, n, r, hd, n_sub, has_v, has_side):
    refs = list(refs)
    xn_ref, wu_ref, wv_ref, wg_ref, gv_ref, ws_ref, bmix_ref = refs[:7]
    del refs[:7]
    side_in = [refs.pop(0)] if has_side else []
    oa_ref = refs.pop(0)
    v_refs = [refs.pop(0)] if has_v else []
    side_out = [refs.pop(0)] if has_side else []
    w_ref, wm_ref = refs
    d = xn_ref.shape[-1]

    @pl.when(pl.program_id(1) == 0)
    def _():
        w_ref[:, :hd] = wu_ref[...].astype(BF16)
        w_ref[:, hd:2 * hd] = wv_ref[...].astype(BF16)
        w_ref[:, 2 * hd:] = wg_ref[...].astype(BF16)
        row = lax.broadcasted_iota(jnp.int32, (CHUNK, CHUNK), 0)
        col = lax.broadcasted_iota(jnp.int32, (CHUNK, CHUNK), 1)
        w_tril = jnp.where(row >= col, ws_ref[...], 0.0).astype(BF16)
        if tc == TC_PROMPT:
            time_of_row = tc * (row % r) + row // r
            same_seq = None
        else:
            time_of_row = row // r
            same_seq = (row % r) == (col % r)
        sel = (col == time_of_row).astype(BF16)
        sw = jnp.dot(sel, w_tril, preferred_element_type=F32).astype(BF16)
        wm = lax.dot_general(sw, sel, (((1,), (1,)), ((), ())), preferred_element_type=F32)
        if same_seq is not None:
            wm = jnp.where(same_seq, wm, 0.0)
        wm_ref[...] = wm.astype(BF16)

    ns = n // n_sub
    zs = [jnp.dot(xn_ref[:, k * ns:(k + 1) * ns, :].reshape(tc * ns, d), w_ref[...],
                  preferred_element_type=F32) for k in range(n_sub)]
    wm = wm_ref[...]
    bm = bmix_ref[...]
    for k, z in enumerate(zs):
        k0 = k * ns
        u = jax.nn.gelu(z[:, :hd])
        v = jax.nn.gelu(z[:, hd:2 * hd])
        ga = z[:, 2 * hd:]
        ms = jnp.mean(v * v, axis=-1, keepdims=True)
        vn3 = (v * lax.rsqrt(ms + EPS) * gv_ref[...]).reshape(tc, ns, hd)
        pre3 = (u * jax.nn.silu(ga)).reshape(tc, ns, hd)
        if v_refs:
            v_refs[0][:, k0:k0 + ns, :] = vn3
        for c in range(ns // r):
            vg = vn3[:, c * r:(c + 1) * r, :].reshape(CHUNK, hd)
            mixed = jnp.dot(wm, vg.astype(BF16), preferred_element_type=F32) + bm
            og = pre3[:, c * r:(c + 1) * r, :].reshape(CHUNK, hd) * mixed
            oa_ref[:, k0 + c * r:k0 + (c + 1) * r, :] = og.reshape(tc, r, hd).astype(oa_ref.dtype)
    _transpose_side_blocks(side_in, side_out)


def _branch_a(xn, w_in, gv, w_s, bmix, n, n_sub, want_v, transpose_b):
    tc, nk, d = xn.shape
    nh, _, hd = gv.shape
    r = CHUNK // tc
    nt = nk // n
    out_map = lambda h, i: (0, i, h)
    in_specs = [pl.BlockSpec((tc, n, d), lambda h, i: (0, i, 0)),
                pl.BlockSpec((d, hd), lambda h, i: (0, h)),
                pl.BlockSpec((d, hd), lambda h, i: (0, nh + h)),
                pl.BlockSpec((d, hd), lambda h, i: (0, 2 * nh + h)),
                pl.BlockSpec((None, 1, hd), lambda h, i: (h, 0, 0)),
                pl.BlockSpec((None, CHUNK, CHUNK), lambda h, i: (h, 0, 0)),
                pl.BlockSpec((None, CHUNK, hd), lambda h, i: (h, 0, 0))]
    args = [xn, w_in, w_in, w_in, gv, w_s, bmix]
    out_shape = [jax.ShapeDtypeStruct((tc, nk, nh * hd), BF16)]
    out_specs = [pl.BlockSpec((tc, n, hd), out_map)]
    if want_v:
        out_shape.append(jax.ShapeDtypeStruct((tc, nk, nh * hd), F32))
        out_specs.append(pl.BlockSpec((tc, n, hd), out_map))
    if transpose_b:
        col0 = 3 * nh * hd // LANES
        n_cols = w_in.shape[1] - 3 * nh * hd
        assert nh * nt * LANES == n_cols, "one 128-column block per grid step"
        in_specs.append(pl.BlockSpec((d, LANES), lambda h, i: (0, col0 + h * nt + i)))
        args.append(w_in)
        out_shape.append(jax.ShapeDtypeStruct((n_cols, d), BF16))
        out_specs.append(pl.BlockSpec((LANES, d), lambda h, i: (h * nt + i, 0)))
    res = list(pl.pallas_call(
        functools.partial(_branch_a_kernel, tc=tc, n=n, r=r, hd=hd, n_sub=n_sub,
                          has_v=want_v, has_side=transpose_b),
        grid=(nh, nt),
        in_specs=in_specs,
        out_specs=out_specs,
        out_shape=out_shape,
        scratch_shapes=[pltpu.VMEM((d, 3 * hd), BF16), pltpu.VMEM((CHUNK, CHUNK), BF16)],
        compiler_params=_params("parallel", "arbitrary"),
    )(*args))
    out_a = res.pop(0)
    v_rows = res.pop(0) if want_v else None
    w_bt = res.pop(0) if transpose_b else None
    return out_a, v_rows, w_bt


def _transpose_side_blocks(in_refs, out_refs):
    for w_ref, o_ref in zip(in_refs, out_refs):
        o_ref[...] = w_ref[...].T.astype(o_ref.dtype)


def _stream_maps(counts):
    starts = [sum(counts[:s]) for s in range(len(counts))]
    maps = [lambda j, st=st, c=c: jnp.clip(j - st, 0, c - 1) for st, c in zip(starts, counts)]
    owns = [lambda j, st=st, c=c: jnp.logical_and(j >= st, j < st + c) for st, c in zip(starts, counts)]
    return maps, owns


def _proj_b_kernel(w_ref, *refs, counts, n_side):
    ns = len(counts)
    xn_refs, side_in = refs[:ns], refs[ns:ns + n_side]
    o_refs, side_out = refs[ns + n_side:2 * ns + n_side], refs[2 * ns + n_side:]
    _, owns = _stream_maps(counts)
    j = pl.program_id(0)
    for s, (xn_ref, o_ref) in enumerate(zip(xn_refs, o_refs)):
        @pl.when(owns[s](j))
        def _(s=s, xn_ref=xn_ref, o_ref=o_ref):
            if s == 0:
                _transpose_side_blocks(side_in, side_out)
            for t in range(xn_ref.shape[0] // LANE_TILE):
                ls = slice(t * LANE_TILE, (t + 1) * LANE_TILE)
                xn = xn_ref[ls, :]
                for m in range(w_ref.shape[0] // ROW_CHUNK):
                    sl = slice(m * ROW_CHUNK, (m + 1) * ROW_CHUNK)
                    o_ref[sl, ls] = lax.dot_general(
                        w_ref[sl, :], xn, (((1,), (1,)), ((), ())),
                        preferred_element_type=F32).astype(o_ref.dtype)


def _proj_b(w_bt, xns, side=()):
    rows, d = w_bt.shape
    counts = [x.shape[0] for x in xns]
    maps, _ = _stream_maps(counts)
    in_specs = [pl.BlockSpec((rows, d), lambda j: (0, 0), pipeline_mode=pl.Buffered(1))]
    in_specs += [pl.BlockSpec((None, x.shape[1], d), lambda j, m=m: (m(j), 0, 0))
                 for x, m in zip(xns, maps)]
    out_specs = [pl.BlockSpec((None, rows, x.shape[1]), lambda j, m=m: (m(j), 0, 0))
                 for x, m in zip(xns, maps)]
    out_shape = [jax.ShapeDtypeStruct((x.shape[0], rows, x.shape[1]), BF16) for x in xns]
    for w, n_rows, row_block in side:
        n_cols = w.shape[1]
        assert counts[0] * LANES == n_cols, "one 128-column block per step of the first stream set"
        in_specs.append(pl.BlockSpec((n_rows, LANES), lambda j, rb=row_block: (rb, maps[0](j))))
        out_specs.append(pl.BlockSpec((LANES, n_rows), lambda j: (maps[0](j), 0)))
        out_shape.append(jax.ShapeDtypeStruct((n_cols, n_rows), BF16))
    res = pl.pallas_call(
        functools.partial(_proj_b_kernel, counts=counts, n_side=len(side)),
        grid=(sum(counts),),
        in_specs=in_specs,
        out_specs=out_specs,
        out_shape=out_shape,
        compiler_params=_params("arbitrary"),
    )(w_bt, *xns, *[w for w, _, _ in side])
    return tuple(res[:len(xns)]), tuple(res[len(xns):])


def _shift_rows(x, d):
    if d % 8 == 0:
        return jnp.concatenate([jnp.zeros((d, x.shape[1]), x.dtype), x[:-d]], axis=0)
    row = lax.broadcasted_iota(jnp.int32, x.shape, 0)
    return jnp.where(row >= d, pltpu.roll(x, d, 0), 0.0)


def _ssm_kernel(x_ref, mp_ref, q_ref, ct_ref, dsk_ref, *rest, tc, gb, nk, fold, has_h0):
    if has_h0:
        h0r_ref, h0i_ref, y_ref, hfr_ref, hfi_ref = rest
    else:
        y_ref, hf_ref = rest
    rows = tc * GROUP
    half = q_ref.shape[2] // 2
    b_tc = tc.bit_length() - 1

    def one_group(g):
        r0 = g * GROUP if isinstance(g, int) else pl.multiple_of(g * GROUP, GROUP)
        if fold == 1:
            xg = x_ref[:, pl.ds(r0, GROUP), :]
        else:
            xg = jnp.stack([x_ref[j // fold, pl.ds(r0, GROUP), (j % fold) * nk:(j % fold + 1) * nk]
                            for j in range(tc)])
        res = jnp.dot(mp_ref[g], xg.reshape(rows, nk), preferred_element_type=F32)
        y_local = res[:rows]
        s_re = res[rows:rows + half]
        s_im = res[rows + half:]
        ct = ct_ref[g]
        if has_h0:
            h_re, h_im = h0r_ref[g], h0i_ref[g]
            a_re, a_im = ct[:, b_tc:b_tc + 1], ct[:, POW_COLS + b_tc:POW_COLS + b_tc + 1]
            hfr_ref[g] = a_re * h_re - a_im * h_im + s_re
            hfi_ref[g] = a_re * h_im + a_im * h_re + s_im
            h_in = jnp.concatenate([h_re, h_im], axis=0)
        else:
            tiles = []
            for pair in range(nk // (2 * LANES)):
                lo, mid, hi = 2 * pair * LANES, (2 * pair + 1) * LANES, (2 * pair + 2) * LANES
                xr = jnp.concatenate([s_re[:, lo:mid], s_re[:, mid:hi]], axis=0).T
                xi = jnp.concatenate([s_im[:, lo:mid], s_im[:, mid:hi]], axis=0).T
                for s in range(SCAN_STEPS):
                    cr, ci = ct[s:s + 1], ct[SCAN_ROWS + s:SCAN_ROWS + s + 1]
                    rr, ri = _shift_rows(xr, 1 << s), _shift_rows(xi, 1 << s)
                    xr, xi = xr + cr * rr - ci * ri, xi + cr * ri + ci * rr
                hf_ref[g, 2 * pair:2 * pair + 1, :] = xr[LANES - 1:]
                hf_ref[g, 2 * pair + 1:2 * pair + 2, :] = xi[LANES - 1:]
                hr_t = _shift_rows(xr, 1).T
                hi_t = _shift_rows(xi, 1).T
                tiles.append(jnp.concatenate([hr_t[:half], hi_t[:half]], axis=0))
                tiles.append(jnp.concatenate([hr_t[half:], hi_t[half:]], axis=0))
            h_in = jnp.concatenate(tiles, axis=1)
        y = y_local + jnp.dot(q_ref[g], h_in.astype(BF16), preferred_element_type=F32)
        dsk = jnp.concatenate([dsk_ref[pl.ds(r0, GROUP), :]] * (nk // LANES), axis=1)
        y3 = y.reshape(tc, GROUP, nk) + dsk[None] * xg.astype(F32)
        ya = jax.nn.gelu(y3).astype(y_ref.dtype)
        if fold == 1:
            y_ref[:, pl.ds(r0, GROUP), :] = ya
        else:
            for j in range(tc):
                y_ref[j // fold, pl.ds(r0, GROUP), (j % fold) * nk:(j % fold + 1) * nk] = ya[j]

    def body(gi, carry):
        for k in range(GROUPS_PER_TRIP):
            one_group(gi * GROUPS_PER_TRIP + k)
        return carry

    if gb == GROUPS_PER_TRIP:
        body(0, 0)
    else:
        lax.fori_loop(0, gb // GROUPS_PER_TRIP, body, 0)


def _ssm(zb, mp, q, ct, dsk, h0, gb, fold):
    tcf, _, nkf = zb.shape
    tc, nk = tcf * fold, nkf // fold
    ng, mrows, rows = mp.shape
    p2 = mrows - rows
    cb = ng * GROUP
    has_h0 = h0 is not None
    in_specs = [pl.BlockSpec((tcf, gb * GROUP, nkf), lambda i: (0, i, 0)),
                pl.BlockSpec((gb, mrows, rows), lambda i: (i, 0, 0)),
                pl.BlockSpec((gb, rows, p2), lambda i: (i, 0, 0)),
                pl.BlockSpec((gb,) + ct.shape[1:], lambda i: (i, 0, 0)),
                pl.BlockSpec((gb * GROUP, LANES), lambda i: (i, 0))]
    args = [zb, mp, q, ct, dsk]
    if has_h0:
        state_spec = pl.BlockSpec((gb, p2 // 2, nk), lambda i: (i, 0, 0))
        in_specs += [state_spec, state_spec]
        args += list(h0)
        hf_specs = [state_spec, state_spec]
        hf_shapes = [jax.ShapeDtypeStruct((ng, p2 // 2, nk), F32)] * 2
    else:
        hf_specs = [pl.BlockSpec((gb, nk // LANES, LANES), lambda i: (i, 0, 0))]
        hf_shapes = [jax.ShapeDtypeStruct((ng, nk // LANES, LANES), F32)]
    res = pl.pallas_call(
        functools.partial(_ssm_kernel, tc=tc, gb=gb, nk=nk, fold=fold, has_h0=has_h0),
        grid=(ng // gb,),
        in_specs=in_specs,
        out_specs=[pl.BlockSpec((tcf, gb * GROUP, nkf), lambda i: (0, i, 0))] + hf_specs,
        out_shape=[jax.ShapeDtypeStruct((tcf, cb, nkf), BF16)] + hf_shapes,
        compiler_params=_params("parallel"),
    )(*args)
    return res[0], tuple(res[1:])


def _glu_out_kernel(wg_ref, bg_ref, wo_ref, *refs, counts, has_cast, n_tiles):
    ns = len(counts)
    refs = list(refs)
    y_refs, gate_refs = refs[:ns], refs[ns:2 * ns]
    del refs[:2 * ns]
    cast_in = refs.pop(0) if has_cast else None
    r_refs = refs[:ns]
    del refs[:ns]
    cast_out = refs.pop(0) if has_cast else None
    ob_refs = refs
    _, owns = _stream_maps(counts)
    j = pl.program_id(0)
    for s in range(ns):
        @pl.when(owns[s](j))
        def _(s=s, y_ref=y_refs[s], gate_ref=gate_refs[s], r_ref=r_refs[s]):
            if s == 0 and has_cast:
                cast_out[...] = cast_in[...].astype(cast_out.dtype)
            for t, ob_ref in enumerate(ob_refs[:n_tiles[s]]):
                ls = slice(t * LANE_TILE, (t + 1) * LANE_TILE)
                y = y_ref[:, ls]
                for m in range(wg_ref.shape[0] // ROW_CHUNK):
                    sl = slice(m * ROW_CHUNK, (m + 1) * ROW_CHUNK)
                    g = jnp.dot(wg_ref[sl, :], y, preferred_element_type=F32) + bg_ref[sl, :]
                    ob = (y_ref[sl, ls].astype(F32) * jax.nn.sigmoid(g)
                          * jax.nn.silu(gate_ref[sl, ls].astype(F32)))
                    ob_ref[sl, :] = ob.astype(ob_ref.dtype)
            for t, ob_ref in enumerate(ob_refs[:n_tiles[s]]):
                ob = ob_ref[...]
                for m in range(wo_ref.shape[0] // ROW_CHUNK):
                    sl = slice(m * ROW_CHUNK, (m + 1) * ROW_CHUNK)
                    r_ref[t * LANE_TILE:(t + 1) * LANE_TILE, sl] = jnp.dot(
                        wo_ref[sl, :], ob, preferred_element_type=F32).T.astype(r_ref.dtype)


def _glu_out(y_ts, zbs, wg_t, bg, wo_t, cast_rows=None):
    cb = wg_t.shape[0]
    d = wo_t.shape[0]
    counts = [y.shape[0] for y in y_ts]
    maps, _ = _stream_maps(counts)
    const = lambda j: (0, 0)
    in_specs = [pl.BlockSpec((cb, cb), const, pipeline_mode=pl.Buffered(1)),
                pl.BlockSpec((cb, LANE_TILE), const, pipeline_mode=pl.Buffered(1)),
                pl.BlockSpec((d, cb), const, pipeline_mode=pl.Buffered(1))]
    in_specs += [pl.BlockSpec((None, cb, y.shape[2]), lambda j, m=m: (m(j), 0, 0)) for y, m in zip(y_ts, maps)]
    in_specs += [pl.BlockSpec((None, cb, y.shape[2]), lambda j, m=m: (m(j), 1, 0)) for y, m in zip(y_ts, maps)]
    args = [wg_t, bg, wo_t, *y_ts, *zbs]
    out_specs = [pl.BlockSpec((None, y.shape[2], d), lambda j, m=m: (m(j), 0, 0)) for y, m in zip(y_ts, maps)]
    out_shape = [jax.ShapeDtypeStruct((y.shape[0], y.shape[2], d), BF16) for y in y_ts]
    if cast_rows is not None:
        w, n_rows = cast_rows
        assert counts[0] * LANES == n_rows, "one 128-row block per step of the first stream set"
        blk = pl.BlockSpec((LANES, w.shape[1]), lambda j: (maps[0](j), 0))
        in_specs.append(blk)
        args.append(w)
        out_specs.append(blk)
        out_shape.append(jax.ShapeDtypeStruct((n_rows, w.shape[1]), BF16))
    n_tiles = [y.shape[2] // LANE_TILE for y in y_ts]
    res = pl.pallas_call(
        functools.partial(_glu_out_kernel, counts=counts, has_cast=cast_rows is not None,
                          n_tiles=n_tiles),
        grid=(sum(counts),),
        in_specs=in_specs,
        out_specs=out_specs,
        out_shape=out_shape,
        scratch_shapes=[pltpu.VMEM((cb, LANE_TILE), BF16)] * max(n_tiles),
        compiler_params=_params("arbitrary"),
    )(*args)
    return tuple(res[:len(y_ts)]), (res[len(y_ts)] if cast_rows is not None else None)


def _out_norm_kernel(x_ref, oa_ref, r_ref, wo_ref, g_ref, y_ref, ms_ref, *, tc, n):
    ca = oa_ref.shape[-1]
    d = x_ref.shape[-1]
    nc = d // LANES
    p = jnp.dot(oa_ref[...].reshape(tc * n, ca), wo_ref[...], preferred_element_type=F32)
    pitch = ms_ref.shape[1] // n
    for j in range(tc):
        s = p[j * n:(j + 1) * n] + r_ref[j].astype(F32)
        for c in range(nc):
            ms_ref[c, pl.ds(j, n, stride=pitch), :] = s[:, c * LANES:(c + 1) * LANES]
    mixed = [ms_ref[c].reshape(n, pitch, LANES)[:, :tc, :].reshape(n * tc, LANES) for c in range(nc)]
    acc = x_ref[...] + jnp.concatenate(mixed, axis=1)
    ms = jnp.mean(acc * acc, axis=-1, keepdims=True)
    y_ref[...] = acc * lax.rsqrt(ms + EPS) * g_ref[...]


def _out_norm(x2d, oa, r, wo_a, gf, n):
    tc, nk, ca = oa.shape
    t, d = x2d.shape
    return pl.pallas_call(
        functools.partial(_out_norm_kernel, tc=tc, n=n),
        grid=(nk // n,),
        in_specs=[pl.BlockSpec((n * tc, d), lambda i: (i, 0)),
                  pl.BlockSpec((tc, n, ca), lambda i: (0, i, 0)),
                  pl.BlockSpec((tc, n, d), lambda i: (0, i, 0)),
                  pl.BlockSpec((ca, d), lambda i: (0, 0), pipeline_mode=pl.Buffered(1)),
                  pl.BlockSpec((1, d), lambda i: (0, 0))],
        out_specs=pl.BlockSpec((n * tc, d), lambda i: (i, 0)),
        out_shape=jax.ShapeDtypeStruct((t, d), F32),
        scratch_shapes=[pltpu.VMEM((d // LANES, n * UNPERMUTE_PITCH, LANES), F32)],
        compiler_params=_params("parallel"),
    )(x2d, oa, r, wo_a, gf.reshape(1, d))


def _discretise(a_re, a_im, log_dt):
    dt = jnp.exp(log_dt)[:, None]
    x_re, x_im = a_re * dt, a_im * dt
    n = (2.0 ** jnp.arange(N_POW, dtype=F32))[None, :, None]
    mag = jnp.exp(n * x_re[:, None, :])
    ang = n * x_im[:, None, :]
    pw_r, pw_i = mag * jnp.cos(ang), mag * jnp.sin(ang)
    lr, li = pw_r[:, 0], pw_i[:, 0]
    den = a_re * a_re + a_im * a_im
    nr, ni = lr - 1.0, li
    qr, qi = (nr * a_re + ni * a_im) / den, (ni * a_re - nr * a_im) / den
    cat = lambda a, b: jnp.concatenate([a, b], axis=-1)
    rowtab = jnp.stack([cat(lr, lr), cat(li, li), cat(qr, qi), cat(-qi, qr)], axis=1)
    zpad = jnp.zeros_like(pw_r[:, :POW_COLS - N_POW])
    lt = jnp.concatenate([pw_r, zpad, pw_i, zpad], axis=1).transpose(0, 2, 1)
    b0 = TC_PROMPT.bit_length() - 1
    sc_r, sc_i = pw_r[:, b0:b0 + SCAN_STEPS], pw_i[:, b0:b0 + SCAN_STEPS]
    zrow = jnp.zeros_like(cat(sc_r, sc_r)[:, :SCAN_ROWS - SCAN_STEPS])
    scan_rows = jnp.concatenate([cat(sc_r, sc_r), zrow, cat(sc_i, sc_i), zrow], axis=1)
    return rowtab, lt, scan_rows


def _ssm_tables_kernel(rt_ref, btr_ref, bti_ref, cr_ref, ci_ref,
                       mp_ref, q_ref, mps_ref, qs_ref, *, gb, tc, tcs, unroll):
    rows = tc * GROUP
    rows_s = tcs * GROUP
    half = rt_ref.shape[2] // 2
    lane_k = lax.broadcasted_iota(jnp.int32, (GROUP, rows), 1)
    lane_q = lax.broadcasted_iota(jnp.int32, (1, 2 * half), 1)

    def one_group(g):
        rt = rt_ref[g]
        l_r, l_i, qu, qv = rt[0:1], rt[1:2], rt[2:3], rt[3:4]
        btr, bti = btr_ref[g], bti_ref[g]
        bb = btr * qu + bti * qv
        bs = btr * qv - bti * qu
        c_r, c_i = cr_ref[g], ci_ref[g]
        c2r = jnp.concatenate([c_r, c_r], axis=-1)
        c2i = jnp.concatenate([c_i, c_i], axis=-1)
        c2n = jnp.concatenate([c_r, -c_i], axis=-1)
        cur_r = jnp.ones((1, 2 * half), F32)
        cur_i = jnp.zeros((1, 2 * half), F32)
        et_blocks = []
        for n in range(tc):
            et_blocks.append(cur_r * bb + cur_i * bs)
            cur_r, cur_i = cur_r * l_r - cur_i * l_i, cur_r * l_i + cur_i * l_r
            u = jnp.where(lane_q < half, cur_r, -cur_i)
            v = jnp.where(lane_q < half, -cur_i, -cur_r)
            blk = (c2r * u + c2i * v).astype(q_ref.dtype)
            q_ref[g, n * GROUP:(n + 1) * GROUP, :] = blk
            if n < tcs:
                qs_ref[g, n * GROUP:(n + 1) * GROUP, :] = blk
        p = jnp.concatenate(et_blocks[::-1], axis=0).T
        mp_ref[g, rows:, :] = p.astype(mp_ref.dtype)
        mps_ref[g, rows_s:, :] = p[:, rows - rows_s:].astype(mps_ref.dtype)
        krow = jnp.dot(c2n, p, precision=lax.Precision.HIGHEST, preferred_element_type=F32)
        for i in range(tc):
            sh = (tc - 1 - i) * GROUP
            blk = krow if sh == 0 else pltpu.roll(krow, rows - sh, 1)
            blk = jnp.where(lane_k < (i + 1) * GROUP, blk, 0.0)
            mp_ref[g, i * GROUP:(i + 1) * GROUP, :] = blk.astype(mp_ref.dtype)
            if i < tcs:
                mps_ref[g, i * GROUP:(i + 1) * GROUP, :] = blk[:, :rows_s].astype(mps_ref.dtype)

    def body(gi, carry):
        for k in range(unroll):
            one_group(gi * unroll + k)
        return carry

    if unroll == gb:
        body(0, 0)
    else:
        lax.fori_loop(0, gb // unroll, body, 0)


def _ssm_tables_job(rowtab, b_re, b_im, c_re, c_im, tcs, gb):
    ng, _, p2 = rowtab.shape
    tc = TC_PROMPT
    rows, rows_s = tc * GROUP, tcs * GROUP
    dup = lambda a: jnp.concatenate([a, a], axis=-1)
    args = (rowtab, dup(b_re.transpose(0, 2, 1)), dup(b_im.transpose(0, 2, 1)), c_re, c_im)
    shapes = [(ng, rows + p2, rows), (ng, rows, p2), (ng, rows_s + p2, rows_s), (ng, rows_s, p2)]
    in_specs = [pl.BlockSpec((gb,) + a.shape[1:], lambda i: (i, 0, 0)) for a in args]
    out_specs = [pl.BlockSpec((gb,) + s[1:], lambda i: (i, 0, 0)) for s in shapes]
    out_shape = [jax.ShapeDtypeStruct(s, BF16) for s in shapes]
    body = functools.partial(_ssm_tables_kernel, gb=gb, tc=tc, tcs=tcs, unroll=gb)
    return args, in_specs, out_specs, out_shape, body, ng // gb


def _mix_bias(b_s, hd, tc):
    nh = b_s.shape[0]
    r = CHUNK // tc
    if tc == TC_PROMPT:
        bm = b_s.reshape(nh, r, tc).transpose(0, 2, 1).reshape(nh, CHUNK)
    else:
        bm = jnp.repeat(b_s[:, :tc], r, axis=1)
    return jnp.broadcast_to(bm[:, :, None], (nh, CHUNK, hd))


def _tiles(tc, nk):
    fold = max(1, LANE_TILE // nk)
    return NORM_ROWS // tc, min(BRANCH_A_ROWS, tc * nk) // tc, fold


def kernel(x_prompt, x_sample, state_ssm_re, state_ssm_im, g_norm, w_in, g_v, w_s, b_s,
           a_re, a_im, log_dt, b_re, b_im, c_re, c_im, d_skip, w_glu, b_glu, w_out, g_final):
    depth = g_norm.shape[0]
    assert depth == 1, "single-layer step"
    bsz, seq, d = x_prompt.shape
    dbsz, dseq, _ = x_sample.shape
    nh = w_s.shape[1]
    ng, npst = a_re.shape[1:]
    cb = ng * GROUP
    ca = (w_in.shape[2] - 2 * cb) // 3
    hd = ca // nh
    assert seq // TC_PROMPT == LANES and dbsz == LANES and CHUNK % dseq == 0
    assert dseq & (dseq - 1) == 0 and dseq <= TC_PROMPT, "sample chunk must be a power of two"
    assert ca == cb == d

    l = 0
    gv = g_v[l].reshape(nh, 1, hd)
    rowtab, lt, scan_rows = _discretise(a_re[l], a_im[l], log_dt[l])

    assert bsz % 2 == 0
    tcs = (TC_PROMPT, dseq)
    nks = (bsz * seq // TC_PROMPT, dbsz)
    xs = (x_prompt.reshape(bsz * seq, d), x_sample.reshape(dbsz * dseq, d))
    tiles = [_tiles(tc, nk) for tc, nk in zip(tcs, nks)]
    tables_job = _ssm_tables_job(rowtab, b_re[l], b_im[l], c_re[l], c_im[l], dseq,
                                 ng // (nks[0] // tiles[0][0]))
    xns, out_as, v_rows = [], [], []
    tabs = w_bt = None
    for s, (x2d, tc, (n_norm, n_a, _)) in enumerate(zip(xs, tcs, tiles)):
        xn, made = _norm_permute(x2d, g_norm[l], tc, n_norm, tables_job if s == 0 else None)
        tabs = made if s == 0 else tabs
        out_a, v, made_bt = _branch_a(xn, w_in[l], gv, w_s[l], _mix_bias(b_s[l], hd, tc), n_a,
                                      tc * n_a // PROJ_ROWS, s == 1, s == 0)
        w_bt = made_bt if s == 0 else w_bt
        xns.append(xn)
        out_as.append(out_a)
        v_rows.append(v)
    mp, q, mp_s, q_s = tabs
    folds = [t[2] for t in tiles]
    slabs = [xn.reshape(tc // f, f * nk, d) for xn, tc, nk, f in zip(xns, tcs, nks, folds)]
    zbs, (wg_t, wo_bt) = _proj_b(w_bt, slabs, ((w_glu[l], cb, 0), (w_out[l], cb, 1)))
    dsk = jnp.broadcast_to(d_skip[l][:, None], (cb, LANES))
    h0 = (state_ssm_re[l].transpose(1, 2, 0), state_ssm_im[l].transpose(1, 2, 0))
    y_p_t, hf_p = _ssm(zbs[0], mp, q, scan_rows, dsk, None, SSM_GROUPS_PER_STEP, folds[0])
    y_s_t, hf_s = _ssm(zbs[1], mp_s, q_s, lt, dsk, h0, SSM_GROUPS_PER_STEP_1CHUNK, folds[1])
    bg = jnp.broadcast_to(b_glu[l][:, None], (cb, LANE_TILE))
    rs, wo_a = _glu_out((y_p_t, y_s_t), zbs, wg_t, bg, wo_bt, (w_out[l], ca))
    y_p, y_s = [_out_norm(x2d, out_a, r.reshape(tc, nk, d), wo_a, g_final, OUT_ROWS // tc)
                for x2d, out_a, r, tc, nk in zip(xs, out_as, rs, tcs, nks)]
    v_s = v_rows[1]

    hf_p = hf_p[0].reshape(ng, bsz // 2, 2, 2, npst)
    re_p = hf_p[:, :, 0].reshape(ng, bsz, npst).transpose(1, 0, 2)[None]
    im_p = hf_p[:, :, 1].reshape(ng, bsz, npst).transpose(1, 0, 2)[None]
    re_s = hf_s[0].transpose(2, 0, 1)[None]
    im_s = hf_s[1].transpose(2, 0, 1)[None]
    v_out = v_s.transpose(1, 0, 2)[None]

    dt_p, dt_s = x_prompt.dtype, x_sample.dtype
    return (y_p.reshape(bsz, seq, d).astype(dt_p), y_s.reshape(dbsz, dseq, d).astype(dt_s),
            re_p.astype(dt_p), im_p.astype(dt_p), re_s.astype(dt_s), im_s.astype(dt_s),
            v_out.astype(dt_s))
```

```python
import functools

import jax
import jax.numpy as jnp
from jax import lax
from jax.experimental import pallas as pl
from jax.experimental.pallas import tpu as pltpu

EPS = 1e-6
LANES = 128
CHUNK = 128
GROUP = 16
TC_PROMPT = 16
ROW_CHUNK = 512
LANE_TILE = 256
PROJ_ROWS = 512
BRANCH_A_ROWS = 2048
NORM_ROWS = 512
OUT_ROWS = 512
UNPERMUTE_PITCH = 24
SSM_GROUPS_PER_STEP = 8
SSM_GROUPS_PER_STEP_1CHUNK = 32
POW_COLS = 16
SCAN_STEPS = 7
N_POW = 11
SCAN_ROWS = 8
GROUPS_PER_TRIP = 8
VMEM_LIMIT = 56 * 1024 * 1024

F32 = jnp.float32
BF16 = jnp.bfloat16


def _params(*sem):
    return pltpu.CompilerParams(dimension_semantics=sem, vmem_limit_bytes=VMEM_LIMIT)


def _norm_permute_kernel(*refs, tcs, ns, counts, n_side_in, side_body):
    k = len(tcs)
    x_refs, g_ref = refs[:k], refs[k]
    side_in = refs[k + 1:k + 1 + n_side_in]
    o_refs = refs[k + 1 + n_side_in:2 * k + 1 + n_side_in]
    side_out = refs[2 * k + 1 + n_side_in:]
    _, owns = _stream_maps(counts)
    step = pl.program_id(0)
    for s in range(k):
        @pl.when(owns[s](step))
        def _(s=s, tc=tcs[s], n=ns[s], x_ref=x_refs[s], o_ref=o_refs[s]):
            x = x_ref[...]
            ms = jnp.mean(x * x, axis=-1, keepdims=True)
            xn = (x * lax.rsqrt(ms + EPS) * g_ref[...]).astype(BF16)
            rows = n * tc
            dst = lax.broadcasted_iota(jnp.int32, (rows, rows), 0)
            src = lax.broadcasted_iota(jnp.int32, (rows, rows), 1)
            sel = (src == tc * (dst % n) + dst // n).astype(BF16)
            o_ref[...] = jnp.dot(sel, xn, preferred_element_type=F32).astype(o_ref.dtype).reshape(o_ref.shape)
            if s == 0 and side_body is not None:
                side_body(*side_in, *side_out)


def _norm_permute(x2ds, g, tcs, ns, side_job=None):
    d = x2ds[0].shape[1]
    nks = [x.shape[0] // tc for x, tc in zip(x2ds, tcs)]
    counts = [nk // n for nk, n in zip(nks, ns)]
    maps, _ = _stream_maps(counts)
    s_args, s_in, s_out, s_shape, s_body, s_steps = side_job or ((), [], [], [], None, counts[0])
    assert s_steps == counts[0], "the side job needs one block per grid step of the first stream set"
    res = pl.pallas_call(
        functools.partial(_norm_permute_kernel, tcs=tuple(tcs), ns=tuple(ns), counts=counts,
                          n_side_in=len(s_args), side_body=s_body),
        grid=(sum(counts),),
        in_specs=[pl.BlockSpec((n * tc, d), lambda i, m=m: (m(i), 0)) for tc, n, m in zip(tcs, ns, maps)]
        + [pl.BlockSpec((1, d), lambda i: (0, 0))] + list(s_in),
        out_specs=[pl.BlockSpec((tc, n, d), lambda i, m=m: (0, m(i), 0)) for tc, n, m in zip(tcs, ns, maps)]
        + list(s_out),
        out_shape=[jax.ShapeDtypeStruct((tc, nk, d), BF16) for tc, nk in zip(tcs, nks)] + list(s_shape),
        compiler_params=_params("arbitrary"),
    )(*x2ds, g.reshape(1, d), *s_args)
    return tuple(res[:len(tcs)]), tuple(res[len(tcs):])


def _branch_a_kernel(*refs, tc, n, r, hd, n_sub, has_v, has_side):
    refs = list(refs)
    xn_ref, wu_ref, wv_ref, wg_ref, gv_ref, ws_ref, bmix_ref = refs[:7]
    del refs[:7]
    side_in = [refs.pop(0)] if has_side else []
    oa_ref = refs.pop(0)
    v_refs = [refs.pop(0)] if has_v else []
    side_out = [refs.pop(0)] if has_side else []
    w_ref, wm_ref = refs
    d = xn_ref.shape[-1]

    @pl.when(pl.program_id(1) == 0)
    def _():
        w_ref[:, :hd] = wu_ref[...].astype(BF16)
        w_ref[:, hd:2 * hd] = wv_ref[...].astype(BF16)
        w_ref[:, 2 * hd:] = wg_ref[...].astype(BF16)
        row = lax.broadcasted_iota(jnp.int32, (CHUNK, CHUNK), 0)
        col = lax.broadcasted_iota(jnp.int32, (CHUNK, CHUNK), 1)
        w_tril = jnp.where(row >= col, ws_ref[...], 0.0).astype(BF16)
        if tc == TC_PROMPT:
            time_of_row = tc * (row % r) + row // r
            same_seq = None
        else:
            time_of_row = row // r
            same_seq = (row % r) == (col % r)
        sel = (col == time_of_row).astype(BF16)
        sw = jnp.dot(sel, w_tril, preferred_element_type=F32).astype(BF16)
        wm = lax.dot_general(sw, sel, (((1,), (1,)), ((), ())), preferred_element_type=F32)
        if same_seq is not None:
            wm = jnp.where(same_seq, wm, 0.0)
        wm_ref[...] = wm.astype(BF16)

    ns = n // n_sub
    zs = [jnp.dot(xn_ref[:, k * ns:(k + 1) * ns, :].reshape(tc * ns, d), w_ref[...],
                  preferred_element_type=F32) for k in range(n_sub)]
    wm = wm_ref[...]
    bm = bmix_ref[...]
    for k, z in enumerate(zs):
        k0 = k * ns
        u = jax.nn.gelu(z[:, :hd])
        v = jax.nn.gelu(z[:, hd:2 * hd])
        ga = z[:, 2 * hd:]
        ms = jnp.mean(v * v, axis=-1, keepdims=True)
        vn3 = (v * lax.rsqrt(ms + EPS) * gv_ref[...]).reshape(tc, ns, hd)
        pre3 = (u * jax.nn.silu(ga)).reshape(tc, ns, hd)
        if v_refs:
            v_refs[0][:, k0:k0 + ns, :] = vn3
        for c in range(ns // r):
            vg = vn3[:, c * r:(c + 1) * r, :].reshape(CHUNK, hd)
            mixed = jnp.dot(wm, vg.astype(BF16), preferred_element_type=F32) + bm
            og = pre3[:, c * r:(c + 1) * r, :].reshape(CHUNK, hd) * mixed
            oa_ref[:, k0 + c * r:k0 + (c + 1) * r, :] = og.reshape(tc, r, hd).astype(oa_ref.dtype)
    _transpose_side_blocks(side_in, side_out)


def _branch_a(xn, w_in, gv, w_s, bmix, n, n_sub, want_v, transpose_b):
    tc, nk, d = xn.shape
    nh, _, hd = gv.shape
    r = CHUNK // tc
    nt = nk // n
    out_map = lambda h, i: (0, i, h)
    in_specs = [pl.BlockSpec((tc, n, d), lambda h, i: (0, i, 0)),
                pl.BlockSpec((d, hd), lambda h, i: (0, h)),
                pl.BlockSpec((d, hd), lambda h, i: (0, nh + h)),
                pl.BlockSpec((d, hd), lambda h, i: (0, 2 * nh + h)),
                pl.BlockSpec((None, 1, hd), lambda h, i: (h, 0, 0)),
                pl.BlockSpec((None, CHUNK, CHUNK), lambda h, i: (h, 0, 0)),
                pl.BlockSpec((None, CHUNK, hd), lambda h, i: (h, 0, 0))]
    args = [xn, w_in, w_in, w_in, gv, w_s, bmix]
    out_shape = [jax.ShapeDtypeStruct((tc, nk, nh * hd), BF16)]
    out_specs = [pl.BlockSpec((tc, n, hd), out_map)]
    if want_v:
        out_shape.append(jax.ShapeDtypeStruct((tc, nk, nh * hd), F32))
        out_specs.append(pl.BlockSpec((tc, n, hd), out_map))
    if transpose_b:
        col0 = 3 * nh * hd // LANES
        n_cols = w_in.shape[1] - 3 * nh * hd
        assert nh * nt * LANES == n_cols, "one 128-column block per grid step"
        in_specs.append(pl.BlockSpec((d, LANES), lambda h, i: (0, col0 + h * nt + i)))
        args.append(w_in)
        out_shape.append(jax.ShapeDtypeStruct((n_cols, d), BF16))
        out_specs.append(pl.BlockSpec((LANES, d), lambda h, i: (h * nt + i, 0)))
    res = list(pl.pallas_call(
        functools.partial(_branch_a_kernel, tc=tc, n=n, r=r, hd=hd, n_sub=n_sub,
                          has_v=want_v, has_side=transpose_b),
        grid=(nh, nt),
        in_specs=in_specs,
        out_specs=out_specs,
        out_shape=out_shape,
        scratch_shapes=[pltpu.VMEM((d, 3 * hd), BF16), pltpu.VMEM((CHUNK, CHUNK), BF16)],
        compiler_params=_params("parallel", "arbitrary"),
    )(*args))
    out_a = res.pop(0)
    v_rows = res.pop(0) if want_v else None
    w_bt = res.pop(0) if transpose_b else None
    return out_a, v_rows, w_bt


def _transpose_side_blocks(in_refs, out_refs):
    for w_ref, o_ref in zip(in_refs, out_refs):
        o_ref[...] = w_ref[...].T.astype(o_ref.dtype)


def _stream_maps(counts):
    starts = [sum(counts[:s]) for s in range(len(counts))]
    maps = [lambda j, st=st, c=c: jnp.clip(j - st, 0, c - 1) for st, c in zip(starts, counts)]
    owns = [lambda j, st=st, c=c: jnp.logical_and(j >= st, j < st + c) for st, c in zip(starts, counts)]
    return maps, owns


def _proj_b_kernel(w_ref, *refs, counts, n_side):
    ns = len(counts)
    xn_refs, side_in = refs[:ns], refs[ns:ns + n_side]
    o_refs, side_out = refs[ns + n_side:2 * ns + n_side], refs[2 * ns + n_side:]
    _, owns = _stream_maps(counts)
    j = pl.program_id(0)
    for s, (xn_ref, o_ref) in enumerate(zip(xn_refs, o_refs)):
        @pl.when(owns[s](j))
        def _(s=s, xn_ref=xn_ref, o_ref=o_ref):
            if s == 0:
                _transpose_side_blocks(side_in, side_out)
            for t in range(xn_ref.shape[0] // LANE_TILE):
                ls = slice(t * LANE_TILE, (t + 1) * LANE_TILE)
                xn = xn_ref[ls, :]
                for m in range(w_ref.shape[0] // ROW_CHUNK):
                    sl = slice(m * ROW_CHUNK, (m + 1) * ROW_CHUNK)
                    o_ref[sl, ls] = lax.dot_general(
                        w_ref[sl, :], xn, (((1,), (1,)), ((), ())),
                        preferred_element_type=F32).astype(o_ref.dtype)


def _proj_b(w_bt, xns, side=()):
    rows, d = w_bt.shape
    counts = [x.shape[0] for x in xns]
    maps, _ = _stream_maps(counts)
    in_specs = [pl.BlockSpec((rows, d), lambda j: (0, 0), pipeline_mode=pl.Buffered(1))]
    in_specs += [pl.BlockSpec((None, x.shape[1], d), lambda j, m=m: (m(j), 0, 0))
                 for x, m in zip(xns, maps)]
    out_specs = [pl.BlockSpec((None, rows, x.shape[1]), lambda j, m=m: (m(j), 0, 0))
                 for x, m in zip(xns, maps)]
    out_shape = [jax.ShapeDtypeStruct((x.shape[0], rows, x.shape[1]), BF16) for x in xns]
    for w, n_rows, row_block in side:
        n_cols = w.shape[1]
        assert counts[0] * LANES == n_cols, "one 128-column block per step of the first stream set"
        in_specs.append(pl.BlockSpec((n_rows, LANES), lambda j, rb=row_block: (rb, maps[0](j))))
        out_specs.append(pl.BlockSpec((LANES, n_rows), lambda j: (maps[0](j), 0)))
        out_shape.append(jax.ShapeDtypeStruct((n_cols, n_rows), BF16))
    res = pl.pallas_call(
        functools.partial(_proj_b_kernel, counts=counts, n_side=len(side)),
        grid=(sum(counts),),
        in_specs=in_specs,
        out_specs=out_specs,
        out_shape=out_shape,
        compiler_params=_params("arbitrary"),
    )(w_bt, *xns, *[w for w, _, _ in side])
    return tuple(res[:len(xns)]), tuple(res[len(xns):])


def _shift_rows(x, d):
    if d % 8 == 0:
        return jnp.concatenate([jnp.zeros((d, x.shape[1]), x.dtype), x[:-d]], axis=0)
    row = lax.broadcasted_iota(jnp.int32, x.shape, 0)
    return jnp.where(row >= d, pltpu.roll(x, d, 0), 0.0)


def _ssm_kernel(x_ref, mp_ref, q_ref, ct_ref, dsk_ref, *rest, tc, gb, nk, fold, has_h0):
    if has_h0:
        h0r_ref, h0i_ref, y_ref, hfr_ref, hfi_ref = rest
    else:
        y_ref, hf_ref = rest
    rows = tc * GROUP
    half = q_ref.shape[2] // 2
    b_tc = tc.bit_length() - 1

    def one_group(g):
        r0 = g * GROUP if isinstance(g, int) else pl.multiple_of(g * GROUP, GROUP)
        if fold == 1:
            xg = x_ref[:, pl.ds(r0, GROUP), :]
        else:
            xg = jnp.stack([x_ref[j // fold, pl.ds(r0, GROUP), (j % fold) * nk:(j % fold + 1) * nk]
                            for j in range(tc)])
        res = jnp.dot(mp_ref[g], xg.reshape(rows, nk), preferred_element_type=F32)
        y_local = res[:rows]
        s_re = res[rows:rows + half]
        s_im = res[rows + half:]
        ct = ct_ref[g]
        if has_h0:
            h_re, h_im = h0r_ref[g], h0i_ref[g]
            a_re, a_im = ct[:, b_tc:b_tc + 1], ct[:, POW_COLS + b_tc:POW_COLS + b_tc + 1]
            hfr_ref[g] = a_re * h_re - a_im * h_im + s_re
            hfi_ref[g] = a_re * h_im + a_im * h_re + s_im
            h_in = jnp.concatenate([h_re, h_im], axis=0)
        else:
            tiles = []
            for pair in range(nk // (2 * LANES)):
                lo, mid, hi = 2 * pair * LANES, (2 * pair + 1) * LANES, (2 * pair + 2) * LANES
                xr = jnp.concatenate([s_re[:, lo:mid], s_re[:, mid:hi]], axis=0).T
                xi = jnp.concatenate([s_im[:, lo:mid], s_im[:, mid:hi]], axis=0).T
                for s in range(SCAN_STEPS):
                    cr, ci = ct[s:s + 1], ct[SCAN_ROWS + s:SCAN_ROWS + s + 1]
                    rr, ri = _shift_rows(xr, 1 << s), _shift_rows(xi, 1 << s)
                    xr, xi = xr + cr * rr - ci * ri, xi + cr * ri + ci * rr
                hf_ref[g, 2 * pair:2 * pair + 1, :] = xr[LANES - 1:]
                hf_ref[g, 2 * pair + 1:2 * pair + 2, :] = xi[LANES - 1:]
                hr_t = _shift_rows(xr, 1).T
                hi_t = _shift_rows(xi, 1).T
                tiles.append(jnp.concatenate([hr_t[:half], hi_t[:half]], axis=0))
                tiles.append(jnp.concatenate([hr_t[half:], hi_t[half:]], axis=0))
            h_in = jnp.concatenate(tiles, axis=1)
        y = y_local + jnp.dot(q_ref[g], h_in.astype(BF16), preferred_element_type=F32)
        dsk = jnp.concatenate([dsk_ref[pl.ds(r0, GROUP), :]] * (nk // LANES), axis=1)
        y3 = y.reshape(tc, GROUP, nk) + dsk[None] * xg.astype(F32)
        ya = jax.nn.gelu(y3).astype(y_ref.dtype)
        if fold == 1:
            y_ref[:, pl.ds(r0, GROUP), :] = ya
        else:
            for j in range(tc):
                y_ref[j // fold, pl.ds(r0, GROUP), (j % fold) * nk:(j % fold + 1) * nk] = ya[j]

    def body(gi, carry):
        for k in range(GROUPS_PER_TRIP):
            one_group(gi * GROUPS_PER_TRIP + k)
        return carry

    if gb == GROUPS_PER_TRIP:
        body(0, 0)
    else:
        lax.fori_loop(0, gb // GROUPS_PER_TRIP, body, 0)


def _ssm(zb, mp, q, ct, dsk, h0, gb, fold):
    tcf, _, nkf = zb.shape
    tc, nk = tcf * fold, nkf // fold
    ng, mrows, rows = mp.shape
    p2 = mrows - rows
    cb = ng * GROUP
    has_h0 = h0 is not None
    in_specs = [pl.BlockSpec((tcf, gb * GROUP, nkf), lambda i: (0, i, 0)),
                pl.BlockSpec((gb, mrows, rows), lambda i: (i, 0, 0)),
                pl.BlockSpec((gb, rows, p2), lambda i: (i, 0, 0)),
                pl.BlockSpec((gb,) + ct.shape[1:], lambda i: (i, 0, 0)),
                pl.BlockSpec((gb * GROUP, LANES), lambda i: (i, 0))]
    args = [zb, mp, q, ct, dsk]
    if has_h0:
        state_spec = pl.BlockSpec((gb, p2 // 2, nk), lambda i: (i, 0, 0))
        in_specs += [state_spec, state_spec]
        args += list(h0)
        hf_specs = [state_spec, state_spec]
        hf_shapes = [jax.ShapeDtypeStruct((ng, p2 // 2, nk), F32)] * 2
    else:
        hf_specs = [pl.BlockSpec((gb, nk // LANES, LANES), lambda i: (i, 0, 0))]
        hf_shapes = [jax.ShapeDtypeStruct((ng, nk // LANES, LANES), F32)]
    res = pl.pallas_call(
        functools.partial(_ssm_kernel, tc=tc, gb=gb, nk=nk, fold=fold, has_h0=has_h0),
        grid=(ng // gb,),
        in_specs=in_specs,
        out_specs=[pl.BlockSpec((tcf, gb * GROUP, nkf), lambda i: (0, i, 0))] + hf_specs,
        out_shape=[jax.ShapeDtypeStruct((tcf, cb, nkf), BF16)] + hf_shapes,
        compiler_params=_params("parallel"),
    )(*args)
    return res[0], tuple(res[1:])


def _glu_out_kernel(wg_ref, bg_ref, wo_ref, *refs, counts, has_cast, n_tiles):
    ns = len(counts)
    refs = list(refs)
    y_refs, gate_refs = refs[:ns], refs[ns:2 * ns]
    del refs[:2 * ns]
    cast_in = refs.pop(0) if has_cast else None
    r_refs = refs[:ns]
    del refs[:ns]
    cast_out = refs.pop(0) if has_cast else None
    ob_refs = refs
    _, owns = _stream_maps(counts)
    j = pl.program_id(0)
    for s in range(ns):
        @pl.when(owns[s](j))
        def _(s=s, y_ref=y_refs[s], gate_ref=gate_refs[s], r_ref=r_refs[s]):
            if s == 0 and has_cast:
                cast_out[...] = cast_in[...].astype(cast_out.dtype)
            for t, ob_ref in enumerate(ob_refs[:n_tiles[s]]):
                ls = slice(t * LANE_TILE, (t + 1) * LANE_TILE)
                y = y_ref[:, ls]
                for m in range(wg_ref.shape[0] // ROW_CHUNK):
                    sl = slice(m * ROW_CHUNK, (m + 1) * ROW_CHUNK)
                    g = jnp.dot(wg_ref[sl, :], y, preferred_element_type=F32) + bg_ref[sl, :]
                    ob = (y_ref[sl, ls].astype(F32) * jax.nn.sigmoid(g)
                          * jax.nn.silu(gate_ref[sl, ls].astype(F32)))
                    ob_ref[sl, :] = ob.astype(ob_ref.dtype)
            for t, ob_ref in enumerate(ob_refs[:n_tiles[s]]):
                ob = ob_ref[...]
                for m in range(wo_ref.shape[0] // ROW_CHUNK):
                    sl = slice(m * ROW_CHUNK, (m + 1) * ROW_CHUNK)
                    r_ref[t * LANE_TILE:(t + 1) * LANE_TILE, sl] = jnp.dot(
                        wo_ref[sl, :], ob, preferred_element_type=F32).T.astype(r_ref.dtype)


def _glu_out(y_ts, zbs, wg_t, bg, wo_t, cast_rows=None):
    cb = wg_t.shape[0]
    d = wo_t.shape[0]
    counts = [y.shape[0] for y in y_ts]
    maps, _ = _stream_maps(counts)
    const = lambda j: (0, 0)
    in_specs = [pl.BlockSpec((cb, cb), const, pipeline_mode=pl.Buffered(1)),
                pl.BlockSpec((cb, LANE_TILE), const, pipeline_mode=pl.Buffered(1)),
                pl.BlockSpec((d, cb), const, pipeline_mode=pl.Buffered(1))]
    in_specs += [pl.BlockSpec((None, cb, y.shape[2]), lambda j, m=m: (m(j), 0, 0)) for y, m in zip(y_ts, maps)]
    in_specs += [pl.BlockSpec((None, cb, y.shape[2]), lambda j, m=m: (m(j), 1, 0)) for y, m in zip(y_ts, maps)]
    args = [wg_t, bg, wo_t, *y_ts, *zbs]
    out_specs = [pl.BlockSpec((None, y.shape[2], d), lambda j, m=m: (m(j), 0, 0)) for y, m in zip(y_ts, maps)]
    out_shape = [jax.ShapeDtypeStruct((y.shape[0], y.shape[2], d), BF16) for y in y_ts]
    if cast_rows is not None:
        w, n_rows = cast_rows
        assert counts[0] * LANES == n_rows, "one 128-row block per step of the first stream set"
        blk = pl.BlockSpec((LANES, w.shape[1]), lambda j: (maps[0](j), 0))
        in_specs.append(blk)
        args.append(w)
        out_specs.append(blk)
        out_shape.append(jax.ShapeDtypeStruct((n_rows, w.shape[1]), BF16))
    n_tiles = [y.shape[2] // LANE_TILE for y in y_ts]
    res = pl.pallas_call(
        functools.partial(_glu_out_kernel, counts=counts, has_cast=cast_rows is not None,
                          n_tiles=n_tiles),
        grid=(sum(counts),),
        in_specs=in_specs,
        out_specs=out_specs,
        out_shape=out_shape,
        scratch_shapes=[pltpu.VMEM((cb, LANE_TILE), BF16)] * max(n_tiles),
        compiler_params=_params("arbitrary"),
    )(*args)
    return tuple(res[:len(y_ts)]), (res[len(y_ts)] if cast_rows is not None else None)


def _out_norm_kernel(x_ref, oa_ref, r_ref, wo_ref, g_ref, y_ref, ms_ref, *, tc, n):
    ca = oa_ref.shape[-1]
    d = x_ref.shape[-1]
    nc = d // LANES
    p = jnp.dot(oa_ref[...].reshape(tc * n, ca), wo_ref[...], preferred_element_type=F32)
    pitch = ms_ref.shape[1] // n
    for j in range(tc):
        s = p[j * n:(j + 1) * n] + r_ref[j].astype(F32)
        for c in range(nc):
            ms_ref[c, pl.ds(j, n, stride=pitch), :] = s[:, c * LANES:(c + 1) * LANES]
    mixed = [ms_ref[c].reshape(n, pitch, LANES)[:, :tc, :].reshape(n * tc, LANES) for c in range(nc)]
    acc = x_ref[...] + jnp.concatenate(mixed, axis=1)
    ms = jnp.mean(acc * acc, axis=-1, keepdims=True)
    y_ref[...] = acc * lax.rsqrt(ms + EPS) * g_ref[...]


def _out_norm(x2d, oa, r, wo_a, gf, n):
    tc, nk, ca = oa.shape
    t, d = x2d.shape
    return pl.pallas_call(
        functools.partial(_out_norm_kernel, tc=tc, n=n),
        grid=(nk // n,),
        in_specs=[pl.BlockSpec((n * tc, d), lambda i: (i, 0)),
                  pl.BlockSpec((tc, n, ca), lambda i: (0, i, 0)),
                  pl.BlockSpec((tc, n, d), lambda i: (0, i, 0)),
                  pl.BlockSpec((ca, d), lambda i: (0, 0), pipeline_mode=pl.Buffered(1)),
                  pl.BlockSpec((1, d), lambda i: (0, 0))],
        out_specs=pl.BlockSpec((n * tc, d), lambda i: (i, 0)),
        out_shape=jax.ShapeDtypeStruct((t, d), F32),
        scratch_shapes=[pltpu.VMEM((d // LANES, n * UNPERMUTE_PITCH, LANES), F32)],
        compiler_params=_params("parallel"),
    )(x2d, oa, r, wo_a, gf.reshape(1, d))


def _discretise(a_re, a_im, log_dt):
    dt = jnp.exp(log_dt)[:, None]
    x_re, x_im = a_re * dt, a_im * dt
    n = (2.0 ** jnp.arange(N_POW, dtype=F32))[None, :, None]
    mag = jnp.exp(n * x_re[:, None, :])
    ang = n * x_im[:, None, :]
    pw_r, pw_i = mag * jnp.cos(ang), mag * jnp.sin(ang)
    lr, li = pw_r[:, 0], pw_i[:, 0]
    den = a_re * a_re + a_im * a_im
    nr, ni = lr - 1.0, li
    qr, qi = (nr * a_re + ni * a_im) / den, (ni * a_re - nr * a_im) / den
    cat = lambda a, b: jnp.concatenate([a, b], axis=-1)
    rowtab = jnp.stack([cat(lr, lr), cat(li, li), cat(qr, qi), cat(-qi, qr)], axis=1)
    zpad = jnp.zeros_like(pw_r[:, :POW_COLS - N_POW])
    lt = jnp.concatenate([pw_r, zpad, pw_i, zpad], axis=1).transpose(0, 2, 1)
    b0 = TC_PROMPT.bit_length() - 1
    sc_r, sc_i = pw_r[:, b0:b0 + SCAN_STEPS], pw_i[:, b0:b0 + SCAN_STEPS]
    zrow = jnp.zeros_like(cat(sc_r, sc_r)[:, :SCAN_ROWS - SCAN_STEPS])
    scan_rows = jnp.concatenate([cat(sc_r, sc_r), zrow, cat(sc_i, sc_i), zrow], axis=1)
    return rowtab, lt, scan_rows


def _ssm_tables_kernel(rt_ref, btr_ref, bti_ref, c2r_ref, c2i_ref, c2n_ref,
                       mp_ref, q_ref, mps_ref, qs_ref, *, gb, tc, tcs, unroll):
    rows = tc * GROUP
    rows_s = tcs * GROUP
    half = rt_ref.shape[2] // 2
    lane_k = lax.broadcasted_iota(jnp.int32, (GROUP, rows), 1)
    lane_q = lax.broadcasted_iota(jnp.int32, (1, 2 * half), 1)

    def one_group(g):
        rt = rt_ref[g]
        l_r, l_i, qu, qv = rt[0:1], rt[1:2], rt[2:3], rt[3:4]
        btr, bti = btr_ref[g], bti_ref[g]
        bb = btr * qu + bti * qv
        bs = btr * qv - bti * qu
        c2r, c2i = c2r_ref[g], c2i_ref[g]
        cur_r = jnp.ones((1, 2 * half), F32)
        cur_i = jnp.zeros((1, 2 * half), F32)
        et_blocks = []
        for n in range(tc):
            et_blocks.append(cur_r * bb + cur_i * bs)
            cur_r, cur_i = cur_r * l_r - cur_i * l_i, cur_r * l_i + cur_i * l_r
            u = jnp.where(lane_q < half, cur_r, -cur_i)
            v = jnp.where(lane_q < half, -cur_i, -cur_r)
            blk = (c2r * u + c2i * v).astype(q_ref.dtype)
            q_ref[g, n * GROUP:(n + 1) * GROUP, :] = blk
            if n < tcs:
                qs_ref[g, n * GROUP:(n + 1) * GROUP, :] = blk
        p = jnp.concatenate(et_blocks[::-1], axis=0).T
        mp_ref[g, rows:, :] = p.astype(mp_ref.dtype)
        mps_ref[g, rows_s:, :] = p[:, rows - rows_s:].astype(mps_ref.dtype)
        krow = jnp.dot(c2n_ref[g], p, precision=lax.Precision.HIGHEST, preferred_element_type=F32)
        for i in range(tc):
            sh = (tc - 1 - i) * GROUP
            blk = krow if sh == 0 else pltpu.roll(krow, rows - sh, 1)
            blk = jnp.where(lane_k < (i + 1) * GROUP, blk, 0.0)
            mp_ref[g, i * GROUP:(i + 1) * GROUP, :] = blk.astype(mp_ref.dtype)
            if i < tcs:
                mps_ref[g, i * GROUP:(i + 1) * GROUP, :] = blk[:, :rows_s].astype(mps_ref.dtype)

    def body(gi, carry):
        for k in range(unroll):
            one_group(gi * unroll + k)
        return carry

    if unroll == gb:
        body(0, 0)
    else:
        lax.fori_loop(0, gb // unroll, body, 0)


def _ssm_tables_job(rowtab, b_re, b_im, c_re, c_im, tcs, gb):
    ng, _, p2 = rowtab.shape
    tc = TC_PROMPT
    rows, rows_s = tc * GROUP, tcs * GROUP
    dup = lambda a: jnp.concatenate([a, a], axis=-1)
    args = (rowtab, dup(b_re.transpose(0, 2, 1)), dup(b_im.transpose(0, 2, 1)),
            dup(c_re), dup(c_im), jnp.concatenate([c_re, -c_im], axis=-1))
    shapes = [(ng, rows + p2, rows), (ng, rows, p2), (ng, rows_s + p2, rows_s), (ng, rows_s, p2)]
    blk = lambda i: (jnp.minimum(i, ng // gb - 1), 0, 0)
    in_specs = [pl.BlockSpec((gb,) + a.shape[1:], blk) for a in args]
    out_specs = [pl.BlockSpec((gb,) + s[1:], blk) for s in shapes]
    out_shape = [jax.ShapeDtypeStruct(s, BF16) for s in shapes]
    body = functools.partial(_ssm_tables_kernel, gb=gb, tc=tc, tcs=tcs, unroll=gb)
    return args, in_specs, out_specs, out_shape, body, ng // gb


def _mix_bias(b_s, hd, tc):
    nh = b_s.shape[0]
    r = CHUNK // tc
    if tc == TC_PROMPT:
        bm = b_s.reshape(nh, r, tc).transpose(0, 2, 1).reshape(nh, CHUNK)
    else:
        bm = jnp.repeat(b_s[:, :tc], r, axis=1)
    return jnp.broadcast_to(bm[:, :, None], (nh, CHUNK, hd))


def _tiles(tc, nk):
    fold = max(1, LANE_TILE // nk)
    return NORM_ROWS // tc, min(BRANCH_A_ROWS, tc * nk) // tc, fold


def kernel(x_prompt, x_sample, state_ssm_re, state_ssm_im, g_norm, w_in, g_v, w_s, b_s,
           a_re, a_im, log_dt, b_re, b_im, c_re, c_im, d_skip, w_glu, b_glu, w_out, g_final):
    depth = g_norm.shape[0]
    assert depth == 1, "single-layer step"
    bsz, seq, d = x_prompt.shape
    dbsz, dseq, _ = x_sample.shape
    nh = w_s.shape[1]
    ng, npst = a_re.shape[1:]
    cb = ng * GROUP
    ca = (w_in.shape[2] - 2 * cb) // 3
    hd = ca // nh
    assert seq // TC_PROMPT == LANES and dbsz == LANES and CHUNK % dseq == 0
    assert dseq & (dseq - 1) == 0 and dseq <= TC_PROMPT, "sample chunk must be a power of two"
    assert ca == cb == d

    l = 0
    gv = g_v[l].reshape(nh, 1, hd)
    rowtab, lt, scan_rows = _discretise(a_re[l], a_im[l], log_dt[l])

    assert bsz % 2 == 0
    tcs = (TC_PROMPT, dseq)
    nks = (bsz * seq // TC_PROMPT, dbsz)
    xs = (x_prompt.reshape(bsz * seq, d), x_sample.reshape(dbsz * dseq, d))
    tiles = [_tiles(tc, nk) for tc, nk in zip(tcs, nks)]
    tables_job = _ssm_tables_job(rowtab, b_re[l], b_im[l], c_re[l], c_im[l], dseq,
                                 ng // (nks[0] // tiles[0][0]))
    out_as, v_rows = [], []
    w_bt = None
    xns, tabs = _norm_permute(xs, g_norm[l], tcs, [t[0] for t in tiles], tables_job)
    for s, (xn, tc, (_, n_a, _)) in enumerate(zip(xns, tcs, tiles)):
        out_a, v, made_bt = _branch_a(xn, w_in[l], gv, w_s[l], _mix_bias(b_s[l], hd, tc), n_a,
                                      tc * n_a // PROJ_ROWS, s == 1, s == 0)
        w_bt = made_bt if s == 0 else w_bt
        out_as.append(out_a)
        v_rows.append(v)
    mp, q, mp_s, q_s = tabs
    folds = [t[2] for t in tiles]
    slabs = [xn.reshape(tc // f, f * nk, d) for xn, tc, nk, f in zip(xns, tcs, nks, folds)]
    zbs, (wg_t, wo_bt) = _proj_b(w_bt, slabs, ((w_glu[l], cb, 0), (w_out[l], cb, 1)))
    dsk = jnp.broadcast_to(d_skip[l][:, None], (cb, LANES))
    h0 = (state_ssm_re[l].transpose(1, 2, 0), state_ssm_im[l].transpose(1, 2, 0))
    y_p_t, hf_p = _ssm(zbs[0], mp, q, scan_rows, dsk, None, SSM_GROUPS_PER_STEP, folds[0])
    y_s_t, hf_s = _ssm(zbs[1], mp_s, q_s, lt, dsk, h0, SSM_GROUPS_PER_STEP_1CHUNK, folds[1])
    bg = jnp.broadcast_to(b_glu[l][:, None], (cb, LANE_TILE))
    rs, wo_a = _glu_out((y_p_t, y_s_t), zbs, wg_t, bg, wo_bt, (w_out[l], ca))
    y_p, y_s = [_out_norm(x2d, out_a, r.reshape(tc, nk, d), wo_a, g_final, OUT_ROWS // tc)
                for x2d, out_a, r, tc, nk in zip(xs, out_as, rs, tcs, nks)]
    v_s = v_rows[1]

    hf_p = hf_p[0].reshape(ng, bsz // 2, 2, 2, npst)
    re_p = hf_p[:, :, 0].reshape(ng, bsz, npst).transpose(1, 0, 2)[None]
    im_p = hf_p[:, :, 1].reshape(ng, bsz, npst).transpose(1, 0, 2)[None]
    re_s = hf_s[0].transpose(2, 0, 1)[None]
    im_s = hf_s[1].transpose(2, 0, 1)[None]
    v_out = v_s.transpose(1, 0, 2)[None]

    dt_p, dt_s = x_prompt.dtype, x_sample.dtype
    return (y_p.reshape(bsz, seq, d).astype(dt_p), y_s.reshape(dbsz, dseq, d).astype(dt_s),
            re_p.astype(dt_p), im_p.astype(dt_p), re_s.astype(dt_s), im_s.astype(dt_s),
            v_out.astype(dt_s))
```
